```python
import math
import jax, jax.numpy as jnp
from jax import lax
import numpy as np


D_MODEL = 1024
BATCH = 8
SEQ = 4096
DEPTH = 2

CTX_LEN = 256
GRID_W = 64
N_EVEN = (DEPTH + 1) // 2
N_ODD = DEPTH // 2
EPS = 1e-6

D_FOURIER = D_MODEL // 2
N_FOURIER_GROUPS = 4
FOURIER_GROUP_DIM = D_FOURIER // N_FOURIER_GROUPS

D_HYENA = D_MODEL // 2
HYENA_EMB_DIM = 33
HYENA_BANDS = (HYENA_EMB_DIM - 1) // 2
HYENA_FILTER_WIDTH = 64
HYENA_TARGET = 1e-2
HYENA_FAST_DECAY_PCT = 0.3
HYENA_SLOW_DECAY_PCT = 1.5
HYENA_MIN_DECAY = math.log(HYENA_TARGET) / HYENA_FAST_DECAY_PCT
HYENA_MAX_DECAY = math.log(HYENA_TARGET) / HYENA_SLOW_DECAY_PCT

MLA_HEADS = 16
Q_LORA_RANK = 256
KV_LORA_RANK = 128
QK_NOPE_DIM = 64
QK_ROPE_DIM = 32
V_HEAD_DIM = 64
QK_HEAD_DIM = QK_NOPE_DIM + QK_ROPE_DIM
ROPE_THETA = 10000.0
Q_BLOCK = 128

D_FF = 2816

kernel_name = 'hybrid_fourier_hyena_mla_dit_block'


def _f32(a):
    return a.astype(jnp.float32)


def _rms_norm(x, g):
    x32 = _f32(x)
    y = x32 * lax.rsqrt(jnp.mean(x32 * x32, axis=-1, keepdims=True) + EPS)
    return (y * _f32(g)).astype(x.dtype)


def _dwconv3(x, w, b):
    xp = jnp.pad(x, ((0, 0), (1, 1), (0, 0)))
    return xp[:, :-2] * w[0] + xp[:, 1:-1] * w[1] + xp[:, 2:] * w[2] + b


def _adaln(cond, w_mod, b_mod):
    mod = jax.nn.silu(cond) @ w_mod + b_mod
    return jnp.split(mod[:, None, :], 6, axis=-1)


def _modulate(h, shift, scale):
    return h * (1 + scale) + shift


def _hyena_filter_spectrum(L, w1, b1, w2, b2, w3, b3, w4, freq):
    pos = jnp.arange(L, dtype=jnp.float32)
    t = pos / max(L - 1, 1)
    bands = jnp.linspace(1e-4, HYENA_BANDS - 1, HYENA_BANDS, dtype=jnp.float32)
    ang = (2.0 * math.pi / L) * pos[:, None] * bands[None, :]
    z = jnp.concatenate([t[:, None], jnp.cos(ang), -jnp.sin(ang)], axis=-1)
    f = _f32(freq)
    h = jnp.sin(f * (z @ _f32(w1) + _f32(b1)))
    h = jnp.sin(f * (h @ _f32(w2) + _f32(b2)))
    h = jnp.sin(f * (h @ _f32(w3) + _f32(b3)))
    h = h @ _f32(w4)
    deltas = jnp.linspace(HYENA_MIN_DECAY, HYENA_MAX_DECAY, D_HYENA, dtype=jnp.float32)
    decay = jnp.exp(-t[:, None] * jnp.abs(deltas)[None, :])
    h_fwd = h[:, :D_HYENA] * decay
    h_bwd = h[:, D_HYENA:] * decay
    two_sided = jnp.concatenate([h_fwd, jnp.zeros((1, D_HYENA), jnp.float32), h_bwd[:0:-1]], axis=0)
    two_sided = two_sided / jnp.sum(jnp.abs(two_sided), axis=0, keepdims=True)
    return jnp.fft.rfft(two_sided, axis=0)


def _long_conv(u, spectrum, bias):
    L = u.shape[1]
    u32 = _f32(u)
    y = jnp.fft.irfft(jnp.fft.rfft(u32, n=2 * L, axis=1) * spectrum[None], n=2 * L, axis=1)[:, :L]
    return (y + u32 * _f32(bias)).astype(u.dtype)


def _fourier_hyena_mixer(h, w_in, w_out, conv_w, conv_b, w1, b1, w2, b2, w3, b3, w4, freq, hy_bias):
    B, L, _ = h.shape
    proj = h @ w_in
    u_f = _f32(proj[..., :D_FOURIER]).reshape(B, L, N_FOURIER_GROUPS, FOURIER_GROUP_DIM)
    y_f = jnp.fft.fft2(u_f, axes=(1, 3), norm='ortho').real.reshape(B, L, D_FOURIER).astype(h.dtype)
    u_h = _dwconv3(proj[..., D_FOURIER:], conv_w, conv_b)
    x0, x1, v = jnp.split(u_h, 3, axis=-1)
    spectrum = _hyena_filter_spectrum(L, w1, b1, w2, b2, w3, b3, w4, freq)
    y_h = x0 * _long_conv(v * x1, spectrum, hy_bias)
    return jnp.concatenate([y_f, y_h], axis=-1) @ w_out


def _rope_2d(x, rope):
    cos_r, sin_r, cos_c, sin_c = rope
    q4 = QK_ROPE_DIM // 4
    half = QK_ROPE_DIM // 2

    def rot(xa, cos, sin):
        x1, x2 = xa[..., :q4], xa[..., q4:]
        cos = cos[None, :, None, :]
        sin = sin[None, :, None, :]
        return jnp.concatenate([x1 * cos - x2 * sin, x1 * sin + x2 * cos], axis=-1)

    out = jnp.concatenate([rot(x[..., :half], cos_r, sin_r), rot(x[..., half:], cos_c, sin_c)], axis=-1)
    return out.astype(x.dtype)


def _mla_queries(a_q, q_a_norm, w_uq, q_norm, rope):
    B, L, _ = a_q.shape
    q = (_rms_norm(a_q, q_a_norm) @ w_uq).reshape(B, L, MLA_HEADS, QK_HEAD_DIM)
    q = _rms_norm(q, q_norm)
    if rope is not None:
        q = jnp.concatenate([q[..., :QK_NOPE_DIM], _rope_2d(q[..., QK_NOPE_DIM:], rope)], axis=-1)
    return q


def _mla_keys_values(a_kv, kv_a_norm, w_ukv, k_norm, rope):
    B, L, _ = a_kv.shape
    ckv = _rms_norm(a_kv[..., :KV_LORA_RANK], kv_a_norm)
    k_pe = a_kv[..., KV_LORA_RANK:]
    kv = (ckv @ w_ukv).reshape(B, L, MLA_HEADS, QK_NOPE_DIM + V_HEAD_DIM)
    k_nope, v = kv[..., :QK_NOPE_DIM], kv[..., QK_NOPE_DIM:]
    k_pe = jnp.broadcast_to(k_pe[:, :, None, :], (B, L, MLA_HEADS, QK_ROPE_DIM))
    k = _rms_norm(jnp.concatenate([k_nope, k_pe], axis=-1), k_norm)
    if rope is not None:
        k = jnp.concatenate([k[..., :QK_NOPE_DIM], _rope_2d(k[..., QK_NOPE_DIM:], rope)], axis=-1)
    return k, v


def _attend_blocks(q, k, v):
    B, S, H, Dq = q.shape
    nb = S // Q_BLOCK
    qb = q.reshape(B, nb, Q_BLOCK, H, Dq).transpose(1, 0, 2, 3, 4)
    scale = QK_HEAD_DIM ** -0.5

    def one(qblk):
        s = jnp.einsum('bqhd,bkhd->bhqk', qblk, k, preferred_element_type=jnp.float32) * scale
        p = jax.nn.softmax(s, axis=-1).astype(v.dtype)
        return jnp.einsum('bhqk,bkhv->bqhv', p, v)

    o = lax.map(one, qb)
    return o.transpose(1, 0, 2, 3, 4).reshape(B, S, H * V_HEAD_DIM)


def _conv_ffn(h, w_up, conv_w, conv_b, w_down):
    gate, val = jnp.split(h @ w_up, 2, axis=-1)
    return (jax.nn.silu(_dwconv3(gate, conv_w, conv_b)) * val) @ w_down


def setup_inputs(seed: int = 0) -> dict:
    key = jax.random.key(seed)
    ks = iter(jax.random.split(key, 40))

    def nrm(shape, scale):
        return scale * jax.random.normal(next(ks), shape, jnp.float32)

    def gain(shape):
        return 1.0 + 0.02 * jax.random.normal(next(ks), shape, jnp.float32)

    D = D_MODEL
    return {
        'x': nrm((BATCH, SEQ, D), 1.0),
        'c': nrm((BATCH, D), 1.0),
        'ctx': nrm((BATCH, CTX_LEN, D), 1.0),
        'c_ctx': nrm((D,), 1.0),
        'norm1': gain((DEPTH, D)),
        'norm2': gain((DEPTH, D)),
        'w_mod': nrm((DEPTH, D, 6 * D), 0.5 * D ** -0.5),
        'b_mod': nrm((DEPTH, 6 * D), 0.01),
        'ffn_w_up': nrm((DEPTH, D, 2 * D_FF), D ** -0.5),
        'ffn_conv_w': nrm((DEPTH, 3, D_FF), 3 ** -0.5),
        'ffn_conv_b': nrm((DEPTH, D_FF), 0.01),
        'ffn_w_down': nrm((DEPTH, D_FF, D), D_FF ** -0.5),
        'fh_w_in': nrm((N_EVEN, D, D_FOURIER + 3 * D_HYENA), D ** -0.5),
        'fh_w_out': nrm((N_EVEN, D_FOURIER + D_HYENA, D), (D_FOURIER + D_HYENA) ** -0.5),
        'hy_conv_w': nrm((N_EVEN, 3, 3 * D_HYENA), 3 ** -0.5),
        'hy_conv_b': nrm((N_EVEN, 3 * D_HYENA), 0.01),
        'hy_filt_w1': nrm((N_EVEN, HYENA_EMB_DIM, HYENA_FILTER_WIDTH), HYENA_EMB_DIM ** -0.5),
        'hy_filt_b1': nrm((N_EVEN, HYENA_FILTER_WIDTH), 0.1),
        'hy_filt_w2': nrm((N_EVEN, HYENA_FILTER_WIDTH, HYENA_FILTER_WIDTH), HYENA_FILTER_WIDTH ** -0.5),
        'hy_filt_b2': nrm((N_EVEN, HYENA_FILTER_WIDTH), 0.1),
        'hy_filt_w3': nrm((N_EVEN, HYENA_FILTER_WIDTH, HYENA_FILTER_WIDTH), HYENA_FILTER_WIDTH ** -0.5),
        'hy_filt_b3': nrm((N_EVEN, HYENA_FILTER_WIDTH), 0.1),
        'hy_filt_w4': nrm((N_EVEN, HYENA_FILTER_WIDTH, 2 * D_HYENA), HYENA_FILTER_WIDTH ** -0.5),
        'hy_freq': gain((N_EVEN, HYENA_FILTER_WIDTH)),
        'hy_bias': nrm((N_EVEN, D_HYENA), 0.5),
        'mla_w_in': nrm((N_ODD, D, Q_LORA_RANK + KV_LORA_RANK + QK_ROPE_DIM), D ** -0.5),
        'mla_q_a_norm': gain((N_ODD, Q_LORA_RANK)),
        'mla_w_uq': nrm((N_ODD, Q_LORA_RANK, MLA_HEADS * QK_HEAD_DIM), Q_LORA_RANK ** -0.5),
        'mla_kv_a_norm': gain((N_ODD, KV_LORA_RANK)),
        'mla_w_ukv': nrm((N_ODD, KV_LORA_RANK, MLA_HEADS * (QK_NOPE_DIM + V_HEAD_DIM)), KV_LORA_RANK ** -0.5),
        'mla_q_norm': gain((N_ODD, QK_HEAD_DIM)),
        'mla_k_norm': gain((N_ODD, QK_HEAD_DIM)),
        'mla_w_o': nrm((N_ODD, MLA_HEADS * V_HEAD_DIM, D), (MLA_HEADS * V_HEAD_DIM) ** -0.5),
    }


def reference(x, c, ctx, c_ctx, norm1, norm2, w_mod, b_mod, ffn_w_up, ffn_conv_w, ffn_conv_b, ffn_w_down,
              fh_w_in, fh_w_out, hy_conv_w, hy_conv_b, hy_filt_w1, hy_filt_b1, hy_filt_w2, hy_filt_b2,
              hy_filt_w3, hy_filt_b3, hy_filt_w4, hy_freq, hy_bias, mla_w_in, mla_q_a_norm, mla_w_uq,
              mla_kv_a_norm, mla_w_ukv, mla_q_norm, mla_k_norm, mla_w_o):
    L = x.shape[1]
    ROWS = L // GRID_W
    rows = jnp.broadcast_to(jnp.arange(ROWS, dtype=jnp.float32)[:, None], (ROWS, GRID_W)).reshape(-1)
    cols = jnp.broadcast_to(jnp.arange(GRID_W, dtype=jnp.float32)[None, :], (ROWS, GRID_W)).reshape(-1)
    n_freq = QK_ROPE_DIM // 4
    inv_freq = ROPE_THETA ** (-jnp.arange(n_freq, dtype=jnp.float32) / n_freq)
    ang_r = rows[:, None] * inv_freq[None, :]
    ang_c = cols[:, None] * inv_freq[None, :]
    rope = (jnp.cos(ang_r), jnp.sin(ang_r), jnp.cos(ang_c), jnp.sin(ang_c))

    for i in range(DEPTH):
        last = i == DEPTH - 1
        odd = i % 2 == 1
        j = i // 2
        sh1, sc1, g1, sh2, sc2, g2 = _adaln(c, w_mod[i], b_mod[i])
        hx = _modulate(_rms_norm(x, norm1[i]), sh1, sc1)
        if (not last) or odd:
            csh1, csc1, cg1, csh2, csc2, cg2 = _adaln(c_ctx[None, :], w_mod[i], b_mod[i])
            hc = _modulate(_rms_norm(ctx, norm1[i]), csh1, csc1)
        if not odd:
            fh = (fh_w_in[j], fh_w_out[j], hy_conv_w[j], hy_conv_b[j], hy_filt_w1[j], hy_filt_b1[j],
                  hy_filt_w2[j], hy_filt_b2[j], hy_filt_w3[j], hy_filt_b3[j], hy_filt_w4[j], hy_freq[j], hy_bias[j])
            yx = _fourier_hyena_mixer(hx, *fh)
            if not last:
                yc = _fourier_hyena_mixer(hc, *fh)
        else:
            w_in = mla_w_in[j]
            q_args = (mla_q_a_norm[j], mla_w_uq[j], mla_q_norm[j])
            kv_args = (mla_kv_a_norm[j], mla_w_ukv[j], mla_k_norm[j])
            a_x = hx @ w_in
            q_x = _mla_queries(a_x[..., :Q_LORA_RANK], *q_args, rope)
            k_x, v_x = _mla_keys_values(a_x[..., Q_LORA_RANK:], *kv_args, rope)
            a_c = hc @ (w_in if not last else w_in[:, Q_LORA_RANK:])
            k_c, v_c = _mla_keys_values(a_c[..., -(KV_LORA_RANK + QK_ROPE_DIM):], *kv_args, None)
            k_all = jnp.concatenate([k_x, k_c], axis=1)
            v_all = jnp.concatenate([v_x, v_c], axis=1)
            yx = _attend_blocks(q_x, k_all, v_all) @ mla_w_o[j]
            if not last:
                q_c = _mla_queries(a_c[..., :Q_LORA_RANK], *q_args, None)
                yc = _attend_blocks(q_c, k_c, v_c) @ mla_w_o[j]
        ffn = (ffn_w_up[i], ffn_conv_w[i], ffn_conv_b[i], ffn_w_down[i])
        x = x + g1 * yx
        x = x + g2 * _conv_ffn(_modulate(_rms_norm(x, norm2[i]), sh2, sc2), *ffn)
        if not last:
            ctx = ctx + cg1 * yc
            ctx = ctx + cg2 * _conv_ffn(_modulate(_rms_norm(ctx, norm2[i]), csh2, csc2), *ffn)
    return x
```

```python
import functools
import math

import jax
import jax.numpy as jnp
from jax import lax
from jax.experimental import pallas as pl
from jax.experimental.pallas import tpu as pltpu

F32 = jnp.float32
BF16 = jnp.bfloat16
HIGHEST = lax.Precision.HIGHEST

EPS = 1e-6
D_MODEL = 1024
D_FF = 2816
D_FOURIER = 512
FOURIER_GROUP_DIM = 128
D_HYENA = 512
HYENA_EMB_DIM = 33
HYENA_EMB_PAD = 64
HYENA_BANDS = 16
HYENA_WIDTH = 64
HYENA_MIN_DECAY = math.log(1e-2) / 0.3
HYENA_MAX_DECAY = math.log(1e-2) / 1.5
MLA_HEADS = 16
Q_LORA = 256
KV_LORA = 128
NOPE = 64
ROPE = 32
QK_DIM = NOPE + ROPE
V_DIM = 64
HEAD_PAD = 128
MLA_IN_PAD = 512
GRID_W = 64
ROPE_THETA = 10000.0

HALO = 16
VMEM_LIMIT = 56 * 1024 * 1024


def _cparams(sem):
    return pltpu.CompilerParams(dimension_semantics=sem, vmem_limit_bytes=VMEM_LIMIT)


def _dot(a, b):
    return jnp.dot(a, b, preferred_element_type=F32)


def _norm_mod(x, g, shift, scale):
    ms = jnp.mean(x * x, axis=-1, keepdims=True)
    return (x * lax.rsqrt(ms + EPS) * g) * (1.0 + scale) + shift


def _silu(x):
    return x * (1.0 / (1.0 + jnp.exp(-x)))


def _row_tile(L, want):
    t = min(L, want)
    assert L % t == 0 and t % HALO == 0
    return t


def _mod_kernel(c_ref, w_ref, b_ref, o_ref):
    s = _silu(c_ref[...])
    o_ref[...] = jnp.dot(s, w_ref[...], precision=HIGHEST, preferred_element_type=F32) + b_ref[...]


def _adaln(cond, w_mod, b_mod):
    R, D = cond.shape
    n = w_mod.shape[1]
    tn = 768
    return pl.pallas_call(
        _mod_kernel,
        out_shape=jax.ShapeDtypeStruct((R, n), F32),
        grid=(n // tn,),
        in_specs=[pl.BlockSpec((R, D), lambda j: (0, 0)),
                  pl.BlockSpec((D, tn), lambda j: (0, j)),
                  pl.BlockSpec((1, tn), lambda j: (0, j))],
        out_specs=pl.BlockSpec((R, tn), lambda j: (0, j)),
        compiler_params=_cparams(("arbitrary",)),
        name="adaln_mod",
    )(cond, w_mod, b_mod.reshape(1, n))


def _halo_specs(tm, D, L):
    nb = L // HALO
    per = tm // HALO
    cur = pl.BlockSpec((1, tm, D), lambda b, i, *_: (b, i, 0))
    prev = pl.BlockSpec((1, HALO, D), lambda b, i, *_: (b, jnp.maximum(i * per - 1, 0), 0))
    nxt = pl.BlockSpec((1, HALO, D), lambda b, i, *_: (b, jnp.minimum((i + 1) * per, nb - 1), 0))
    return cur, prev, nxt


def _mod_spec(arr):
    D = arr.shape[-1]
    if arr.shape[0] == 1:
        return pl.BlockSpec((1, 1, D), lambda b, *_: (0, 0, 0))
    return pl.BlockSpec((1, 1, D), lambda b, *_: (b, 0, 0))


def _fill_hn(hn_ref, xc_ref, xp_ref, xn_ref, g, shift, scale, tm, i, nt):
    hp = _norm_mod(xp_ref[0], g, shift, scale)
    hn_ref[0:HALO, :] = jnp.where(i > 0, hp, 0.0).astype(BF16)
    hn_ref[HALO:HALO + tm, :] = _norm_mod(xc_ref[0], g, shift, scale).astype(BF16)
    hx = _norm_mod(xn_ref[0], g, shift, scale)
    hn_ref[HALO + tm:, :] = jnp.where(i < nt - 1, hx, 0.0).astype(BF16)


def _dwconv3_rows(p, w, tm):
    n = tm + 2 * HALO
    up = pltpu.roll(p, 1, 0)
    dn = pltpu.roll(p, n - 1, 0)
    c = up * w[0:1] + p * w[1:2] + dn * w[2:3]
    return c[HALO:HALO + tm]


def _fh_in_kernel(xc_ref, xp_ref, xn_ref, g_ref, sh_ref, sc_ref, w_ref, cw_ref, cb_ref, gm_ref,
                  z_ref, u_ref, x0_ref, hn_ref, *, tm):
    i = pl.program_id(1)
    nt = pl.num_programs(1)
    _fill_hn(hn_ref, xc_ref, xp_ref, xn_ref, g_ref[...], sh_ref[0], sc_ref[0], tm, i, nt)
    proj = _dot(hn_ref[...], w_ref[...])
    uf = proj[HALO:HALO + tm, :D_FOURIER].astype(BF16)
    z_ref[0] = _dot(uf, gm_ref[...]).astype(BF16)
    c = _dwconv3_rows(proj[:, D_FOURIER:], cw_ref[...], tm) + cb_ref[...]
    x0 = c[:, :D_HYENA]
    x1 = c[:, D_HYENA:2 * D_HYENA]
    v = c[:, 2 * D_HYENA:]
    x0_ref[0] = x0
    u_ref[0] = (v * x1).astype(BF16)


def _fh_in(x, g, shift, scale, w_in, conv_w, conv_b, gmat):
    B, L, D = x.shape
    tm = _row_tile(L, 512)
    cur, prev, nxt = _halo_specs(tm, D, L)
    n_in = w_in.shape[1]
    nh = 3 * D_HYENA
    full = lambda shape: pl.BlockSpec(shape, lambda b, i: (0,) * len(shape))
    out_tile = lambda c: pl.BlockSpec((1, tm, c), lambda b, i: (b, i, 0))
    return pl.pallas_call(
        functools.partial(_fh_in_kernel, tm=tm),
        out_shape=(jax.ShapeDtypeStruct((B, L, 2 * D_FOURIER), BF16),
                   jax.ShapeDtypeStruct((B, L, D_HYENA), BF16),
                   jax.ShapeDtypeStruct((B, L, D_HYENA), F32)),
        grid=(B, L // tm),
        in_specs=[cur, prev, nxt, full((1, D)), _mod_spec(shift), _mod_spec(scale),
                  full((D, n_in)), full((3, nh)), full((1, nh)), full((D_FOURIER, 2 * D_FOURIER))],
        out_specs=(out_tile(2 * D_FOURIER), out_tile(D_HYENA), out_tile(D_HYENA)),
        scratch_shapes=[pltpu.VMEM((tm + 2 * HALO, D), BF16)],
        compiler_params=_cparams(("parallel", "arbitrary")),
        name="fh_in",
    )(x, x, x, g.reshape(1, D), shift, scale, w_in, conv_w, conv_b.reshape(1, nh), gmat)


def _fourier_kernel(c_ref, s_ref, z_ref, o_ref, *, scale):
    z = z_ref[0]
    y = _dot(c_ref[...], z[:, :D_FOURIER]) + _dot(s_ref[...], z[:, D_FOURIER:])
    o_ref[0] = (y * scale).astype(BF16)


def _fourier_seq(z, cl, sl):
    B, L, _ = z.shape
    tf = _row_tile(L, 512)
    scale = 1.0 / math.sqrt(L * FOURIER_GROUP_DIM)
    return pl.pallas_call(
        functools.partial(_fourier_kernel, scale=scale),
        out_shape=jax.ShapeDtypeStruct((B, L, D_FOURIER), BF16),
        grid=(B, L // tf),
        in_specs=[pl.BlockSpec((tf, L), lambda b, i: (i, 0)),
                  pl.BlockSpec((tf, L), lambda b, i: (i, 0)),
                  pl.BlockSpec((1, L, 2 * D_FOURIER), lambda b, i: (b, 0, 0))],
        out_specs=pl.BlockSpec((1, tf, D_FOURIER), lambda b, i: (b, i, 0)),
        compiler_params=_cparams(("parallel", "arbitrary")),
        name="fourier_seq",
    )(cl, sl, z)


def _filter_kernel(z_ref, w1_ref, b1_ref, w2_ref, b2_ref, w3_ref, b3_ref, w4_ref, f_ref, dl_ref,
                   k_ref, n_ref):
    i = pl.program_id(1)
    z = z_ref[0]
    f = f_ref[...]
    hd = lambda a, w: jnp.dot(a, w, precision=HIGHEST, preferred_element_type=F32)
    h = jnp.sin(f * (hd(z, w1_ref[...]) + b1_ref[...]))
    h = jnp.sin(f * (hd(h, w2_ref[...]) + b2_ref[...]))
    h = jnp.sin(f * (hd(h, w3_ref[...]) + b3_ref[...]))
    h = hd(h, w4_ref[0])
    t = z[:, 0:1]
    valid = z[:, HYENA_EMB_DIM:HYENA_EMB_DIM + 1]
    k = h * jnp.exp(-t * jnp.abs(dl_ref[...])) * valid
    k_ref[0] = k

    @pl.when(i == 0)
    def _():
        n_ref[...] = jnp.zeros_like(n_ref)

    n_ref[0] += jnp.sum(jnp.abs(k), axis=0, keepdims=True)


def _hyena_filter(L, w1, b1, w2, b2, w3, b3, w4, freq):
    pos = jnp.arange(L, dtype=F32)
    bands = jnp.linspace(1e-4, HYENA_BANDS - 1, HYENA_BANDS, dtype=F32)

    def emb(p, valid):
        t = p / max(L - 1, 1)
        ang = (2.0 * math.pi / L) * p[:, None] * bands[None, :]
        pad = jnp.zeros((L, HYENA_EMB_PAD - HYENA_EMB_DIM - 1), F32)
        return jnp.concatenate([t[:, None], jnp.cos(ang), -jnp.sin(ang), valid[:, None], pad], axis=-1)

    zf = emb(pos, jnp.ones((L,), F32))
    zb = emb(jnp.where(pos > 0, L - pos, 0.0), (pos > 0).astype(F32))
    z = jnp.stack([zf, zb])
    w1p = jnp.concatenate([w1, jnp.zeros((HYENA_EMB_PAD - HYENA_EMB_DIM, HYENA_WIDTH), F32)], axis=0)
    w4s = jnp.stack([w4[:, :D_HYENA], w4[:, D_HYENA:]])
    deltas = jnp.linspace(HYENA_MIN_DECAY, HYENA_MAX_DECAY, D_HYENA, dtype=F32).reshape(1, D_HYENA)
    tr = _row_tile(L, 512)
    W = HYENA_WIDTH
    full = lambda shape: pl.BlockSpec(shape, lambda s, i: (0,) * len(shape))
    return pl.pallas_call(
        _filter_kernel,
        out_shape=(jax.ShapeDtypeStruct((2, L, D_HYENA), F32), jax.ShapeDtypeStruct((2, 1, D_HYENA), F32)),
        grid=(2, L // tr),
        in_specs=[pl.BlockSpec((1, tr, HYENA_EMB_PAD), lambda s, i: (s, i, 0)),
                  full((HYENA_EMB_PAD, W)), full((1, W)), full((W, W)), full((1, W)),
                  full((W, W)), full((1, W)),
                  pl.BlockSpec((1, W, D_HYENA), lambda s, i: (s, 0, 0)),
                  full((1, W)), full((1, D_HYENA))],
        out_specs=(pl.BlockSpec((1, tr, D_HYENA), lambda s, i: (s, i, 0)),
                   pl.BlockSpec((1, 1, D_HYENA), lambda s, i: (s, 0, 0))),
        compiler_params=_cparams(("arbitrary", "arbitrary")),
        name="hyena_filter",
    )(z, w1p, b1.reshape(1, W), w2, b2.reshape(1, W), w3, b3.reshape(1, W), w4s, freq.reshape(1, W), deltas)


def _hy_fwd_kernel(c_ref, s_ref, u_ref, *rest, raw, tf):
    u = u_ref[0]
    ure = _dot(c_ref[...], u)
    uim = -_dot(s_ref[...], u)
    if raw:
        re_ref, im_ref = rest
        re_ref[0] = ure
        im_ref[0] = uim
        return
    kre_ref, kim_ref, re_ref, im_ref = rest
    kre = kre_ref[...]
    kim = kim_ref[...]
    row = lax.broadcasted_iota(jnp.int32, (tf, 1), 0) + pl.program_id(1) * tf
    packed = row == 0
    a = uim * kim
    re_ref[0] = (ure * kre - jnp.where(packed, 0.0, a)).astype(BF16)
    im_ref[0] = jnp.where(packed, a, ure * kim + uim * kre).astype(BF16)


def _hy_fwd(u, cb, sbf, kre=None, kim=None):
    B, L, C = u.shape
    tf = _row_tile(L, 512)
    raw = kre is None
    mat = pl.BlockSpec((tf, L), lambda b, i: (i, 0))
    tile = pl.BlockSpec((1, tf, C), lambda b, i: (b, i, 0))
    in_specs = [mat, mat, pl.BlockSpec((1, L, C), lambda b, i: (b, 0, 0))]
    args = [cb, sbf, u]
    if not raw:
        in_specs += [pl.BlockSpec((tf, C), lambda b, i: (i, 0))] * 2
        args += [kre, kim]
    dt = F32 if raw else BF16
    return pl.pallas_call(
        functools.partial(_hy_fwd_kernel, raw=raw, tf=tf),
        out_shape=(jax.ShapeDtypeStruct((B, L, C), dt), jax.ShapeDtypeStruct((B, L, C), dt)),
        grid=(B, L // tf),
        in_specs=in_specs,
        out_specs=(tile, tile),
        compiler_params=_cparams(("parallel", "arbitrary")),
        name="hyena_fwd_raw" if raw else "hyena_fwd",
    )(*args)


def _spec_combine_kernel(re_ref, im_ref, n_ref, kre_ref, kim_ref, *, tf, n_fft):
    row = lax.broadcasted_iota(jnp.int32, (tf, 1), 0) + pl.program_id(0) * tf
    sgn = (1 - 2 * (row & 1)).astype(F32)
    wgt = jnp.where(row == 0, 1.0 / n_fft, 2.0 / n_fft)
    s = wgt / (n_ref[0] + n_ref[1])
    kre_ref[...] = (re_ref[0] + sgn * re_ref[1]) * s
    kim_ref[...] = (im_ref[0] + sgn * im_ref[1]) * s


def _spec_combine(re, im, norms):
    _, L, C = re.shape
    tf = _row_tile(L, 512)
    pair = pl.BlockSpec((2, tf, C), lambda i: (0, i, 0))
    tile = pl.BlockSpec((tf, C), lambda i: (i, 0))
    return pl.pallas_call(
        functools.partial(_spec_combine_kernel, tf=tf, n_fft=2 * L),
        out_shape=(jax.ShapeDtypeStruct((L, C), F32), jax.ShapeDtypeStruct((L, C), F32)),
        grid=(L // tf,),
        in_specs=[pair, pair, pl.BlockSpec((2, 1, C), lambda i: (0, 0, 0))],
        out_specs=(tile, tile),
        compiler_params=_cparams(("arbitrary",)),
        name="hyena_spec_combine",
    )(re, im, norms)


def _hy_inv_kernel(c_ref, s_ref, re_ref, im_ref, x0_ref, u_ref, b_ref, o_ref):
    y = _dot(c_ref[...], re_ref[0]) - _dot(s_ref[...], im_ref[0])
    o_ref[0] = (x0_ref[0] * (y + u_ref[0].astype(F32) * b_ref[...])).astype(BF16)


def _hy_inv(yre, yim, x0, u, bias, cb, sbi):
    B, L, C = u.shape
    tt = _row_tile(L, 512)
    mat = pl.BlockSpec((tt, L), lambda b, i: (i, 0))
    whole = pl.BlockSpec((1, L, C), lambda b, i: (b, 0, 0))
    tile = pl.BlockSpec((1, tt, C), lambda b, i: (b, i, 0))
    return pl.pallas_call(
        _hy_inv_kernel,
        out_shape=jax.ShapeDtypeStruct((B, L, C), BF16),
        grid=(B, L // tt),
        in_specs=[mat, mat, whole, whole, tile, tile, pl.BlockSpec((1, C), lambda b, i: (0, 0))],
        out_specs=tile,
        compiler_params=_cparams(("parallel", "arbitrary")),
        name="hyena_inv",
    )(cb, sbi, yre, yim, x0, u, bias.reshape(1, C))


def _proj_res_kernel(x_ref, g_ref, w_ref, *rest):
    *y_refs, o_ref = rest
    acc = None
    off = 0
    for y_ref in y_refs:
        c = y_ref.shape[-1]
        t = _dot(y_ref[0], w_ref[off:off + c, :])
        acc = t if acc is None else acc + t
        off += c
    o_ref[0] = x_ref[0] + g_ref[0] * acc


def _proj_residual(x, gate, w, ys):
    B, L, D = x.shape
    tm = _row_tile(L, 512)
    tile = lambda c: pl.BlockSpec((1, tm, c), lambda b, i: (b, i, 0))
    return pl.pallas_call(
        _proj_res_kernel,
        out_shape=jax.ShapeDtypeStruct((B, L, D), F32),
        grid=(B, L // tm),
        in_specs=[tile(D), _mod_spec(gate), pl.BlockSpec(w.shape, lambda b, i: (0, 0))]
                 + [tile(y.shape[-1]) for y in ys],
        out_specs=tile(D),
        compiler_params=_cparams(("parallel", "arbitrary")),
        name="proj_residual",
    )(x, gate, w, *ys)


def _ffn_kernel(xc_ref, xp_ref, xn_ref, g_ref, sh_ref, sc_ref, gate_ref, wg_ref, wv_ref, cw_ref, cb_ref,
                wd_ref, o_ref, hn_ref, acc_ref, *, tm):
    i = pl.program_id(1)
    nt = pl.num_programs(1)
    k = pl.program_id(2)
    nk = pl.num_programs(2)

    @pl.when(k == 0)
    def _():
        _fill_hn(hn_ref, xc_ref, xp_ref, xn_ref, g_ref[...], sh_ref[0], sc_ref[0], tm, i, nt)
        acc_ref[...] = jnp.zeros_like(acc_ref)

    gx = _dot(hn_ref[...], wg_ref[...])
    vx = _dot(hn_ref[HALO:HALO + tm, :], wv_ref[...])
    cv = _dwconv3_rows(gx, cw_ref[...], tm) + cb_ref[...]
    act = (_silu(cv) * vx).astype(BF16)
    acc_ref[...] += _dot(act, wd_ref[...])

    @pl.when(k == nk - 1)
    def _():
        o_ref[0] = xc_ref[0] + gate_ref[0] * acc_ref[...]


def _conv_ffn(x, g, shift, scale, gate, w_up, conv_w, conv_b, w_down):
    B, L, D = x.shape
    tm = _row_tile(L, 1024)
    fc = 256
    nk = D_FF // fc
    cur, prev, nxt = _halo_specs(tm, D, L)
    return pl.pallas_call(
        functools.partial(_ffn_kernel, tm=tm),
        out_shape=jax.ShapeDtypeStruct((B, L, D), F32),
        grid=(B, L // tm, nk),
        in_specs=[cur, prev, nxt, pl.BlockSpec((1, D), lambda b, i, k: (0, 0)),
                  _mod_spec(shift), _mod_spec(scale), _mod_spec(gate),
                  pl.BlockSpec((D, fc), lambda b, i, k: (0, k)),
                  pl.BlockSpec((D, fc), lambda b, i, k: (0, k + nk)),
                  pl.BlockSpec((3, fc), lambda b, i, k: (0, k)),
                  pl.BlockSpec((1, fc), lambda b, i, k: (0, k)),
                  pl.BlockSpec((fc, D), lambda b, i, k: (k, 0))],
        out_specs=pl.BlockSpec((1, tm, D), lambda b, i, k: (b, i, 0)),
        scratch_shapes=[pltpu.VMEM((tm + 2 * HALO, D), BF16), pltpu.VMEM((tm, D), F32)],
        compiler_params=_cparams(("parallel", "arbitrary", "arbitrary")),
        name="conv_ffn",
    )(x, x, x, g.reshape(1, D), shift, scale, gate, w_up, w_up, conv_w, conv_b.reshape(1, D_FF), w_down)


def _rms(v, g, n):
    return v * lax.rsqrt(jnp.sum(v * v, axis=-1, keepdims=True) * (1.0 / n) + EPS) * g


def _rope(xh, cos, sa, sb):
    return xh * cos + pltpu.roll(xh, 8, 1) * sa + pltpu.roll(xh, HEAD_PAD - 8, 1) * sb


def _mla_proj_kernel(x_ref, g_ref, sh_ref, sc_ref, win_ref, qan_ref, wuq_ref, qn_ref, kvn_ref, wuk_ref,
                     wuv_ref, kn_ref, cos_ref, sa_ref, sb_ref, *outs, with_q):
    hn = _norm_mod(x_ref[0], g_ref[...], sh_ref[0], sc_ref[0]).astype(BF16)
    a = _dot(hn, win_ref[...])
    cos = cos_ref[...]
    sa = sa_ref[...]
    sb = sb_ref[...]
    if with_q:
        q_ref, kt_ref, v_ref = outs
        qa = _rms(a[:, :Q_LORA], qan_ref[...], Q_LORA).astype(BF16)
        qf = _dot(qa, wuq_ref[...])
        qn = qn_ref[...]
        for h in range(MLA_HEADS):
            qh = _rms(qf[:, h * HEAD_PAD:(h + 1) * HEAD_PAD], qn, QK_DIM)
            qh = _rope(qh, cos, sa, sb) * (QK_DIM ** -0.5)
            q_ref[0, :, h * HEAD_PAD:(h + 1) * HEAD_PAD] = qh.astype(BF16)
    else:
        kt_ref, v_ref = outs
    ckv = _rms(a[:, Q_LORA:Q_LORA + KV_LORA], kvn_ref[...], KV_LORA).astype(BF16)
    kpe = pltpu.roll(a[:, Q_LORA + KV_LORA:], NOPE, 1)
    kf = _dot(ckv, wuk_ref[...])
    v_ref[0] = _dot(ckv, wuv_ref[...]).astype(BF16)
    kn = kn_ref[...]
    for h in range(MLA_HEADS):
        kh = _rms(kf[:, h * HEAD_PAD:(h + 1) * HEAD_PAD] + kpe, kn, QK_DIM)
        kh = _rope(kh, cos, sa, sb)
        kt_ref[0, h * HEAD_PAD:(h + 1) * HEAD_PAD, :] = kh.T.astype(BF16)


def _mla_proj(x, g, shift, scale, wts, rope_tabs, with_q):
    B, L, D = x.shape
    tm = _row_tile(L, 256)
    win, qan, wuq, qn, kvn, wuk, wuv, kn = wts
    cos, sa, sb = rope_tabs
    HP = MLA_HEADS * HEAD_PAD
    full = lambda a: pl.BlockSpec(a.shape, lambda b, i: (0,) * a.ndim)
    tab = pl.BlockSpec((tm, HEAD_PAD), lambda b, i: (i, 0))
    out_shape = [jax.ShapeDtypeStruct((B, HP, L), BF16), jax.ShapeDtypeStruct((B, L, MLA_HEADS * V_DIM), BF16)]
    out_specs = [pl.BlockSpec((1, HP, tm), lambda b, i: (b, 0, i)),
                 pl.BlockSpec((1, tm, MLA_HEADS * V_DIM), lambda b, i: (b, i, 0))]
    if with_q:
        out_shape = [jax.ShapeDtypeStruct((B, L, HP), BF16)] + out_shape
        out_specs = [pl.BlockSpec((1, tm, HP), lambda b, i: (b, i, 0))] + out_specs
    g2 = g.reshape(1, D)
    return pl.pallas_call(
        functools.partial(_mla_proj_kernel, with_q=with_q),
        out_shape=tuple(out_shape),
        grid=(B, L // tm),
        in_specs=[pl.BlockSpec((1, tm, D), lambda b, i: (b, i, 0)), full(g2), _mod_spec(shift), _mod_spec(scale),
                  full(win), full(qan), full(wuq), full(qn), full(kvn), full(wuk), full(wuv), full(kn),
                  tab, tab, tab],
        out_specs=tuple(out_specs),
        compiler_params=_cparams(("parallel", "arbitrary")),
        name="mla_proj_q" if with_q else "mla_proj_kv",
    )(x, g2, shift, scale, win, qan, wuq, qn, kvn, wuk, wuv, kn, cos, sa, sb)


def _attn_kernel(q_ref, ktx_ref, ktc_ref, vx_ref, vc_ref, o_ref):
    vx = vx_ref[0]
    vc = vc_ref[0]
    lane = lax.broadcasted_iota(jnp.int32, (1, 2 * V_DIM), 1)
    out = None
    for j in range(2):
        qh = q_ref[0, :, j * HEAD_PAD:(j + 1) * HEAD_PAD]
        sx = _dot(qh, ktx_ref[0, j * HEAD_PAD:(j + 1) * HEAD_PAD, :])
        sc = _dot(qh, ktc_ref[0, j * HEAD_PAD:(j + 1) * HEAD_PAD, :])
        m = jnp.maximum(jnp.max(sx, axis=-1, keepdims=True), jnp.max(sc, axis=-1, keepdims=True))
        px = jnp.exp(sx - m)
        pc = jnp.exp(sc - m)
        den = jnp.sum(px, axis=-1, keepdims=True) + jnp.sum(pc, axis=-1, keepdims=True)
        own = jnp.where(lane // V_DIM == j, 1.0, 0.0).astype(BF16)
        oj = (_dot(px.astype(BF16), vx * own) + _dot(pc.astype(BF16), vc * own)) * (1.0 / den)
        out = oj if out is None else out + oj
    o_ref[0] = out.astype(BF16)


def _attention(q, ktx, ktc, vx, vc):
    B, L, HP = q.shape
    Lc = vc.shape[1]
    tq = _row_tile(L, 256)
    nhp = MLA_HEADS // 2
    return pl.pallas_call(
        _attn_kernel,
        out_shape=jax.ShapeDtypeStruct((B, L, MLA_HEADS * V_DIM), BF16),
        grid=(B, nhp, L // tq),
        in_specs=[pl.BlockSpec((1, tq, 2 * HEAD_PAD), lambda b, h, i: (b, i, h)),
                  pl.BlockSpec((1, 2 * HEAD_PAD, L), lambda b, h, i: (b, h, 0)),
                  pl.BlockSpec((1, 2 * HEAD_PAD, Lc), lambda b, h, i: (b, h, 0)),
                  pl.BlockSpec((1, L, 2 * V_DIM), lambda b, h, i: (b, 0, h)),
                  pl.BlockSpec((1, Lc, 2 * V_DIM), lambda b, h, i: (b, 0, h))],
        out_specs=pl.BlockSpec((1, tq, 2 * V_DIM), lambda b, h, i: (b, i, h)),
        compiler_params=_cparams(("parallel", "parallel", "arbitrary")),
        name="mla_attention",
    )(q, ktx, ktc, vx, vc)


def _trig_matrix(n, period):
    blk = 64
    r = jnp.arange(n, dtype=jnp.int32)[:, None]
    c_hi = jnp.arange(n // blk, dtype=jnp.int32)[None, :] * blk
    c_lo = jnp.arange(blk, dtype=jnp.int32)[None, :]
    w = 2.0 * math.pi / period
    a = ((r * c_hi) % period).astype(F32) * w
    b = ((r * c_lo) % period).astype(F32) * w
    ca, sa, cb, sb = jnp.cos(a), jnp.sin(a), jnp.cos(b), jnp.sin(b)
    cos = (ca[:, :, None] * cb[:, None, :] - sa[:, :, None] * sb[:, None, :]).reshape(n, n)
    sin = (sa[:, :, None] * cb[:, None, :] + ca[:, :, None] * sb[:, None, :]).reshape(n, n)
    return cos, sin


def _dft_tables(L):
    cl, sl = _trig_matrix(L, L)
    cb, sb = _trig_matrix(L, 2 * L)
    alt = (1 - 2 * (jnp.arange(L, dtype=jnp.int32) & 1)).astype(F32)
    idx = jnp.arange(L, dtype=jnp.int32)
    sbf = jnp.where(idx[:, None] == 0, -alt[None, :], sb)
    sbi = jnp.where(idx[None, :] == 0, -alt[:, None], sb)
    return tuple(m.astype(BF16) for m in (cl, sl, cb, sbf, sbi))


def _group_dft_matrix():
    g = FOURIER_GROUP_DIM
    ng = D_FOURIER // g
    c, s = _trig_matrix(g, g)
    eye = jnp.eye(ng, dtype=F32)
    return jnp.concatenate([jnp.kron(eye, c), -jnp.kron(eye, s)], axis=1).astype(BF16)


def _rope_tables(L):
    pos = jnp.arange(L, dtype=jnp.int32)
    rows = (pos // GRID_W).astype(F32)
    cols = (pos % GRID_W).astype(F32)
    nf = ROPE // 4
    inv_freq = ROPE_THETA ** (-jnp.arange(nf, dtype=F32) / nf)
    ar = rows[:, None] * inv_freq[None, :]
    ac = cols[:, None] * inv_freq[None, :]
    one = jnp.ones((L, NOPE), F32)
    z_n = jnp.zeros((L, NOPE), F32)
    z_f = jnp.zeros((L, nf), F32)
    tail = jnp.zeros((L, HEAD_PAD - QK_DIM), F32)
    cos = jnp.concatenate([one, jnp.cos(ar), jnp.cos(ar), jnp.cos(ac), jnp.cos(ac), tail], axis=1)
    sa = jnp.concatenate([z_n, z_f, jnp.sin(ar), z_f, jnp.sin(ac), tail], axis=1)
    sb = jnp.concatenate([z_n, -jnp.sin(ar), z_f, -jnp.sin(ac), z_f, tail], axis=1)
    return cos, sa, sb


def _identity_rope_tables(L):
    return jnp.ones((L, HEAD_PAD), F32), jnp.zeros((L, HEAD_PAD), F32), jnp.zeros((L, HEAD_PAD), F32)


def _pad_heads(w, per_head, keep):
    K = w.shape[0]
    w = w.reshape(K, MLA_HEADS, per_head)[:, :, :keep]
    w = jnp.pad(w, ((0, 0), (0, 0), (0, HEAD_PAD - keep)))
    return w.reshape(K, MLA_HEADS * HEAD_PAD)


def _pad_lanes(v, n):
    return jnp.pad(v, (0, n - v.shape[0])).reshape(1, n)


def _fourier_hyena_mixer(x, g, shift, scale, w_in, conv_w, conv_b, gmat, tabs, kre, kim, hy_bias):
    cl, sl, cb, sbf, sbi = tabs
    z, u, x0 = _fh_in(x, g, shift, scale, w_in, conv_w, conv_b, gmat)
    y_f = _fourier_seq(z, cl, sl)
    yre, yim = _hy_fwd(u, cb, sbf, kre, kim)
    y_h = _hy_inv(yre, yim, x0, u, hy_bias, cb, sbi)
    return y_f, y_h


def kernel(x, c, ctx, c_ctx, norm1, norm2, w_mod, b_mod, ffn_w_up, ffn_conv_w, ffn_conv_b, ffn_w_down,
           fh_w_in, fh_w_out, hy_conv_w, hy_conv_b, hy_filt_w1, hy_filt_b1, hy_filt_w2, hy_filt_b2,
           hy_filt_w3, hy_filt_b3, hy_filt_w4, hy_freq, hy_bias, mla_w_in, mla_q_a_norm, mla_w_uq,
           mla_kv_a_norm, mla_w_ukv, mla_q_norm, mla_k_norm, mla_w_o):
    B, L, D = x.shape
    Lc = ctx.shape[1]
    bf = lambda a: a.astype(BF16)

    n_cond = -(-(B + 1) // 8) * 8
    cond = jnp.concatenate([c, c_ctx[None, :], jnp.zeros((n_cond - B - 1, D), F32)], axis=0)

    def mods(i):
        m = _adaln(cond, w_mod[i], b_mod[i])
        mx = [m[:B, j * D:(j + 1) * D].reshape(B, 1, D) for j in range(6)]
        mc = [m[B:B + 1, j * D:(j + 1) * D].reshape(1, 1, D) for j in range(6)]
        return mx, mc

    (sh1, sc1, g1, sh2, sc2, g2), (csh1, csc1, cg1, csh2, csc2, cg2) = mods(0)
    gmat = _group_dft_matrix()
    w_in0 = bf(fh_w_in[0])
    w_out0 = bf(fh_w_out[0])
    ffn0 = (bf(ffn_w_up[0]), ffn_conv_w[0], ffn_conv_b[0], bf(ffn_w_down[0]))
    filt = (hy_filt_w1[0], hy_filt_b1[0], hy_filt_w2[0], hy_filt_b2[0], hy_filt_w3[0], hy_filt_b3[0],
            hy_filt_w4[0], hy_freq[0])

    def mixer_layer(h, n_seq, m1, m2):
        s1, c1, gt1 = m1
        s2, c2, gt2 = m2
        tabs = _dft_tables(n_seq)
        taps, norms = _hyena_filter(n_seq, *filt)
        k_re, k_im = _hy_fwd(bf(taps), tabs[2], tabs[3])
        kre, kim = _spec_combine(k_re, k_im, norms)
        y_f, y_h = _fourier_hyena_mixer(h, norm1[0], s1, c1, w_in0, hy_conv_w[0], hy_conv_b[0], gmat, tabs,
                                        kre, kim, hy_bias[0])
        h = _proj_residual(h, gt1, w_out0, [y_f, y_h])
        return _conv_ffn(h, norm2[0], s2, c2, gt2, *ffn0)

    x = mixer_layer(x, L, (sh1, sc1, g1), (sh2, sc2, g2))
    ctx = mixer_layer(ctx, Lc, (csh1, csc1, cg1), (csh2, csc2, cg2))

    (sh1, sc1, g1, sh2, sc2, g2), (csh1, csc1, _, _, _, _) = mods(1)
    w_in1 = bf(jnp.pad(mla_w_in[0], ((0, 0), (0, MLA_IN_PAD - mla_w_in.shape[2]))))
    wuq = bf(_pad_heads(mla_w_uq[0], QK_DIM, QK_DIM))
    wuk = bf(_pad_heads(mla_w_ukv[0], NOPE + V_DIM, NOPE))
    wuv = bf(mla_w_ukv[0].reshape(KV_LORA, MLA_HEADS, NOPE + V_DIM)[:, :, NOPE:].reshape(KV_LORA, MLA_HEADS * V_DIM))
    wts = (w_in1, mla_q_a_norm[0].reshape(1, Q_LORA), wuq, _pad_lanes(mla_q_norm[0], HEAD_PAD),
           mla_kv_a_norm[0].reshape(1, KV_LORA), wuk, wuv, _pad_lanes(mla_k_norm[0], HEAD_PAD))
    q, ktx, vx = _mla_proj(x, norm1[1], sh1, sc1, wts, _rope_tables(L), True)
    ktc, vc = _mla_proj(ctx, norm1[1], csh1, csc1, wts, _identity_rope_tables(Lc), False)
    o = _attention(q, ktx, ktc, vx, vc)
    x = _proj_residual(x, g1, bf(mla_w_o[0]), [o])
    ffn1 = (bf(ffn_w_up[1]), ffn_conv_w[1], ffn_conv_b[1], bf(ffn_w_down[1]))
    return _conv_ffn(x, norm2[1], sh2, sc2, g2, *ffn1)
```

```python
import functools
import math

import jax
import jax.numpy as jnp
from jax import lax
from jax.experimental import pallas as pl
from jax.experimental.pallas import tpu as pltpu

F32 = jnp.float32
BF16 = jnp.bfloat16
HIGHEST = lax.Precision.HIGHEST

EPS = 1e-6
D_MODEL = 1024
D_FF = 2816
D_FOURIER = 512
FOURIER_GROUP_DIM = 128
D_HYENA = 512
HYENA_EMB_DIM = 33
HYENA_EMB_PAD = 64
HYENA_BANDS = 16
HYENA_WIDTH = 64
HYENA_MIN_DECAY = math.log(1e-2) / 0.3
HYENA_MAX_DECAY = math.log(1e-2) / 1.5
MLA_HEADS = 16
Q_LORA = 256
KV_LORA = 128
NOPE = 64
ROPE = 32
QK_DIM = NOPE + ROPE
V_DIM = 64
HEAD_PAD = 128
MLA_IN_PAD = 512
GRID_W = 64
ROPE_THETA = 10000.0

FFN_CHUNK = 256
HALO = 16
VMEM_LIMIT = 56 * 1024 * 1024


def _cparams(sem):
    return pltpu.CompilerParams(dimension_semantics=sem, vmem_limit_bytes=VMEM_LIMIT)


def _dot(a, b):
    return jnp.dot(a, b, preferred_element_type=F32)


def _norm_mod(x, g, shift, scale):
    ms = jnp.mean(x * x, axis=-1, keepdims=True)
    return (x * lax.rsqrt(ms + EPS) * g) * (1.0 + scale) + shift


def _silu(x):
    return x * (1.0 / (1.0 + jnp.exp(-x)))


def _row_tile(L, want):
    t = min(L, want)
    assert L % t == 0 and t % HALO == 0
    return t


def _mod_kernel(c_ref, w_ref, b_ref, o_ref):
    s = _silu(c_ref[...])
    o_ref[...] = jnp.dot(s, w_ref[...], precision=HIGHEST, preferred_element_type=F32) + b_ref[...]


def _adaln(cond, w_mod, b_mod):
    R, D = cond.shape
    n = w_mod.shape[1]
    tn = 768
    return pl.pallas_call(
        _mod_kernel,
        out_shape=jax.ShapeDtypeStruct((R, n), F32),
        grid=(n // tn,),
        in_specs=[pl.BlockSpec((R, D), lambda j: (0, 0)),
                  pl.BlockSpec((D, tn), lambda j: (0, j)),
                  pl.BlockSpec((1, tn), lambda j: (0, j))],
        out_specs=pl.BlockSpec((R, tn), lambda j: (0, j)),
        compiler_params=_cparams(("arbitrary",)),
        name="adaln_mod",
    )(cond, w_mod, b_mod.reshape(1, n))


def _halo_specs(tm, D, L):
    nb = L // HALO
    per = tm // HALO
    cur = pl.BlockSpec((1, tm, D), lambda b, i, *_: (b, i, 0))
    prev = pl.BlockSpec((1, HALO, D), lambda b, i, *_: (b, jnp.maximum(i * per - 1, 0), 0))
    nxt = pl.BlockSpec((1, HALO, D), lambda b, i, *_: (b, jnp.minimum((i + 1) * per, nb - 1), 0))
    return cur, prev, nxt


def _mod_spec(arr):
    D = arr.shape[-1]
    if arr.shape[0] == 1:
        return pl.BlockSpec((1, 1, D), lambda b, *_: (0, 0, 0))
    return pl.BlockSpec((1, 1, D), lambda b, *_: (b, 0, 0))


def _fill_hn(hn_ref, xc_ref, xp_ref, xn_ref, g, shift, scale, tm, i, nt):
    hp = _norm_mod(xp_ref[0], g, shift, scale)
    hn_ref[0:HALO, :] = jnp.where(i > 0, hp, 0.0).astype(BF16)
    hn_ref[HALO:HALO + tm, :] = _norm_mod(xc_ref[0], g, shift, scale).astype(BF16)
    hx = _norm_mod(xn_ref[0], g, shift, scale)
    hn_ref[HALO + tm:, :] = jnp.where(i < nt - 1, hx, 0.0).astype(BF16)


def _dwconv3_rows(p, w, tm):
    n = tm + 2 * HALO
    up = pltpu.roll(p, 1, 0)
    dn = pltpu.roll(p, n - 1, 0)
    c = up * w[0:1] + p * w[1:2] + dn * w[2:3]
    return c[HALO:HALO + tm]


def _fh_in_kernel(xc_ref, xp_ref, xn_ref, g_ref, sh_ref, sc_ref, w_ref, cw_ref, cb_ref, gm_ref,
                  z_ref, u_ref, x0_ref, hn_ref, *, tm):
    i = pl.program_id(1)
    nt = pl.num_programs(1)
    _fill_hn(hn_ref, xc_ref, xp_ref, xn_ref, g_ref[...], sh_ref[0], sc_ref[0], tm, i, nt)
    proj = _dot(hn_ref[...], w_ref[...])
    uf = proj[HALO:HALO + tm, :D_FOURIER].astype(BF16)
    z_ref[0] = _dot(uf, gm_ref[...]).astype(BF16)
    c = _dwconv3_rows(proj[:, D_FOURIER:], cw_ref[...], tm) + cb_ref[...]
    x0 = c[:, :D_HYENA]
    x1 = c[:, D_HYENA:2 * D_HYENA]
    v = c[:, 2 * D_HYENA:]
    x0_ref[0] = x0
    u_ref[0] = (v * x1).astype(BF16)


def _fh_in(x, g, shift, scale, w_in, conv_w, conv_b, gmat):
    B, L, D = x.shape
    tm = _row_tile(L, 512)
    cur, prev, nxt = _halo_specs(tm, D, L)
    n_in = w_in.shape[1]
    nh = 3 * D_HYENA
    full = lambda shape: pl.BlockSpec(shape, lambda b, i: (0,) * len(shape))
    out_tile = lambda c: pl.BlockSpec((1, tm, c), lambda b, i: (b, i, 0))
    return pl.pallas_call(
        functools.partial(_fh_in_kernel, tm=tm),
        out_shape=(jax.ShapeDtypeStruct((B, L, 2 * D_FOURIER), BF16),
                   jax.ShapeDtypeStruct((B, L, D_HYENA), BF16),
                   jax.ShapeDtypeStruct((B, L, D_HYENA), F32)),
        grid=(B, L // tm),
        in_specs=[cur, prev, nxt, full((1, D)), _mod_spec(shift), _mod_spec(scale),
                  full((D, n_in)), full((3, nh)), full((1, nh)), full((D_FOURIER, 2 * D_FOURIER))],
        out_specs=(out_tile(2 * D_FOURIER), out_tile(D_HYENA), out_tile(D_HYENA)),
        scratch_shapes=[pltpu.VMEM((tm + 2 * HALO, D), BF16)],
        compiler_params=_cparams(("parallel", "arbitrary")),
        name="fh_in",
    )(x, x, x, g.reshape(1, D), shift, scale, w_in, conv_w, conv_b.reshape(1, nh), gmat)


def _fourier_kernel(c_ref, s_ref, z_ref, o_ref, *, scale):
    z = z_ref[0]
    y = _dot(c_ref[...], z[:, :D_FOURIER]) + _dot(s_ref[...], z[:, D_FOURIER:])
    o_ref[0] = (y * scale).astype(BF16)


def _fourier_seq(z, cl, sl):
    B, L, _ = z.shape
    tf = _row_tile(L, 512)
    scale = 1.0 / math.sqrt(L * FOURIER_GROUP_DIM)
    return pl.pallas_call(
        functools.partial(_fourier_kernel, scale=scale),
        out_shape=jax.ShapeDtypeStruct((B, L, D_FOURIER), BF16),
        grid=(B, L // tf),
        in_specs=[pl.BlockSpec((tf, L), lambda b, i: (i, 0)),
                  pl.BlockSpec((tf, L), lambda b, i: (i, 0)),
                  pl.BlockSpec((1, L, 2 * D_FOURIER), lambda b, i: (b, 0, 0))],
        out_specs=pl.BlockSpec((1, tf, D_FOURIER), lambda b, i: (b, i, 0)),
        compiler_params=_cparams(("parallel", "arbitrary")),
        name="fourier_seq",
    )(cl, sl, z)


def _filter_kernel(z_ref, w1_ref, b1_ref, w2_ref, b2_ref, w3_ref, b3_ref, w4_ref, f_ref, dl_ref,
                   k_ref, n_ref):
    i = pl.program_id(1)
    z = z_ref[0]
    f = f_ref[...]
    hd = lambda a, w: jnp.dot(a, w, precision=HIGHEST, preferred_element_type=F32)
    h = jnp.sin(f * (hd(z, w1_ref[...]) + b1_ref[...]))
    h = jnp.sin(f * (hd(h, w2_ref[...]) + b2_ref[...]))
    h = jnp.sin(f * (hd(h, w3_ref[...]) + b3_ref[...]))
    h = hd(h, w4_ref[0])
    t = z[:, 0:1]
    valid = z[:, HYENA_EMB_DIM:HYENA_EMB_DIM + 1]
    k = h * jnp.exp(-t * jnp.abs(dl_ref[...])) * valid
    k_ref[0] = k

    @pl.when(i == 0)
    def _():
        n_ref[...] = jnp.zeros_like(n_ref)

    n_ref[0] += jnp.sum(jnp.abs(k), axis=0, keepdims=True)


def _hyena_filter(L, w1, b1, w2, b2, w3, b3, w4, freq):
    pos = jnp.arange(L, dtype=F32)
    bands = jnp.linspace(1e-4, HYENA_BANDS - 1, HYENA_BANDS, dtype=F32)

    def emb(p, valid):
        t = p / max(L - 1, 1)
        ang = (2.0 * math.pi / L) * p[:, None] * bands[None, :]
        pad = jnp.zeros((L, HYENA_EMB_PAD - HYENA_EMB_DIM - 1), F32)
        return jnp.concatenate([t[:, None], jnp.cos(ang), -jnp.sin(ang), valid[:, None], pad], axis=-1)

    zf = emb(pos, jnp.ones((L,), F32))
    zb = emb(jnp.where(pos > 0, L - pos, 0.0), (pos > 0).astype(F32))
    z = jnp.stack([zf, zb])
    w1p = jnp.concatenate([w1, jnp.zeros((HYENA_EMB_PAD - HYENA_EMB_DIM, HYENA_WIDTH), F32)], axis=0)
    w4s = jnp.stack([w4[:, :D_HYENA], w4[:, D_HYENA:]])
    deltas = jnp.linspace(HYENA_MIN_DECAY, HYENA_MAX_DECAY, D_HYENA, dtype=F32).reshape(1, D_HYENA)
    tr = _row_tile(L, 512)
    W = HYENA_WIDTH
    full = lambda shape: pl.BlockSpec(shape, lambda s, i: (0,) * len(shape))
    return pl.pallas_call(
        _filter_kernel,
        out_shape=(jax.ShapeDtypeStruct((2, L, D_HYENA), F32), jax.ShapeDtypeStruct((2, 1, D_HYENA), F32)),
        grid=(2, L // tr),
        in_specs=[pl.BlockSpec((1, tr, HYENA_EMB_PAD), lambda s, i: (s, i, 0)),
                  full((HYENA_EMB_PAD, W)), full((1, W)), full((W, W)), full((1, W)),
                  full((W, W)), full((1, W)),
                  pl.BlockSpec((1, W, D_HYENA), lambda s, i: (s, 0, 0)),
                  full((1, W)), full((1, D_HYENA))],
        out_specs=(pl.BlockSpec((1, tr, D_HYENA), lambda s, i: (s, i, 0)),
                   pl.BlockSpec((1, 1, D_HYENA), lambda s, i: (s, 0, 0))),
        compiler_params=_cparams(("arbitrary", "arbitrary")),
        name="hyena_filter",
    )(z, w1p, b1.reshape(1, W), w2, b2.reshape(1, W), w3, b3.reshape(1, W), w4s, freq.reshape(1, W), deltas)


def _hy_fwd_kernel(c_ref, s_ref, u_ref, *rest, raw, tf):
    u = u_ref[0]
    ure = _dot(c_ref[...], u)
    uim = -_dot(s_ref[...], u)
    if raw:
        re_ref, im_ref = rest
        re_ref[0] = ure
        im_ref[0] = uim
        return
    kre_ref, kim_ref, re_ref, im_ref = rest
    kre = kre_ref[...]
    kim = kim_ref[...]
    row = lax.broadcasted_iota(jnp.int32, (tf, 1), 0) + pl.program_id(1) * tf
    packed = row == 0
    a = uim * kim
    re_ref[0] = (ure * kre - jnp.where(packed, 0.0, a)).astype(BF16)
    im_ref[0] = jnp.where(packed, a, ure * kim + uim * kre).astype(BF16)


def _hy_fwd(u, cb, sbf, kre=None, kim=None):
    B, L, C = u.shape
    tf = _row_tile(L, 512)
    raw = kre is None
    mat = pl.BlockSpec((tf, L), lambda b, i: (i, 0))
    tile = pl.BlockSpec((1, tf, C), lambda b, i: (b, i, 0))
    in_specs = [mat, mat, pl.BlockSpec((1, L, C), lambda b, i: (b, 0, 0))]
    args = [cb, sbf, u]
    if not raw:
        in_specs += [pl.BlockSpec((tf, C), lambda b, i: (i, 0))] * 2
        args += [kre, kim]
    dt = F32 if raw else BF16
    return pl.pallas_call(
        functools.partial(_hy_fwd_kernel, raw=raw, tf=tf),
        out_shape=(jax.ShapeDtypeStruct((B, L, C), dt), jax.ShapeDtypeStruct((B, L, C), dt)),
        grid=(B, L // tf),
        in_specs=in_specs,
        out_specs=(tile, tile),
        compiler_params=_cparams(("parallel", "arbitrary")),
        name="hyena_fwd_raw" if raw else "hyena_fwd",
    )(*args)


def _spec_combine_kernel(re_ref, im_ref, n_ref, kre_ref, kim_ref, *, tf, n_fft):
    row = lax.broadcasted_iota(jnp.int32, (tf, 1), 0) + pl.program_id(0) * tf
    sgn = (1 - 2 * (row & 1)).astype(F32)
    wgt = jnp.where(row == 0, 1.0 / n_fft, 2.0 / n_fft)
    s = wgt / (n_ref[0] + n_ref[1])
    kre_ref[...] = (re_ref[0] + sgn * re_ref[1]) * s
    kim_ref[...] = (im_ref[0] + sgn * im_ref[1]) * s


def _spec_combine(re, im, norms):
    _, L, C = re.shape
    tf = _row_tile(L, 512)
    pair = pl.BlockSpec((2, tf, C), lambda i: (0, i, 0))
    tile = pl.BlockSpec((tf, C), lambda i: (i, 0))
    return pl.pallas_call(
        functools.partial(_spec_combine_kernel, tf=tf, n_fft=2 * L),
        out_shape=(jax.ShapeDtypeStruct((L, C), F32), jax.ShapeDtypeStruct((L, C), F32)),
        grid=(L // tf,),
        in_specs=[pair, pair, pl.BlockSpec((2, 1, C), lambda i: (0, 0, 0))],
        out_specs=(tile, tile),
        compiler_params=_cparams(("arbitrary",)),
        name="hyena_spec_combine",
    )(re, im, norms)


def _hy_inv_kernel(c_ref, s_ref, re_ref, im_ref, x0_ref, u_ref, b_ref, o_ref):
    y = _dot(c_ref[...], re_ref[0]) - _dot(s_ref[...], im_ref[0])
    o_ref[0] = (x0_ref[0] * (y + u_ref[0].astype(F32) * b_ref[...])).astype(BF16)


def _hy_inv(yre, yim, x0, u, bias, cb, sbi):
    B, L, C = u.shape
    tt = _row_tile(L, 512)
    mat = pl.BlockSpec((tt, L), lambda b, i: (i, 0))
    whole = pl.BlockSpec((1, L, C), lambda b, i: (b, 0, 0))
    tile = pl.BlockSpec((1, tt, C), lambda b, i: (b, i, 0))
    return pl.pallas_call(
        _hy_inv_kernel,
        out_shape=jax.ShapeDtypeStruct((B, L, C), BF16),
        grid=(B, L // tt),
        in_specs=[mat, mat, whole, whole, tile, tile, pl.BlockSpec((1, C), lambda b, i: (0, 0))],
        out_specs=tile,
        compiler_params=_cparams(("parallel", "arbitrary")),
        name="hyena_inv",
    )(cb, sbi, yre, yim, x0, u, bias.reshape(1, C))


def _proj_res_kernel(x_ref, g_ref, w_ref, *rest):
    *y_refs, o_ref = rest
    acc = None
    off = 0
    for y_ref in y_refs:
        c = y_ref.shape[-1]
        t = _dot(y_ref[0], w_ref[off:off + c, :])
        acc = t if acc is None else acc + t
        off += c
    o_ref[0] = x_ref[0] + g_ref[0] * acc


def _proj_residual(x, gate, w, ys):
    B, L, D = x.shape
    tm = _row_tile(L, 512)
    tile = lambda c: pl.BlockSpec((1, tm, c), lambda b, i: (b, i, 0))
    return pl.pallas_call(
        _proj_res_kernel,
        out_shape=jax.ShapeDtypeStruct((B, L, D), F32),
        grid=(B, L // tm),
        in_specs=[tile(D), _mod_spec(gate), pl.BlockSpec(w.shape, lambda b, i: (0, 0))]
                 + [tile(y.shape[-1]) for y in ys],
        out_specs=tile(D),
        compiler_params=_cparams(("parallel", "arbitrary")),
        name="proj_residual",
    )(x, gate, w, *ys)


def _ffn_kernel(xc_ref, xp_ref, xn_ref, g_ref, sh_ref, sc_ref, gate_ref, wu_ref, cw_ref, cb_ref,
                wd_ref, o_ref, hn_ref, act_ref, *, tm):
    i = pl.program_id(1)
    nt = pl.num_programs(1)
    _fill_hn(hn_ref, xc_ref, xp_ref, xn_ref, g_ref[...], sh_ref[0], sc_ref[0], tm, i, nt)
    for c in range(0, D_FF, FFN_CHUNK):
        cols = slice(c, c + FFN_CHUNK)
        gx = _dot(hn_ref[...], wu_ref[:, cols])
        vx = _dot(hn_ref[HALO:HALO + tm, :], wu_ref[:, D_FF + c:D_FF + c + FFN_CHUNK])
        cv = _dwconv3_rows(gx, cw_ref[:, cols], tm) + cb_ref[:, cols]
        act_ref[:, cols] = (_silu(cv) * vx).astype(BF16)
    o_ref[0] = xc_ref[0] + gate_ref[0] * _dot(act_ref[...], wd_ref[...])


def _conv_ffn(x, g, shift, scale, gate, w_up, conv_w, conv_b, w_down):
    B, L, D = x.shape
    tm = _row_tile(L, 512)
    cur, prev, nxt = _halo_specs(tm, D, L)
    const = lambda shape: pl.BlockSpec(shape, lambda b, i: (0,) * len(shape), pipeline_mode=pl.Buffered(1))
    return pl.pallas_call(
        functools.partial(_ffn_kernel, tm=tm),
        out_shape=jax.ShapeDtypeStruct((B, L, D), F32),
        grid=(B, L // tm),
        in_specs=[cur, prev, nxt, const((1, D)), _mod_spec(shift), _mod_spec(scale), _mod_spec(gate),
                  const((D, 2 * D_FF)), const((3, D_FF)), const((1, D_FF)), const((D_FF, D))],
        out_specs=pl.BlockSpec((1, tm, D), lambda b, i: (b, i, 0)),
        scratch_shapes=[pltpu.VMEM((tm + 2 * HALO, D), BF16), pltpu.VMEM((tm, D_FF), BF16)],
        compiler_params=_cparams(("parallel", "arbitrary")),
        name="conv_ffn",
    )(x, x, x, g.reshape(1, D), shift, scale, gate, w_up, conv_w, conv_b.reshape(1, D_FF), w_down)


def _rms(v, g, n):
    return v * lax.rsqrt(jnp.sum(v * v, axis=-1, keepdims=True) * (1.0 / n) + EPS) * g


def _head_norm_rope_t(xt, gain, cos, sa, sb):
    r = lax.rsqrt(jnp.sum(xt * xt, axis=0, keepdims=True) * (1.0 / QK_DIM) + EPS)
    y = xt * r * gain
    return y * cos + pltpu.roll(y, 8, 0) * sa + pltpu.roll(y, HEAD_PAD - 8, 0) * sb


def _mla_proj_kernel(x_ref, g_ref, sh_ref, sc_ref, win_ref, qan_ref, wuq_ref, qn_ref, kvn_ref, wuk_ref,
                     wuv_ref, kn_ref, cos_ref, sa_ref, sb_ref, *outs, with_q):
    hn = _norm_mod(x_ref[0], g_ref[...], sh_ref[0], sc_ref[0]).astype(BF16)
    a = _dot(hn, win_ref[...])
    cos = cos_ref[...]
    sa = sa_ref[...]
    sb = sb_ref[...]
    if with_q:
        qt_ref, k_ref, vt_ref = outs
        qat = _rms(a[:, :Q_LORA], qan_ref[...], Q_LORA).T.astype(BF16)
        qft = _dot(wuq_ref[...], qat)
        qn = qn_ref[...]
        for h in range(MLA_HEADS):
            rows = slice(h * HEAD_PAD, (h + 1) * HEAD_PAD)
            qt_ref[0, rows, :] = _head_norm_rope_t(qft[rows], qn, cos, sa, sb).astype(BF16)
    else:
        k_ref, vt_ref = outs
    ckvt = _rms(a[:, Q_LORA:Q_LORA + KV_LORA], kvn_ref[...], KV_LORA).T.astype(BF16)
    kpet = pltpu.roll(a[:, Q_LORA + KV_LORA:].T, NOPE, 0)
    kft = _dot(wuk_ref[...], ckvt)
    vt_ref[0] = _dot(wuv_ref[...], ckvt).astype(BF16)
    kn = kn_ref[...]
    for h in range(MLA_HEADS):
        rows = slice(h * HEAD_PAD, (h + 1) * HEAD_PAD)
        kt = _head_norm_rope_t(kft[rows] + kpet, kn, cos, sa, sb)
        k_ref[0, :, rows] = kt.T.astype(BF16)


def _mla_proj(x, g, shift, scale, wts, rope_tabs, with_q):
    B, L, D = x.shape
    tm = _row_tile(L, 256)
    win, qan, wuq, qn, kvn, wuk, wuv, kn = wts
    cos, sa, sb = rope_tabs
    HP = MLA_HEADS * HEAD_PAD
    HV = MLA_HEADS * V_DIM
    full = lambda a: pl.BlockSpec(a.shape, lambda b, i: (0,) * a.ndim)
    tab = pl.BlockSpec((HEAD_PAD, tm), lambda b, i: (0, i))
    out_shape = [jax.ShapeDtypeStruct((B, L, HP), BF16), jax.ShapeDtypeStruct((B, HV, L), BF16)]
    out_specs = [pl.BlockSpec((1, tm, HP), lambda b, i: (b, i, 0)),
                 pl.BlockSpec((1, HV, tm), lambda b, i: (b, 0, i))]
    if with_q:
        out_shape = [jax.ShapeDtypeStruct((B, HP, L), BF16)] + out_shape
        out_specs = [pl.BlockSpec((1, HP, tm), lambda b, i: (b, 0, i))] + out_specs
    g2 = g.reshape(1, D)
    gain_tab = lambda v: jnp.broadcast_to(v.reshape(HEAD_PAD, 1), (HEAD_PAD, tm))
    qn = gain_tab(qn)
    kn = gain_tab(kn)
    return pl.pallas_call(
        functools.partial(_mla_proj_kernel, with_q=with_q),
        out_shape=tuple(out_shape),
        grid=(B, L // tm),
        in_specs=[pl.BlockSpec((1, tm, D), lambda b, i: (b, i, 0)), full(g2), _mod_spec(shift), _mod_spec(scale),
                  full(win), full(qan), full(wuq), full(qn), full(kvn), full(wuk), full(wuv), full(kn),
                  tab, tab, tab],
        out_specs=tuple(out_specs),
        compiler_params=_cparams(("parallel", "arbitrary")),
        name="mla_proj_q" if with_q else "mla_proj_kv",
    )(x, g2, shift, scale, win, qan, wuq, qn, kvn, wuk, wuv, kn, cos, sa, sb)


def _attn_kernel(qt_ref, kx_ref, kc_ref, vtx_ref, vtc_ref, o_ref, ax_ref, ac_ref, bx_ref, bc_ref):
    t = pl.program_id(0)
    io = (qt_ref, kx_ref, kc_ref, vtx_ref, vtc_ref, o_ref)

    @pl.when(t == 0)
    def _():
        bx_ref[...] = jnp.zeros(bx_ref.shape, F32)
        bc_ref[...] = jnp.zeros(bc_ref.shape, F32)

    @pl.when(t % 2 == 0)
    def _():
        _attn_step(*io, ax_ref, ac_ref, bx_ref, bc_ref)

    @pl.when(t % 2 == 1)
    def _():
        _attn_step(*io, bx_ref, bc_ref, ax_ref, ac_ref)


def _attn_step(qt_ref, kx_ref, kc_ref, vtx_ref, vtc_ref, o_ref, wx_ref, wc_ref, rx_ref, rc_ref):
    tq = qt_ref.shape[2]
    q0 = qt_ref[0, :HEAD_PAD, :]
    q1 = qt_ref[0, HEAD_PAD:, :]
    zq = jnp.zeros_like(q0)
    qbd = jnp.concatenate([jnp.concatenate([q0, zq], axis=1), jnp.concatenate([zq, q1], axis=1)], axis=0)
    wx_ref[...] = _dot(kx_ref[0], qbd)
    wc_ref[...] = _dot(kc_ref[0], qbd)

    sx = rx_ref[...]
    sc = rc_ref[...]
    m = jnp.maximum(jnp.max(sx, axis=0, keepdims=True), jnp.max(sc, axis=0, keepdims=True))
    px = jnp.exp2((sx - m).astype(BF16))
    pc = jnp.exp2((sc - m).astype(BF16))
    vtx = vtx_ref[0]
    vtc = vtc_ref[0]
    halves = []
    for j in range(2):
        own = slice(j * V_DIM, (j + 1) * V_DIM)
        oth = slice((1 - j) * V_DIM, (2 - j) * V_DIM)
        cols = slice(j * tq, (j + 1) * tq)
        parts_x = [vtx[own], jnp.ones_like(vtx[oth])]
        parts_c = [vtc[own], jnp.ones_like(vtc[oth])]
        if j == 1:
            parts_x.reverse()
            parts_c.reverse()
        r = _dot(jnp.concatenate(parts_x, axis=0), px[:, cols]) + _dot(jnp.concatenate(parts_c, axis=0), pc[:, cols])
        halves.append(r[own] * (1.0 / r[oth][0:1]))
    o_ref[0] = jnp.concatenate(halves, axis=0).T.astype(BF16)


def _attention(qt, kx, kc, vtx, vtc):
    B, HP, L = qt.shape
    Lc = kc.shape[1]
    tq = _row_tile(L, 256)
    nhp = MLA_HEADS // 2
    nq = L // tq
    n_tiles = B * nhp * nq

    def tile(t):
        t = jnp.clip(t, 0, n_tiles - 1)
        return t // (nhp * nq), (t // nq) % nhp, t % nq

    def logits_side(f):
        return lambda t: f(*tile(t))

    def value_side(f):
        return lambda t: f(*tile(t - 1))

    return pl.pallas_call(
        _attn_kernel,
        out_shape=jax.ShapeDtypeStruct((B, L, MLA_HEADS * V_DIM), BF16),
        grid=(n_tiles + 1,),
        in_specs=[pl.BlockSpec((1, 2 * HEAD_PAD, tq), logits_side(lambda b, h, i: (b, h, i))),
                  pl.BlockSpec((1, L, 2 * HEAD_PAD), logits_side(lambda b, h, i: (b, 0, h))),
                  pl.BlockSpec((1, Lc, 2 * HEAD_PAD), logits_side(lambda b, h, i: (b, 0, h))),
                  pl.BlockSpec((1, 2 * V_DIM, L), value_side(lambda b, h, i: (b, h, 0))),
                  pl.BlockSpec((1, 2 * V_DIM, Lc), value_side(lambda b, h, i: (b, h, 0)))],
        out_specs=pl.BlockSpec((1, tq, 2 * V_DIM), value_side(lambda b, h, i: (b, i, h))),
        scratch_shapes=[pltpu.VMEM((L, 2 * tq), F32), pltpu.VMEM((Lc, 2 * tq), F32),
                        pltpu.VMEM((L, 2 * tq), F32), pltpu.VMEM((Lc, 2 * tq), F32)],
        compiler_params=_cparams(("arbitrary",)),
        name="mla_attention",
    )(qt, kx, kc, vtx, vtc)


def _trig_matrix(n, period):
    blk = 64
    r = jnp.arange(n, dtype=jnp.int32)[:, None]
    c_hi = jnp.arange(n // blk, dtype=jnp.int32)[None, :] * blk
    c_lo = jnp.arange(blk, dtype=jnp.int32)[None, :]
    w = 2.0 * math.pi / period
    a = ((r * c_hi) % period).astype(F32) * w
    b = ((r * c_lo) % period).astype(F32) * w
    ca, sa, cb, sb = jnp.cos(a), jnp.sin(a), jnp.cos(b), jnp.sin(b)
    cos = (ca[:, :, None] * cb[:, None, :] - sa[:, :, None] * sb[:, None, :]).reshape(n, n)
    sin = (sa[:, :, None] * cb[:, None, :] + ca[:, :, None] * sb[:, None, :]).reshape(n, n)
    return cos, sin


def _dft_tables(L):
    cl, sl = _trig_matrix(L, L)
    cb, sb = _trig_matrix(L, 2 * L)
    alt = (1 - 2 * (jnp.arange(L, dtype=jnp.int32) & 1)).astype(F32)
    idx = jnp.arange(L, dtype=jnp.int32)
    sbf = jnp.where(idx[:, None] == 0, -alt[None, :], sb)
    sbi = jnp.where(idx[None, :] == 0, -alt[:, None], sb)
    return tuple(m.astype(BF16) for m in (cl, sl, cb, sbf, sbi))


def _group_dft_matrix():
    g = FOURIER_GROUP_DIM
    ng = D_FOURIER // g
    c, s = _trig_matrix(g, g)
    eye = jnp.eye(ng, dtype=F32)
    return jnp.concatenate([jnp.kron(eye, c), -jnp.kron(eye, s)], axis=1).astype(BF16)


def _rope_tables(L):
    pos = jnp.arange(L, dtype=jnp.int32)
    rows = (pos // GRID_W).astype(F32)
    cols = (pos % GRID_W).astype(F32)
    nf = ROPE // 4
    inv_freq = ROPE_THETA ** (-jnp.arange(nf, dtype=F32) / nf)
    ar = rows[:, None] * inv_freq[None, :]
    ac = cols[:, None] * inv_freq[None, :]
    one = jnp.ones((L, NOPE), F32)
    z_n = jnp.zeros((L, NOPE), F32)
    z_f = jnp.zeros((L, nf), F32)
    tail = jnp.zeros((L, HEAD_PAD - QK_DIM), F32)
    cos = jnp.concatenate([one, jnp.cos(ar), jnp.cos(ar), jnp.cos(ac), jnp.cos(ac), tail], axis=1)
    sa = jnp.concatenate([z_n, z_f, jnp.sin(ar), z_f, jnp.sin(ac), tail], axis=1)
    sb = jnp.concatenate([z_n, -jnp.sin(ar), z_f, -jnp.sin(ac), z_f, tail], axis=1)
    return cos.T, sa.T, sb.T


def _identity_rope_tables(L):
    return jnp.ones((HEAD_PAD, L), F32), jnp.zeros((HEAD_PAD, L), F32), jnp.zeros((HEAD_PAD, L), F32)


def _pad_heads(w, per_head, keep):
    K = w.shape[0]
    w = w.reshape(K, MLA_HEADS, per_head)[:, :, :keep]
    w = jnp.pad(w, ((0, 0), (0, 0), (0, HEAD_PAD - keep)))
    return w.reshape(K, MLA_HEADS * HEAD_PAD)


def _pad_lanes(v, n):
    return jnp.pad(v, (0, n - v.shape[0])).reshape(1, n)


def _fourier_hyena_mixer(x, g, shift, scale, w_in, conv_w, conv_b, gmat, tabs, kre, kim, hy_bias):
    cl, sl, cb, sbf, sbi = tabs
    z, u, x0 = _fh_in(x, g, shift, scale, w_in, conv_w, conv_b, gmat)
    y_f = _fourier_seq(z, cl, sl)
    yre, yim = _hy_fwd(u, cb, sbf, kre, kim)
    y_h = _hy_inv(yre, yim, x0, u, hy_bias, cb, sbi)
    return y_f, y_h


def kernel(x, c, ctx, c_ctx, norm1, norm2, w_mod, b_mod, ffn_w_up, ffn_conv_w, ffn_conv_b, ffn_w_down,
           fh_w_in, fh_w_out, hy_conv_w, hy_conv_b, hy_filt_w1, hy_filt_b1, hy_filt_w2, hy_filt_b2,
           hy_filt_w3, hy_filt_b3, hy_filt_w4, hy_freq, hy_bias, mla_w_in, mla_q_a_norm, mla_w_uq,
           mla_kv_a_norm, mla_w_ukv, mla_q_norm, mla_k_norm, mla_w_o):
    B, L, D = x.shape
    Lc = ctx.shape[1]
    bf = lambda a: a.astype(BF16)

    n_cond = -(-(B + 1) // 8) * 8
    cond = jnp.concatenate([c, c_ctx[None, :], jnp.zeros((n_cond - B - 1, D), F32)], axis=0)

    def mods(i):
        m = _adaln(cond, w_mod[i], b_mod[i])
        mx = [m[:B, j * D:(j + 1) * D].reshape(B, 1, D) for j in range(6)]
        mc = [m[B:B + 1, j * D:(j + 1) * D].reshape(1, 1, D) for j in range(6)]
        return mx, mc

    (sh1, sc1, g1, sh2, sc2, g2), (csh1, csc1, cg1, csh2, csc2, cg2) = mods(0)
    gmat = _group_dft_matrix()
    w_in0 = bf(fh_w_in[0])
    w_out0 = bf(fh_w_out[0])
    ffn0 = (bf(ffn_w_up[0]), ffn_conv_w[0], ffn_conv_b[0], bf(ffn_w_down[0]))
    filt = (hy_filt_w1[0], hy_filt_b1[0], hy_filt_w2[0], hy_filt_b2[0], hy_filt_w3[0], hy_filt_b3[0],
            hy_filt_w4[0], hy_freq[0])

    def mixer_layer(h, n_seq, m1, m2):
        s1, c1, gt1 = m1
        s2, c2, gt2 = m2
        tabs = _dft_tables(n_seq)
        taps, norms = _hyena_filter(n_seq, *filt)
        k_re, k_im = _hy_fwd(bf(taps), tabs[2], tabs[3])
        kre, kim = _spec_combine(k_re, k_im, norms)
        y_f, y_h = _fourier_hyena_mixer(h, norm1[0], s1, c1, w_in0, hy_conv_w[0], hy_conv_b[0], gmat, tabs,
                                        kre, kim, hy_bias[0])
        h = _proj_residual(h, gt1, w_out0, [y_f, y_h])
        return _conv_ffn(h, norm2[0], s2, c2, gt2, *ffn0)

    x = mixer_layer(x, L, (sh1, sc1, g1), (sh2, sc2, g2))
    ctx = mixer_layer(ctx, Lc, (csh1, csc1, cg1), (csh2, csc2, cg2))

    (sh1, sc1, g1, sh2, sc2, g2), (csh1, csc1, _, _, _, _) = mods(1)
    w_in1 = bf(jnp.pad(mla_w_in[0], ((0, 0), (0, MLA_IN_PAD - mla_w_in.shape[2]))))
    wuq = bf(_pad_heads(mla_w_uq[0], QK_DIM, QK_DIM).T)
    wuk = bf(_pad_heads(mla_w_ukv[0], NOPE + V_DIM, NOPE).T)
    wuv = bf(mla_w_ukv[0].reshape(KV_LORA, MLA_HEADS, NOPE + V_DIM)[:, :, NOPE:].reshape(KV_LORA, MLA_HEADS * V_DIM).T)
    q_gain = mla_q_norm[0] * (QK_DIM ** -0.5 * math.log2(math.e))
    wts = (w_in1, mla_q_a_norm[0].reshape(1, Q_LORA), wuq, _pad_lanes(q_gain, HEAD_PAD),
           mla_kv_a_norm[0].reshape(1, KV_LORA), wuk, wuv, _pad_lanes(mla_k_norm[0], HEAD_PAD))
    qt, kx, vtx = _mla_proj(x, norm1[1], sh1, sc1, wts, _rope_tables(L), True)
    kc, vtc = _mla_proj(ctx, norm1[1], csh1, csc1, wts, _identity_rope_tables(Lc), False)
    o = _attention(qt, kx, kc, vtx, vtc)
    x = _proj_residual(x, g1, bf(mla_w_o[0]), [o])
    ffn1 = (bf(ffn_w_up[1]), ffn_conv_w[1], ffn_conv_b[1], bf(ffn_w_down[1]))
    return _conv_ffn(x, norm2[1], sh2, sc2, g2, *ffn1)
```

```python
import functools
import math

import jax
import jax.numpy as jnp
from jax import lax
from jax.experimental import pallas as pl
from jax.experimental.pallas import tpu as pltpu

F32 = jnp.float32
BF16 = jnp.bfloat16
HIGHEST = lax.Precision.HIGHEST

EPS = 1e-6
D_MODEL = 1024
D_FF = 2816
D_FOURIER = 512
FOURIER_GROUP_DIM = 128
D_HYENA = 512
HYENA_EMB_DIM = 33
HYENA_EMB_PAD = 64
HYENA_BANDS = 16
HYENA_WIDTH = 64
HYENA_MIN_DECAY = math.log(1e-2) / 0.3
HYENA_MAX_DECAY = math.log(1e-2) / 1.5
MLA_HEADS = 16
Q_LORA = 256
KV_LORA = 128
NOPE = 64
ROPE = 32
QK_DIM = NOPE + ROPE
V_DIM = 64
HEAD_PAD = 128
MLA_IN_PAD = 512
GRID_W = 64
ROPE_THETA = 10000.0

FFN_CHUNK = 256
FFT_RADIX = 128
HALO = 16
VMEM_LIMIT = 56 * 1024 * 1024


def _cparams(sem):
    return pltpu.CompilerParams(dimension_semantics=sem, vmem_limit_bytes=VMEM_LIMIT)


def _dot(a, b):
    return jnp.dot(a, b, preferred_element_type=F32)


def _norm_mod(x, g, shift, scale):
    ms = jnp.mean(x * x, axis=-1, keepdims=True)
    return (x * lax.rsqrt(ms + EPS) * g) * (1.0 + scale) + shift


def _silu(x):
    return x * (1.0 / (1.0 + jnp.exp(-x)))


def _row_tile(L, want):
    t = min(L, want)
    assert L % t == 0 and t % HALO == 0
    return t


def _mod_kernel(c_ref, w_ref, b_ref, o_ref):
    s = _silu(c_ref[...])
    o_ref[...] = jnp.dot(s, w_ref[...], precision=HIGHEST, preferred_element_type=F32) + b_ref[...]


def _adaln(cond, w_mod, b_mod):
    R, D = cond.shape
    n = w_mod.shape[1]
    tn = 768
    return pl.pallas_call(
        _mod_kernel,
        out_shape=jax.ShapeDtypeStruct((R, n), F32),
        grid=(n // tn,),
        in_specs=[pl.BlockSpec((R, D), lambda j: (0, 0)),
                  pl.BlockSpec((D, tn), lambda j: (0, j)),
                  pl.BlockSpec((1, tn), lambda j: (0, j))],
        out_specs=pl.BlockSpec((R, tn), lambda j: (0, j)),
        compiler_params=_cparams(("arbitrary",)),
        name="adaln_mod",
    )(cond, w_mod, b_mod.reshape(1, n))


def _halo_specs(tm, D, L):
    nb = L // HALO
    per = tm // HALO
    cur = pl.BlockSpec((1, tm, D), lambda b, i, *_: (b, i, 0))
    prev = pl.BlockSpec((1, HALO, D), lambda b, i, *_: (b, jnp.maximum(i * per - 1, 0), 0))
    nxt = pl.BlockSpec((1, HALO, D), lambda b, i, *_: (b, jnp.minimum((i + 1) * per, nb - 1), 0))
    return cur, prev, nxt


def _mod_spec(arr):
    D = arr.shape[-1]
    if arr.shape[0] == 1:
        return pl.BlockSpec((1, 1, D), lambda b, *_: (0, 0, 0))
    return pl.BlockSpec((1, 1, D), lambda b, *_: (b, 0, 0))


def _fill_hn(hn_ref, xc_ref, xp_ref, xn_ref, g, shift, scale, tm, i, nt):
    hp = _norm_mod(xp_ref[0], g, shift, scale)
    hn_ref[0:HALO, :] = jnp.where(i > 0, hp, 0.0).astype(BF16)
    hn_ref[HALO:HALO + tm, :] = _norm_mod(xc_ref[0], g, shift, scale).astype(BF16)
    hx = _norm_mod(xn_ref[0], g, shift, scale)
    hn_ref[HALO + tm:, :] = jnp.where(i < nt - 1, hx, 0.0).astype(BF16)


def _dwconv3_rows(p, w, tm):
    n = tm + 2 * HALO
    up = pltpu.roll(p, 1, 0)
    dn = pltpu.roll(p, n - 1, 0)
    c = up * w[0:1] + p * w[1:2] + dn * w[2:3]
    return c[HALO:HALO + tm]


def _fh_in_kernel(xc_ref, xp_ref, xn_ref, g_ref, sh_ref, sc_ref, w_ref, cw_ref, cb_ref, gm_ref,
                  z_ref, u_ref, x0_ref, hn_ref, *, tm):
    i = pl.program_id(1)
    nt = pl.num_programs(1)
    _fill_hn(hn_ref, xc_ref, xp_ref, xn_ref, g_ref[...], sh_ref[0], sc_ref[0], tm, i, nt)
    proj = _dot(hn_ref[...], w_ref[...])
    uf = proj[HALO:HALO + tm, :D_FOURIER].astype(BF16)
    z_ref[0] = _dot(uf, gm_ref[...]).astype(BF16)
    c = _dwconv3_rows(proj[:, D_FOURIER:], cw_ref[...], tm) + cb_ref[...]
    x0 = c[:, :D_HYENA]
    x1 = c[:, D_HYENA:2 * D_HYENA]
    v = c[:, 2 * D_HYENA:]
    x0_ref[0] = x0
    u_ref[0] = (v * x1).astype(BF16)


def _fh_in(x, g, shift, scale, w_in, conv_w, conv_b, gmat):
    B, L, D = x.shape
    tm = _row_tile(L, 512)
    cur, prev, nxt = _halo_specs(tm, D, L)
    n_in = w_in.shape[1]
    nh = 3 * D_HYENA
    full = lambda shape: pl.BlockSpec(shape, lambda b, i: (0,) * len(shape))
    out_tile = lambda c: pl.BlockSpec((1, tm, c), lambda b, i: (b, i, 0))
    return pl.pallas_call(
        functools.partial(_fh_in_kernel, tm=tm),
        out_shape=(jax.ShapeDtypeStruct((B, L, 2 * D_FOURIER), BF16),
                   jax.ShapeDtypeStruct((B, L, D_HYENA), BF16),
                   jax.ShapeDtypeStruct((B, L, D_HYENA), F32)),
        grid=(B, L // tm),
        in_specs=[cur, prev, nxt, full((1, D)), _mod_spec(shift), _mod_spec(scale),
                  full((D, n_in)), full((3, nh)), full((1, nh)), full((D_FOURIER, 2 * D_FOURIER))],
        out_specs=(out_tile(2 * D_FOURIER), out_tile(D_HYENA), out_tile(D_HYENA)),
        scratch_shapes=[pltpu.VMEM((tm + 2 * HALO, D), BF16)],
        compiler_params=_cparams(("parallel", "arbitrary")),
        name="fh_in",
    )(x, x, x, g.reshape(1, D), shift, scale, w_in, conv_w, conv_b.reshape(1, nh), gmat)


def _fourier_kernel(c_ref, s_ref, z_ref, o_ref, *, scale):
    z = z_ref[0]
    y = _dot(c_ref[...], z[:, :D_FOURIER]) + _dot(s_ref[...], z[:, D_FOURIER:])
    o_ref[0] = (y * scale).astype(BF16)


def _fourier_seq(z, cl, sl):
    B, L, _ = z.shape
    tf = _row_tile(L, 512)
    scale = 1.0 / math.sqrt(L * FOURIER_GROUP_DIM)
    return pl.pallas_call(
        functools.partial(_fourier_kernel, scale=scale),
        out_shape=jax.ShapeDtypeStruct((B, L, D_FOURIER), BF16),
        grid=(B, L // tf),
        in_specs=[pl.BlockSpec((tf, L), lambda b, i: (i, 0)),
                  pl.BlockSpec((tf, L), lambda b, i: (i, 0)),
                  pl.BlockSpec((1, L, 2 * D_FOURIER), lambda b, i: (b, 0, 0))],
        out_specs=pl.BlockSpec((1, tf, D_FOURIER), lambda b, i: (b, i, 0)),
        compiler_params=_cparams(("parallel", "arbitrary")),
        name="fourier_seq",
    )(cl, sl, z)


def _fft_f1_kernel(cs_ref, z_ref, o_ref, *, nc):
    cs = _dot(cs_ref[...], z_ref[0])
    cz = cs[:FFT_RADIX]
    sz = cs[FFT_RADIX:]
    w = 2 * D_FOURIER
    for j in range(nc):
        re = slice(j * w, j * w + D_FOURIER)
        im = slice(j * w + D_FOURIER, (j + 1) * w)
        o_ref[0, 0, j] = (cz[:, re] + sz[:, im]).astype(BF16)
        o_ref[0, 1, j] = (cz[:, im] - sz[:, re]).astype(BF16)


def _fft_f2_kernel(t_ref, b_ref, o_ref, *, scale):
    n_hi = b_ref.shape[2]
    blk = b_ref[0].reshape(2 * n_hi * 8, D_FOURIER)
    y = _dot(t_ref[0], blk) * scale
    o_ref[0, :, 0] = y.reshape(n_hi, 8, D_FOURIER).astype(BF16)


def _fourier_seq_fft(z, cs, tmat):
    B, L, _ = z.shape
    n_hi = L // FFT_RADIX
    nc = 8
    ng = FFT_RADIX // 8
    z2 = z.reshape(B, FFT_RADIX, n_hi * 2 * D_FOURIER)
    bc = pl.pallas_call(
        functools.partial(_fft_f1_kernel, nc=nc),
        out_shape=jax.ShapeDtypeStruct((B, 2, n_hi, FFT_RADIX, D_FOURIER), BF16),
        grid=(B, n_hi // nc),
        in_specs=[pl.BlockSpec((2 * FFT_RADIX, FFT_RADIX), lambda b, i: (0, 0)),
                  pl.BlockSpec((1, FFT_RADIX, nc * 2 * D_FOURIER), lambda b, i: (b, 0, i))],
        out_specs=pl.BlockSpec((1, 2, nc, FFT_RADIX, D_FOURIER), lambda b, i: (b, 0, i, 0, 0)),
        compiler_params=_cparams(("parallel", "arbitrary")),
        name="fourier_fft1",
    )(cs, z2)
    scale = 1.0 / math.sqrt(L * FOURIER_GROUP_DIM)
    y = pl.pallas_call(
        functools.partial(_fft_f2_kernel, scale=scale),
        out_shape=jax.ShapeDtypeStruct((B, n_hi, ng, 8, D_FOURIER), BF16),
        grid=(B, ng),
        in_specs=[pl.BlockSpec((1, n_hi * 8, 2 * n_hi * 8), lambda b, g: (g, 0, 0)),
                  pl.BlockSpec((1, 2, n_hi, 8, D_FOURIER), lambda b, g: (b, 0, 0, g, 0))],
        out_specs=pl.BlockSpec((1, n_hi, 1, 8, D_FOURIER), lambda b, g: (b, 0, g, 0, 0)),
        compiler_params=_cparams(("parallel", "arbitrary")),
        name="fourier_fft2",
    )(tmat, bc)
    return y.reshape(B, L, D_FOURIER)


def _filter_kernel(z_ref, w1_ref, b1_ref, w2_ref, b2_ref, w3_ref, b3_ref, w4_ref, f_ref, dl_ref,
                   k_ref, n_ref):
    i = pl.program_id(1)
    z = z_ref[0]
    f = f_ref[...]
    hd = lambda a, w: jnp.dot(a, w, precision=HIGHEST, preferred_element_type=F32)
    h = jnp.sin(f * (hd(z, w1_ref[...]) + b1_ref[...]))
    h = jnp.sin(f * (hd(h, w2_ref[...]) + b2_ref[...]))
    h = jnp.sin(f * (hd(h, w3_ref[...]) + b3_ref[...]))
    h = hd(h, w4_ref[0])
    t = z[:, 0:1]
    valid = z[:, HYENA_EMB_DIM:HYENA_EMB_DIM + 1]
    k = h * jnp.exp(-t * jnp.abs(dl_ref[...])) * valid
    k_ref[0] = k

    @pl.when(i == 0)
    def _():
        n_ref[...] = jnp.zeros_like(n_ref)

    n_ref[0] += jnp.sum(jnp.abs(k), axis=0, keepdims=True)


def _hyena_filter(L, w1, b1, w2, b2, w3, b3, w4, freq):
    pos = jnp.arange(L, dtype=F32)
    bands = jnp.linspace(1e-4, HYENA_BANDS - 1, HYENA_BANDS, dtype=F32)

    def emb(p, valid):
        t = p / max(L - 1, 1)
        ang = (2.0 * math.pi / L) * p[:, None] * bands[None, :]
        pad = jnp.zeros((L, HYENA_EMB_PAD - HYENA_EMB_DIM - 1), F32)
        return jnp.concatenate([t[:, None], jnp.cos(ang), -jnp.sin(ang), valid[:, None], pad], axis=-1)

    zf = emb(pos, jnp.ones((L,), F32))
    zb = emb(jnp.where(pos > 0, L - pos, 0.0), (pos > 0).astype(F32))
    z = jnp.stack([zf, zb])
    w1p = jnp.concatenate([w1, jnp.zeros((HYENA_EMB_PAD - HYENA_EMB_DIM, HYENA_WIDTH), F32)], axis=0)
    w4s = jnp.stack([w4[:, :D_HYENA], w4[:, D_HYENA:]])
    deltas = jnp.linspace(HYENA_MIN_DECAY, HYENA_MAX_DECAY, D_HYENA, dtype=F32).reshape(1, D_HYENA)
    tr = _row_tile(L, 512)
    W = HYENA_WIDTH
    full = lambda shape: pl.BlockSpec(shape, lambda s, i: (0,) * len(shape))
    return pl.pallas_call(
        _filter_kernel,
        out_shape=(jax.ShapeDtypeStruct((2, L, D_HYENA), F32), jax.ShapeDtypeStruct((2, 1, D_HYENA), F32)),
        grid=(2, L // tr),
        in_specs=[pl.BlockSpec((1, tr, HYENA_EMB_PAD), lambda s, i: (s, i, 0)),
                  full((HYENA_EMB_PAD, W)), full((1, W)), full((W, W)), full((1, W)),
                  full((W, W)), full((1, W)),
                  pl.BlockSpec((1, W, D_HYENA), lambda s, i: (s, 0, 0)),
                  full((1, W)), full((1, D_HYENA))],
        out_specs=(pl.BlockSpec((1, tr, D_HYENA), lambda s, i: (s, i, 0)),
                   pl.BlockSpec((1, 1, D_HYENA), lambda s, i: (s, 0, 0))),
        compiler_params=_cparams(("arbitrary", "arbitrary")),
        name="hyena_filter",
    )(z, w1p, b1.reshape(1, W), w2, b2.reshape(1, W), w3, b3.reshape(1, W), w4s, freq.reshape(1, W), deltas)


def _hy_fwd_kernel(c_ref, s_ref, u_ref, *rest, raw, tf):
    u = u_ref[0]
    ure = _dot(c_ref[...], u)
    uim = -_dot(s_ref[...], u)
    if raw:
        re_ref, im_ref = rest
        re_ref[0] = ure
        im_ref[0] = uim
        return
    kre_ref, kim_ref, re_ref, im_ref = rest
    kre = kre_ref[...]
    kim = kim_ref[...]
    row = lax.broadcasted_iota(jnp.int32, (tf, 1), 0) + pl.program_id(1) * tf
    packed = row == 0
    a = uim * kim
    re_ref[0] = (ure * kre - jnp.where(packed, 0.0, a)).astype(BF16)
    im_ref[0] = jnp.where(packed, a, ure * kim + uim * kre).astype(BF16)


def _hy_fwd(u, cb, sbf, kre=None, kim=None):
    B, L, C = u.shape
    tf = _row_tile(L, 512)
    raw = kre is None
    mat = pl.BlockSpec((tf, L), lambda b, i: (i, 0))
    tile = pl.BlockSpec((1, tf, C), lambda b, i: (b, i, 0))
    in_specs = [mat, mat, pl.BlockSpec((1, L, C), lambda b, i: (b, 0, 0))]
    args = [cb, sbf, u]
    if not raw:
        in_specs += [pl.BlockSpec((tf, C), lambda b, i: (i, 0))] * 2
        args += [kre, kim]
    dt = F32 if raw else BF16
    return pl.pallas_call(
        functools.partial(_hy_fwd_kernel, raw=raw, tf=tf),
        out_shape=(jax.ShapeDtypeStruct((B, L, C), dt), jax.ShapeDtypeStruct((B, L, C), dt)),
        grid=(B, L // tf),
        in_specs=in_specs,
        out_specs=(tile, tile),
        compiler_params=_cparams(("parallel", "arbitrary")),
        name="hyena_fwd_raw" if raw else "hyena_fwd",
    )(*args)


def _spec_combine_kernel(re_ref, im_ref, n_ref, kre_ref, kim_ref, *, tf, n_fft):
    row = lax.broadcasted_iota(jnp.int32, (tf, 1), 0) + pl.program_id(0) * tf
    sgn = (1 - 2 * (row & 1)).astype(F32)
    wgt = jnp.where(row == 0, 1.0 / n_fft, 2.0 / n_fft)
    s = wgt / (n_ref[0] + n_ref[1])
    kre_ref[...] = (re_ref[0] + sgn * re_ref[1]) * s
    kim_ref[...] = (im_ref[0] + sgn * im_ref[1]) * s


def _spec_combine(re, im, norms):
    _, L, C = re.shape
    tf = _row_tile(L, 512)
    pair = pl.BlockSpec((2, tf, C), lambda i: (0, i, 0))
    tile = pl.BlockSpec((tf, C), lambda i: (i, 0))
    return pl.pallas_call(
        functools.partial(_spec_combine_kernel, tf=tf, n_fft=2 * L),
        out_shape=(jax.ShapeDtypeStruct((L, C), F32), jax.ShapeDtypeStruct((L, C), F32)),
        grid=(L // tf,),
        in_specs=[pair, pair, pl.BlockSpec((2, 1, C), lambda i: (0, 0, 0))],
        out_specs=(tile, tile),
        compiler_params=_cparams(("arbitrary",)),
        name="hyena_spec_combine",
    )(re, im, norms)


def _hy_inv_kernel(c_ref, s_ref, re_ref, im_ref, x0_ref, u_ref, b_ref, o_ref):
    y = _dot(c_ref[...], re_ref[0]) - _dot(s_ref[...], im_ref[0])
    o_ref[0] = (x0_ref[0] * (y + u_ref[0].astype(F32) * b_ref[...])).astype(BF16)


def _hy_inv(yre, yim, x0, u, bias, cb, sbi):
    B, L, C = u.shape
    tt = _row_tile(L, 512)
    mat = pl.BlockSpec((tt, L), lambda b, i: (i, 0))
    whole = pl.BlockSpec((1, L, C), lambda b, i: (b, 0, 0))
    tile = pl.BlockSpec((1, tt, C), lambda b, i: (b, i, 0))
    return pl.pallas_call(
        _hy_inv_kernel,
        out_shape=jax.ShapeDtypeStruct((B, L, C), BF16),
        grid=(B, L // tt),
        in_specs=[mat, mat, whole, whole, tile, tile, pl.BlockSpec((1, C), lambda b, i: (0, 0))],
        out_specs=tile,
        compiler_params=_cparams(("parallel", "arbitrary")),
        name="hyena_inv",
    )(cb, sbi, yre, yim, x0, u, bias.reshape(1, C))


def _fft_a_kernel(f_ref, x_ref, o_ref):
    o_ref[0] = _dot(f_ref[...], x_ref[0]).astype(BF16)


def _hy_fft_a(x2, f1):
    Bx, R, W = x2.shape
    tc = 8192
    return pl.pallas_call(
        _fft_a_kernel,
        out_shape=jax.ShapeDtypeStruct((Bx, f1.shape[0], W), BF16),
        grid=(Bx, W // tc),
        in_specs=[pl.BlockSpec(f1.shape, lambda b, i: (0, 0)),
                  pl.BlockSpec((1, R, tc), lambda b, i: (b, 0, i))],
        out_specs=pl.BlockSpec((1, f1.shape[0], tc), lambda b, i: (b, 0, i)),
        compiler_params=_cparams(("parallel", "arbitrary")),
        name="hyena_fft_a",
    )(f1, x2)


def _fft_b_kernel(gf_ref, a_ref, *rest, raw, n_fft):
    nb = a_ref.shape[0]
    gf = gf_ref[0]
    if raw:
        n_ref, o_ref = rest
        s = 1.0 / (n_fft * (n_ref[0] + n_ref[1]))
    else:
        gi_ref, k_ref, o_ref = rest
        gi = gi_ref[0]
        kr = k_ref[0, 0, 0]
        ki = k_ref[0, 1, 0]
    for b in range(nb):
        a = a_ref[b, :, 0].reshape(2 * FFT_RADIX, D_HYENA)
        x = _dot(gf, a)
        if raw:
            o_ref[b, :, 0] = (x * s).reshape(2, FFT_RADIX, D_HYENA)
            continue
        xr = x[:FFT_RADIX]
        xi = x[FFT_RADIX:]
        y = jnp.concatenate([xr * kr - xi * ki, xr * ki + xi * kr], axis=0).astype(BF16)
        o_ref[b, :, 0] = _dot(gi, y).reshape(2, FFT_RADIX, D_HYENA).astype(BF16)


def _hy_fft_b(a5, gf, gi=None, kspec=None, norms=None):
    Bx, _, n_hi, _, C = a5.shape
    raw = kspec is None
    blk = pl.BlockSpec((Bx, 2, 1, FFT_RADIX, C), lambda k: (0, 0, k, 0, 0))
    tab = pl.BlockSpec((1, 2 * FFT_RADIX, 2 * FFT_RADIX), lambda k: (k, 0, 0))
    if raw:
        in_specs = [tab, blk, pl.BlockSpec((2, 1, C), lambda k: (0, 0, 0))]
        args = (gf, a5, norms)
    else:
        kblk = pl.BlockSpec((1, 2, 1, FFT_RADIX, C), lambda k: (0, 0, k, 0, 0))
        in_specs = [tab, blk, tab, kblk]
        args = (gf, a5, gi, kspec)
    return pl.pallas_call(
        functools.partial(_fft_b_kernel, raw=raw, n_fft=n_hi * FFT_RADIX),
        out_shape=jax.ShapeDtypeStruct(a5.shape, F32 if raw else BF16),
        grid=(n_hi,),
        in_specs=in_specs,
        out_specs=blk,
        compiler_params=_cparams(("arbitrary",)),
        name="hyena_fft_spec" if raw else "hyena_fft_b",
    )(*args)


def _fft_c_kernel(f_ref, v_ref, x0_ref, u_ref, b_ref, o_ref):
    y = _dot(f_ref[...], v_ref[0])
    o_ref[0] = (x0_ref[0] * (y + u_ref[0].astype(F32) * b_ref[...])).astype(BF16)


def _hy_fft_c(v2, f3, x0, u, bias_t):
    B, R2, W = v2.shape
    R = f3.shape[0]
    tc = 8192
    tile = pl.BlockSpec((1, R, tc), lambda b, i: (b, 0, i))
    return pl.pallas_call(
        _fft_c_kernel,
        out_shape=jax.ShapeDtypeStruct((B, R, W), BF16),
        grid=(B, W // tc),
        in_specs=[pl.BlockSpec(f3.shape, lambda b, i: (0, 0)),
                  pl.BlockSpec((1, R2, tc), lambda b, i: (b, 0, i)),
                  tile, tile, pl.BlockSpec((1, tc), lambda b, i: (0, i))],
        out_specs=tile,
        compiler_params=_cparams(("parallel", "arbitrary")),
        name="hyena_fft_c",
    )(f3, v2, x0, u, bias_t)


def _proj_res_kernel(x_ref, g_ref, w_ref, *rest):
    *y_refs, o_ref = rest
    acc = None
    off = 0
    for y_ref in y_refs:
        c = y_ref.shape[-1]
        t = _dot(y_ref[0], w_ref[off:off + c, :])
        acc = t if acc is None else acc + t
        off += c
    o_ref[0] = x_ref[0] + g_ref[0] * acc


def _proj_residual(x, gate, w, ys):
    B, L, D = x.shape
    tm = _row_tile(L, 512)
    tile = lambda c: pl.BlockSpec((1, tm, c), lambda b, i: (b, i, 0))
    return pl.pallas_call(
        _proj_res_kernel,
        out_shape=jax.ShapeDtypeStruct((B, L, D), F32),
        grid=(B, L // tm),
        in_specs=[tile(D), _mod_spec(gate), pl.BlockSpec(w.shape, lambda b, i: (0, 0))]
                 + [tile(y.shape[-1]) for y in ys],
        out_specs=tile(D),
        compiler_params=_cparams(("parallel", "arbitrary")),
        name="proj_residual",
    )(x, gate, w, *ys)


def _ffn_kernel(xc_ref, xp_ref, xn_ref, g_ref, sh_ref, sc_ref, gate_ref, wu_ref, cw_ref, cb_ref,
                wd_ref, o_ref, hn_ref, act_ref, *, tm):
    i = pl.program_id(1)
    nt = pl.num_programs(1)
    _fill_hn(hn_ref, xc_ref, xp_ref, xn_ref, g_ref[...], sh_ref[0], sc_ref[0], tm, i, nt)
    for c in range(0, D_FF, FFN_CHUNK):
        cols = slice(c, c + FFN_CHUNK)
        gx = _dot(hn_ref[...], wu_ref[:, cols])
        vx = _dot(hn_ref[HALO:HALO + tm, :], wu_ref[:, D_FF + c:D_FF + c + FFN_CHUNK])
        cv = _dwconv3_rows(gx, cw_ref[:, cols], tm) + cb_ref[:, cols]
        act_ref[:, cols] = (_silu(cv) * vx).astype(BF16)
    o_ref[0] = xc_ref[0] + gate_ref[0] * _dot(act_ref[...], wd_ref[...])


def _conv_ffn(x, g, shift, scale, gate, w_up, conv_w, conv_b, w_down):
    B, L, D = x.shape
    tm = _row_tile(L, 512)
    cur, prev, nxt = _halo_specs(tm, D, L)
    const = lambda shape: pl.BlockSpec(shape, lambda b, i: (0,) * len(shape), pipeline_mode=pl.Buffered(1))
    return pl.pallas_call(
        functools.partial(_ffn_kernel, tm=tm),
        out_shape=jax.ShapeDtypeStruct((B, L, D), F32),
        grid=(B, L // tm),
        in_specs=[cur, prev, nxt, const((1, D)), _mod_spec(shift), _mod_spec(scale), _mod_spec(gate),
                  const((D, 2 * D_FF)), const((3, D_FF)), const((1, D_FF)), const((D_FF, D))],
        out_specs=pl.BlockSpec((1, tm, D), lambda b, i: (b, i, 0)),
        scratch_shapes=[pltpu.VMEM((tm + 2 * HALO, D), BF16), pltpu.VMEM((tm, D_FF), BF16)],
        compiler_params=_cparams(("parallel", "arbitrary")),
        name="conv_ffn",
    )(x, x, x, g.reshape(1, D), shift, scale, gate, w_up, conv_w, conv_b.reshape(1, D_FF), w_down)


def _rms(v, g, n):
    return v * lax.rsqrt(jnp.sum(v * v, axis=-1, keepdims=True) * (1.0 / n) + EPS) * g


def _head_norm_rope_t(xt, gain, cos, sa, sb):
    r = lax.rsqrt(jnp.sum(xt * xt, axis=0, keepdims=True) * (1.0 / QK_DIM) + EPS)
    y = xt * r * gain
    return y * cos + pltpu.roll(y, 8, 0) * sa + pltpu.roll(y, HEAD_PAD - 8, 0) * sb


def _mla_proj_kernel(x_ref, g_ref, sh_ref, sc_ref, win_ref, qan_ref, wuq_ref, qn_ref, kvn_ref, wuk_ref,
                     wuv_ref, kn_ref, cos_ref, sa_ref, sb_ref, *outs, with_q):
    hn = _norm_mod(x_ref[0], g_ref[...], sh_ref[0], sc_ref[0]).astype(BF16)
    a = _dot(hn, win_ref[...])
    cos = cos_ref[...]
    sa = sa_ref[...]
    sb = sb_ref[...]
    if with_q:
        qt_ref, k_ref, vt_ref = outs
        qat = _rms(a[:, :Q_LORA], qan_ref[...], Q_LORA).T.astype(BF16)
        qft = _dot(wuq_ref[...], qat)
        qn = qn_ref[...]
        for h in range(MLA_HEADS):
            rows = slice(h * HEAD_PAD, (h + 1) * HEAD_PAD)
            qt_ref[0, rows, :] = _head_norm_rope_t(qft[rows], qn, cos, sa, sb).astype(BF16)
    else:
        k_ref, vt_ref = outs
    ckvt = _rms(a[:, Q_LORA:Q_LORA + KV_LORA], kvn_ref[...], KV_LORA).T.astype(BF16)
    kpet = pltpu.roll(a[:, Q_LORA + KV_LORA:].T, NOPE, 0)
    kft = _dot(wuk_ref[...], ckvt)
    vt_ref[0] = _dot(wuv_ref[...], ckvt).astype(BF16)
    kn = kn_ref[...]
    for h in range(MLA_HEADS):
        rows = slice(h * HEAD_PAD, (h + 1) * HEAD_PAD)
        kt = _head_norm_rope_t(kft[rows] + kpet, kn, cos, sa, sb)
        k_ref[0, :, rows] = kt.T.astype(BF16)


def _mla_proj(x, g, shift, scale, wts, rope_tabs, with_q):
    B, L, D = x.shape
    tm = _row_tile(L, 256)
    win, qan, wuq, qn, kvn, wuk, wuv, kn = wts
    cos, sa, sb = rope_tabs
    HP = MLA_HEADS * HEAD_PAD
    HV = MLA_HEADS * V_DIM
    full = lambda a: pl.BlockSpec(a.shape, lambda b, i: (0,) * a.ndim)
    tab = pl.BlockSpec((HEAD_PAD, tm), lambda b, i: (0, i))
    out_shape = [jax.ShapeDtypeStruct((B, L, HP), BF16), jax.ShapeDtypeStruct((B, HV, L), BF16)]
    out_specs = [pl.BlockSpec((1, tm, HP), lambda b, i: (b, i, 0)),
                 pl.BlockSpec((1, HV, tm), lambda b, i: (b, 0, i))]
    if with_q:
        out_shape = [jax.ShapeDtypeStruct((B, HP, L), BF16)] + out_shape
        out_specs = [pl.BlockSpec((1, HP, tm), lambda b, i: (b, 0, i))] + out_specs
    g2 = g.reshape(1, D)
    gain_tab = lambda v: jnp.broadcast_to(v.reshape(HEAD_PAD, 1), (HEAD_PAD, tm))
    qn = gain_tab(qn)
    kn = gain_tab(kn)
    return pl.pallas_call(
        functools.partial(_mla_proj_kernel, with_q=with_q),
        out_shape=tuple(out_shape),
        grid=(B, L // tm),
        in_specs=[pl.BlockSpec((1, tm, D), lambda b, i: (b, i, 0)), full(g2), _mod_spec(shift), _mod_spec(scale),
                  full(win), full(qan), full(wuq), full(qn), full(kvn), full(wuk), full(wuv), full(kn),
                  tab, tab, tab],
        out_specs=tuple(out_specs),
        compiler_params=_cparams(("parallel", "arbitrary")),
        name="mla_proj_q" if with_q else "mla_proj_kv",
    )(x, g2, shift, scale, win, qan, wuq, qn, kvn, wuk, wuv, kn, cos, sa, sb)


def _attn_kernel(qt_ref, kx_ref, kc_ref, vtx_ref, vtc_ref, o_ref, ax_ref, ac_ref, bx_ref, bc_ref):
    t = pl.program_id(0)
    io = (qt_ref, kx_ref, kc_ref, vtx_ref, vtc_ref, o_ref)

    @pl.when(t == 0)
    def _():
        bx_ref[...] = jnp.zeros(bx_ref.shape, F32)
        bc_ref[...] = jnp.zeros(bc_ref.shape, F32)

    @pl.when(t % 2 == 0)
    def _():
        _attn_step(*io, ax_ref, ac_ref, bx_ref, bc_ref)

    @pl.when(t % 2 == 1)
    def _():
        _attn_step(*io, bx_ref, bc_ref, ax_ref, ac_ref)


def _attn_step(qt_ref, kx_ref, kc_ref, vtx_ref, vtc_ref, o_ref, wx_ref, wc_ref, rx_ref, rc_ref):
    tq = qt_ref.shape[2]
    q0 = qt_ref[0, :HEAD_PAD, :]
    q1 = qt_ref[0, HEAD_PAD:, :]
    zq = jnp.zeros_like(q0)
    qbd = jnp.concatenate([jnp.concatenate([q0, zq], axis=1), jnp.concatenate([zq, q1], axis=1)], axis=0)
    wx_ref[...] = _dot(kx_ref[0], qbd)
    wc_ref[...] = _dot(kc_ref[0], qbd)

    sx = rx_ref[...]
    sc = rc_ref[...]
    m = jnp.maximum(jnp.max(sx, axis=0, keepdims=True), jnp.max(sc, axis=0, keepdims=True))
    px = jnp.exp2((sx - m).astype(BF16))
    pc = jnp.exp2((sc - m).astype(BF16))
    vtx = vtx_ref[0]
    vtc = vtc_ref[0]
    halves = []
    for j in range(2):
        own = slice(j * V_DIM, (j + 1) * V_DIM)
        oth = slice((1 - j) * V_DIM, (2 - j) * V_DIM)
        cols = slice(j * tq, (j + 1) * tq)
        parts_x = [vtx[own], jnp.ones_like(vtx[oth])]
        parts_c = [vtc[own], jnp.ones_like(vtc[oth])]
        if j == 1:
            parts_x.reverse()
            parts_c.reverse()
        r = _dot(jnp.concatenate(parts_x, axis=0), px[:, cols]) + _dot(jnp.concatenate(parts_c, axis=0), pc[:, cols])
        halves.append(r[own] * (1.0 / r[oth][0:1]))
    o_ref[0] = jnp.concatenate(halves, axis=0).T.astype(BF16)


def _attention(qt, kx, kc, vtx, vtc):
    B, HP, L = qt.shape
    Lc = kc.shape[1]
    tq = _row_tile(L, 256)
    nhp = MLA_HEADS // 2
    nq = L // tq
    n_tiles = B * nhp * nq

    def tile(t):
        t = jnp.clip(t, 0, n_tiles - 1)
        return t // (nhp * nq), (t // nq) % nhp, t % nq

    def logits_side(f):
        return lambda t: f(*tile(t))

    def value_side(f):
        return lambda t: f(*tile(t - 1))

    return pl.pallas_call(
        _attn_kernel,
        out_shape=jax.ShapeDtypeStruct((B, L, MLA_HEADS * V_DIM), BF16),
        grid=(n_tiles + 1,),
        in_specs=[pl.BlockSpec((1, 2 * HEAD_PAD, tq), logits_side(lambda b, h, i: (b, h, i))),
                  pl.BlockSpec((1, L, 2 * HEAD_PAD), logits_side(lambda b, h, i: (b, 0, h))),
                  pl.BlockSpec((1, Lc, 2 * HEAD_PAD), logits_side(lambda b, h, i: (b, 0, h))),
                  pl.BlockSpec((1, 2 * V_DIM, L), value_side(lambda b, h, i: (b, h, 0))),
                  pl.BlockSpec((1, 2 * V_DIM, Lc), value_side(lambda b, h, i: (b, h, 0)))],
        out_specs=pl.BlockSpec((1, tq, 2 * V_DIM), value_side(lambda b, h, i: (b, i, h))),
        scratch_shapes=[pltpu.VMEM((L, 2 * tq), F32), pltpu.VMEM((Lc, 2 * tq), F32),
                        pltpu.VMEM((L, 2 * tq), F32), pltpu.VMEM((Lc, 2 * tq), F32)],
        compiler_params=_cparams(("arbitrary",)),
        name="mla_attention",
    )(qt, kx, kc, vtx, vtc)


def _trig_matrix(n, period):
    blk = 64
    r = jnp.arange(n, dtype=jnp.int32)[:, None]
    c_hi = jnp.arange(n // blk, dtype=jnp.int32)[None, :] * blk
    c_lo = jnp.arange(blk, dtype=jnp.int32)[None, :]
    w = 2.0 * math.pi / period
    a = ((r * c_hi) % period).astype(F32) * w
    b = ((r * c_lo) % period).astype(F32) * w
    ca, sa, cb, sb = jnp.cos(a), jnp.sin(a), jnp.cos(b), jnp.sin(b)
    cos = (ca[:, :, None] * cb[:, None, :] - sa[:, :, None] * sb[:, None, :]).reshape(n, n)
    sin = (sa[:, :, None] * cb[:, None, :] + ca[:, :, None] * sb[:, None, :]).reshape(n, n)
    return cos, sin


def _dft_tables(L):
    cl, sl = _trig_matrix(L, L)
    cb, sb = _trig_matrix(L, 2 * L)
    alt = (1 - 2 * (jnp.arange(L, dtype=jnp.int32) & 1)).astype(F32)
    idx = jnp.arange(L, dtype=jnp.int32)
    sbf = jnp.where(idx[:, None] == 0, -alt[None, :], sb)
    sbi = jnp.where(idx[None, :] == 0, -alt[:, None], sb)
    return tuple(m.astype(BF16) for m in (cl, sl, cb, sbf, sbi))


def _angle(num, period):
    return (num % period).astype(F32) * (2.0 * math.pi / period)


def _fourier_fft_tables(L):
    n_hi = L // FFT_RADIX
    i = jnp.arange(FFT_RADIX, dtype=jnp.int32)
    a = _angle(i[:, None] * i[None, :], FFT_RADIX)
    cs = jnp.concatenate([jnp.cos(a), jnp.sin(a)], axis=0).astype(BF16)
    ng = FFT_RADIX // 8
    g = jnp.arange(ng, dtype=jnp.int32)[:, None, None, None]
    k1 = jnp.arange(n_hi, dtype=jnp.int32)[None, :, None, None]
    j = jnp.arange(8, dtype=jnp.int32)[None, None, :, None]
    n1 = jnp.arange(n_hi, dtype=jnp.int32)[None, None, None, :]
    ang = _angle(n1 * (FFT_RADIX * k1 + 8 * g + j), L)
    eye = jnp.eye(8, dtype=F32)
    blocks = [t[:, :, :, :, None] * eye[None, None, :, None, :] for t in (jnp.cos(ang), jnp.sin(ang))]
    tmat = jnp.stack(blocks, axis=3)
    return cs, tmat.reshape(ng, n_hi * 8, 2 * n_hi * 8).astype(BF16)


def _hyena_fft_tables(L):
    n_fft = 2 * L
    n_hi = n_fft // FFT_RADIX
    k1 = jnp.arange(n_hi, dtype=jnp.int32)
    a1 = _angle(k1[:, None] * k1[None, :], n_hi)
    c1, s1 = jnp.cos(a1), jnp.sin(a1)
    f1 = jnp.concatenate([c1, -s1], axis=0).astype(BF16)
    f3 = jnp.concatenate([c1, -s1], axis=1)[:n_hi // 2].astype(BF16)
    r = jnp.arange(FFT_RADIX, dtype=jnp.int32)
    k = k1[:, None, None] + n_hi * r[None, :, None]
    a2 = _angle(k * r[None, None, :], n_fft)
    c2, s2 = jnp.cos(a2), jnp.sin(a2)
    gf = jnp.concatenate([jnp.concatenate([c2, s2], axis=2), jnp.concatenate([-s2, c2], axis=2)], axis=1)
    c2t, s2t = jnp.swapaxes(c2, 1, 2), jnp.swapaxes(s2, 1, 2)
    gi = jnp.concatenate([jnp.concatenate([c2t, -s2t], axis=2), jnp.concatenate([s2t, c2t], axis=2)], axis=1)
    return f1, f3, gf.astype(BF16), gi.astype(BF16)


def _group_dft_matrix():
    g = FOURIER_GROUP_DIM
    ng = D_FOURIER // g
    c, s = _trig_matrix(g, g)
    eye = jnp.eye(ng, dtype=F32)
    return jnp.concatenate([jnp.kron(eye, c), -jnp.kron(eye, s)], axis=1).astype(BF16)


def _rope_tables(L):
    pos = jnp.arange(L, dtype=jnp.int32)
    rows = (pos // GRID_W).astype(F32)
    cols = (pos % GRID_W).astype(F32)
    nf = ROPE // 4
    inv_freq = ROPE_THETA ** (-jnp.arange(nf, dtype=F32) / nf)
    ar = rows[:, None] * inv_freq[None, :]
    ac = cols[:, None] * inv_freq[None, :]
    one = jnp.ones((L, NOPE), F32)
    z_n = jnp.zeros((L, NOPE), F32)
    z_f = jnp.zeros((L, nf), F32)
    tail = jnp.zeros((L, HEAD_PAD - QK_DIM), F32)
    cos = jnp.concatenate([one, jnp.cos(ar), jnp.cos(ar), jnp.cos(ac), jnp.cos(ac), tail], axis=1)
    sa = jnp.concatenate([z_n, z_f, jnp.sin(ar), z_f, jnp.sin(ac), tail], axis=1)
    sb = jnp.concatenate([z_n, -jnp.sin(ar), z_f, -jnp.sin(ac), z_f, tail], axis=1)
    return cos.T, sa.T, sb.T


def _identity_rope_tables(L):
    return jnp.ones((HEAD_PAD, L), F32), jnp.zeros((HEAD_PAD, L), F32), jnp.zeros((HEAD_PAD, L), F32)


def _pad_heads(w, per_head, keep):
    K = w.shape[0]
    w = w.reshape(K, MLA_HEADS, per_head)[:, :, :keep]
    w = jnp.pad(w, ((0, 0), (0, 0), (0, HEAD_PAD - keep)))
    return w.reshape(K, MLA_HEADS * HEAD_PAD)


def _pad_lanes(v, n):
    return jnp.pad(v, (0, n - v.shape[0])).reshape(1, n)


def _fourier_hyena_mixer(x, g, shift, scale, w_in, conv_w, conv_b, gmat, taps, norms, hy_bias):
    B, L, _ = x.shape
    z, u, x0 = _fh_in(x, g, shift, scale, w_in, conv_w, conv_b, gmat)
    if (L // FFT_RADIX) % 16 == 0:
        cs, tmat = _fourier_fft_tables(L)
        y_f = _fourier_seq_fft(z, cs, tmat)
        f1, f3, gf, gi = _hyena_fft_tables(L)
        n_hi = 2 * L // FFT_RADIX
        W = FFT_RADIX * D_HYENA
        spec5 = lambda a: a.reshape(a.shape[0], 2, n_hi, FFT_RADIX, D_HYENA)
        kspec = _hy_fft_b(spec5(_hy_fft_a(taps.astype(BF16).reshape(1, n_hi, W), f1)), gf, norms=norms)
        a_u = _hy_fft_a(u.reshape(B, n_hi // 2, W), f1[:, :n_hi // 2])
        v = _hy_fft_b(spec5(a_u), gf, gi, kspec)
        bias_t = jnp.tile(hy_bias.reshape(1, D_HYENA), (1, FFT_RADIX))
        y_h = _hy_fft_c(v.reshape(B, 2 * n_hi, W), f3, x0.reshape(B, n_hi // 2, W), u.reshape(B, n_hi // 2, W), bias_t)
        return y_f, y_h.reshape(B, L, D_HYENA)
    cl, sl, cb, sbf, sbi = _dft_tables(L)
    k_re, k_im = _hy_fwd(taps.astype(BF16), cb, sbf)
    kre, kim = _spec_combine(k_re, k_im, norms)
    y_f = _fourier_seq(z, cl, sl)
    yre, yim = _hy_fwd(u, cb, sbf, kre, kim)
    y_h = _hy_inv(yre, yim, x0, u, hy_bias, cb, sbi)
    return y_f, y_h


def kernel(x, c, ctx, c_ctx, norm1, norm2, w_mod, b_mod, ffn_w_up, ffn_conv_w, ffn_conv_b, ffn_w_down,
           fh_w_in, fh_w_out, hy_conv_w, hy_conv_b, hy_filt_w1, hy_filt_b1, hy_filt_w2, hy_filt_b2,
           hy_filt_w3, hy_filt_b3, hy_filt_w4, hy_freq, hy_bias, mla_w_in, mla_q_a_norm, mla_w_uq,
           mla_kv_a_norm, mla_w_ukv, mla_q_norm, mla_k_norm, mla_w_o):
    B, L, D = x.shape
    Lc = ctx.shape[1]
    bf = lambda a: a.astype(BF16)

    n_cond = -(-(B + 1) // 8) * 8
    cond = jnp.concatenate([c, c_ctx[None, :], jnp.zeros((n_cond - B - 1, D), F32)], axis=0)

    def mods(i):
        m = _adaln(cond, w_mod[i], b_mod[i])
        mx = [m[:B, j * D:(j + 1) * D].reshape(B, 1, D) for j in range(6)]
        mc = [m[B:B + 1, j * D:(j + 1) * D].reshape(1, 1, D) for j in range(6)]
        return mx, mc

    (sh1, sc1, g1, sh2, sc2, g2), (csh1, csc1, cg1, csh2, csc2, cg2) = mods(0)
    gmat = _group_dft_matrix()
    w_in0 = bf(fh_w_in[0])
    w_out0 = bf(fh_w_out[0])
    ffn0 = (bf(ffn_w_up[0]), ffn_conv_w[0], ffn_conv_b[0], bf(ffn_w_down[0]))
    filt = (hy_filt_w1[0], hy_filt_b1[0], hy_filt_w2[0], hy_filt_b2[0], hy_filt_w3[0], hy_filt_b3[0],
            hy_filt_w4[0], hy_freq[0])

    def mixer_layer(h, n_seq, m1, m2):
        s1, c1, gt1 = m1
        s2, c2, gt2 = m2
        taps, norms = _hyena_filter(n_seq, *filt)
        y_f, y_h = _fourier_hyena_mixer(h, norm1[0], s1, c1, w_in0, hy_conv_w[0], hy_conv_b[0], gmat, taps,
                                        norms, hy_bias[0])
        h = _proj_residual(h, gt1, w_out0, [y_f, y_h])
        return _conv_ffn(h, norm2[0], s2, c2, gt2, *ffn0)

    x = mixer_layer(x, L, (sh1, sc1, g1), (sh2, sc2, g2))
    ctx = mixer_layer(ctx, Lc, (csh1, csc1, cg1), (csh2, csc2, cg2))

    (sh1, sc1, g1, sh2, sc2, g2), (csh1, csc1, _, _, _, _) = mods(1)
    w_in1 = bf(jnp.pad(mla_w_in[0], ((0, 0), (0, MLA_IN_PAD - mla_w_in.shape[2]))))
    wuq = bf(_pad_heads(mla_w_uq[0], QK_DIM, QK_DIM).T)
    wuk = bf(_pad_heads(mla_w_ukv[0], NOPE + V_DIM, NOPE).T)
    wuv = bf(mla_w_ukv[0].reshape(KV_LORA, MLA_HEADS, NOPE + V_DIM)[:, :, NOPE:].reshape(KV_LORA, MLA_HEADS * V_DIM).T)
    q_gain = mla_q_norm[0] * (QK_DIM ** -0.5 * math.log2(math.e))
    wts = (w_in1, mla_q_a_norm[0].reshape(1, Q_LORA), wuq, _pad_lanes(q_gain, HEAD_PAD),
           mla_kv_a_norm[0].reshape(1, KV_LORA), wuk, wuv, _pad_lanes(mla_k_norm[0], HEAD_PAD))
    qt, kx, vtx = _mla_proj(x, norm1[1], sh1, sc1, wts, _rope_tables(L), True)
    kc, vtc = _mla_proj(ctx, norm1[1], csh1, csc1, wts, _identity_rope_tables(Lc), False)
    o = _attention(qt, kx, kc, vtx, vtc)
    x = _proj_residual(x, g1, bf(mla_w_o[0]), [o])
    ffn1 = (bf(ffn_w_up[1]), ffn_conv_w[1], ffn_conv_b[1], bf(ffn_w_down[1]))
    return _conv_ffn(x, norm2[1], sh2, sc2, g2, *ffn1)
```

```python
import functools
import math

import jax
import jax.numpy as jnp
from jax import lax
from jax.experimental import pallas as pl
from jax.experimental.pallas import tpu as pltpu

F32 = jnp.float32
BF16 = jnp.bfloat16
HIGHEST = lax.Precision.HIGHEST

EPS = 1e-6
D_MODEL = 1024
D_FF = 2816
D_FOURIER = 512
FOURIER_GROUP_DIM = 128
D_HYENA = 512
HYENA_EMB_DIM = 33
HYENA_EMB_PAD = 64
HYENA_BANDS = 16
HYENA_WIDTH = 64
HYENA_MIN_DECAY = math.log(1e-2) / 0.3
HYENA_MAX_DECAY = math.log(1e-2) / 1.5
MLA_HEADS = 16
Q_LORA = 256
KV_LORA = 128
NOPE = 64
ROPE = 32
QK_DIM = NOPE + ROPE
V_DIM = 64
HEAD_PAD = 128
MLA_IN_PAD = 512
GRID_W = 64
ROPE_THETA = 10000.0

FFN_CHUNK = 256
FFT_RADIX = 128
FFT_KRON = 16
HALO = 16
VMEM_LIMIT = 56 * 1024 * 1024


def _cparams(sem):
    return pltpu.CompilerParams(dimension_semantics=sem, vmem_limit_bytes=VMEM_LIMIT)


def _dot(a, b):
    return jnp.dot(a, b, preferred_element_type=F32)


def _norm_mod(x, g, shift, scale):
    ms = jnp.mean(x * x, axis=-1, keepdims=True)
    return (x * lax.rsqrt(ms + EPS) * g) * (1.0 + scale) + shift


def _silu(x):
    return x * (1.0 / (1.0 + jnp.exp(-x)))


def _row_tile(L, want):
    t = min(L, want)
    assert L % t == 0 and t % HALO == 0
    return t


def _mod_kernel(c_ref, w_ref, b_ref, o_ref):
    s = _silu(c_ref[...])
    o_ref[...] = jnp.dot(s, w_ref[...], precision=HIGHEST, preferred_element_type=F32) + b_ref[...]


def _adaln(cond, w_mod, b_mod):
    R, D = cond.shape
    n = w_mod.shape[1]
    tn = 768
    return pl.pallas_call(
        _mod_kernel,
        out_shape=jax.ShapeDtypeStruct((R, n), F32),
        grid=(n // tn,),
        in_specs=[pl.BlockSpec((R, D), lambda j: (0, 0)),
                  pl.BlockSpec((D, tn), lambda j: (0, j)),
                  pl.BlockSpec((1, tn), lambda j: (0, j))],
        out_specs=pl.BlockSpec((R, tn), lambda j: (0, j)),
        compiler_params=_cparams(("arbitrary",)),
        name="adaln_mod",
    )(cond, w_mod, b_mod.reshape(1, n))


def _halo_specs(tm, D, L):
    nb = L // HALO
    per = tm // HALO
    cur = pl.BlockSpec((1, tm, D), lambda b, i, *_: (b, i, 0))
    prev = pl.BlockSpec((1, HALO, D), lambda b, i, *_: (b, jnp.maximum(i * per - 1, 0), 0))
    nxt = pl.BlockSpec((1, HALO, D), lambda b, i, *_: (b, jnp.minimum((i + 1) * per, nb - 1), 0))
    return cur, prev, nxt


def _mod_spec(arr):
    D = arr.shape[-1]
    if arr.shape[0] == 1:
        return pl.BlockSpec((1, 1, D), lambda b, *_: (0, 0, 0))
    return pl.BlockSpec((1, 1, D), lambda b, *_: (b, 0, 0))


def _fill_hn(hn_ref, xc_ref, xp_ref, xn_ref, g, shift, scale, tm, i, nt):
    hp = _norm_mod(xp_ref[0], g, shift, scale)
    hn_ref[0:HALO, :] = jnp.where(i > 0, hp, 0.0).astype(BF16)
    hn_ref[HALO:HALO + tm, :] = _norm_mod(xc_ref[0], g, shift, scale).astype(BF16)
    hx = _norm_mod(xn_ref[0], g, shift, scale)
    hn_ref[HALO + tm:, :] = jnp.where(i < nt - 1, hx, 0.0).astype(BF16)


def _dwconv3_rows(p, w, tm):
    n = tm + 2 * HALO
    up = pltpu.roll(p, 1, 0)
    dn = pltpu.roll(p, n - 1, 0)
    c = up * w[0:1] + p * w[1:2] + dn * w[2:3]
    return c[HALO:HALO + tm]


def _fh_in_kernel(xc_ref, xp_ref, xn_ref, g_ref, sh_ref, sc_ref, w_ref, cw_ref, cb_ref, gm_ref, *rest, tm, n_hi):
    if n_hi:
        perm_ref, z_ref, u_ref, x0_ref, hn_ref = rest
    else:
        z_ref, u_ref, x0_ref, hn_ref = rest
    i = pl.program_id(1)
    nt = pl.num_programs(1)
    _fill_hn(hn_ref, xc_ref, xp_ref, xn_ref, g_ref[...], sh_ref[0], sc_ref[0], tm, i, nt)
    proj = _dot(hn_ref[...], w_ref[...])
    uf = proj[HALO:HALO + tm, :D_FOURIER].astype(BF16)
    if n_hi:
        uf = _dot(perm_ref[...], uf).astype(BF16)
        z = _dot(uf, gm_ref[...]).astype(BF16)
        z_ref[0] = z.reshape(n_hi, tm // n_hi, 2 * D_FOURIER)
    else:
        z_ref[0] = _dot(uf, gm_ref[...]).astype(BF16)
    c = _dwconv3_rows(proj[:, D_FOURIER:], cw_ref[...], tm) + cb_ref[...]
    x0 = c[:, :D_HYENA]
    x1 = c[:, D_HYENA:2 * D_HYENA]
    v = c[:, 2 * D_HYENA:]
    x0_ref[0] = x0
    u_ref[0] = (v * x1).astype(BF16)


def _fh_in(x, g, shift, scale, w_in, conv_w, conv_b, gmat, n_hi=0):
    B, L, D = x.shape
    tm = _row_tile(L, 512)
    cur, prev, nxt = _halo_specs(tm, D, L)
    n_in = w_in.shape[1]
    nh = 3 * D_HYENA
    full = lambda shape: pl.BlockSpec(shape, lambda b, i: (0,) * len(shape))
    out_tile = lambda c: pl.BlockSpec((1, tm, c), lambda b, i: (b, i, 0))
    in_specs = [cur, prev, nxt, full((1, D)), _mod_spec(shift), _mod_spec(scale),
                full((D, n_in)), full((3, nh)), full((1, nh)), full((D_FOURIER, 2 * D_FOURIER))]
    args = [x, x, x, g.reshape(1, D), shift, scale, w_in, conv_w, conv_b.reshape(1, nh), gmat]
    if n_hi:
        per = tm // n_hi
        r = jnp.arange(tm, dtype=jnp.int32)
        src = n_hi * (r % per) + r // per
        args.append((src[:, None] == r[None, :]).astype(BF16))
        in_specs.append(full((tm, tm)))
        z_shape = jax.ShapeDtypeStruct((B, n_hi, L // n_hi, 2 * D_FOURIER), BF16)
        z_spec = pl.BlockSpec((1, n_hi, per, 2 * D_FOURIER), lambda b, i: (b, 0, i, 0))
    else:
        z_shape = jax.ShapeDtypeStruct((B, L, 2 * D_FOURIER), BF16)
        z_spec = out_tile(2 * D_FOURIER)
    return pl.pallas_call(
        functools.partial(_fh_in_kernel, tm=tm, n_hi=n_hi),
        out_shape=(z_shape, jax.ShapeDtypeStruct((B, L, D_HYENA), BF16), jax.ShapeDtypeStruct((B, L, D_HYENA), F32)),
        grid=(B, L // tm),
        in_specs=in_specs,
        out_specs=(z_spec, out_tile(D_HYENA), out_tile(D_HYENA)),
        scratch_shapes=[pltpu.VMEM((tm + 2 * HALO, D), BF16)],
        compiler_params=_cparams(("parallel", "arbitrary")),
        name="fh_in",
    )(*args)


def _fourier_kernel(c_ref, s_ref, z_ref, o_ref, *, scale):
    z = z_ref[0]
    y = _dot(c_ref[...], z[:, :D_FOURIER]) + _dot(s_ref[...], z[:, D_FOURIER:])
    o_ref[0] = (y * scale).astype(BF16)


def _fourier_seq(z, cl, sl):
    B, L, _ = z.shape
    tf = _row_tile(L, 512)
    scale = 1.0 / math.sqrt(L * FOURIER_GROUP_DIM)
    return pl.pallas_call(
        functools.partial(_fourier_kernel, scale=scale),
        out_shape=jax.ShapeDtypeStruct((B, L, D_FOURIER), BF16),
        grid=(B, L // tf),
        in_specs=[pl.BlockSpec((tf, L), lambda b, i: (i, 0)),
                  pl.BlockSpec((tf, L), lambda b, i: (i, 0)),
                  pl.BlockSpec((1, L, 2 * D_FOURIER), lambda b, i: (b, 0, 0))],
        out_specs=pl.BlockSpec((1, tf, D_FOURIER), lambda b, i: (b, i, 0)),
        compiler_params=_cparams(("parallel", "arbitrary")),
        name="fourier_seq",
    )(cl, sl, z)


def _fft_f1_kernel(cs_ref, z_ref, o_ref, *, nc):
    for j in range(nc):
        cs = _dot(cs_ref[...], z_ref[0, j])
        cz = cs[:FFT_RADIX]
        sz = cs[FFT_RADIX:]
        o_ref[0, 0, j] = (cz[:, :D_FOURIER] + sz[:, D_FOURIER:]).astype(BF16)
        o_ref[0, 1, j] = (cz[:, D_FOURIER:] - sz[:, :D_FOURIER]).astype(BF16)


def _fft_f2_kernel(t_ref, b_ref, o_ref, *, scale):
    n_hi = b_ref.shape[2]
    blk = b_ref[0].reshape(2 * n_hi * 8, D_FOURIER)
    y = _dot(t_ref[0], blk) * scale
    o_ref[0, :, 0] = y.reshape(n_hi, 8, D_FOURIER).astype(BF16)


def _fourier_seq_fft(zp, cs, tmat):
    B, n_hi, _, _ = zp.shape
    L = n_hi * FFT_RADIX
    nc = 8
    ng = FFT_RADIX // 8
    bc = pl.pallas_call(
        functools.partial(_fft_f1_kernel, nc=nc),
        out_shape=jax.ShapeDtypeStruct((B, 2, n_hi, FFT_RADIX, D_FOURIER), BF16),
        grid=(B, n_hi // nc),
        in_specs=[pl.BlockSpec((2 * FFT_RADIX, FFT_RADIX), lambda b, i: (0, 0)),
                  pl.BlockSpec((1, nc, FFT_RADIX, 2 * D_FOURIER), lambda b, i: (b, i, 0, 0))],
        out_specs=pl.BlockSpec((1, 2, nc, FFT_RADIX, D_FOURIER), lambda b, i: (b, 0, i, 0, 0)),
        compiler_params=_cparams(("parallel", "arbitrary")),
        name="fourier_fft1",
    )(cs, zp)
    scale = 1.0 / math.sqrt(L * FOURIER_GROUP_DIM)
    y = pl.pallas_call(
        functools.partial(_fft_f2_kernel, scale=scale),
        out_shape=jax.ShapeDtypeStruct((B, n_hi, ng, 8, D_FOURIER), BF16),
        grid=(B, ng),
        in_specs=[pl.BlockSpec((1, n_hi * 8, 2 * n_hi * 8), lambda b, g: (g, 0, 0)),
                  pl.BlockSpec((1, 2, n_hi, 8, D_FOURIER), lambda b, g: (b, 0, 0, g, 0))],
        out_specs=pl.BlockSpec((1, n_hi, 1, 8, D_FOURIER), lambda b, g: (b, 0, g, 0, 0)),
        compiler_params=_cparams(("parallel", "arbitrary")),
        name="fourier_fft2",
    )(tmat, bc)
    return y.reshape(B, L, D_FOURIER)


def _filter_kernel(z_ref, w1_ref, b1_ref, w2_ref, b2_ref, w3_ref, b3_ref, w4_ref, f_ref, dl_ref,
                   k_ref, n_ref):
    i = pl.program_id(1)
    z = z_ref[0]
    f = f_ref[...]
    hd = lambda a, w: jnp.dot(a, w, precision=HIGHEST, preferred_element_type=F32)
    h = jnp.sin(f * (hd(z, w1_ref[...]) + b1_ref[...]))
    h = jnp.sin(f * (hd(h, w2_ref[...]) + b2_ref[...]))
    h = jnp.sin(f * (hd(h, w3_ref[...]) + b3_ref[...]))
    h = hd(h, w4_ref[0])
    t = z[:, 0:1]
    valid = z[:, HYENA_EMB_DIM:HYENA_EMB_DIM + 1]
    k = h * jnp.exp(-t * jnp.abs(dl_ref[...])) * valid
    k_ref[0] = k

    @pl.when(i == 0)
    def _():
        n_ref[...] = jnp.zeros_like(n_ref)

    n_ref[0] += jnp.sum(jnp.abs(k), axis=0, keepdims=True)


def _hyena_filter(L, w1, b1, w2, b2, w3, b3, w4, freq):
    pos = jnp.arange(L, dtype=F32)
    bands = jnp.linspace(1e-4, HYENA_BANDS - 1, HYENA_BANDS, dtype=F32)

    def emb(p, valid):
        t = p / max(L - 1, 1)
        ang = (2.0 * math.pi / L) * p[:, None] * bands[None, :]
        pad = jnp.zeros((L, HYENA_EMB_PAD - HYENA_EMB_DIM - 1), F32)
        return jnp.concatenate([t[:, None], jnp.cos(ang), -jnp.sin(ang), valid[:, None], pad], axis=-1)

    zf = emb(pos, jnp.ones((L,), F32))
    zb = emb(jnp.where(pos > 0, L - pos, 0.0), (pos > 0).astype(F32))
    z = jnp.stack([zf, zb])
    w1p = jnp.concatenate([w1, jnp.zeros((HYENA_EMB_PAD - HYENA_EMB_DIM, HYENA_WIDTH), F32)], axis=0)
    w4s = jnp.stack([w4[:, :D_HYENA], w4[:, D_HYENA:]])
    deltas = jnp.linspace(HYENA_MIN_DECAY, HYENA_MAX_DECAY, D_HYENA, dtype=F32).reshape(1, D_HYENA)
    tr = _row_tile(L, 512)
    W = HYENA_WIDTH
    full = lambda shape: pl.BlockSpec(shape, lambda s, i: (0,) * len(shape))
    return pl.pallas_call(
        _filter_kernel,
        out_shape=(jax.ShapeDtypeStruct((2, L, D_HYENA), F32), jax.ShapeDtypeStruct((2, 1, D_HYENA), F32)),
        grid=(2, L // tr),
        in_specs=[pl.BlockSpec((1, tr, HYENA_EMB_PAD), lambda s, i: (s, i, 0)),
                  full((HYENA_EMB_PAD, W)), full((1, W)), full((W, W)), full((1, W)),
                  full((W, W)), full((1, W)),
                  pl.BlockSpec((1, W, D_HYENA), lambda s, i: (s, 0, 0)),
                  full((1, W)), full((1, D_HYENA))],
        out_specs=(pl.BlockSpec((1, tr, D_HYENA), lambda s, i: (s, i, 0)),
                   pl.BlockSpec((1, 1, D_HYENA), lambda s, i: (s, 0, 0))),
        compiler_params=_cparams(("arbitrary", "arbitrary")),
        name="hyena_filter",
    )(z, w1p, b1.reshape(1, W), w2, b2.reshape(1, W), w3, b3.reshape(1, W), w4s, freq.reshape(1, W), deltas)


def _hy_fwd_kernel(c_ref, s_ref, u_ref, *rest, raw, tf):
    u = u_ref[0]
    ure = _dot(c_ref[...], u)
    uim = -_dot(s_ref[...], u)
    if raw:
        re_ref, im_ref = rest
        re_ref[0] = ure
        im_ref[0] = uim
        return
    kre_ref, kim_ref, re_ref, im_ref = rest
    kre = kre_ref[...]
    kim = kim_ref[...]
    row = lax.broadcasted_iota(jnp.int32, (tf, 1), 0) + pl.program_id(1) * tf
    packed = row == 0
    a = uim * kim
    re_ref[0] = (ure * kre - jnp.where(packed, 0.0, a)).astype(BF16)
    im_ref[0] = jnp.where(packed, a, ure * kim + uim * kre).astype(BF16)


def _hy_fwd(u, cb, sbf, kre=None, kim=None):
    B, L, C = u.shape
    tf = _row_tile(L, 512)
    raw = kre is None
    mat = pl.BlockSpec((tf, L), lambda b, i: (i, 0))
    tile = pl.BlockSpec((1, tf, C), lambda b, i: (b, i, 0))
    in_specs = [mat, mat, pl.BlockSpec((1, L, C), lambda b, i: (b, 0, 0))]
    args = [cb, sbf, u]
    if not raw:
        in_specs += [pl.BlockSpec((tf, C), lambda b, i: (i, 0))] * 2
        args += [kre, kim]
    dt = F32 if raw else BF16
    return pl.pallas_call(
        functools.partial(_hy_fwd_kernel, raw=raw, tf=tf),
        out_shape=(jax.ShapeDtypeStruct((B, L, C), dt), jax.ShapeDtypeStruct((B, L, C), dt)),
        grid=(B, L // tf),
        in_specs=in_specs,
        out_specs=(tile, tile),
        compiler_params=_cparams(("parallel", "arbitrary")),
        name="hyena_fwd_raw" if raw else "hyena_fwd",
    )(*args)


def _spec_combine_kernel(re_ref, im_ref, n_ref, kre_ref, kim_ref, *, tf, n_fft):
    row = lax.broadcasted_iota(jnp.int32, (tf, 1), 0) + pl.program_id(0) * tf
    sgn = (1 - 2 * (row & 1)).astype(F32)
    wgt = jnp.where(row == 0, 1.0 / n_fft, 2.0 / n_fft)
    s = wgt / (n_ref[0] + n_ref[1])
    kre_ref[...] = (re_ref[0] + sgn * re_ref[1]) * s
    kim_ref[...] = (im_ref[0] + sgn * im_ref[1]) * s


def _spec_combine(re, im, norms):
    _, L, C = re.shape
    tf = _row_tile(L, 512)
    pair = pl.BlockSpec((2, tf, C), lambda i: (0, i, 0))
    tile = pl.BlockSpec((tf, C), lambda i: (i, 0))
    return pl.pallas_call(
        functools.partial(_spec_combine_kernel, tf=tf, n_fft=2 * L),
        out_shape=(jax.ShapeDtypeStruct((L, C), F32), jax.ShapeDtypeStruct((L, C), F32)),
        grid=(L // tf,),
        in_specs=[pair, pair, pl.BlockSpec((2, 1, C), lambda i: (0, 0, 0))],
        out_specs=(tile, tile),
        compiler_params=_cparams(("arbitrary",)),
        name="hyena_spec_combine",
    )(re, im, norms)


def _hy_inv_kernel(c_ref, s_ref, re_ref, im_ref, x0_ref, u_ref, b_ref, o_ref):
    y = _dot(c_ref[...], re_ref[0]) - _dot(s_ref[...], im_ref[0])
    o_ref[0] = (x0_ref[0] * (y + u_ref[0].astype(F32) * b_ref[...])).astype(BF16)


def _hy_inv(yre, yim, x0, u, bias, cb, sbi):
    B, L, C = u.shape
    tt = _row_tile(L, 512)
    mat = pl.BlockSpec((tt, L), lambda b, i: (i, 0))
    whole = pl.BlockSpec((1, L, C), lambda b, i: (b, 0, 0))
    tile = pl.BlockSpec((1, tt, C), lambda b, i: (b, i, 0))
    return pl.pallas_call(
        _hy_inv_kernel,
        out_shape=jax.ShapeDtypeStruct((B, L, C), BF16),
        grid=(B, L // tt),
        in_specs=[mat, mat, whole, whole, tile, tile, pl.BlockSpec((1, C), lambda b, i: (0, 0))],
        out_specs=tile,
        compiler_params=_cparams(("parallel", "arbitrary")),
        name="hyena_inv",
    )(cb, sbi, yre, yim, x0, u, bias.reshape(1, C))


def _fft_a_kernel(t_ref, x_ref, o_ref):
    r2, kr, c = x_ref.shape[1:]
    y = _dot(t_ref[...], x_ref[0].reshape(r2 * kr, c))
    o_ref[0] = y.reshape(2, y.shape[0] // (2 * kr), kr, c).astype(BF16)


def _hy_fft_a(x4, t1):
    Bx, R, _, C = x4.shape
    n_hi = t1.shape[0] // (2 * FFT_KRON)
    return pl.pallas_call(
        _fft_a_kernel,
        out_shape=jax.ShapeDtypeStruct((Bx, 2, n_hi, FFT_RADIX, C), BF16),
        grid=(Bx, FFT_RADIX // FFT_KRON),
        in_specs=[pl.BlockSpec(t1.shape, lambda b, i: (0, 0)),
                  pl.BlockSpec((1, R, FFT_KRON, C), lambda b, i: (b, 0, i, 0))],
        out_specs=pl.BlockSpec((1, 2, n_hi, FFT_KRON, C), lambda b, i: (b, 0, 0, i, 0)),
        compiler_params=_cparams(("parallel", "arbitrary")),
        name="hyena_fft_a",
    )(t1, x4)


def _fft_b_kernel(gf_ref, a_ref, *rest, raw, n_fft):
    nb = a_ref.shape[0]
    gf = gf_ref[0]
    if raw:
        n_ref, o_ref = rest
        s = 1.0 / (n_fft * (n_ref[0] + n_ref[1]))
    else:
        gi_ref, k_ref, o_ref = rest
        gi = gi_ref[0]
        kr = k_ref[0, 0, 0]
        ki = k_ref[0, 1, 0]
    for b in range(nb):
        a = a_ref[b, :, 0].reshape(2 * FFT_RADIX, D_HYENA)
        x = _dot(gf, a)
        if raw:
            o_ref[b, :, 0] = (x * s).reshape(2, FFT_RADIX, D_HYENA)
            continue
        xr = x[:FFT_RADIX]
        xi = x[FFT_RADIX:]
        y = jnp.concatenate([xr * kr - xi * ki, xr * ki + xi * kr], axis=0).astype(BF16)
        o_ref[b, :, 0] = _dot(gi, y).reshape(2, FFT_RADIX, D_HYENA).astype(BF16)


def _hy_fft_b(a5, gf, gi=None, kspec=None, norms=None):
    Bx, _, n_hi, _, C = a5.shape
    raw = kspec is None
    blk = pl.BlockSpec((Bx, 2, 1, FFT_RADIX, C), lambda k: (0, 0, k, 0, 0))
    tab = pl.BlockSpec((1, 2 * FFT_RADIX, 2 * FFT_RADIX), lambda k: (k, 0, 0))
    if raw:
        in_specs = [tab, blk, pl.BlockSpec((2, 1, C), lambda k: (0, 0, 0))]
        args = (gf, a5, norms)
    else:
        kblk = pl.BlockSpec((1, 2, 1, FFT_RADIX, C), lambda k: (0, 0, k, 0, 0))
        in_specs = [tab, blk, tab, kblk]
        args = (gf, a5, gi, kspec)
    return pl.pallas_call(
        functools.partial(_fft_b_kernel, raw=raw, n_fft=n_hi * FFT_RADIX),
        out_shape=jax.ShapeDtypeStruct(a5.shape, F32 if raw else BF16),
        grid=(n_hi,),
        in_specs=in_specs,
        out_specs=blk,
        compiler_params=_cparams(("arbitrary",)),
        name="hyena_fft_spec" if raw else "hyena_fft_b",
    )(*args)


def _fft_c_kernel(t_ref, v_ref, x0_ref, u_ref, b_ref, o_ref):
    _, n_hi, kr, c = v_ref.shape[1:]
    y = _dot(t_ref[...], v_ref[0].reshape(2 * n_hi * kr, c))
    y = y.reshape(y.shape[0] // kr, kr, c)
    o_ref[0] = (x0_ref[0] * (y + u_ref[0].astype(F32) * b_ref[...])).astype(BF16)


def _hy_fft_c(v5, t3, x0, u, bias):
    B, _, n_hi, _, C = v5.shape
    R = t3.shape[0] // FFT_KRON
    tile = pl.BlockSpec((1, R, FFT_KRON, C), lambda b, i: (b, 0, i, 0))
    return pl.pallas_call(
        _fft_c_kernel,
        out_shape=jax.ShapeDtypeStruct((B, R, FFT_RADIX, C), BF16),
        grid=(B, FFT_RADIX // FFT_KRON),
        in_specs=[pl.BlockSpec(t3.shape, lambda b, i: (0, 0)),
                  pl.BlockSpec((1, 2, n_hi, FFT_KRON, C), lambda b, i: (b, 0, 0, i, 0)),
                  tile, tile, pl.BlockSpec((1, C), lambda b, i: (0, 0))],
        out_specs=tile,
        compiler_params=_cparams(("parallel", "arbitrary")),
        name="hyena_fft_c",
    )(t3, v5, x0, u, bias.reshape(1, C))


def _proj_res_kernel(x_ref, g_ref, w_ref, *rest):
    *y_refs, o_ref = rest
    acc = None
    off = 0
    for y_ref in y_refs:
        c = y_ref.shape[-1]
        t = _dot(y_ref[0], w_ref[off:off + c, :])
        acc = t if acc is None else acc + t
        off += c
    o_ref[0] = x_ref[0] + g_ref[0] * acc


def _proj_residual(x, gate, w, ys):
    B, L, D = x.shape
    tm = _row_tile(L, 512)
    tile = lambda c: pl.BlockSpec((1, tm, c), lambda b, i: (b, i, 0))
    return pl.pallas_call(
        _proj_res_kernel,
        out_shape=jax.ShapeDtypeStruct((B, L, D), F32),
        grid=(B, L // tm),
        in_specs=[tile(D), _mod_spec(gate), pl.BlockSpec(w.shape, lambda b, i: (0, 0))]
                 + [tile(y.shape[-1]) for y in ys],
        out_specs=tile(D),
        compiler_params=_cparams(("parallel", "arbitrary")),
        name="proj_residual",
    )(x, gate, w, *ys)


def _ffn_kernel(xc_ref, xp_ref, xn_ref, g_ref, sh_ref, sc_ref, gate_ref, wu_ref, cw_ref, cb_ref,
                wd_ref, o_ref, hn_ref, act_ref, *, tm):
    i = pl.program_id(1)
    nt = pl.num_programs(1)
    _fill_hn(hn_ref, xc_ref, xp_ref, xn_ref, g_ref[...], sh_ref[0], sc_ref[0], tm, i, nt)
    for c in range(0, D_FF, FFN_CHUNK):
        cols = slice(c, c + FFN_CHUNK)
        gx = _dot(hn_ref[...], wu_ref[:, cols])
        vx = _dot(hn_ref[HALO:HALO + tm, :], wu_ref[:, D_FF + c:D_FF + c + FFN_CHUNK])
        cv = _dwconv3_rows(gx, cw_ref[:, cols], tm) + cb_ref[:, cols]
        act_ref[:, cols] = (_silu(cv) * vx).astype(BF16)
    o_ref[0] = xc_ref[0] + gate_ref[0] * _dot(act_ref[...], wd_ref[...])


def _conv_ffn(x, g, shift, scale, gate, w_up, conv_w, conv_b, w_down):
    B, L, D = x.shape
    tm = _row_tile(L, 512)
    cur, prev, nxt = _halo_specs(tm, D, L)
    const = lambda shape: pl.BlockSpec(shape, lambda b, i: (0,) * len(shape), pipeline_mode=pl.Buffered(1))
    return pl.pallas_call(
        functools.partial(_ffn_kernel, tm=tm),
        out_shape=jax.ShapeDtypeStruct((B, L, D), F32),
        grid=(B, L // tm),
        in_specs=[cur, prev, nxt, const((1, D)), _mod_spec(shift), _mod_spec(scale), _mod_spec(gate),
                  const((D, 2 * D_FF)), const((3, D_FF)), const((1, D_FF)), const((D_FF, D))],
        out_specs=pl.BlockSpec((1, tm, D), lambda b, i: (b, i, 0)),
        scratch_shapes=[pltpu.VMEM((tm + 2 * HALO, D), BF16), pltpu.VMEM((tm, D_FF), BF16)],
        compiler_params=_cparams(("parallel", "arbitrary")),
        name="conv_ffn",
    )(x, x, x, g.reshape(1, D), shift, scale, gate, w_up, conv_w, conv_b.reshape(1, D_FF), w_down)


def _rms(v, g, n):
    return v * lax.rsqrt(jnp.sum(v * v, axis=-1, keepdims=True) * (1.0 / n) + EPS) * g


def _rope_gain_tables(gain, cos, sa, sb):
    rot = slice(NOPE, QK_DIM)
    gr = gain[rot]
    return gain[:NOPE], gr * cos[rot], pltpu.roll(gr, 8, 0) * sa[rot], pltpu.roll(gr, ROPE - 8, 0) * sb[rot]


def _rope_t(yr, gc, gsa, gsb):
    return yr * gc + pltpu.roll(yr, 8, 0) * gsa + pltpu.roll(yr, ROPE - 8, 0) * gsb


def _sumsq(x):
    return jnp.sum(x * x, axis=0, keepdims=True)


def _mla_proj_kernel(x_ref, g_ref, sh_ref, sc_ref, win_ref, qan_ref, wuq_ref, qn_ref, kvn_ref, wuk_ref,
                     wuv_ref, kn_ref, cos_ref, sa_ref, sb_ref, *outs, with_q):
    hn = _norm_mod(x_ref[0], g_ref[...], sh_ref[0], sc_ref[0]).astype(BF16)
    a = _dot(hn, win_ref[...])
    tm = a.shape[0]
    cos = cos_ref[...]
    sa = sa_ref[...]
    sb = sb_ref[...]
    pad = jnp.zeros((HEAD_PAD - QK_DIM, tm), BF16)
    if with_q:
        qt_ref, k_ref, vt_ref = outs
        qat = _rms(a[:, :Q_LORA], qan_ref[...], Q_LORA).T.astype(BF16)
        qft = _dot(wuq_ref[...], qat)
        gn, gc, gsa, gsb = _rope_gain_tables(qn_ref[...], cos, sa, sb)
        for h in range(MLA_HEADS):
            base = h * HEAD_PAD
            xn = qft[base:base + NOPE]
            xr = qft[base + NOPE:base + QK_DIM]
            r = lax.rsqrt((_sumsq(xn) + _sumsq(xr)) * (1.0 / QK_DIM) + EPS)
            qt_ref[0, base:base + NOPE, :] = (xn * r * gn).astype(BF16)
            qt_ref[0, base + NOPE:base + QK_DIM, :] = _rope_t(xr * r, gc, gsa, gsb).astype(BF16)
            qt_ref[0, base + QK_DIM:base + HEAD_PAD, :] = pad
    else:
        k_ref, vt_ref = outs
    ckvt = _rms(a[:, Q_LORA:Q_LORA + KV_LORA], kvn_ref[...], KV_LORA).T.astype(BF16)
    kft = _dot(wuk_ref[...], ckvt)
    vt_ref[0] = _dot(wuv_ref[...], ckvt).astype(BF16)
    kpe = a[:, Q_LORA + KV_LORA:].T[:ROPE]
    gn, gc, gsa, gsb = _rope_gain_tables(kn_ref[...], cos, sa, sb)
    kpe_rot = _rope_t(kpe, gc, gsa, gsb)
    kpe_ss = _sumsq(kpe)
    for h in range(MLA_HEADS):
        base = h * HEAD_PAD
        xn = kft[base:base + NOPE]
        r = lax.rsqrt((_sumsq(xn) + kpe_ss) * (1.0 / QK_DIM) + EPS)
        kt = jnp.concatenate([xn * r * gn, kpe_rot * r, pad.astype(F32)], axis=0)
        k_ref[0, :, base:base + HEAD_PAD] = kt.T.astype(BF16)


def _mla_proj(x, g, shift, scale, wts, rope_tabs, with_q):
    B, L, D = x.shape
    tm = _row_tile(L, 256)
    win, qan, wuq, qn, kvn, wuk, wuv, kn = wts
    cos, sa, sb = rope_tabs
    HP = MLA_HEADS * HEAD_PAD
    HV = MLA_HEADS * V_DIM
    full = lambda a: pl.BlockSpec(a.shape, lambda b, i: (0,) * a.ndim)
    tab = pl.BlockSpec((HEAD_PAD, tm), lambda b, i: (0, i))
    out_shape = [jax.ShapeDtypeStruct((B, L, HP), BF16), jax.ShapeDtypeStruct((B, HV, L), BF16)]
    out_specs = [pl.BlockSpec((1, tm, HP), lambda b, i: (b, i, 0)),
                 pl.BlockSpec((1, HV, tm), lambda b, i: (b, 0, i))]
    if with_q:
        out_shape = [jax.ShapeDtypeStruct((B, HP, L), BF16)] + out_shape
        out_specs = [pl.BlockSpec((1, HP, tm), lambda b, i: (b, 0, i))] + out_specs
    g2 = g.reshape(1, D)
    gain_tab = lambda v: jnp.broadcast_to(v.reshape(HEAD_PAD, 1), (HEAD_PAD, tm))
    qn = gain_tab(qn)
    kn = gain_tab(kn)
    return pl.pallas_call(
        functools.partial(_mla_proj_kernel, with_q=with_q),
        out_shape=tuple(out_shape),
        grid=(B, L // tm),
        in_specs=[pl.BlockSpec((1, tm, D), lambda b, i: (b, i, 0)), full(g2), _mod_spec(shift), _mod_spec(scale),
                  full(win), full(qan), full(wuq), full(qn), full(kvn), full(wuk), full(wuv), full(kn),
                  tab, tab, tab],
        out_specs=tuple(out_specs),
        compiler_params=_cparams(("parallel", "arbitrary")),
        name="mla_proj_q" if with_q else "mla_proj_kv",
    )(x, g2, shift, scale, win, qan, wuq, qn, kvn, wuk, wuv, kn, cos, sa, sb)


def _attn_kernel(qt_ref, kx_ref, kc_ref, vtx_ref, vtc_ref, o_ref, ax_ref, ac_ref, bx_ref, bc_ref):
    t = pl.program_id(0)
    io = (qt_ref, kx_ref, kc_ref, vtx_ref, vtc_ref, o_ref)

    @pl.when(t == 0)
    def _():
        bx_ref[...] = jnp.zeros(bx_ref.shape, F32)
        bc_ref[...] = jnp.zeros(bc_ref.shape, F32)

    @pl.when(t % 2 == 0)
    def _():
        _attn_step(*io, ax_ref, ac_ref, bx_ref, bc_ref)

    @pl.when(t % 2 == 1)
    def _():
        _attn_step(*io, bx_ref, bc_ref, ax_ref, ac_ref)


def _attn_step(qt_ref, kx_ref, kc_ref, vtx_ref, vtc_ref, o_ref, wx_ref, wc_ref, rx_ref, rc_ref):
    tq = qt_ref.shape[2]
    q0 = qt_ref[0, :HEAD_PAD, :]
    q1 = qt_ref[0, HEAD_PAD:, :]
    zq = jnp.zeros_like(q0)
    qbd = jnp.concatenate([jnp.concatenate([q0, zq], axis=1), jnp.concatenate([zq, q1], axis=1)], axis=0)
    wx_ref[...] = _dot(kx_ref[0], qbd)
    wc_ref[...] = _dot(kc_ref[0], qbd)

    sx = rx_ref[...]
    sc = rc_ref[...]
    m = jnp.maximum(jnp.max(sx, axis=0, keepdims=True), jnp.max(sc, axis=0, keepdims=True))
    px = jnp.exp2((sx - m).astype(BF16))
    pc = jnp.exp2((sc - m).astype(BF16))
    vtx = vtx_ref[0]
    vtc = vtc_ref[0]
    halves = []
    for j in range(2):
        own = slice(j * V_DIM, (j + 1) * V_DIM)
        oth = slice((1 - j) * V_DIM, (2 - j) * V_DIM)
        cols = slice(j * tq, (j + 1) * tq)
        parts_x = [vtx[own], jnp.ones_like(vtx[oth])]
        parts_c = [vtc[own], jnp.ones_like(vtc[oth])]
        if j == 1:
            parts_x.reverse()
            parts_c.reverse()
        r = _dot(jnp.concatenate(parts_x, axis=0), px[:, cols]) + _dot(jnp.concatenate(parts_c, axis=0), pc[:, cols])
        halves.append(r[own] * (1.0 / r[oth][0:1]))
    o_ref[0] = jnp.concatenate(halves, axis=0).T.astype(BF16)


def _attention(qt, kx, kc, vtx, vtc):
    B, HP, L = qt.shape
    Lc = kc.shape[1]
    tq = _row_tile(L, 256)
    nhp = MLA_HEADS // 2
    nq = L // tq
    n_tiles = B * nhp * nq

    def tile(t):
        t = jnp.clip(t, 0, n_tiles - 1)
        return t // (nhp * nq), (t // nq) % nhp, t % nq

    def logits_side(f):
        return lambda t: f(*tile(t))

    def value_side(f):
        return lambda t: f(*tile(t - 1))

    return pl.pallas_call(
        _attn_kernel,
        out_shape=jax.ShapeDtypeStruct((B, L, MLA_HEADS * V_DIM), BF16),
        grid=(n_tiles + 1,),
        in_specs=[pl.BlockSpec((1, 2 * HEAD_PAD, tq), logits_side(lambda b, h, i: (b, h, i))),
                  pl.BlockSpec((1, L, 2 * HEAD_PAD), logits_side(lambda b, h, i: (b, 0, h))),
                  pl.BlockSpec((1, Lc, 2 * HEAD_PAD), logits_side(lambda b, h, i: (b, 0, h))),
                  pl.BlockSpec((1, 2 * V_DIM, L), value_side(lambda b, h, i: (b, h, 0))),
                  pl.BlockSpec((1, 2 * V_DIM, Lc), value_side(lambda b, h, i: (b, h, 0)))],
        out_specs=pl.BlockSpec((1, tq, 2 * V_DIM), value_side(lambda b, h, i: (b, i, h))),
        scratch_shapes=[pltpu.VMEM((L, 2 * tq), F32), pltpu.VMEM((Lc, 2 * tq), F32),
                        pltpu.VMEM((L, 2 * tq), F32), pltpu.VMEM((Lc, 2 * tq), F32)],
        compiler_params=_cparams(("arbitrary",)),
        name="mla_attention",
    )(qt, kx, kc, vtx, vtc)


def _trig_matrix(n, period):
    blk = 64
    r = jnp.arange(n, dtype=jnp.int32)[:, None]
    c_hi = jnp.arange(n // blk, dtype=jnp.int32)[None, :] * blk
    c_lo = jnp.arange(blk, dtype=jnp.int32)[None, :]
    w = 2.0 * math.pi / period
    a = ((r * c_hi) % period).astype(F32) * w
    b = ((r * c_lo) % period).astype(F32) * w
    ca, sa, cb, sb = jnp.cos(a), jnp.sin(a), jnp.cos(b), jnp.sin(b)
    cos = (ca[:, :, None] * cb[:, None, :] - sa[:, :, None] * sb[:, None, :]).reshape(n, n)
    sin = (sa[:, :, None] * cb[:, None, :] + ca[:, :, None] * sb[:, None, :]).reshape(n, n)
    return cos, sin


def _dft_tables(L):
    cl, sl = _trig_matrix(L, L)
    cb, sb = _trig_matrix(L, 2 * L)
    alt = (1 - 2 * (jnp.arange(L, dtype=jnp.int32) & 1)).astype(F32)
    idx = jnp.arange(L, dtype=jnp.int32)
    sbf = jnp.where(idx[:, None] == 0, -alt[None, :], sb)
    sbi = jnp.where(idx[None, :] == 0, -alt[:, None], sb)
    return tuple(m.astype(BF16) for m in (cl, sl, cb, sbf, sbi))


def _angle(num, period):
    return (num % period).astype(F32) * (2.0 * math.pi / period)


def _fourier_fft_tables(L):
    n_hi = L // FFT_RADIX
    i = jnp.arange(FFT_RADIX, dtype=jnp.int32)
    a = _angle(i[:, None] * i[None, :], FFT_RADIX)
    cs = jnp.concatenate([jnp.cos(a), jnp.sin(a)], axis=0).astype(BF16)
    ng = FFT_RADIX // 8
    g = jnp.arange(ng, dtype=jnp.int32)[:, None, None, None]
    k1 = jnp.arange(n_hi, dtype=jnp.int32)[None, :, None, None]
    j = jnp.arange(8, dtype=jnp.int32)[None, None, :, None]
    n1 = jnp.arange(n_hi, dtype=jnp.int32)[None, None, None, :]
    ang = _angle(n1 * (FFT_RADIX * k1 + 8 * g + j), L)
    eye = jnp.eye(8, dtype=F32)
    blocks = [t[:, :, :, :, None] * eye[None, None, :, None, :] for t in (jnp.cos(ang), jnp.sin(ang))]
    tmat = jnp.stack(blocks, axis=3)
    return cs, tmat.reshape(ng, n_hi * 8, 2 * n_hi * 8).astype(BF16)


def _hyena_fft_tables(L):
    n_fft = 2 * L
    n_hi = n_fft // FFT_RADIX
    k2 = jnp.arange(n_hi, dtype=jnp.int32)
    a1 = _angle(k2[:, None] * k2[None, :], n_hi)
    c1, s1 = jnp.cos(a1), jnp.sin(a1)
    eye = jnp.eye(FFT_KRON, dtype=F32)
    kron = lambda m: (m[:, None, :, None] * eye[None, :, None, :]).reshape(m.shape[0] * FFT_KRON, m.shape[1] * FFT_KRON)
    t1 = jnp.concatenate([kron(c1), kron(-s1)], axis=0).astype(BF16)
    t1_half = jnp.concatenate([kron(c1[:, :n_hi // 2]), kron(-s1[:, :n_hi // 2])], axis=0).astype(BF16)
    t3 = jnp.concatenate([kron(c1[:n_hi // 2]), kron(-s1[:n_hi // 2])], axis=1).astype(BF16)
    r = jnp.arange(FFT_RADIX, dtype=jnp.int32)
    k = k2[:, None, None] + n_hi * r[None, :, None]
    a2 = _angle(k * r[None, None, :], n_fft)
    c2, s2 = jnp.cos(a2), jnp.sin(a2)
    gf = jnp.concatenate([jnp.concatenate([c2, s2], axis=2), jnp.concatenate([-s2, c2], axis=2)], axis=1)
    c2t, s2t = jnp.swapaxes(c2, 1, 2), jnp.swapaxes(s2, 1, 2)
    gi = jnp.concatenate([jnp.concatenate([c2t, -s2t], axis=2), jnp.concatenate([s2t, c2t], axis=2)], axis=1)
    return t1, t1_half, t3, gf.astype(BF16), gi.astype(BF16)


def _group_dft_matrix():
    g = FOURIER_GROUP_DIM
    ng = D_FOURIER // g
    c, s = _trig_matrix(g, g)
    eye = jnp.eye(ng, dtype=F32)
    return jnp.concatenate([jnp.kron(eye, c), -jnp.kron(eye, s)], axis=1).astype(BF16)


def _rope_tables(L):
    pos = jnp.arange(L, dtype=jnp.int32)
    rows = (pos // GRID_W).astype(F32)
    cols = (pos % GRID_W).astype(F32)
    nf = ROPE // 4
    inv_freq = ROPE_THETA ** (-jnp.arange(nf, dtype=F32) / nf)
    ar = rows[:, None] * inv_freq[None, :]
    ac = cols[:, None] * inv_freq[None, :]
    one = jnp.ones((L, NOPE), F32)
    z_n = jnp.zeros((L, NOPE), F32)
    z_f = jnp.zeros((L, nf), F32)
    tail = jnp.zeros((L, HEAD_PAD - QK_DIM), F32)
    cos = jnp.concatenate([one, jnp.cos(ar), jnp.cos(ar), jnp.cos(ac), jnp.cos(ac), tail], axis=1)
    sa = jnp.concatenate([z_n, z_f, jnp.sin(ar), z_f, jnp.sin(ac), tail], axis=1)
    sb = jnp.concatenate([z_n, -jnp.sin(ar), z_f, -jnp.sin(ac), z_f, tail], axis=1)
    return cos.T, sa.T, sb.T


def _identity_rope_tables(L):
    return jnp.ones((HEAD_PAD, L), F32), jnp.zeros((HEAD_PAD, L), F32), jnp.zeros((HEAD_PAD, L), F32)


def _pad_heads(w, per_head, keep):
    K = w.shape[0]
    w = w.reshape(K, MLA_HEADS, per_head)[:, :, :keep]
    w = jnp.pad(w, ((0, 0), (0, 0), (0, HEAD_PAD - keep)))
    return w.reshape(K, MLA_HEADS * HEAD_PAD)


def _pad_lanes(v, n):
    return jnp.pad(v, (0, n - v.shape[0])).reshape(1, n)


def _fourier_hyena_mixer(x, g, shift, scale, w_in, conv_w, conv_b, gmat, taps, norms, hy_bias):
    B, L, _ = x.shape
    if (L // FFT_RADIX) % 16 == 0:
        z, u, x0 = _fh_in(x, g, shift, scale, w_in, conv_w, conv_b, gmat, n_hi=L // FFT_RADIX)
        cs, tmat = _fourier_fft_tables(L)
        y_f = _fourier_seq_fft(z, cs, tmat)
        t1, t1_half, t3, gf, gi = _hyena_fft_tables(L)
        n_hi = 2 * L // FFT_RADIX
        rows = lambda a, r: a.reshape(a.shape[0], r, FFT_RADIX, D_HYENA)
        kspec = _hy_fft_b(_hy_fft_a(rows(taps.astype(BF16).reshape(1, 2 * L, D_HYENA), n_hi), t1), gf, norms=norms)
        v = _hy_fft_b(_hy_fft_a(rows(u, n_hi // 2), t1_half), gf, gi, kspec)
        y_h = _hy_fft_c(v, t3, rows(x0, n_hi // 2), rows(u, n_hi // 2), hy_bias)
        return y_f, y_h.reshape(B, L, D_HYENA)
    z, u, x0 = _fh_in(x, g, shift, scale, w_in, conv_w, conv_b, gmat)
    cl, sl, cb, sbf, sbi = _dft_tables(L)
    k_re, k_im = _hy_fwd(taps.astype(BF16), cb, sbf)
    kre, kim = _spec_combine(k_re, k_im, norms)
    y_f = _fourier_seq(z, cl, sl)
    yre, yim = _hy_fwd(u, cb, sbf, kre, kim)
    y_h = _hy_inv(yre, yim, x0, u, hy_bias, cb, sbi)
    return y_f, y_h


def kernel(x, c, ctx, c_ctx, norm1, norm2, w_mod, b_mod, ffn_w_up, ffn_conv_w, ffn_conv_b, ffn_w_down,
           fh_w_in, fh_w_out, hy_conv_w, hy_conv_b, hy_filt_w1, hy_filt_b1, hy_filt_w2, hy_filt_b2,
           hy_filt_w3, hy_filt_b3, hy_filt_w4, hy_freq, hy_bias, mla_w_in, mla_q_a_norm, mla_w_uq,
           mla_kv_a_norm, mla_w_ukv, mla_q_norm, mla_k_norm, mla_w_o):
    B, L, D = x.shape
    Lc = ctx.shape[1]
    bf = lambda a: a.astype(BF16)

    n_cond = -(-(B + 1) // 8) * 8
    cond = jnp.concatenate([c, c_ctx[None, :], jnp.zeros((n_cond - B - 1, D), F32)], axis=0)

    def mods(i):
        m = _adaln(cond, w_mod[i], b_mod[i])
        mx = [m[:B, j * D:(j + 1) * D].reshape(B, 1, D) for j in range(6)]
        mc = [m[B:B + 1, j * D:(j + 1) * D].reshape(1, 1, D) for j in range(6)]
        return mx, mc

    (sh1, sc1, g1, sh2, sc2, g2), (csh1, csc1, cg1, csh2, csc2, cg2) = mods(0)
    gmat = _group_dft_matrix()
    w_in0 = bf(fh_w_in[0])
    w_out0 = bf(fh_w_out[0])
    ffn0 = (bf(ffn_w_up[0]), ffn_conv_w[0], ffn_conv_b[0], bf(ffn_w_down[0]))
    filt = (hy_filt_w1[0], hy_filt_b1[0], hy_filt_w2[0], hy_filt_b2[0], hy_filt_w3[0], hy_filt_b3[0],
            hy_filt_w4[0], hy_freq[0])

    def mixer_layer(h, n_seq, m1, m2):
        s1, c1, gt1 = m1
        s2, c2, gt2 = m2
        taps, norms = _hyena_filter(n_seq, *filt)
        y_f, y_h = _fourier_hyena_mixer(h, norm1[0], s1, c1, w_in0, hy_conv_w[0], hy_conv_b[0], gmat, taps,
                                        norms, hy_bias[0])
        h = _proj_residual(h, gt1, w_out0, [y_f, y_h])
        return _conv_ffn(h, norm2[0], s2, c2, gt2, *ffn0)

    x = mixer_layer(x, L, (sh1, sc1, g1), (sh2, sc2, g2))
    ctx = mixer_layer(ctx, Lc, (csh1, csc1, cg1), (csh2, csc2, cg2))

    (sh1, sc1, g1, sh2, sc2, g2), (csh1, csc1, _, _, _, _) = mods(1)
    w_in1 = bf(jnp.pad(mla_w_in[0], ((0, 0), (0, MLA_IN_PAD - mla_w_in.shape[2]))))
    wuq = bf(_pad_heads(mla_w_uq[0], QK_DIM, QK_DIM).T)
    wuk = bf(_pad_heads(mla_w_ukv[0], NOPE + V_DIM, NOPE).T)
    wuv = bf(mla_w_ukv[0].reshape(KV_LORA, MLA_HEADS, NOPE + V_DIM)[:, :, NOPE:].reshape(KV_LORA, MLA_HEADS * V_DIM).T)
    q_gain = mla_q_norm[0] * (QK_DIM ** -0.5 * math.log2(math.e))
    wts = (w_in1, mla_q_a_norm[0].reshape(1, Q_LORA), wuq, _pad_lanes(q_gain, HEAD_PAD),
           mla_kv_a_norm[0].reshape(1, KV_LORA), wuk, wuv, _pad_lanes(mla_k_norm[0], HEAD_PAD))
    qt, kx, vtx = _mla_proj(x, norm1[1], sh1, sc1, wts, _rope_tables(L), True)
    kc, vtc = _mla_proj(ctx, norm1[1], csh1, csc1, wts, _identity_rope_tables(Lc), False)
    o = _attention(qt, kx, kc, vtx, vtc)
    x = _proj_residual(x, g1, bf(mla_w_o[0]), [o])
    ffn1 = (bf(ffn_w_up[1]), ffn_conv_w[1], ffn_conv_b[1], bf(ffn_w_down[1]))
    return _conv_ffn(x, norm2[1], sh2, sc2, g2, *ffn1)
```

```python
import functools
import math

import jax
import jax.numpy as jnp
from jax import lax
from jax.experimental import pallas as pl
from jax.experimental.pallas import tpu as pltpu

F32 = jnp.float32
BF16 = jnp.bfloat16
HIGHEST = lax.Precision.HIGHEST

EPS = 1e-6
D_MODEL = 1024
D_FF = 2816
D_FOURIER = 512
FOURIER_GROUP_DIM = 128
D_HYENA = 512
HYENA_EMB_DIM = 33
HYENA_EMB_PAD = 64
HYENA_BANDS = 16
HYENA_WIDTH = 64
HYENA_MIN_DECAY = math.log(1e-2) / 0.3
HYENA_MAX_DECAY = math.log(1e-2) / 1.5
MLA_HEADS = 16
Q_LORA = 256
KV_LORA = 128
NOPE = 64
ROPE = 32
QK_DIM = NOPE + ROPE
V_DIM = 64
HEAD_PAD = 128
MLA_IN_PAD = 512
GRID_W = 64
ROPE_THETA = 10000.0

ATTN_DTYPE = jnp.float8_e4m3fn
ATTN_P_SHIFT = 8.0
FFN_CHUNK = 256
FFT_RADIX = 128
FFT_KRON = 16
HALO = 16
VMEM_LIMIT = 56 * 1024 * 1024


def _cparams(sem):
    return pltpu.CompilerParams(dimension_semantics=sem, vmem_limit_bytes=VMEM_LIMIT)


def _dot(a, b):
    return jnp.dot(a, b, preferred_element_type=F32)


def _norm_mod(x, g, shift, scale):
    ms = jnp.mean(x * x, axis=-1, keepdims=True)
    return (x * lax.rsqrt(ms + EPS) * g) * (1.0 + scale) + shift


def _silu(x):
    return x * (1.0 / (1.0 + jnp.exp(-x)))


def _row_tile(L, want):
    t = min(L, want)
    assert L % t == 0 and t % HALO == 0
    return t


def _mod_kernel(c_ref, w_ref, b_ref, o_ref):
    s = _silu(c_ref[...])
    o_ref[...] = jnp.dot(s, w_ref[...], precision=HIGHEST, preferred_element_type=F32) + b_ref[...]


def _adaln(cond, w_mod, b_mod):
    R, D = cond.shape
    n = w_mod.shape[1]
    tn = 768
    return pl.pallas_call(
        _mod_kernel,
        out_shape=jax.ShapeDtypeStruct((R, n), F32),
        grid=(n // tn,),
        in_specs=[pl.BlockSpec((R, D), lambda j: (0, 0)),
                  pl.BlockSpec((D, tn), lambda j: (0, j)),
                  pl.BlockSpec((1, tn), lambda j: (0, j))],
        out_specs=pl.BlockSpec((R, tn), lambda j: (0, j)),
        compiler_params=_cparams(("arbitrary",)),
        name="adaln_mod",
    )(cond, w_mod, b_mod.reshape(1, n))


def _halo_specs(tm, D, L):
    nb = L // HALO
    per = tm // HALO
    cur = pl.BlockSpec((1, tm, D), lambda b, i, *_: (b, i, 0))
    prev = pl.BlockSpec((1, HALO, D), lambda b, i, *_: (b, jnp.maximum(i * per - 1, 0), 0))
    nxt = pl.BlockSpec((1, HALO, D), lambda b, i, *_: (b, jnp.minimum((i + 1) * per, nb - 1), 0))
    return cur, prev, nxt


def _mod_spec(arr):
    D = arr.shape[-1]
    if arr.shape[0] == 1:
        return pl.BlockSpec((1, 1, D), lambda b, *_: (0, 0, 0))
    return pl.BlockSpec((1, 1, D), lambda b, *_: (b, 0, 0))


def _fill_hn(hn_ref, xc_ref, xp_ref, xn_ref, g, shift, scale, tm, i, nt):
    hp = _norm_mod(xp_ref[0], g, shift, scale)
    hn_ref[0:HALO, :] = jnp.where(i > 0, hp, 0.0).astype(BF16)
    hn_ref[HALO:HALO + tm, :] = _norm_mod(xc_ref[0], g, shift, scale).astype(BF16)
    hx = _norm_mod(xn_ref[0], g, shift, scale)
    hn_ref[HALO + tm:, :] = jnp.where(i < nt - 1, hx, 0.0).astype(BF16)


def _dwconv3_rows(p, w, tm):
    n = tm + 2 * HALO
    up = pltpu.roll(p, 1, 0)
    dn = pltpu.roll(p, n - 1, 0)
    c = up * w[0:1] + p * w[1:2] + dn * w[2:3]
    return c[HALO:HALO + tm]


def _fh_in_kernel(xc_ref, xp_ref, xn_ref, g_ref, sh_ref, sc_ref, w_ref, cw_ref, cb_ref, gm_ref, *rest, tm, n_hi):
    if n_hi:
        perm_ref, z_ref, u_ref, x0_ref, hn_ref = rest
    else:
        z_ref, u_ref, x0_ref, hn_ref = rest
    i = pl.program_id(1)
    nt = pl.num_programs(1)
    _fill_hn(hn_ref, xc_ref, xp_ref, xn_ref, g_ref[...], sh_ref[0], sc_ref[0], tm, i, nt)
    proj = _dot(hn_ref[...], w_ref[...])
    uf = proj[HALO:HALO + tm, :D_FOURIER].astype(BF16)
    if n_hi:
        uf = _dot(perm_ref[...], uf).astype(BF16)
        z = _dot(uf, gm_ref[...]).astype(BF16)
        z_ref[0] = z.reshape(n_hi, tm // n_hi, 2 * D_FOURIER)
    else:
        z_ref[0] = _dot(uf, gm_ref[...]).astype(BF16)
    c = _dwconv3_rows(proj[:, D_FOURIER:], cw_ref[...], tm) + cb_ref[...]
    x0 = c[:, :D_HYENA]
    x1 = c[:, D_HYENA:2 * D_HYENA]
    v = c[:, 2 * D_HYENA:]
    x0_ref[0] = x0
    u_ref[0] = (v * x1).astype(BF16)


def _fh_in(x, g, shift, scale, w_in, conv_w, conv_b, gmat, n_hi=0):
    B, L, D = x.shape
    tm = _row_tile(L, 512)
    cur, prev, nxt = _halo_specs(tm, D, L)
    n_in = w_in.shape[1]
    nh = 3 * D_HYENA
    full = lambda shape: pl.BlockSpec(shape, lambda b, i: (0,) * len(shape))
    out_tile = lambda c: pl.BlockSpec((1, tm, c), lambda b, i: (b, i, 0))
    in_specs = [cur, prev, nxt, full((1, D)), _mod_spec(shift), _mod_spec(scale),
                full((D, n_in)), full((3, nh)), full((1, nh)), full((D_FOURIER, 2 * D_FOURIER))]
    args = [x, x, x, g.reshape(1, D), shift, scale, w_in, conv_w, conv_b.reshape(1, nh), gmat]
    if n_hi:
        per = tm // n_hi
        r = jnp.arange(tm, dtype=jnp.int32)
        src = n_hi * (r % per) + r // per
        args.append((src[:, None] == r[None, :]).astype(BF16))
        in_specs.append(full((tm, tm)))
        z_shape = jax.ShapeDtypeStruct((B, n_hi, L // n_hi, 2 * D_FOURIER), BF16)
        z_spec = pl.BlockSpec((1, n_hi, per, 2 * D_FOURIER), lambda b, i: (b, 0, i, 0))
    else:
        z_shape = jax.ShapeDtypeStruct((B, L, 2 * D_FOURIER), BF16)
        z_spec = out_tile(2 * D_FOURIER)
    return pl.pallas_call(
        functools.partial(_fh_in_kernel, tm=tm, n_hi=n_hi),
        out_shape=(z_shape, jax.ShapeDtypeStruct((B, L, D_HYENA), BF16), jax.ShapeDtypeStruct((B, L, D_HYENA), F32)),
        grid=(B, L // tm),
        in_specs=in_specs,
        out_specs=(z_spec, out_tile(D_HYENA), out_tile(D_HYENA)),
        scratch_shapes=[pltpu.VMEM((tm + 2 * HALO, D), BF16)],
        compiler_params=_cparams(("parallel", "arbitrary")),
        name="fh_in",
    )(*args)


def _fourier_kernel(c_ref, s_ref, z_ref, o_ref, *, scale):
    z = z_ref[0]
    y = _dot(c_ref[...], z[:, :D_FOURIER]) + _dot(s_ref[...], z[:, D_FOURIER:])
    o_ref[0] = (y * scale).astype(BF16)


def _fourier_seq(z, cl, sl):
    B, L, _ = z.shape
    tf = _row_tile(L, 512)
    scale = 1.0 / math.sqrt(L * FOURIER_GROUP_DIM)
    return pl.pallas_call(
        functools.partial(_fourier_kernel, scale=scale),
        out_shape=jax.ShapeDtypeStruct((B, L, D_FOURIER), BF16),
        grid=(B, L // tf),
        in_specs=[pl.BlockSpec((tf, L), lambda b, i: (i, 0)),
                  pl.BlockSpec((tf, L), lambda b, i: (i, 0)),
                  pl.BlockSpec((1, L, 2 * D_FOURIER), lambda b, i: (b, 0, 0))],
        out_specs=pl.BlockSpec((1, tf, D_FOURIER), lambda b, i: (b, i, 0)),
        compiler_params=_cparams(("parallel", "arbitrary")),
        name="fourier_seq",
    )(cl, sl, z)


def _fft_f1_kernel(cs_ref, z_ref, o_ref, *, nc):
    for j in range(nc):
        cs = _dot(cs_ref[...], z_ref[0, j])
        cz = cs[:FFT_RADIX]
        sz = cs[FFT_RADIX:]
        o_ref[0, 0, j] = (cz[:, :D_FOURIER] + sz[:, D_FOURIER:]).astype(BF16)
        o_ref[0, 1, j] = (cz[:, D_FOURIER:] - sz[:, :D_FOURIER]).astype(BF16)


def _fft_f2_kernel(t_ref, b_ref, o_ref, *, scale):
    n_hi = b_ref.shape[2]
    blk = b_ref[0].reshape(2 * n_hi * 8, D_FOURIER)
    y = _dot(t_ref[0], blk) * scale
    o_ref[0, :, 0] = y.reshape(n_hi, 8, D_FOURIER).astype(BF16)


def _fourier_seq_fft(zp, cs, tmat):
    B, n_hi, _, _ = zp.shape
    L = n_hi * FFT_RADIX
    nc = 8
    ng = FFT_RADIX // 8
    bc = pl.pallas_call(
        functools.partial(_fft_f1_kernel, nc=nc),
        out_shape=jax.ShapeDtypeStruct((B, 2, n_hi, FFT_RADIX, D_FOURIER), BF16),
        grid=(B, n_hi // nc),
        in_specs=[pl.BlockSpec((2 * FFT_RADIX, FFT_RADIX), lambda b, i: (0, 0)),
                  pl.BlockSpec((1, nc, FFT_RADIX, 2 * D_FOURIER), lambda b, i: (b, i, 0, 0))],
        out_specs=pl.BlockSpec((1, 2, nc, FFT_RADIX, D_FOURIER), lambda b, i: (b, 0, i, 0, 0)),
        compiler_params=_cparams(("parallel", "arbitrary")),
        name="fourier_fft1",
    )(cs, zp)
    scale = 1.0 / math.sqrt(L * FOURIER_GROUP_DIM)
    y = pl.pallas_call(
        functools.partial(_fft_f2_kernel, scale=scale),
        out_shape=jax.ShapeDtypeStruct((B, n_hi, ng, 8, D_FOURIER), BF16),
        grid=(B, ng),
        in_specs=[pl.BlockSpec((1, n_hi * 8, 2 * n_hi * 8), lambda b, g: (g, 0, 0)),
                  pl.BlockSpec((1, 2, n_hi, 8, D_FOURIER), lambda b, g: (b, 0, 0, g, 0))],
        out_specs=pl.BlockSpec((1, n_hi, 1, 8, D_FOURIER), lambda b, g: (b, 0, g, 0, 0)),
        compiler_params=_cparams(("parallel", "arbitrary")),
        name="fourier_fft2",
    )(tmat, bc)
    return y.reshape(B, L, D_FOURIER)


def _filter_kernel(z_ref, w1_ref, b1_ref, w2_ref, b2_ref, w3_ref, b3_ref, w4_ref, f_ref, dl_ref,
                   k_ref, n_ref):
    i = pl.program_id(1)
    z = z_ref[0]
    f = f_ref[...]
    hd = lambda a, w: jnp.dot(a, w, precision=HIGHEST, preferred_element_type=F32)
    h = jnp.sin(f * (hd(z, w1_ref[...]) + b1_ref[...]))
    h = jnp.sin(f * (hd(h, w2_ref[...]) + b2_ref[...]))
    h = jnp.sin(f * (hd(h, w3_ref[...]) + b3_ref[...]))
    h = hd(h, w4_ref[0])
    t = z[:, 0:1]
    valid = z[:, HYENA_EMB_DIM:HYENA_EMB_DIM + 1]
    k = h * jnp.exp(-t * jnp.abs(dl_ref[...])) * valid
    k_ref[0] = k

    @pl.when(i == 0)
    def _():
        n_ref[...] = jnp.zeros_like(n_ref)

    n_ref[0] += jnp.sum(jnp.abs(k), axis=0, keepdims=True)


def _hyena_filter(L, w1, b1, w2, b2, w3, b3, w4, freq):
    pos = jnp.arange(L, dtype=F32)
    bands = jnp.linspace(1e-4, HYENA_BANDS - 1, HYENA_BANDS, dtype=F32)

    def emb(p, valid):
        t = p / max(L - 1, 1)
        ang = (2.0 * math.pi / L) * p[:, None] * bands[None, :]
        pad = jnp.zeros((L, HYENA_EMB_PAD - HYENA_EMB_DIM - 1), F32)
        return jnp.concatenate([t[:, None], jnp.cos(ang), -jnp.sin(ang), valid[:, None], pad], axis=-1)

    zf = emb(pos, jnp.ones((L,), F32))
    zb = emb(jnp.where(pos > 0, L - pos, 0.0), (pos > 0).astype(F32))
    z = jnp.stack([zf, zb])
    w1p = jnp.concatenate([w1, jnp.zeros((HYENA_EMB_PAD - HYENA_EMB_DIM, HYENA_WIDTH), F32)], axis=0)
    w4s = jnp.stack([w4[:, :D_HYENA], w4[:, D_HYENA:]])
    deltas = jnp.linspace(HYENA_MIN_DECAY, HYENA_MAX_DECAY, D_HYENA, dtype=F32).reshape(1, D_HYENA)
    tr = _row_tile(L, 512)
    W = HYENA_WIDTH
    full = lambda shape: pl.BlockSpec(shape, lambda s, i: (0,) * len(shape))
    return pl.pallas_call(
        _filter_kernel,
        out_shape=(jax.ShapeDtypeStruct((2, L, D_HYENA), F32), jax.ShapeDtypeStruct((2, 1, D_HYENA), F32)),
        grid=(2, L // tr),
        in_specs=[pl.BlockSpec((1, tr, HYENA_EMB_PAD), lambda s, i: (s, i, 0)),
                  full((HYENA_EMB_PAD, W)), full((1, W)), full((W, W)), full((1, W)),
                  full((W, W)), full((1, W)),
                  pl.BlockSpec((1, W, D_HYENA), lambda s, i: (s, 0, 0)),
                  full((1, W)), full((1, D_HYENA))],
        out_specs=(pl.BlockSpec((1, tr, D_HYENA), lambda s, i: (s, i, 0)),
                   pl.BlockSpec((1, 1, D_HYENA), lambda s, i: (s, 0, 0))),
        compiler_params=_cparams(("arbitrary", "arbitrary")),
        name="hyena_filter",
    )(z, w1p, b1.reshape(1, W), w2, b2.reshape(1, W), w3, b3.reshape(1, W), w4s, freq.reshape(1, W), deltas)


def _hy_fwd_kernel(c_ref, s_ref, u_ref, *rest, raw, tf):
    u = u_ref[0]
    ure = _dot(c_ref[...], u)
    uim = -_dot(s_ref[...], u)
    if raw:
        re_ref, im_ref = rest
        re_ref[0] = ure
        im_ref[0] = uim
        return
    kre_ref, kim_ref, re_ref, im_ref = rest
    kre = kre_ref[...]
    kim = kim_ref[...]
    row = lax.broadcasted_iota(jnp.int32, (tf, 1), 0) + pl.program_id(1) * tf
    packed = row == 0
    a = uim * kim
    re_ref[0] = (ure * kre - jnp.where(packed, 0.0, a)).astype(BF16)
    im_ref[0] = jnp.where(packed, a, ure * kim + uim * kre).astype(BF16)


def _hy_fwd(u, cb, sbf, kre=None, kim=None):
    B, L, C = u.shape
    tf = _row_tile(L, 512)
    raw = kre is None
    mat = pl.BlockSpec((tf, L), lambda b, i: (i, 0))
    tile = pl.BlockSpec((1, tf, C), lambda b, i: (b, i, 0))
    in_specs = [mat, mat, pl.BlockSpec((1, L, C), lambda b, i: (b, 0, 0))]
    args = [cb, sbf, u]
    if not raw:
        in_specs += [pl.BlockSpec((tf, C), lambda b, i: (i, 0))] * 2
        args += [kre, kim]
    dt = F32 if raw else BF16
    return pl.pallas_call(
        functools.partial(_hy_fwd_kernel, raw=raw, tf=tf),
        out_shape=(jax.ShapeDtypeStruct((B, L, C), dt), jax.ShapeDtypeStruct((B, L, C), dt)),
        grid=(B, L // tf),
        in_specs=in_specs,
        out_specs=(tile, tile),
        compiler_params=_cparams(("parallel", "arbitrary")),
        name="hyena_fwd_raw" if raw else "hyena_fwd",
    )(*args)


def _spec_combine_kernel(re_ref, im_ref, n_ref, kre_ref, kim_ref, *, tf, n_fft):
    row = lax.broadcasted_iota(jnp.int32, (tf, 1), 0) + pl.program_id(0) * tf
    sgn = (1 - 2 * (row & 1)).astype(F32)
    wgt = jnp.where(row == 0, 1.0 / n_fft, 2.0 / n_fft)
    s = wgt / (n_ref[0] + n_ref[1])
    kre_ref[...] = (re_ref[0] + sgn * re_ref[1]) * s
    kim_ref[...] = (im_ref[0] + sgn * im_ref[1]) * s


def _spec_combine(re, im, norms):
    _, L, C = re.shape
    tf = _row_tile(L, 512)
    pair = pl.BlockSpec((2, tf, C), lambda i: (0, i, 0))
    tile = pl.BlockSpec((tf, C), lambda i: (i, 0))
    return pl.pallas_call(
        functools.partial(_spec_combine_kernel, tf=tf, n_fft=2 * L),
        out_shape=(jax.ShapeDtypeStruct((L, C), F32), jax.ShapeDtypeStruct((L, C), F32)),
        grid=(L // tf,),
        in_specs=[pair, pair, pl.BlockSpec((2, 1, C), lambda i: (0, 0, 0))],
        out_specs=(tile, tile),
        compiler_params=_cparams(("arbitrary",)),
        name="hyena_spec_combine",
    )(re, im, norms)


def _hy_inv_kernel(c_ref, s_ref, re_ref, im_ref, x0_ref, u_ref, b_ref, o_ref):
    y = _dot(c_ref[...], re_ref[0]) - _dot(s_ref[...], im_ref[0])
    o_ref[0] = (x0_ref[0] * (y + u_ref[0].astype(F32) * b_ref[...])).astype(BF16)


def _hy_inv(yre, yim, x0, u, bias, cb, sbi):
    B, L, C = u.shape
    tt = _row_tile(L, 512)
    mat = pl.BlockSpec((tt, L), lambda b, i: (i, 0))
    whole = pl.BlockSpec((1, L, C), lambda b, i: (b, 0, 0))
    tile = pl.BlockSpec((1, tt, C), lambda b, i: (b, i, 0))
    return pl.pallas_call(
        _hy_inv_kernel,
        out_shape=jax.ShapeDtypeStruct((B, L, C), BF16),
        grid=(B, L // tt),
        in_specs=[mat, mat, whole, whole, tile, tile, pl.BlockSpec((1, C), lambda b, i: (0, 0))],
        out_specs=tile,
        compiler_params=_cparams(("parallel", "arbitrary")),
        name="hyena_inv",
    )(cb, sbi, yre, yim, x0, u, bias.reshape(1, C))


def _fft_a_kernel(t_ref, x_ref, o_ref):
    r2, kr, c = x_ref.shape[1:]
    y = _dot(t_ref[...], x_ref[0].reshape(r2 * kr, c))
    o_ref[0] = y.reshape(2, y.shape[0] // (2 * kr), kr, c).astype(BF16)


def _hy_fft_a(x4, t1):
    Bx, R, _, C = x4.shape
    n_hi = t1.shape[0] // (2 * FFT_KRON)
    return pl.pallas_call(
        _fft_a_kernel,
        out_shape=jax.ShapeDtypeStruct((Bx, 2, n_hi, FFT_RADIX, C), BF16),
        grid=(Bx, FFT_RADIX // FFT_KRON),
        in_specs=[pl.BlockSpec(t1.shape, lambda b, i: (0, 0)),
                  pl.BlockSpec((1, R, FFT_KRON, C), lambda b, i: (b, 0, i, 0))],
        out_specs=pl.BlockSpec((1, 2, n_hi, FFT_KRON, C), lambda b, i: (b, 0, 0, i, 0)),
        compiler_params=_cparams(("parallel", "arbitrary")),
        name="hyena_fft_a",
    )(t1, x4)


def _fft_b_kernel(gf_ref, a_ref, *rest, raw, n_fft):
    nb = a_ref.shape[0]
    gf = gf_ref[0]
    if raw:
        n_ref, o_ref = rest
        s = 1.0 / (n_fft * (n_ref[0] + n_ref[1]))
    else:
        gi_ref, k_ref, o_ref = rest
        gi = gi_ref[0]
        kr = k_ref[0, 0, 0]
        ki = k_ref[0, 1, 0]
    for b in range(nb):
        a = a_ref[b, :, 0].reshape(2 * FFT_RADIX, D_HYENA)
        x = _dot(gf, a)
        if raw:
            o_ref[b, :, 0] = (x * s).reshape(2, FFT_RADIX, D_HYENA)
            continue
        xr = x[:FFT_RADIX]
        xi = x[FFT_RADIX:]
        y = jnp.concatenate([xr * kr - xi * ki, xr * ki + xi * kr], axis=0).astype(BF16)
        o_ref[b, :, 0] = _dot(gi, y).reshape(2, FFT_RADIX, D_HYENA).astype(BF16)


def _hy_fft_b(a5, gf, gi=None, kspec=None, norms=None):
    Bx, _, n_hi, _, C = a5.shape
    raw = kspec is None
    blk = pl.BlockSpec((Bx, 2, 1, FFT_RADIX, C), lambda k: (0, 0, k, 0, 0))
    tab = pl.BlockSpec((1, 2 * FFT_RADIX, 2 * FFT_RADIX), lambda k: (k, 0, 0))
    if raw:
        in_specs = [tab, blk, pl.BlockSpec((2, 1, C), lambda k: (0, 0, 0))]
        args = (gf, a5, norms)
    else:
        kblk = pl.BlockSpec((1, 2, 1, FFT_RADIX, C), lambda k: (0, 0, k, 0, 0))
        in_specs = [tab, blk, tab, kblk]
        args = (gf, a5, gi, kspec)
    return pl.pallas_call(
        functools.partial(_fft_b_kernel, raw=raw, n_fft=n_hi * FFT_RADIX),
        out_shape=jax.ShapeDtypeStruct(a5.shape, F32 if raw else BF16),
        grid=(n_hi,),
        in_specs=in_specs,
        out_specs=blk,
        compiler_params=_cparams(("arbitrary",)),
        name="hyena_fft_spec" if raw else "hyena_fft_b",
    )(*args)


def _fft_c_kernel(t_ref, v_ref, x0_ref, u_ref, b_ref, o_ref):
    _, n_hi, kr, c = v_ref.shape[1:]
    y = _dot(t_ref[...], v_ref[0].reshape(2 * n_hi * kr, c))
    y = y.reshape(y.shape[0] // kr, kr, c)
    o_ref[0] = (x0_ref[0] * (y + u_ref[0].astype(F32) * b_ref[...])).astype(BF16)


def _hy_fft_c(v5, t3, x0, u, bias):
    B, _, n_hi, _, C = v5.shape
    R = t3.shape[0] // FFT_KRON
    tile = pl.BlockSpec((1, R, FFT_KRON, C), lambda b, i: (b, 0, i, 0))
    return pl.pallas_call(
        _fft_c_kernel,
        out_shape=jax.ShapeDtypeStruct((B, R, FFT_RADIX, C), BF16),
        grid=(B, FFT_RADIX // FFT_KRON),
        in_specs=[pl.BlockSpec(t3.shape, lambda b, i: (0, 0)),
                  pl.BlockSpec((1, 2, n_hi, FFT_KRON, C), lambda b, i: (b, 0, 0, i, 0)),
                  tile, tile, pl.BlockSpec((1, C), lambda b, i: (0, 0))],
        out_specs=tile,
        compiler_params=_cparams(("parallel", "arbitrary")),
        name="hyena_fft_c",
    )(t3, v5, x0, u, bias.reshape(1, C))


def _proj_res_kernel(x_ref, g_ref, w_ref, *rest):
    *y_refs, o_ref = rest
    acc = None
    off = 0
    for y_ref in y_refs:
        c = y_ref.shape[-1]
        t = _dot(y_ref[0], w_ref[off:off + c, :])
        acc = t if acc is None else acc + t
        off += c
    o_ref[0] = x_ref[0] + g_ref[0] * acc


def _proj_residual(x, gate, w, ys):
    B, L, D = x.shape
    tm = _row_tile(L, 512)
    tile = lambda c: pl.BlockSpec((1, tm, c), lambda b, i: (b, i, 0))
    return pl.pallas_call(
        _proj_res_kernel,
        out_shape=jax.ShapeDtypeStruct((B, L, D), F32),
        grid=(B, L // tm),
        in_specs=[tile(D), _mod_spec(gate), pl.BlockSpec(w.shape, lambda b, i: (0, 0))]
                 + [tile(y.shape[-1]) for y in ys],
        out_specs=tile(D),
        compiler_params=_cparams(("parallel", "arbitrary")),
        name="proj_residual",
    )(x, gate, w, *ys)


def _ffn_kernel(xc_ref, xp_ref, xn_ref, g_ref, sh_ref, sc_ref, gate_ref, wu_ref, cw_ref, cb_ref,
                wd_ref, o_ref, hn_ref, act_ref, *, tm):
    i = pl.program_id(1)
    nt = pl.num_programs(1)
    _fill_hn(hn_ref, xc_ref, xp_ref, xn_ref, g_ref[...], sh_ref[0], sc_ref[0], tm, i, nt)
    for c in range(0, D_FF, FFN_CHUNK):
        cols = slice(c, c + FFN_CHUNK)
        gx = _dot(hn_ref[...], wu_ref[:, cols])
        vx = _dot(hn_ref[HALO:HALO + tm, :], wu_ref[:, D_FF + c:D_FF + c + FFN_CHUNK])
        cv = _dwconv3_rows(gx, cw_ref[:, cols], tm) + cb_ref[:, cols]
        act_ref[:, cols] = (_silu(cv) * vx).astype(BF16)
    o_ref[0] = xc_ref[0] + gate_ref[0] * _dot(act_ref[...], wd_ref[...])


def _conv_ffn(x, g, shift, scale, gate, w_up, conv_w, conv_b, w_down):
    B, L, D = x.shape
    tm = _row_tile(L, 512)
    cur, prev, nxt = _halo_specs(tm, D, L)
    const = lambda shape: pl.BlockSpec(shape, lambda b, i: (0,) * len(shape), pipeline_mode=pl.Buffered(1))
    return pl.pallas_call(
        functools.partial(_ffn_kernel, tm=tm),
        out_shape=jax.ShapeDtypeStruct((B, L, D), F32),
        grid=(B, L // tm),
        in_specs=[cur, prev, nxt, const((1, D)), _mod_spec(shift), _mod_spec(scale), _mod_spec(gate),
                  const((D, 2 * D_FF)), const((3, D_FF)), const((1, D_FF)), const((D_FF, D))],
        out_specs=pl.BlockSpec((1, tm, D), lambda b, i: (b, i, 0)),
        scratch_shapes=[pltpu.VMEM((tm + 2 * HALO, D), BF16), pltpu.VMEM((tm, D_FF), BF16)],
        compiler_params=_cparams(("parallel", "arbitrary")),
        name="conv_ffn",
    )(x, x, x, g.reshape(1, D), shift, scale, gate, w_up, conv_w, conv_b.reshape(1, D_FF), w_down)


def _rms(v, g, n):
    return v * lax.rsqrt(jnp.sum(v * v, axis=-1, keepdims=True) * (1.0 / n) + EPS) * g


def _rope_gain_tables(gain, cos, sa, sb):
    rot = slice(NOPE, QK_DIM)
    gr = gain[rot]
    return gain[:NOPE], gr * cos[rot], pltpu.roll(gr, 8, 0) * sa[rot], pltpu.roll(gr, ROPE - 8, 0) * sb[rot]


def _rope_t(yr, gc, gsa, gsb):
    return yr * gc + pltpu.roll(yr, 8, 0) * gsa + pltpu.roll(yr, ROPE - 8, 0) * gsb


def _sumsq(x):
    return jnp.sum(x * x, axis=0, keepdims=True)


def _mla_proj_kernel(x_ref, g_ref, sh_ref, sc_ref, win_ref, qan_ref, wuq_ref, qn_ref, kvn_ref, wuk_ref,
                     wuv_ref, kn_ref, cos_ref, sa_ref, sb_ref, *outs, with_q):
    hn = _norm_mod(x_ref[0], g_ref[...], sh_ref[0], sc_ref[0]).astype(BF16)
    a = _dot(hn, win_ref[...])
    tm = a.shape[0]
    cos = cos_ref[...]
    sa = sa_ref[...]
    sb = sb_ref[...]
    pad = jnp.zeros((HEAD_PAD - QK_DIM, tm), ATTN_DTYPE)
    if with_q:
        qt_ref, k_ref, vt_ref = outs
        qat = _rms(a[:, :Q_LORA], qan_ref[...], Q_LORA).T.astype(BF16)
        qft = _dot(wuq_ref[...], qat)
        gn, gc, gsa, gsb = _rope_gain_tables(qn_ref[...], cos, sa, sb)
        for h in range(MLA_HEADS):
            base = h * HEAD_PAD
            xn = qft[base:base + NOPE]
            xr = qft[base + NOPE:base + QK_DIM]
            r = lax.rsqrt((_sumsq(xn) + _sumsq(xr)) * (1.0 / QK_DIM) + EPS)
            qt_ref[0, base:base + NOPE, :] = (xn * r * gn).astype(ATTN_DTYPE)
            qt_ref[0, base + NOPE:base + QK_DIM, :] = _rope_t(xr * r, gc, gsa, gsb).astype(ATTN_DTYPE)
            qt_ref[0, base + QK_DIM:base + HEAD_PAD, :] = pad
    else:
        k_ref, vt_ref = outs
    ckvt = _rms(a[:, Q_LORA:Q_LORA + KV_LORA], kvn_ref[...], KV_LORA).T.astype(BF16)
    kft = _dot(wuk_ref[...], ckvt)
    vt_ref[0] = _dot(wuv_ref[...], ckvt).astype(ATTN_DTYPE)
    kpe = a[:, Q_LORA + KV_LORA:].T[:ROPE]
    gn, gc, gsa, gsb = _rope_gain_tables(kn_ref[...], cos, sa, sb)
    kpe_rot = _rope_t(kpe, gc, gsa, gsb)
    kpe_ss = _sumsq(kpe)
    for h in range(MLA_HEADS):
        base = h * HEAD_PAD
        xn = kft[base:base + NOPE]
        r = lax.rsqrt((_sumsq(xn) + kpe_ss) * (1.0 / QK_DIM) + EPS)
        kt = jnp.concatenate([xn * r * gn, kpe_rot * r, jnp.zeros((HEAD_PAD - QK_DIM, tm), F32)], axis=0)
        k_ref[0, :, base:base + HEAD_PAD] = kt.T.astype(ATTN_DTYPE)


def _mla_proj(x, g, shift, scale, wts, rope_tabs, with_q):
    B, L, D = x.shape
    tm = _row_tile(L, 256)
    win, qan, wuq, qn, kvn, wuk, wuv, kn = wts
    cos, sa, sb = rope_tabs
    HP = MLA_HEADS * HEAD_PAD
    HV = MLA_HEADS * V_DIM
    full = lambda a: pl.BlockSpec(a.shape, lambda b, i: (0,) * a.ndim)
    tab = pl.BlockSpec((HEAD_PAD, tm), lambda b, i: (0, i))
    out_shape = [jax.ShapeDtypeStruct((B, L, HP), ATTN_DTYPE), jax.ShapeDtypeStruct((B, HV, L), ATTN_DTYPE)]
    out_specs = [pl.BlockSpec((1, tm, HP), lambda b, i: (b, i, 0)),
                 pl.BlockSpec((1, HV, tm), lambda b, i: (b, 0, i))]
    if with_q:
        out_shape = [jax.ShapeDtypeStruct((B, HP, L), ATTN_DTYPE)] + out_shape
        out_specs = [pl.BlockSpec((1, HP, tm), lambda b, i: (b, 0, i))] + out_specs
    g2 = g.reshape(1, D)
    gain_tab = lambda v: jnp.broadcast_to(v.reshape(HEAD_PAD, 1), (HEAD_PAD, tm))
    qn = gain_tab(qn)
    kn = gain_tab(kn)
    return pl.pallas_call(
        functools.partial(_mla_proj_kernel, with_q=with_q),
        out_shape=tuple(out_shape),
        grid=(B, L // tm),
        in_specs=[pl.BlockSpec((1, tm, D), lambda b, i: (b, i, 0)), full(g2), _mod_spec(shift), _mod_spec(scale),
                  full(win), full(qan), full(wuq), full(qn), full(kvn), full(wuk), full(wuv), full(kn),
                  tab, tab, tab],
        out_specs=tuple(out_specs),
        compiler_params=_cparams(("parallel", "arbitrary")),
        name="mla_proj_q" if with_q else "mla_proj_kv",
    )(x, g2, shift, scale, win, qan, wuq, qn, kvn, wuk, wuv, kn, cos, sa, sb)


def _attn_kernel(qt_ref, kx_ref, kc_ref, vtx_ref, vtc_ref, o_ref, ax_ref, ac_ref, am_ref, bx_ref, bc_ref, bm_ref):
    t = pl.program_id(0)
    io = (qt_ref, kx_ref, kc_ref, vtx_ref, vtc_ref, o_ref)

    @pl.when(t == 0)
    def _():
        bx_ref[...] = jnp.zeros(bx_ref.shape, F32)
        bc_ref[...] = jnp.zeros(bc_ref.shape, F32)
        bm_ref[...] = jnp.zeros(bm_ref.shape, F32)

    @pl.when(t % 2 == 0)
    def _():
        _attn_step(*io, (ax_ref, ac_ref, am_ref), (bx_ref, bc_ref, bm_ref))

    @pl.when(t % 2 == 1)
    def _():
        _attn_step(*io, (bx_ref, bc_ref, bm_ref), (ax_ref, ac_ref, am_ref))


def _attn_step(qt_ref, kx_ref, kc_ref, vtx_ref, vtc_ref, o_ref, parked, ready):
    wx_ref, wc_ref, wm_ref = parked
    rx_ref, rc_ref, rm_ref = ready
    tq = qt_ref.shape[2]
    q0 = qt_ref[0, :HEAD_PAD, :]
    q1 = qt_ref[0, HEAD_PAD:, :]
    zq = jnp.zeros_like(q0)
    qbd = jnp.concatenate([jnp.concatenate([q0, zq], axis=1), jnp.concatenate([zq, q1], axis=1)], axis=0)
    nx = _dot(kx_ref[0], qbd)
    nc = _dot(kc_ref[0], qbd)
    wx_ref[...] = nx
    wc_ref[...] = nc
    wm_ref[...] = jnp.maximum(jnp.max(nx, axis=0, keepdims=True), jnp.max(nc, axis=0, keepdims=True))

    sx = rx_ref[...]
    sc = rc_ref[...]
    m = rm_ref[...]
    px = jnp.exp2((sx - (m - ATTN_P_SHIFT)).astype(BF16)).astype(ATTN_DTYPE)
    pc = jnp.exp2((sc - (m - ATTN_P_SHIFT)).astype(BF16)).astype(ATTN_DTYPE)
    vtx = vtx_ref[0]
    vtc = vtc_ref[0]
    halves = []
    for j in range(2):
        own = slice(j * V_DIM, (j + 1) * V_DIM)
        oth = slice((1 - j) * V_DIM, (2 - j) * V_DIM)
        cols = slice(j * tq, (j + 1) * tq)
        parts_x = [vtx[own], jnp.ones_like(vtx[oth])]
        parts_c = [vtc[own], jnp.ones_like(vtc[oth])]
        if j == 1:
            parts_x.reverse()
            parts_c.reverse()
        r = _dot(jnp.concatenate(parts_x, axis=0), px[:, cols]) + _dot(jnp.concatenate(parts_c, axis=0), pc[:, cols])
        halves.append(r[own] * (1.0 / r[oth][0:1]))
    o_ref[0] = jnp.concatenate(halves, axis=0).T.astype(BF16)


def _attention(qt, kx, kc, vtx, vtc):
    B, HP, L = qt.shape
    Lc = kc.shape[1]
    tq = _row_tile(L, 256)
    nhp = MLA_HEADS // 2
    nq = L // tq
    n_tiles = B * nhp * nq

    def tile(t):
        t = jnp.clip(t, 0, n_tiles - 1)
        return t // (nhp * nq), (t // nq) % nhp, t % nq

    def logits_side(f):
        return lambda t: f(*tile(t))

    def value_side(f):
        return lambda t: f(*tile(t - 1))

    return pl.pallas_call(
        _attn_kernel,
        out_shape=jax.ShapeDtypeStruct((B, L, MLA_HEADS * V_DIM), BF16),
        grid=(n_tiles + 1,),
        in_specs=[pl.BlockSpec((1, 2 * HEAD_PAD, tq), logits_side(lambda b, h, i: (b, h, i))),
                  pl.BlockSpec((1, L, 2 * HEAD_PAD), logits_side(lambda b, h, i: (b, 0, h))),
                  pl.BlockSpec((1, Lc, 2 * HEAD_PAD), logits_side(lambda b, h, i: (b, 0, h))),
                  pl.BlockSpec((1, 2 * V_DIM, L), value_side(lambda b, h, i: (b, h, 0))),
                  pl.BlockSpec((1, 2 * V_DIM, Lc), value_side(lambda b, h, i: (b, h, 0)))],
        out_specs=pl.BlockSpec((1, tq, 2 * V_DIM), value_side(lambda b, h, i: (b, i, h))),
        scratch_shapes=[pltpu.VMEM((L, 2 * tq), F32), pltpu.VMEM((Lc, 2 * tq), F32), pltpu.VMEM((1, 2 * tq), F32)] * 2,
        compiler_params=_cparams(("arbitrary",)),
        name="mla_attention",
    )(qt, kx, kc, vtx, vtc)


def _trig_matrix(n, period):
    blk = 64
    r = jnp.arange(n, dtype=jnp.int32)[:, None]
    c_hi = jnp.arange(n // blk, dtype=jnp.int32)[None, :] * blk
    c_lo = jnp.arange(blk, dtype=jnp.int32)[None, :]
    w = 2.0 * math.pi / period
    a = ((r * c_hi) % period).astype(F32) * w
    b = ((r * c_lo) % period).astype(F32) * w
    ca, sa, cb, sb = jnp.cos(a), jnp.sin(a), jnp.cos(b), jnp.sin(b)
    cos = (ca[:, :, None] * cb[:, None, :] - sa[:, :, None] * sb[:, None, :]).reshape(n, n)
    sin = (sa[:, :, None] * cb[:, None, :] + ca[:, :, None] * sb[:, None, :]).reshape(n, n)
    return cos, sin


def _dft_tables(L):
    cl, sl = _trig_matrix(L, L)
    cb, sb = _trig_matrix(L, 2 * L)
    alt = (1 - 2 * (jnp.arange(L, dtype=jnp.int32) & 1)).astype(F32)
    idx = jnp.arange(L, dtype=jnp.int32)
    sbf = jnp.where(idx[:, None] == 0, -alt[None, :], sb)
    sbi = jnp.where(idx[None, :] == 0, -alt[:, None], sb)
    return tuple(m.astype(BF16) for m in (cl, sl, cb, sbf, sbi))


def _angle(num, period):
    return (num % period).astype(F32) * (2.0 * math.pi / period)


def _fourier_fft_tables(L):
    n_hi = L // FFT_RADIX
    i = jnp.arange(FFT_RADIX, dtype=jnp.int32)
    a = _angle(i[:, None] * i[None, :], FFT_RADIX)
    cs = jnp.concatenate([jnp.cos(a), jnp.sin(a)], axis=0).astype(BF16)
    ng = FFT_RADIX // 8
    g = jnp.arange(ng, dtype=jnp.int32)[:, None, None, None]
    k1 = jnp.arange(n_hi, dtype=jnp.int32)[None, :, None, None]
    j = jnp.arange(8, dtype=jnp.int32)[None, None, :, None]
    n1 = jnp.arange(n_hi, dtype=jnp.int32)[None, None, None, :]
    ang = _angle(n1 * (FFT_RADIX * k1 + 8 * g + j), L)
    eye = jnp.eye(8, dtype=F32)
    blocks = [t[:, :, :, :, None] * eye[None, None, :, None, :] for t in (jnp.cos(ang), jnp.sin(ang))]
    tmat = jnp.stack(blocks, axis=3)
    return cs, tmat.reshape(ng, n_hi * 8, 2 * n_hi * 8).astype(BF16)


def _hyena_fft_tables(L):
    n_fft = 2 * L
    n_hi = n_fft // FFT_RADIX
    k2 = jnp.arange(n_hi, dtype=jnp.int32)
    a1 = _angle(k2[:, None] * k2[None, :], n_hi)
    c1, s1 = jnp.cos(a1), jnp.sin(a1)
    eye = jnp.eye(FFT_KRON, dtype=F32)
    kron = lambda m: (m[:, None, :, None] * eye[None, :, None, :]).reshape(m.shape[0] * FFT_KRON, m.shape[1] * FFT_KRON)
    t1 = jnp.concatenate([kron(c1), kron(-s1)], axis=0).astype(BF16)
    t1_half = jnp.concatenate([kron(c1[:, :n_hi // 2]), kron(-s1[:, :n_hi // 2])], axis=0).astype(BF16)
    t3 = jnp.concatenate([kron(c1[:n_hi // 2]), kron(-s1[:n_hi // 2])], axis=1).astype(BF16)
    r = jnp.arange(FFT_RADIX, dtype=jnp.int32)
    k = k2[:, None, None] + n_hi * r[None, :, None]
    a2 = _angle(k * r[None, None, :], n_fft)
    c2, s2 = jnp.cos(a2), jnp.sin(a2)
    gf = jnp.concatenate([jnp.concatenate([c2, s2], axis=2), jnp.concatenate([-s2, c2], axis=2)], axis=1)
    c2t, s2t = jnp.swapaxes(c2, 1, 2), jnp.swapaxes(s2, 1, 2)
    gi = jnp.concatenate([jnp.concatenate([c2t, -s2t], axis=2), jnp.concatenate([s2t, c2t], axis=2)], axis=1)
    return t1, t1_half, t3, gf.astype(BF16), gi.astype(BF16)


def _group_dft_matrix():
    g = FOURIER_GROUP_DIM
    ng = D_FOURIER // g
    c, s = _trig_matrix(g, g)
    eye = jnp.eye(ng, dtype=F32)
    return jnp.concatenate([jnp.kron(eye, c), -jnp.kron(eye, s)], axis=1).astype(BF16)


def _rope_tables(L):
    pos = jnp.arange(L, dtype=jnp.int32)
    rows = (pos // GRID_W).astype(F32)
    cols = (pos % GRID_W).astype(F32)
    nf = ROPE // 4
    inv_freq = ROPE_THETA ** (-jnp.arange(nf, dtype=F32) / nf)
    ar = rows[:, None] * inv_freq[None, :]
    ac = cols[:, None] * inv_freq[None, :]
    one = jnp.ones((L, NOPE), F32)
    z_n = jnp.zeros((L, NOPE), F32)
    z_f = jnp.zeros((L, nf), F32)
    tail = jnp.zeros((L, HEAD_PAD - QK_DIM), F32)
    cos = jnp.concatenate([one, jnp.cos(ar), jnp.cos(ar), jnp.cos(ac), jnp.cos(ac), tail], axis=1)
    sa = jnp.concatenate([z_n, z_f, jnp.sin(ar), z_f, jnp.sin(ac), tail], axis=1)
    sb = jnp.concatenate([z_n, -jnp.sin(ar), z_f, -jnp.sin(ac), z_f, tail], axis=1)
    return cos.T, sa.T, sb.T


def _identity_rope_tables(L):
    return jnp.ones((HEAD_PAD, L), F32), jnp.zeros((HEAD_PAD, L), F32), jnp.zeros((HEAD_PAD, L), F32)


def _pad_heads(w, per_head, keep):
    K = w.shape[0]
    w = w.reshape(K, MLA_HEADS, per_head)[:, :, :keep]
    w = jnp.pad(w, ((0, 0), (0, 0), (0, HEAD_PAD - keep)))
    return w.reshape(K, MLA_HEADS * HEAD_PAD)


def _pad_lanes(v, n):
    return jnp.pad(v, (0, n - v.shape[0])).reshape(1, n)


def _fourier_hyena_mixer(x, g, shift, scale, w_in, conv_w, conv_b, gmat, taps, norms, hy_bias):
    B, L, _ = x.shape
    if (L // FFT_RADIX) % 16 == 0:
        z, u, x0 = _fh_in(x, g, shift, scale, w_in, conv_w, conv_b, gmat, n_hi=L // FFT_RADIX)
        cs, tmat = _fourier_fft_tables(L)
        y_f = _fourier_seq_fft(z, cs, tmat)
        t1, t1_half, t3, gf, gi = _hyena_fft_tables(L)
        n_hi = 2 * L // FFT_RADIX
        rows = lambda a, r: a.reshape(a.shape[0], r, FFT_RADIX, D_HYENA)
        kspec = _hy_fft_b(_hy_fft_a(rows(taps.astype(BF16).reshape(1, 2 * L, D_HYENA), n_hi), t1), gf, norms=norms)
        v = _hy_fft_b(_hy_fft_a(rows(u, n_hi // 2), t1_half), gf, gi, kspec)
        y_h = _hy_fft_c(v, t3, rows(x0, n_hi // 2), rows(u, n_hi // 2), hy_bias)
        return y_f, y_h.reshape(B, L, D_HYENA)
    z, u, x0 = _fh_in(x, g, shift, scale, w_in, conv_w, conv_b, gmat)
    cl, sl, cb, sbf, sbi = _dft_tables(L)
    k_re, k_im = _hy_fwd(taps.astype(BF16), cb, sbf)
    kre, kim = _spec_combine(k_re, k_im, norms)
    y_f = _fourier_seq(z, cl, sl)
    yre, yim = _hy_fwd(u, cb, sbf, kre, kim)
    y_h = _hy_inv(yre, yim, x0, u, hy_bias, cb, sbi)
    return y_f, y_h


def kernel(x, c, ctx, c_ctx, norm1, norm2, w_mod, b_mod, ffn_w_up, ffn_conv_w, ffn_conv_b, ffn_w_down,
           fh_w_in, fh_w_out, hy_conv_w, hy_conv_b, hy_filt_w1, hy_filt_b1, hy_filt_w2, hy_filt_b2,
           hy_filt_w3, hy_filt_b3, hy_filt_w4, hy_freq, hy_bias, mla_w_in, mla_q_a_norm, mla_w_uq,
           mla_kv_a_norm, mla_w_ukv, mla_q_norm, mla_k_norm, mla_w_o):
    B, L, D = x.shape
    Lc = ctx.shape[1]
    bf = lambda a: a.astype(BF16)

    n_cond = -(-(B + 1) // 8) * 8
    cond = jnp.concatenate([c, c_ctx[None, :], jnp.zeros((n_cond - B - 1, D), F32)], axis=0)

    def mods(i):
        m = _adaln(cond, w_mod[i], b_mod[i])
        mx = [m[:B, j * D:(j + 1) * D].reshape(B, 1, D) for j in range(6)]
        mc = [m[B:B + 1, j * D:(j + 1) * D].reshape(1, 1, D) for j in range(6)]
        return mx, mc

    (sh1, sc1, g1, sh2, sc2, g2), (csh1, csc1, cg1, csh2, csc2, cg2) = mods(0)
    gmat = _group_dft_matrix()
    w_in0 = bf(fh_w_in[0])
    w_out0 = bf(fh_w_out[0])
    ffn0 = (bf(ffn_w_up[0]), ffn_conv_w[0], ffn_conv_b[0], bf(ffn_w_down[0]))
    filt = (hy_filt_w1[0], hy_filt_b1[0], hy_filt_w2[0], hy_filt_b2[0], hy_filt_w3[0], hy_filt_b3[0],
            hy_filt_w4[0], hy_freq[0])

    def mixer_layer(h, n_seq, m1, m2):
        s1, c1, gt1 = m1
        s2, c2, gt2 = m2
        taps, norms = _hyena_filter(n_seq, *filt)
        y_f, y_h = _fourier_hyena_mixer(h, norm1[0], s1, c1, w_in0, hy_conv_w[0], hy_conv_b[0], gmat, taps,
                                        norms, hy_bias[0])
        h = _proj_residual(h, gt1, w_out0, [y_f, y_h])
        return _conv_ffn(h, norm2[0], s2, c2, gt2, *ffn0)

    x = mixer_layer(x, L, (sh1, sc1, g1), (sh2, sc2, g2))
    ctx = mixer_layer(ctx, Lc, (csh1, csc1, cg1), (csh2, csc2, cg2))

    (sh1, sc1, g1, sh2, sc2, g2), (csh1, csc1, _, _, _, _) = mods(1)
    w_in1 = bf(jnp.pad(mla_w_in[0], ((0, 0), (0, MLA_IN_PAD - mla_w_in.shape[2]))))
    wuq = bf(_pad_heads(mla_w_uq[0], QK_DIM, QK_DIM).T)
    wuk = bf(_pad_heads(mla_w_ukv[0], NOPE + V_DIM, NOPE).T)
    wuv = bf(mla_w_ukv[0].reshape(KV_LORA, MLA_HEADS, NOPE + V_DIM)[:, :, NOPE:].reshape(KV_LORA, MLA_HEADS * V_DIM).T)
    half_scale = math.sqrt(QK_DIM ** -0.5 * math.log2(math.e))
    wts = (w_in1, mla_q_a_norm[0].reshape(1, Q_LORA), wuq, _pad_lanes(mla_q_norm[0] * half_scale, HEAD_PAD),
           mla_kv_a_norm[0].reshape(1, KV_LORA), wuk, wuv, _pad_lanes(mla_k_norm[0] * half_scale, HEAD_PAD))
    qt, kx, vtx = _mla_proj(x, norm1[1], sh1, sc1, wts, _rope_tables(L), True)
    kc, vtc = _mla_proj(ctx, norm1[1], csh1, csc1, wts, _identity_rope_tables(Lc), False)
    o = _attention(qt, kx, kc, vtx, vtc)
    x = _proj_residual(x, g1, bf(mla_w_o[0]), [o])
    ffn1 = (bf(ffn_w_up[1]), ffn_conv_w[1], ffn_conv_b[1], bf(ffn_w_down[1]))
    return _conv_ffn(x, norm2[1], sh2, sc2, g2, *ffn1)
```

```python
import functools
import math

import jax
import jax.numpy as jnp
import numpy as np
from jax import lax
from jax.experimental import pallas as pl
from jax.experimental.pallas import tpu as pltpu

F32 = jnp.float32
BF16 = jnp.bfloat16
HIGHEST = lax.Precision.HIGHEST

EPS = 1e-6
D_MODEL = 1024
D_FF = 2816
D_FOURIER = 512
FOURIER_GROUP_DIM = 128
D_HYENA = 512
HYENA_EMB_DIM = 33
HYENA_EMB_PAD = 64
HYENA_BANDS = 16
HYENA_WIDTH = 64
HYENA_MIN_DECAY = math.log(1e-2) / 0.3
HYENA_MAX_DECAY = math.log(1e-2) / 1.5
MLA_HEADS = 16
Q_LORA = 256
KV_LORA = 128
NOPE = 64
ROPE = 32
QK_DIM = NOPE + ROPE
V_DIM = 64
HEAD_PAD = 128
MLA_IN_PAD = 512
GRID_W = 64
ROPE_THETA = 10000.0

ATTN_DTYPE = jnp.float8_e4m3fn
ATTN_P_SHIFT = 8.0
FFN_CHUNK = 256
FFT_RADIX = 128
FFT_KRON = 16
HALO = 16
VMEM_LIMIT = 56 * 1024 * 1024


def _cparams(sem, flags=None):
    return pltpu.CompilerParams(dimension_semantics=sem, vmem_limit_bytes=VMEM_LIMIT, flags=flags)


def _dot(a, b):
    return jnp.dot(a, b, preferred_element_type=F32)


def _norm_mod(x, g, shift, scale):
    ms = jnp.mean(x * x, axis=-1, keepdims=True)
    return (x * lax.rsqrt(ms + EPS) * g) * (1.0 + scale) + shift


def _silu(x):
    return x * (1.0 / (1.0 + jnp.exp(-x)))


def _row_tile(L, want):
    t = min(L, want)
    assert L % t == 0 and t % HALO == 0
    return t


def _mod_kernel(c_ref, w_ref, b_ref, o_ref):
    s = _silu(c_ref[...])
    o_ref[...] = jnp.dot(s, w_ref[...], precision=HIGHEST, preferred_element_type=F32) + b_ref[...]


def _adaln(cond, w_mod, b_mod):
    R, D = cond.shape
    n = w_mod.shape[1]
    tn = 768
    return pl.pallas_call(
        _mod_kernel,
        out_shape=jax.ShapeDtypeStruct((R, n), F32),
        grid=(n // tn,),
        in_specs=[pl.BlockSpec((R, D), lambda j: (0, 0)),
                  pl.BlockSpec((D, tn), lambda j: (0, j)),
                  pl.BlockSpec((1, tn), lambda j: (0, j))],
        out_specs=pl.BlockSpec((R, tn), lambda j: (0, j)),
        compiler_params=_cparams(("arbitrary",)),
        name="adaln_mod",
    )(cond, w_mod, b_mod.reshape(1, n))


def _halo_specs(tm, D, L):
    nb = L // HALO
    per = tm // HALO
    cur = pl.BlockSpec((1, tm, D), lambda b, i, *_: (b, i, 0))
    prev = pl.BlockSpec((1, HALO, D), lambda b, i, *_: (b, jnp.maximum(i * per - 1, 0), 0))
    nxt = pl.BlockSpec((1, HALO, D), lambda b, i, *_: (b, jnp.minimum((i + 1) * per, nb - 1), 0))
    return cur, prev, nxt


def _mod_spec(arr):
    D = arr.shape[-1]
    if arr.shape[0] == 1:
        return pl.BlockSpec((1, 1, D), lambda b, *_: (0, 0, 0))
    return pl.BlockSpec((1, 1, D), lambda b, *_: (b, 0, 0))


def _fill_hn_rows(hn_ref, xc, xp, xn, g, shift, scale, tm, i, nt):
    hp = _norm_mod(xp, g, shift, scale)
    hn_ref[0:HALO, :] = jnp.where(i > 0, hp, 0.0).astype(BF16)
    hn_ref[HALO:HALO + tm, :] = _norm_mod(xc, g, shift, scale).astype(BF16)
    hx = _norm_mod(xn, g, shift, scale)
    hn_ref[HALO + tm:, :] = jnp.where(i < nt - 1, hx, 0.0).astype(BF16)


def _fill_hn(hn_ref, xc_ref, xp_ref, xn_ref, g, shift, scale, tm, i, nt):
    _fill_hn_rows(hn_ref, xc_ref[0], xp_ref[0], xn_ref[0], g, shift, scale, tm, i, nt)


def _dwconv3_rows(p, w, tm):
    n = tm + 2 * HALO
    up = pltpu.roll(p, 1, 0)
    dn = pltpu.roll(p, n - 1, 0)
    c = up * w[0:1] + p * w[1:2] + dn * w[2:3]
    return c[HALO:HALO + tm]


def _fh_in_kernel(xc_ref, xp_ref, xn_ref, g_ref, sh_ref, sc_ref, w_ref, cw_ref, cb_ref, gm_ref, *rest, tm, n_hi):
    if n_hi:
        perm_ref, z_ref, u_ref, x0_ref, hn_ref = rest
    else:
        z_ref, u_ref, x0_ref, hn_ref = rest
    i = pl.program_id(1)
    nt = pl.num_programs(1)
    _fill_hn(hn_ref, xc_ref, xp_ref, xn_ref, g_ref[...], sh_ref[0], sc_ref[0], tm, i, nt)
    proj = _dot(hn_ref[...], w_ref[...])
    uf = proj[HALO:HALO + tm, :D_FOURIER].astype(BF16)
    if n_hi:
        uf = _dot(perm_ref[...], uf).astype(BF16)
        z = _dot(uf, gm_ref[...]).astype(BF16)
        z_ref[0] = z.reshape(n_hi, tm // n_hi, 2 * D_FOURIER)
    else:
        z_ref[0] = _dot(uf, gm_ref[...]).astype(BF16)
    c = _dwconv3_rows(proj[:, D_FOURIER:], cw_ref[...], tm) + cb_ref[...]
    x0 = c[:, :D_HYENA]
    x1 = c[:, D_HYENA:2 * D_HYENA]
    v = c[:, 2 * D_HYENA:]
    x0_ref[0] = x0
    u_ref[0] = (v * x1).astype(BF16)


def _fh_in(x, g, shift, scale, w_in, conv_w, conv_b, gmat, n_hi=0):
    B, L, D = x.shape
    tm = _row_tile(L, 512)
    cur, prev, nxt = _halo_specs(tm, D, L)
    n_in = w_in.shape[1]
    nh = 3 * D_HYENA
    full = lambda shape: pl.BlockSpec(shape, lambda b, i: (0,) * len(shape))
    out_tile = lambda c: pl.BlockSpec((1, tm, c), lambda b, i: (b, i, 0))
    in_specs = [cur, prev, nxt, full((1, D)), _mod_spec(shift), _mod_spec(scale),
                full((D, n_in)), full((3, nh)), full((1, nh)), full((D_FOURIER, 2 * D_FOURIER))]
    args = [x, x, x, g.reshape(1, D), shift, scale, w_in, conv_w, conv_b.reshape(1, nh), gmat]
    if n_hi:
        per = tm // n_hi
        r = np.arange(tm)
        src = n_hi * (r % per) + r // per
        args.append(_const(src[:, None] == r[None, :], BF16))
        in_specs.append(full((tm, tm)))
        z_shape = jax.ShapeDtypeStruct((B, n_hi, L // n_hi, 2 * D_FOURIER), BF16)
        z_spec = pl.BlockSpec((1, n_hi, per, 2 * D_FOURIER), lambda b, i: (b, 0, i, 0))
    else:
        z_shape = jax.ShapeDtypeStruct((B, L, 2 * D_FOURIER), BF16)
        z_spec = out_tile(2 * D_FOURIER)
    return pl.pallas_call(
        functools.partial(_fh_in_kernel, tm=tm, n_hi=n_hi),
        out_shape=(z_shape, jax.ShapeDtypeStruct((B, L, D_HYENA), BF16), jax.ShapeDtypeStruct((B, L, D_HYENA), F32)),
        grid=(B, L // tm),
        in_specs=in_specs,
        out_specs=(z_spec, out_tile(D_HYENA), out_tile(D_HYENA)),
        scratch_shapes=[pltpu.VMEM((tm + 2 * HALO, D), BF16)],
        compiler_params=_cparams(("parallel", "arbitrary")),
        name="fh_in",
    )(*args)


def _fourier_kernel(c_ref, s_ref, z_ref, o_ref, *, scale):
    z = z_ref[0]
    y = _dot(c_ref[...], z[:, :D_FOURIER]) + _dot(s_ref[...], z[:, D_FOURIER:])
    o_ref[0] = (y * scale).astype(BF16)


def _fourier_seq(z, cl, sl):
    B, L, _ = z.shape
    tf = _row_tile(L, 512)
    scale = 1.0 / math.sqrt(L * FOURIER_GROUP_DIM)
    return pl.pallas_call(
        functools.partial(_fourier_kernel, scale=scale),
        out_shape=jax.ShapeDtypeStruct((B, L, D_FOURIER), BF16),
        grid=(B, L // tf),
        in_specs=[pl.BlockSpec((tf, L), lambda b, i: (i, 0)),
                  pl.BlockSpec((tf, L), lambda b, i: (i, 0)),
                  pl.BlockSpec((1, L, 2 * D_FOURIER), lambda b, i: (b, 0, 0))],
        out_specs=pl.BlockSpec((1, tf, D_FOURIER), lambda b, i: (b, i, 0)),
        compiler_params=_cparams(("parallel", "arbitrary")),
        name="fourier_seq",
    )(cl, sl, z)


def _fft_f1_kernel(cs_ref, z_ref, o_ref, *, nc):
    for j in range(nc):
        cs = _dot(cs_ref[...], z_ref[0, j])
        cz = cs[:FFT_RADIX]
        sz = cs[FFT_RADIX:]
        o_ref[0, 0, j] = (cz[:, :D_FOURIER] + sz[:, D_FOURIER:]).astype(BF16)
        o_ref[0, 1, j] = (cz[:, D_FOURIER:] - sz[:, :D_FOURIER]).astype(BF16)


def _fft_f2_kernel(t_ref, b_ref, o_ref, *, scale):
    n_hi = b_ref.shape[2]
    blk = b_ref[0].reshape(2 * n_hi * 8, D_FOURIER)
    y = _dot(t_ref[0], blk) * scale
    o_ref[0, :, 0] = y.reshape(n_hi, 8, D_FOURIER).astype(BF16)


def _fourier_seq_fft(zp, cs, tmat):
    B, n_hi, _, _ = zp.shape
    L = n_hi * FFT_RADIX
    nc = 8
    ng = FFT_RADIX // 8
    bc = pl.pallas_call(
        functools.partial(_fft_f1_kernel, nc=nc),
        out_shape=jax.ShapeDtypeStruct((B, 2, n_hi, FFT_RADIX, D_FOURIER), BF16),
        grid=(B, n_hi // nc),
        in_specs=[pl.BlockSpec((2 * FFT_RADIX, FFT_RADIX), lambda b, i: (0, 0)),
                  pl.BlockSpec((1, nc, FFT_RADIX, 2 * D_FOURIER), lambda b, i: (b, i, 0, 0))],
        out_specs=pl.BlockSpec((1, 2, nc, FFT_RADIX, D_FOURIER), lambda b, i: (b, 0, i, 0, 0)),
        compiler_params=_cparams(("parallel", "arbitrary")),
        name="fourier_fft1",
    )(cs, zp)
    scale = 1.0 / math.sqrt(L * FOURIER_GROUP_DIM)
    y = pl.pallas_call(
        functools.partial(_fft_f2_kernel, scale=scale),
        out_shape=jax.ShapeDtypeStruct((B, n_hi, ng, 8, D_FOURIER), BF16),
        grid=(B, ng),
        in_specs=[pl.BlockSpec((1, n_hi * 8, 2 * n_hi * 8), lambda b, g: (g, 0, 0)),
                  pl.BlockSpec((1, 2, n_hi, 8, D_FOURIER), lambda b, g: (b, 0, 0, g, 0))],
        out_specs=pl.BlockSpec((1, n_hi, 1, 8, D_FOURIER), lambda b, g: (b, 0, g, 0, 0)),
        compiler_params=_cparams(("parallel", "arbitrary")),
        name="fourier_fft2",
    )(tmat, bc)
    return y.reshape(B, L, D_FOURIER)


def _filter_kernel(z_ref, w1_ref, b1_ref, w2_ref, b2_ref, w3_ref, b3_ref, w4_ref, f_ref, dl_ref,
                   k_ref, n_ref):
    i = pl.program_id(1)
    z = z_ref[0]
    f = f_ref[...]
    hd = lambda a, w: jnp.dot(a, w, precision=HIGHEST, preferred_element_type=F32)
    h = jnp.sin(f * (hd(z, w1_ref[...]) + b1_ref[...]))
    h = jnp.sin(f * (hd(h, w2_ref[...]) + b2_ref[...]))
    h = jnp.sin(f * (hd(h, w3_ref[...]) + b3_ref[...]))
    h = hd(h, w4_ref[0])
    t = z[:, 0:1]
    valid = z[:, HYENA_EMB_DIM:HYENA_EMB_DIM + 1]
    k = h * jnp.exp(-t * jnp.abs(dl_ref[...])) * valid
    k_ref[0] = k

    @pl.when(i == 0)
    def _():
        n_ref[...] = jnp.zeros_like(n_ref)

    n_ref[0] += jnp.sum(jnp.abs(k), axis=0, keepdims=True)


def _hyena_filter(L, w1, b1, w2, b2, w3, b3, w4, freq):
    pos = np.arange(L, dtype=np.float64)
    bands = np.linspace(1e-4, HYENA_BANDS - 1, HYENA_BANDS)

    def emb(p, valid):
        t = p / max(L - 1, 1)
        ang = (2.0 * math.pi / L) * p[:, None] * bands[None, :]
        pad = np.zeros((L, HYENA_EMB_PAD - HYENA_EMB_DIM - 1))
        return np.concatenate([t[:, None], np.cos(ang), -np.sin(ang), valid[:, None], pad], axis=-1)

    zf = emb(pos, np.ones((L,)))
    zb = emb(np.where(pos > 0, L - pos, 0.0), (pos > 0).astype(np.float64))
    z = _const(np.stack([zf, zb]))
    w1p = jnp.concatenate([w1, jnp.zeros((HYENA_EMB_PAD - HYENA_EMB_DIM, HYENA_WIDTH), F32)], axis=0)
    w4s = jnp.stack([w4[:, :D_HYENA], w4[:, D_HYENA:]])
    deltas = _const(np.linspace(HYENA_MIN_DECAY, HYENA_MAX_DECAY, D_HYENA).reshape(1, D_HYENA))
    tr = _row_tile(L, 512)
    W = HYENA_WIDTH
    full = lambda shape: pl.BlockSpec(shape, lambda s, i: (0,) * len(shape))
    return pl.pallas_call(
        _filter_kernel,
        out_shape=(jax.ShapeDtypeStruct((2, L, D_HYENA), F32), jax.ShapeDtypeStruct((2, 1, D_HYENA), F32)),
        grid=(2, L // tr),
        in_specs=[pl.BlockSpec((1, tr, HYENA_EMB_PAD), lambda s, i: (s, i, 0)),
                  full((HYENA_EMB_PAD, W)), full((1, W)), full((W, W)), full((1, W)),
                  full((W, W)), full((1, W)),
                  pl.BlockSpec((1, W, D_HYENA), lambda s, i: (s, 0, 0)),
                  full((1, W)), full((1, D_HYENA))],
        out_specs=(pl.BlockSpec((1, tr, D_HYENA), lambda s, i: (s, i, 0)),
                   pl.BlockSpec((1, 1, D_HYENA), lambda s, i: (s, 0, 0))),
        compiler_params=_cparams(("arbitrary", "arbitrary")),
        name="hyena_filter",
    )(z, w1p, b1.reshape(1, W), w2, b2.reshape(1, W), w3, b3.reshape(1, W), w4s, freq.reshape(1, W), deltas)


def _hy_fwd_kernel(c_ref, s_ref, u_ref, *rest, raw, tf):
    u = u_ref[0]
    ure = _dot(c_ref[...], u)
    uim = -_dot(s_ref[...], u)
    if raw:
        re_ref, im_ref = rest
        re_ref[0] = ure
        im_ref[0] = uim
        return
    kre_ref, kim_ref, re_ref, im_ref = rest
    kre = kre_ref[...]
    kim = kim_ref[...]
    row = lax.broadcasted_iota(jnp.int32, (tf, 1), 0) + pl.program_id(1) * tf
    packed = row == 0
    a = uim * kim
    re_ref[0] = (ure * kre - jnp.where(packed, 0.0, a)).astype(BF16)
    im_ref[0] = jnp.where(packed, a, ure * kim + uim * kre).astype(BF16)


def _hy_fwd(u, cb, sbf, kre=None, kim=None):
    B, L, C = u.shape
    tf = _row_tile(L, 512)
    raw = kre is None
    mat = pl.BlockSpec((tf, L), lambda b, i: (i, 0))
    tile = pl.BlockSpec((1, tf, C), lambda b, i: (b, i, 0))
    in_specs = [mat, mat, pl.BlockSpec((1, L, C), lambda b, i: (b, 0, 0))]
    args = [cb, sbf, u]
    if not raw:
        in_specs += [pl.BlockSpec((tf, C), lambda b, i: (i, 0))] * 2
        args += [kre, kim]
    dt = F32 if raw else BF16
    return pl.pallas_call(
        functools.partial(_hy_fwd_kernel, raw=raw, tf=tf),
        out_shape=(jax.ShapeDtypeStruct((B, L, C), dt), jax.ShapeDtypeStruct((B, L, C), dt)),
        grid=(B, L // tf),
        in_specs=in_specs,
        out_specs=(tile, tile),
        compiler_params=_cparams(("parallel", "arbitrary")),
        name="hyena_fwd_raw" if raw else "hyena_fwd",
    )(*args)


def _spec_combine_kernel(re_ref, im_ref, n_ref, kre_ref, kim_ref, *, tf, n_fft):
    row = lax.broadcasted_iota(jnp.int32, (tf, 1), 0) + pl.program_id(0) * tf
    sgn = (1 - 2 * (row & 1)).astype(F32)
    wgt = jnp.where(row == 0, 1.0 / n_fft, 2.0 / n_fft)
    s = wgt / (n_ref[0] + n_ref[1])
    kre_ref[...] = (re_ref[0] + sgn * re_ref[1]) * s
    kim_ref[...] = (im_ref[0] + sgn * im_ref[1]) * s


def _spec_combine(re, im, norms):
    _, L, C = re.shape
    tf = _row_tile(L, 512)
    pair = pl.BlockSpec((2, tf, C), lambda i: (0, i, 0))
    tile = pl.BlockSpec((tf, C), lambda i: (i, 0))
    return pl.pallas_call(
        functools.partial(_spec_combine_kernel, tf=tf, n_fft=2 * L),
        out_shape=(jax.ShapeDtypeStruct((L, C), F32), jax.ShapeDtypeStruct((L, C), F32)),
        grid=(L // tf,),
        in_specs=[pair, pair, pl.BlockSpec((2, 1, C), lambda i: (0, 0, 0))],
        out_specs=(tile, tile),
        compiler_params=_cparams(("arbitrary",)),
        name="hyena_spec_combine",
    )(re, im, norms)


def _hy_inv_kernel(c_ref, s_ref, re_ref, im_ref, x0_ref, u_ref, b_ref, o_ref):
    y = _dot(c_ref[...], re_ref[0]) - _dot(s_ref[...], im_ref[0])
    o_ref[0] = (x0_ref[0] * (y + u_ref[0].astype(F32) * b_ref[...])).astype(BF16)


def _hy_inv(yre, yim, x0, u, bias, cb, sbi):
    B, L, C = u.shape
    tt = _row_tile(L, 512)
    mat = pl.BlockSpec((tt, L), lambda b, i: (i, 0))
    whole = pl.BlockSpec((1, L, C), lambda b, i: (b, 0, 0))
    tile = pl.BlockSpec((1, tt, C), lambda b, i: (b, i, 0))
    return pl.pallas_call(
        _hy_inv_kernel,
        out_shape=jax.ShapeDtypeStruct((B, L, C), BF16),
        grid=(B, L // tt),
        in_specs=[mat, mat, whole, whole, tile, tile, pl.BlockSpec((1, C), lambda b, i: (0, 0))],
        out_specs=tile,
        compiler_params=_cparams(("parallel", "arbitrary")),
        name="hyena_inv",
    )(cb, sbi, yre, yim, x0, u, bias.reshape(1, C))


def _fft_a_kernel(t_ref, x_ref, o_ref):
    r2, kr, c = x_ref.shape[1:]
    y = _dot(t_ref[...], x_ref[0].reshape(r2 * kr, c))
    o_ref[0] = y.reshape(2, y.shape[0] // (2 * kr), kr, c).astype(BF16)


def _hy_fft_a(x4, t1):
    Bx, R, _, C = x4.shape
    n_hi = t1.shape[0] // (2 * FFT_KRON)
    return pl.pallas_call(
        _fft_a_kernel,
        out_shape=jax.ShapeDtypeStruct((Bx, 2, n_hi, FFT_RADIX, C), BF16),
        grid=(Bx, FFT_RADIX // FFT_KRON),
        in_specs=[pl.BlockSpec(t1.shape, lambda b, i: (0, 0)),
                  pl.BlockSpec((1, R, FFT_KRON, C), lambda b, i: (b, 0, i, 0))],
        out_specs=pl.BlockSpec((1, 2, n_hi, FFT_KRON, C), lambda b, i: (b, 0, 0, i, 0)),
        compiler_params=_cparams(("parallel", "arbitrary")),
        name="hyena_fft_a",
    )(t1, x4)


def _fft_b_kernel(gf_ref, a_ref, *rest, raw, n_fft):
    nb = a_ref.shape[0]
    gf = gf_ref[0]
    if raw:
        n_ref, o_ref = rest
        s = 1.0 / (n_fft * (n_ref[0] + n_ref[1]))
    else:
        gi_ref, k_ref, o_ref = rest
        gi = gi_ref[0]
        kr = k_ref[0, 0, 0]
        ki = k_ref[0, 1, 0]
    for b in range(nb):
        a = a_ref[b, :, 0].reshape(2 * FFT_RADIX, D_HYENA)
        x = _dot(gf, a)
        if raw:
            o_ref[b, :, 0] = (x * s).reshape(2, FFT_RADIX, D_HYENA)
            continue
        xr = x[:FFT_RADIX]
        xi = x[FFT_RADIX:]
        y = jnp.concatenate([xr * kr - xi * ki, xr * ki + xi * kr], axis=0).astype(BF16)
        o_ref[b, :, 0] = _dot(gi, y).reshape(2, FFT_RADIX, D_HYENA).astype(BF16)


def _hy_fft_b(a5, gf, gi=None, kspec=None, norms=None):
    Bx, _, n_hi, _, C = a5.shape
    raw = kspec is None
    blk = pl.BlockSpec((Bx, 2, 1, FFT_RADIX, C), lambda k: (0, 0, k, 0, 0))
    tab = pl.BlockSpec((1, 2 * FFT_RADIX, 2 * FFT_RADIX), lambda k: (k, 0, 0))
    if raw:
        in_specs = [tab, blk, pl.BlockSpec((2, 1, C), lambda k: (0, 0, 0))]
        args = (gf, a5, norms)
    else:
        kblk = pl.BlockSpec((1, 2, 1, FFT_RADIX, C), lambda k: (0, 0, k, 0, 0))
        in_specs = [tab, blk, tab, kblk]
        args = (gf, a5, gi, kspec)
    return pl.pallas_call(
        functools.partial(_fft_b_kernel, raw=raw, n_fft=n_hi * FFT_RADIX),
        out_shape=jax.ShapeDtypeStruct(a5.shape, F32 if raw else BF16),
        grid=(n_hi,),
        in_specs=in_specs,
        out_specs=blk,
        compiler_params=_cparams(("arbitrary",)),
        name="hyena_fft_spec" if raw else "hyena_fft_b",
    )(*args)


def _fft_c_kernel(t_ref, v_ref, x0_ref, u_ref, b_ref, o_ref):
    _, n_hi, kr, c = v_ref.shape[1:]
    y = _dot(t_ref[...], v_ref[0].reshape(2 * n_hi * kr, c))
    y = y.reshape(y.shape[0] // kr, kr, c)
    o_ref[0] = (x0_ref[0] * (y + u_ref[0].astype(F32) * b_ref[...])).astype(BF16)


def _hy_fft_c(v5, t3, x0, u, bias):
    B, _, n_hi, _, C = v5.shape
    R = t3.shape[0] // FFT_KRON
    tile = pl.BlockSpec((1, R, FFT_KRON, C), lambda b, i: (b, 0, i, 0))
    return pl.pallas_call(
        _fft_c_kernel,
        out_shape=jax.ShapeDtypeStruct((B, R, FFT_RADIX, C), BF16),
        grid=(B, FFT_RADIX // FFT_KRON),
        in_specs=[pl.BlockSpec(t3.shape, lambda b, i: (0, 0)),
                  pl.BlockSpec((1, 2, n_hi, FFT_KRON, C), lambda b, i: (b, 0, 0, i, 0)),
                  tile, tile, pl.BlockSpec((1, C), lambda b, i: (0, 0))],
        out_specs=tile,
        compiler_params=_cparams(("parallel", "arbitrary")),
        name="hyena_fft_c",
    )(t3, v5, x0, u, bias.reshape(1, C))


def _ffn_kernel(xc_ref, xp_ref, xn_ref, g1_ref, wo_ref, g_ref, sh_ref, sc_ref, gate_ref, wu_ref, cw_ref, cb_ref,
                wd_ref, *rest, tm, n_y):
    y_refs = rest[:3 * n_y]
    o_ref, yext_ref, hn_ref, act_ref = rest[3 * n_y:]
    i = pl.program_id(1)
    nt = pl.num_programs(1)
    off = 0
    for k in range(n_y):
        yc_ref, yp_ref, yn_ref = y_refs[3 * k:3 * k + 3]
        c = yc_ref.shape[-1]
        yext_ref[0:HALO, off:off + c] = yp_ref[0].astype(BF16)
        yext_ref[HALO:HALO + tm, off:off + c] = yc_ref[0].astype(BF16)
        yext_ref[HALO + tm:, off:off + c] = yn_ref[0].astype(BF16)
        off += c
    mix = g1_ref[0] * _dot(yext_ref[...], wo_ref[...])
    x1 = xc_ref[0] + mix[HALO:HALO + tm]
    _fill_hn_rows(hn_ref, x1, xp_ref[0] + mix[:HALO], xn_ref[0] + mix[HALO + tm:], g_ref[...], sh_ref[0], sc_ref[0],
                  tm, i, nt)
    for c in range(0, D_FF, FFN_CHUNK):
        cols = slice(c, c + FFN_CHUNK)
        gx = _dot(hn_ref[...], wu_ref[:, cols])
        vx = _dot(hn_ref[HALO:HALO + tm, :], wu_ref[:, D_FF + c:D_FF + c + FFN_CHUNK])
        cv = _dwconv3_rows(gx, cw_ref[:, cols], tm) + cb_ref[:, cols]
        act_ref[:, cols] = (_silu(cv) * vx).astype(BF16)
    o_ref[0] = x1 + gate_ref[0] * _dot(act_ref[...], wd_ref[...])


def _mix_ffn(x, ys, w_o, gate1, g, shift, scale, gate2, w_up, conv_w, conv_b, w_down):
    B, L, D = x.shape
    tm = _row_tile(L, 512)
    cur, prev, nxt = _halo_specs(tm, D, L)
    const = lambda shape: pl.BlockSpec(shape, lambda b, i: (0,) * len(shape), pipeline_mode=pl.Buffered(1))
    y_specs, y_args = [], []
    for y in ys:
        y_specs += list(_halo_specs(tm, y.shape[-1], L))
        y_args += [y, y, y]
    return pl.pallas_call(
        functools.partial(_ffn_kernel, tm=tm, n_y=len(ys)),
        out_shape=jax.ShapeDtypeStruct((B, L, D), F32),
        grid=(B, L // tm),
        in_specs=[cur, prev, nxt, _mod_spec(gate1), const(w_o.shape), const((1, D)), _mod_spec(shift),
                  _mod_spec(scale), _mod_spec(gate2), const((D, 2 * D_FF)), const((3, D_FF)), const((1, D_FF)),
                  const((D_FF, D))] + y_specs,
        out_specs=pl.BlockSpec((1, tm, D), lambda b, i: (b, i, 0)),
        scratch_shapes=[pltpu.VMEM((tm + 2 * HALO, w_o.shape[0]), BF16), pltpu.VMEM((tm + 2 * HALO, D), BF16),
                        pltpu.VMEM((tm, D_FF), BF16)],
        compiler_params=_cparams(("parallel", "arbitrary")),
        name="mix_ffn",
    )(x, x, x, gate1, w_o, g.reshape(1, D), shift, scale, gate2, w_up, conv_w, conv_b.reshape(1, D_FF), w_down,
      *y_args)


def _rms(v, g, n):
    return v * lax.rsqrt(jnp.sum(v * v, axis=-1, keepdims=True) * (1.0 / n) + EPS) * g


def _rope_gain_tables(gain, cos, sa, sb):
    rot = slice(NOPE, QK_DIM)
    gr = gain[rot]
    return gain[:NOPE], gr * cos[rot], pltpu.roll(gr, 8, 0) * sa[rot], pltpu.roll(gr, ROPE - 8, 0) * sb[rot]


def _rope_t(yr, gc, gsa, gsb):
    return yr * gc + pltpu.roll(yr, 8, 0) * gsa + pltpu.roll(yr, ROPE - 8, 0) * gsb


def _sumsq(x):
    return jnp.sum(x * x, axis=0, keepdims=True)


def _mla_proj_kernel(x_ref, g_ref, sh_ref, sc_ref, win_ref, qan_ref, wuq_ref, qn_ref, kvn_ref, wuk_ref,
                     wuv_ref, kn_ref, cos_ref, sa_ref, sb_ref, *outs, with_q):
    hn = _norm_mod(x_ref[0], g_ref[...], sh_ref[0], sc_ref[0]).astype(BF16)
    a = _dot(hn, win_ref[...])
    tm = a.shape[0]
    cos = cos_ref[...]
    sa = sa_ref[...]
    sb = sb_ref[...]
    pad = jnp.zeros((HEAD_PAD - QK_DIM, tm), ATTN_DTYPE)
    if with_q:
        qt_ref, k_ref, vt_ref = outs
        qat = _rms(a[:, :Q_LORA], qan_ref[...], Q_LORA).T.astype(BF16)
        qft = _dot(wuq_ref[...], qat)
        gn, gc, gsa, gsb = _rope_gain_tables(qn_ref[...], cos, sa, sb)
        for h in range(MLA_HEADS):
            base = h * HEAD_PAD
            xn = qft[base:base + NOPE]
            xr = qft[base + NOPE:base + QK_DIM]
            r = lax.rsqrt((_sumsq(xn) + _sumsq(xr)) * (1.0 / QK_DIM) + EPS)
            qt_ref[0, base:base + NOPE, :] = (xn * r * gn).astype(ATTN_DTYPE)
            qt_ref[0, base + NOPE:base + QK_DIM, :] = _rope_t(xr * r, gc, gsa, gsb).astype(ATTN_DTYPE)
            qt_ref[0, base + QK_DIM:base + HEAD_PAD, :] = pad
    else:
        k_ref, vt_ref = outs
    ckvt = _rms(a[:, Q_LORA:Q_LORA + KV_LORA], kvn_ref[...], KV_LORA).T.astype(BF16)
    kft = _dot(wuk_ref[...], ckvt)
    vt_ref[0] = _dot(wuv_ref[...], ckvt).astype(ATTN_DTYPE)
    kpe = a[:, Q_LORA + KV_LORA:].T[:ROPE]
    gn, gc, gsa, gsb = _rope_gain_tables(kn_ref[...], cos, sa, sb)
    kpe_rot = _rope_t(kpe, gc, gsa, gsb)
    kpe_ss = _sumsq(kpe)
    for h in range(MLA_HEADS):
        base = h * HEAD_PAD
        xn = kft[base:base + NOPE]
        r = lax.rsqrt((_sumsq(xn) + kpe_ss) * (1.0 / QK_DIM) + EPS)
        kt = jnp.concatenate([xn * r * gn, kpe_rot * r, jnp.zeros((HEAD_PAD - QK_DIM, tm), F32)], axis=0)
        k_ref[0, :, base:base + HEAD_PAD] = kt.T.astype(ATTN_DTYPE)


def _mla_proj(x, g, shift, scale, wts, rope_tabs, with_q):
    B, L, D = x.shape
    tm = _row_tile(L, 256)
    win, qan, wuq, qn, kvn, wuk, wuv, kn = wts
    cos, sa, sb = rope_tabs
    HP = MLA_HEADS * HEAD_PAD
    HV = MLA_HEADS * V_DIM
    full = lambda a: pl.BlockSpec(a.shape, lambda b, i: (0,) * a.ndim)
    tab = pl.BlockSpec((HEAD_PAD, tm), lambda b, i: (0, i))
    out_shape = [jax.ShapeDtypeStruct((B, L, HP), ATTN_DTYPE), jax.ShapeDtypeStruct((B, HV, L), ATTN_DTYPE)]
    out_specs = [pl.BlockSpec((1, tm, HP), lambda b, i: (b, i, 0)),
                 pl.BlockSpec((1, HV, tm), lambda b, i: (b, 0, i))]
    if with_q:
        out_shape = [jax.ShapeDtypeStruct((B, HP, L), ATTN_DTYPE)] + out_shape
        out_specs = [pl.BlockSpec((1, HP, tm), lambda b, i: (b, 0, i))] + out_specs
    g2 = g.reshape(1, D)
    gain_tab = lambda v: jnp.broadcast_to(v.reshape(HEAD_PAD, 1), (HEAD_PAD, tm))
    qn = gain_tab(qn)
    kn = gain_tab(kn)
    return pl.pallas_call(
        functools.partial(_mla_proj_kernel, with_q=with_q),
        out_shape=tuple(out_shape),
        grid=(B, L // tm),
        in_specs=[pl.BlockSpec((1, tm, D), lambda b, i: (b, i, 0)), full(g2), _mod_spec(shift), _mod_spec(scale),
                  full(win), full(qan), full(wuq), full(qn), full(kvn), full(wuk), full(wuv), full(kn),
                  tab, tab, tab],
        out_specs=tuple(out_specs),
        compiler_params=_cparams(("parallel", "arbitrary")),
        name="mla_proj_q" if with_q else "mla_proj_kv",
    )(x, g2, shift, scale, win, qan, wuq, qn, kvn, wuk, wuv, kn, cos, sa, sb)


def _attn_kernel(qt_ref, kx_ref, kc_ref, vtx_ref, vtc_ref, o_ref, ax_ref, ac_ref, am_ref, bx_ref, bc_ref, bm_ref,
                 px_ref, pc_ref):
    t = pl.program_id(0)
    io = (qt_ref, kx_ref, kc_ref, vtx_ref, vtc_ref, o_ref, px_ref, pc_ref)

    @pl.when(t == 0)
    def _():
        bx_ref[...] = jnp.zeros(bx_ref.shape, BF16)
        bc_ref[...] = jnp.zeros(bc_ref.shape, BF16)
        bm_ref[...] = jnp.zeros(bm_ref.shape, BF16)

    @pl.when(t % 2 == 0)
    def _():
        _attn_step(*io, (ax_ref, ac_ref, am_ref), (bx_ref, bc_ref, bm_ref))

    @pl.when(t % 2 == 1)
    def _():
        _attn_step(*io, (bx_ref, bc_ref, bm_ref), (ax_ref, ac_ref, am_ref))


def _attn_step(qt_ref, kx_ref, kc_ref, vtx_ref, vtc_ref, o_ref, px_ref, pc_ref, parked, ready):
    wx_ref, wc_ref, wm_ref = parked
    rx_ref, rc_ref, rm_ref = ready
    tq = qt_ref.shape[2]
    off = rm_ref[...][None]

    def probs(s_ref):
        n = s_ref.shape[0]
        s = s_ref[...].reshape(n // 16, 16, 2 * tq)
        return jnp.exp2(s - off).reshape(n, 2 * tq).astype(ATTN_DTYPE)

    px_ref[...] = probs(rx_ref)
    pc_ref[...] = probs(rc_ref)

    q0 = qt_ref[0, :HEAD_PAD, :]
    q1 = qt_ref[0, HEAD_PAD:, :]
    zq = jnp.zeros_like(q0)
    qbd = jnp.concatenate([jnp.concatenate([q0, zq], axis=1), jnp.concatenate([zq, q1], axis=1)], axis=0)
    nx = _dot(kx_ref[0], qbd).astype(BF16)
    nc = _dot(kc_ref[0], qbd).astype(BF16)
    wx_ref[...] = nx
    wc_ref[...] = nc
    m = jnp.maximum(jnp.max(nx, axis=0, keepdims=True), jnp.max(nc, axis=0, keepdims=True)).astype(F32)
    wm_ref[...] = jnp.broadcast_to((m - ATTN_P_SHIFT).astype(BF16), wm_ref.shape)

    vtx = vtx_ref[0]
    vtc = vtc_ref[0]
    halves = []
    for j in range(2):
        own = slice(j * V_DIM, (j + 1) * V_DIM)
        oth = slice((1 - j) * V_DIM, (2 - j) * V_DIM)
        cols = slice(j * tq, (j + 1) * tq)
        parts_x = [vtx[own], jnp.ones_like(vtx[oth])]
        parts_c = [vtc[own], jnp.ones_like(vtc[oth])]
        if j == 1:
            parts_x.reverse()
            parts_c.reverse()
        r = (_dot(jnp.concatenate(parts_x, axis=0), px_ref[:, cols])
             + _dot(jnp.concatenate(parts_c, axis=0), pc_ref[:, cols]))
        halves.append(r[own] * (1.0 / r[oth][0:1]))
    o_ref[0] = jnp.concatenate(halves, axis=0).T.astype(BF16)


def _attention(qt, kx, kc, vtx, vtc):
    B, HP, L = qt.shape
    Lc = kc.shape[1]
    tq = _row_tile(L, 256)
    nhp = MLA_HEADS // 2
    nq = L // tq
    n_tiles = B * nhp * nq

    def tile(t):
        t = jnp.clip(t, 0, n_tiles - 1)
        return t // (nhp * nq), (t // nq) % nhp, t % nq

    def logits_side(f):
        return lambda t: f(*tile(t))

    def value_side(f):
        return lambda t: f(*tile(t - 1))

    return pl.pallas_call(
        _attn_kernel,
        out_shape=jax.ShapeDtypeStruct((B, L, MLA_HEADS * V_DIM), BF16),
        grid=(n_tiles + 1,),
        in_specs=[pl.BlockSpec((1, 2 * HEAD_PAD, tq), logits_side(lambda b, h, i: (b, h, i))),
                  pl.BlockSpec((1, L, 2 * HEAD_PAD), logits_side(lambda b, h, i: (b, 0, h))),
                  pl.BlockSpec((1, Lc, 2 * HEAD_PAD), logits_side(lambda b, h, i: (b, 0, h))),
                  pl.BlockSpec((1, 2 * V_DIM, L), value_side(lambda b, h, i: (b, h, 0))),
                  pl.BlockSpec((1, 2 * V_DIM, Lc), value_side(lambda b, h, i: (b, h, 0)))],
        out_specs=pl.BlockSpec((1, tq, 2 * V_DIM), value_side(lambda b, h, i: (b, i, h))),
        scratch_shapes=[pltpu.VMEM((L, 2 * tq), BF16), pltpu.VMEM((Lc, 2 * tq), BF16), pltpu.VMEM((16, 2 * tq), BF16)] * 2
                       + [pltpu.VMEM((L, 2 * tq), ATTN_DTYPE), pltpu.VMEM((Lc, 2 * tq), ATTN_DTYPE)],
        compiler_params=_cparams(("arbitrary",)),
        name="mla_attention",
    )(qt, kx, kc, vtx, vtc)


def _const(a, dtype=F32):
    return jnp.asarray(np.asarray(a, np.float32)).astype(dtype)


def _angle(num, period):
    return (num % period).astype(np.float64) * (2.0 * math.pi / period)


def _trig_matrix(n, period):
    r = np.arange(n, dtype=np.int64)
    a = _angle(r[:, None] * r[None, :], period)
    return np.cos(a), np.sin(a)


def _dft_tables(L):
    cl, sl = _trig_matrix(L, L)
    cb, sb = _trig_matrix(L, 2 * L)
    idx = np.arange(L)
    alt = (1 - 2 * (idx & 1)).astype(np.float64)
    sbf = np.where(idx[:, None] == 0, -alt[None, :], sb)
    sbi = np.where(idx[None, :] == 0, -alt[:, None], sb)
    return tuple(_const(m, BF16) for m in (cl, sl, cb, sbf, sbi))


def _fourier_fft_tables(L):
    n_hi = L // FFT_RADIX
    i = np.arange(FFT_RADIX, dtype=np.int64)
    a = _angle(i[:, None] * i[None, :], FFT_RADIX)
    cs = np.concatenate([np.cos(a), np.sin(a)], axis=0)
    ng = FFT_RADIX // 8
    g = np.arange(ng, dtype=np.int64)[:, None, None, None]
    k1 = np.arange(n_hi, dtype=np.int64)[None, :, None, None]
    j = np.arange(8, dtype=np.int64)[None, None, :, None]
    n1 = np.arange(n_hi, dtype=np.int64)[None, None, None, :]
    ang = _angle(n1 * (FFT_RADIX * k1 + 8 * g + j), L)
    eye = np.eye(8)
    blocks = [t[:, :, :, :, None] * eye[None, None, :, None, :] for t in (np.cos(ang), np.sin(ang))]
    tmat = np.stack(blocks, axis=3)
    return _const(cs, BF16), _const(tmat.reshape(ng, n_hi * 8, 2 * n_hi * 8), BF16)


def _hyena_fft_tables(L):
    n_fft = 2 * L
    n_hi = n_fft // FFT_RADIX
    k2 = np.arange(n_hi, dtype=np.int64)
    a1 = _angle(k2[:, None] * k2[None, :], n_hi)
    c1, s1 = np.cos(a1), np.sin(a1)
    eye = np.eye(FFT_KRON)
    kron = lambda m: (m[:, None, :, None] * eye[None, :, None, :]).reshape(m.shape[0] * FFT_KRON, m.shape[1] * FFT_KRON)
    t1 = _const(np.concatenate([kron(c1), kron(-s1)], axis=0), BF16)
    t1_half = t1[:, :n_hi // 2 * FFT_KRON]
    t3 = t1_half.T
    r = np.arange(FFT_RADIX, dtype=np.int64)
    k = k2[:, None, None] + n_hi * r[None, :, None]
    a2 = _angle(k * r[None, None, :], n_fft)
    c2, s2 = _const(np.cos(a2), BF16), _const(np.sin(a2), BF16)
    gf = jnp.concatenate([jnp.concatenate([c2, s2], axis=2), jnp.concatenate([-s2, c2], axis=2)], axis=1)
    c2t, s2t = jnp.swapaxes(c2, 1, 2), jnp.swapaxes(s2, 1, 2)
    gi = jnp.concatenate([jnp.concatenate([c2t, -s2t], axis=2), jnp.concatenate([s2t, c2t], axis=2)], axis=1)
    return t1, t1_half, t3, gf, gi


def _group_dft_matrix():
    g = FOURIER_GROUP_DIM
    ng = D_FOURIER // g
    c, s = _trig_matrix(g, g)
    eye = np.eye(ng)
    return _const(np.concatenate([np.kron(eye, c), -np.kron(eye, s)], axis=1), BF16)


def _rope_tables(L):
    pos = np.arange(L)
    rows = (pos // GRID_W).astype(np.float64)
    cols = (pos % GRID_W).astype(np.float64)
    nf = ROPE // 4
    inv_freq = ROPE_THETA ** (-np.arange(nf, dtype=np.float64) / nf)
    ar = rows[:, None] * inv_freq[None, :]
    ac = cols[:, None] * inv_freq[None, :]
    one = np.ones((L, NOPE))
    z_n = np.zeros((L, NOPE))
    z_f = np.zeros((L, nf))
    tail = np.zeros((L, HEAD_PAD - QK_DIM))
    cos = np.concatenate([one, np.cos(ar), np.cos(ar), np.cos(ac), np.cos(ac), tail], axis=1)
    sa = np.concatenate([z_n, z_f, np.sin(ar), z_f, np.sin(ac), tail], axis=1)
    sb = np.concatenate([z_n, -np.sin(ar), z_f, -np.sin(ac), z_f, tail], axis=1)
    return _const(cos.T), _const(sa.T), _const(sb.T)


def _identity_rope_tables(L):
    return jnp.ones((HEAD_PAD, L), F32), jnp.zeros((HEAD_PAD, L), F32), jnp.zeros((HEAD_PAD, L), F32)


def _pad_heads(w, per_head, keep):
    K = w.shape[0]
    w = w.reshape(K, MLA_HEADS, per_head)[:, :, :keep]
    w = jnp.pad(w, ((0, 0), (0, 0), (0, HEAD_PAD - keep)))
    return w.reshape(K, MLA_HEADS * HEAD_PAD)


def _pad_lanes(v, n):
    return jnp.pad(v, (0, n - v.shape[0])).reshape(1, n)


def _fourier_hyena_mixer(x, g, shift, scale, w_in, conv_w, conv_b, gmat, taps, norms, hy_bias):
    B, L, _ = x.shape
    if (L // FFT_RADIX) % 16 == 0:
        z, u, x0 = _fh_in(x, g, shift, scale, w_in, conv_w, conv_b, gmat, n_hi=L // FFT_RADIX)
        cs, tmat = _fourier_fft_tables(L)
        y_f = _fourier_seq_fft(z, cs, tmat)
        t1, t1_half, t3, gf, gi = _hyena_fft_tables(L)
        n_hi = 2 * L // FFT_RADIX
        rows = lambda a, r: a.reshape(a.shape[0], r, FFT_RADIX, D_HYENA)
        kspec = _hy_fft_b(_hy_fft_a(rows(taps.astype(BF16).reshape(1, 2 * L, D_HYENA), n_hi), t1), gf, norms=norms)
        v = _hy_fft_b(_hy_fft_a(rows(u, n_hi // 2), t1_half), gf, gi, kspec)
        y_h = _hy_fft_c(v, t3, rows(x0, n_hi // 2), rows(u, n_hi // 2), hy_bias)
        return y_f, y_h.reshape(B, L, D_HYENA)
    z, u, x0 = _fh_in(x, g, shift, scale, w_in, conv_w, conv_b, gmat)
    cl, sl, cb, sbf, sbi = _dft_tables(L)
    k_re, k_im = _hy_fwd(taps.astype(BF16), cb, sbf)
    kre, kim = _spec_combine(k_re, k_im, norms)
    y_f = _fourier_seq(z, cl, sl)
    yre, yim = _hy_fwd(u, cb, sbf, kre, kim)
    y_h = _hy_inv(yre, yim, x0, u, hy_bias, cb, sbi)
    return y_f, y_h


def kernel(x, c, ctx, c_ctx, norm1, norm2, w_mod, b_mod, ffn_w_up, ffn_conv_w, ffn_conv_b, ffn_w_down,
           fh_w_in, fh_w_out, hy_conv_w, hy_conv_b, hy_filt_w1, hy_filt_b1, hy_filt_w2, hy_filt_b2,
           hy_filt_w3, hy_filt_b3, hy_filt_w4, hy_freq, hy_bias, mla_w_in, mla_q_a_norm, mla_w_uq,
           mla_kv_a_norm, mla_w_ukv, mla_q_norm, mla_k_norm, mla_w_o):
    B, L, D = x.shape
    Lc = ctx.shape[1]
    bf = lambda a: a.astype(BF16)

    n_cond = -(-(B + 1) // 8) * 8
    cond = jnp.concatenate([c, c_ctx[None, :], jnp.zeros((n_cond - B - 1, D), F32)], axis=0)

    def mods(i):
        m = _adaln(cond, w_mod[i], b_mod[i])
        mx = [m[:B, j * D:(j + 1) * D].reshape(B, 1, D) for j in range(6)]
        mc = [m[B:B + 1, j * D:(j + 1) * D].reshape(1, 1, D) for j in range(6)]
        return mx, mc

    (sh1, sc1, g1, sh2, sc2, g2), (csh1, csc1, cg1, csh2, csc2, cg2) = mods(0)
    gmat = _group_dft_matrix()
    w_in0 = bf(fh_w_in[0])
    w_out0 = bf(fh_w_out[0])
    ffn0 = (bf(ffn_w_up[0]), ffn_conv_w[0], ffn_conv_b[0], bf(ffn_w_down[0]))
    filt = (hy_filt_w1[0], hy_filt_b1[0], hy_filt_w2[0], hy_filt_b2[0], hy_filt_w3[0], hy_filt_b3[0],
            hy_filt_w4[0], hy_freq[0])

    def mixer_layer(h, n_seq, m1, m2):
        s1, c1, gt1 = m1
        s2, c2, gt2 = m2
        taps, norms = _hyena_filter(n_seq, *filt)
        y_f, y_h = _fourier_hyena_mixer(h, norm1[0], s1, c1, w_in0, hy_conv_w[0], hy_conv_b[0], gmat, taps,
                                        norms, hy_bias[0])
        return _mix_ffn(h, [y_f, y_h], w_out0, gt1, norm2[0], s2, c2, gt2, *ffn0)

    x = mixer_layer(x, L, (sh1, sc1, g1), (sh2, sc2, g2))
    ctx = mixer_layer(ctx, Lc, (csh1, csc1, cg1), (csh2, csc2, cg2))

    (sh1, sc1, g1, sh2, sc2, g2), (csh1, csc1, _, _, _, _) = mods(1)
    w_in1 = bf(jnp.pad(mla_w_in[0], ((0, 0), (0, MLA_IN_PAD - mla_w_in.shape[2]))))
    wuq = bf(_pad_heads(mla_w_uq[0], QK_DIM, QK_DIM).T)
    wuk = bf(_pad_heads(mla_w_ukv[0], NOPE + V_DIM, NOPE).T)
    wuv = bf(mla_w_ukv[0].reshape(KV_LORA, MLA_HEADS, NOPE + V_DIM)[:, :, NOPE:].reshape(KV_LORA, MLA_HEADS * V_DIM).T)
    half_scale = math.sqrt(QK_DIM ** -0.5 * math.log2(math.e))
    wts = (w_in1, mla_q_a_norm[0].reshape(1, Q_LORA), wuq, _pad_lanes(mla_q_norm[0] * half_scale, HEAD_PAD),
           mla_kv_a_norm[0].reshape(1, KV_LORA), wuk, wuv, _pad_lanes(mla_k_norm[0] * half_scale, HEAD_PAD))
    qt, kx, vtx = _mla_proj(x, norm1[1], sh1, sc1, wts, _rope_tables(L), True)
    kc, vtc = _mla_proj(ctx, norm1[1], csh1, csc1, wts, _identity_rope_tables(Lc), False)
    o = _attention(qt, kx, kc, vtx, vtc)
    ffn1 = (bf(ffn_w_up[1]), ffn_conv_w[1], ffn_conv_b[1], bf(ffn_w_down[1]))
    return _mix_ffn(x, [o], bf(mla_w_o[0]), g1, norm2[1], sh2, sc2, g2, *ffn1)
```

```python
import functools
import math

import jax
import jax.numpy as jnp
import numpy as np
from jax import lax
from jax.experimental import pallas as pl
from jax.experimental.pallas import tpu as pltpu

F32 = jnp.float32
BF16 = jnp.bfloat16
HIGHEST = lax.Precision.HIGHEST

EPS = 1e-6
D_MODEL = 1024
D_FF = 2816
D_FOURIER = 512
FOURIER_GROUP_DIM = 128
D_HYENA = 512
HYENA_EMB_DIM = 33
HYENA_EMB_PAD = 64
HYENA_BANDS = 16
HYENA_WIDTH = 64
HYENA_MIN_DECAY = math.log(1e-2) / 0.3
HYENA_MAX_DECAY = math.log(1e-2) / 1.5
MLA_HEADS = 16
Q_LORA = 256
KV_LORA = 128
NOPE = 64
ROPE = 32
QK_DIM = NOPE + ROPE
V_DIM = 64
HEAD_PAD = 128
MLA_IN_PAD = 512
GRID_W = 64
ROPE_THETA = 10000.0

ATTN_DTYPE = jnp.float8_e4m3fn
ATTN_P_SHIFT = 8.0
FFN_CHUNK = 256
FFT_RADIX = 128
FFT_KRON = 16
HALO = 16
VMEM_LIMIT = 56 * 1024 * 1024


def _cparams(sem, flags=None):
    return pltpu.CompilerParams(dimension_semantics=sem, vmem_limit_bytes=VMEM_LIMIT, flags=flags)


def _dot(a, b):
    return jnp.dot(a, b, preferred_element_type=F32)


def _norm_mod(x, g, shift, scale):
    ms = jnp.mean(x * x, axis=-1, keepdims=True)
    return (x * lax.rsqrt(ms + EPS) * g) * (1.0 + scale) + shift


def _silu(x):
    return x * (1.0 / (1.0 + jnp.exp(-x)))


def _row_tile(L, want):
    t = min(L, want)
    assert L % t == 0 and t % HALO == 0
    return t


def _mod_kernel(c_ref, w_ref, b_ref, o_ref):
    s = _silu(c_ref[...])
    o_ref[...] = jnp.dot(s, w_ref[...], precision=HIGHEST, preferred_element_type=F32) + b_ref[...]


def _adaln(cond, w_mod, b_mod):
    R, D = cond.shape
    n = w_mod.shape[1]
    tn = 768
    return pl.pallas_call(
        _mod_kernel,
        out_shape=jax.ShapeDtypeStruct((R, n), F32),
        grid=(n // tn,),
        in_specs=[pl.BlockSpec((R, D), lambda j: (0, 0)),
                  pl.BlockSpec((D, tn), lambda j: (0, j)),
                  pl.BlockSpec((1, tn), lambda j: (0, j))],
        out_specs=pl.BlockSpec((R, tn), lambda j: (0, j)),
        compiler_params=_cparams(("arbitrary",)),
        name="adaln_mod",
    )(cond, w_mod, b_mod.reshape(1, n))


def _halo_specs(tm, D, L):
    nb = L // HALO
    per = tm // HALO
    cur = pl.BlockSpec((1, tm, D), lambda b, i, *_: (b, i, 0))
    prev = pl.BlockSpec((1, HALO, D), lambda b, i, *_: (b, jnp.maximum(i * per - 1, 0), 0))
    nxt = pl.BlockSpec((1, HALO, D), lambda b, i, *_: (b, jnp.minimum((i + 1) * per, nb - 1), 0))
    return cur, prev, nxt


def _mod_spec(arr):
    D = arr.shape[-1]
    if arr.shape[0] == 1:
        return pl.BlockSpec((1, 1, D), lambda b, *_: (0, 0, 0))
    return pl.BlockSpec((1, 1, D), lambda b, *_: (b, 0, 0))


def _fill_hn_rows(hn_ref, xc, xp, xn, g, shift, scale, tm, i, nt):
    hp = _norm_mod(xp, g, shift, scale)
    hn_ref[0:HALO, :] = jnp.where(i > 0, hp, 0.0).astype(BF16)
    hn_ref[HALO:HALO + tm, :] = _norm_mod(xc, g, shift, scale).astype(BF16)
    hx = _norm_mod(xn, g, shift, scale)
    hn_ref[HALO + tm:, :] = jnp.where(i < nt - 1, hx, 0.0).astype(BF16)


def _fill_hn(hn_ref, xc_ref, xp_ref, xn_ref, g, shift, scale, tm, i, nt):
    _fill_hn_rows(hn_ref, xc_ref[0], xp_ref[0], xn_ref[0], g, shift, scale, tm, i, nt)


def _dwconv3_rows(p, w, tm):
    n = tm + 2 * HALO
    up = pltpu.roll(p, 1, 0)
    dn = pltpu.roll(p, n - 1, 0)
    c = up * w[0:1] + p * w[1:2] + dn * w[2:3]
    return c[HALO:HALO + tm]


def _fh_in_kernel(xc_ref, xp_ref, xn_ref, g_ref, sh_ref, sc_ref, w_ref, cw_ref, cb_ref, gm_ref, *rest, tm, n_hi):
    if n_hi:
        perm_ref, z_ref, u_ref, x0_ref, hn_ref = rest
    else:
        z_ref, u_ref, x0_ref, hn_ref = rest
    i = pl.program_id(1)
    nt = pl.num_programs(1)
    _fill_hn(hn_ref, xc_ref, xp_ref, xn_ref, g_ref[...], sh_ref[0], sc_ref[0], tm, i, nt)
    proj = _dot(hn_ref[...], w_ref[...])
    uf = proj[HALO:HALO + tm, :D_FOURIER].astype(BF16)
    if n_hi:
        uf = _dot(perm_ref[...], uf).astype(BF16)
        z = _dot(uf, gm_ref[...]).astype(BF16)
        z_ref[0] = z.reshape(n_hi, tm // n_hi, 2 * D_FOURIER)
    else:
        z_ref[0] = _dot(uf, gm_ref[...]).astype(BF16)
    c = _dwconv3_rows(proj[:, D_FOURIER:], cw_ref[...], tm) + cb_ref[...]
    x0 = c[:, :D_HYENA]
    x1 = c[:, D_HYENA:2 * D_HYENA]
    v = c[:, 2 * D_HYENA:]
    x0_ref[0] = x0
    u_ref[0] = (v * x1).astype(BF16)


def _fh_in(x, g, shift, scale, w_in, conv_w, conv_b, gmat, n_hi=0):
    B, L, D = x.shape
    tm = _row_tile(L, 512)
    cur, prev, nxt = _halo_specs(tm, D, L)
    n_in = w_in.shape[1]
    nh = 3 * D_HYENA
    full = lambda shape: pl.BlockSpec(shape, lambda b, i: (0,) * len(shape))
    out_tile = lambda c: pl.BlockSpec((1, tm, c), lambda b, i: (b, i, 0))
    in_specs = [cur, prev, nxt, full((1, D)), _mod_spec(shift), _mod_spec(scale),
                full((D, n_in)), full((3, nh)), full((1, nh)), full((D_FOURIER, 2 * D_FOURIER))]
    args = [x, x, x, g.reshape(1, D), shift, scale, w_in, conv_w, conv_b.reshape(1, nh), gmat]
    if n_hi:
        per = tm // n_hi
        r = np.arange(tm)
        src = n_hi * (r % per) + r // per
        args.append(_const(src[:, None] == r[None, :], BF16))
        in_specs.append(full((tm, tm)))
        z_shape = jax.ShapeDtypeStruct((B, n_hi, L // n_hi, 2 * D_FOURIER), BF16)
        z_spec = pl.BlockSpec((1, n_hi, per, 2 * D_FOURIER), lambda b, i: (b, 0, i, 0))
    else:
        z_shape = jax.ShapeDtypeStruct((B, L, 2 * D_FOURIER), BF16)
        z_spec = out_tile(2 * D_FOURIER)
    return pl.pallas_call(
        functools.partial(_fh_in_kernel, tm=tm, n_hi=n_hi),
        out_shape=(z_shape, jax.ShapeDtypeStruct((B, L, D_HYENA), BF16), jax.ShapeDtypeStruct((B, L, D_HYENA), F32)),
        grid=(B, L // tm),
        in_specs=in_specs,
        out_specs=(z_spec, out_tile(D_HYENA), out_tile(D_HYENA)),
        scratch_shapes=[pltpu.VMEM((tm + 2 * HALO, D), BF16)],
        compiler_params=_cparams(("parallel", "arbitrary")),
        name="fh_in",
    )(*args)


def _fourier_kernel(c_ref, s_ref, z_ref, o_ref, *, scale):
    z = z_ref[0]
    y = _dot(c_ref[...], z[:, :D_FOURIER]) + _dot(s_ref[...], z[:, D_FOURIER:])
    o_ref[0] = (y * scale).astype(BF16)


def _fourier_seq(z, cl, sl):
    B, L, _ = z.shape
    tf = _row_tile(L, 512)
    scale = 1.0 / math.sqrt(L * FOURIER_GROUP_DIM)
    return pl.pallas_call(
        functools.partial(_fourier_kernel, scale=scale),
        out_shape=jax.ShapeDtypeStruct((B, L, D_FOURIER), BF16),
        grid=(B, L // tf),
        in_specs=[pl.BlockSpec((tf, L), lambda b, i: (i, 0)),
                  pl.BlockSpec((tf, L), lambda b, i: (i, 0)),
                  pl.BlockSpec((1, L, 2 * D_FOURIER), lambda b, i: (b, 0, 0))],
        out_specs=pl.BlockSpec((1, tf, D_FOURIER), lambda b, i: (b, i, 0)),
        compiler_params=_cparams(("parallel", "arbitrary")),
        name="fourier_seq",
    )(cl, sl, z)


def _fft_f1_kernel(cs_ref, z_ref, o_ref, *, nc):
    for j in range(nc):
        cs = _dot(cs_ref[...], z_ref[0, j])
        cz = cs[:FFT_RADIX]
        sz = cs[FFT_RADIX:]
        o_ref[0, 0, j] = (cz[:, :D_FOURIER] + sz[:, D_FOURIER:]).astype(BF16)
        o_ref[0, 1, j] = (cz[:, D_FOURIER:] - sz[:, :D_FOURIER]).astype(BF16)


def _fft_f2_kernel(t_ref, b_ref, o_ref, *, scale):
    n_hi = b_ref.shape[2]
    blk = b_ref[0].reshape(2 * n_hi * 8, D_FOURIER)
    y = _dot(t_ref[0], blk) * scale
    o_ref[0, :, 0] = y.reshape(n_hi, 8, D_FOURIER).astype(BF16)


def _fourier_seq_fft(zp, cs, tmat):
    B, n_hi, _, _ = zp.shape
    L = n_hi * FFT_RADIX
    nc = 8
    ng = FFT_RADIX // 8
    bc = pl.pallas_call(
        functools.partial(_fft_f1_kernel, nc=nc),
        out_shape=jax.ShapeDtypeStruct((B, 2, n_hi, FFT_RADIX, D_FOURIER), BF16),
        grid=(B, n_hi // nc),
        in_specs=[pl.BlockSpec((2 * FFT_RADIX, FFT_RADIX), lambda b, i: (0, 0)),
                  pl.BlockSpec((1, nc, FFT_RADIX, 2 * D_FOURIER), lambda b, i: (b, i, 0, 0))],
        out_specs=pl.BlockSpec((1, 2, nc, FFT_RADIX, D_FOURIER), lambda b, i: (b, 0, i, 0, 0)),
        compiler_params=_cparams(("parallel", "arbitrary")),
        name="fourier_fft1",
    )(cs, zp)
    scale = 1.0 / math.sqrt(L * FOURIER_GROUP_DIM)
    y = pl.pallas_call(
        functools.partial(_fft_f2_kernel, scale=scale),
        out_shape=jax.ShapeDtypeStruct((B, n_hi, ng, 8, D_FOURIER), BF16),
        grid=(B, ng),
        in_specs=[pl.BlockSpec((1, n_hi * 8, 2 * n_hi * 8), lambda b, g: (g, 0, 0)),
                  pl.BlockSpec((1, 2, n_hi, 8, D_FOURIER), lambda b, g: (b, 0, 0, g, 0))],
        out_specs=pl.BlockSpec((1, n_hi, 1, 8, D_FOURIER), lambda b, g: (b, 0, g, 0, 0)),
        compiler_params=_cparams(("parallel", "arbitrary")),
        name="fourier_fft2",
    )(tmat, bc)
    return y.reshape(B, L, D_FOURIER)


def _filter_kernel(z_ref, w1_ref, b1_ref, w2_ref, b2_ref, w3_ref, b3_ref, w4_ref, f_ref, dl_ref,
                   k_ref, n_ref):
    i = pl.program_id(1)
    z = z_ref[0]
    f = f_ref[...]
    hd = lambda a, w: jnp.dot(a, w, precision=HIGHEST, preferred_element_type=F32)
    h = jnp.sin(f * (hd(z, w1_ref[...]) + b1_ref[...]))
    h = jnp.sin(f * (hd(h, w2_ref[...]) + b2_ref[...]))
    h = jnp.sin(f * (hd(h, w3_ref[...]) + b3_ref[...]))
    h = hd(h, w4_ref[0])
    t = z[:, 0:1]
    valid = z[:, HYENA_EMB_DIM:HYENA_EMB_DIM + 1]
    k = h * jnp.exp(-t * jnp.abs(dl_ref[...])) * valid
    k_ref[0] = k

    @pl.when(i == 0)
    def _():
        n_ref[...] = jnp.zeros_like(n_ref)

    n_ref[0] += jnp.sum(jnp.abs(k), axis=0, keepdims=True)


def _hyena_filter(L, w1, b1, w2, b2, w3, b3, w4, freq):
    pos = np.arange(L, dtype=np.float64)
    bands = np.linspace(1e-4, HYENA_BANDS - 1, HYENA_BANDS)

    def emb(p, valid):
        t = p / max(L - 1, 1)
        ang = (2.0 * math.pi / L) * p[:, None] * bands[None, :]
        pad = np.zeros((L, HYENA_EMB_PAD - HYENA_EMB_DIM - 1))
        return np.concatenate([t[:, None], np.cos(ang), -np.sin(ang), valid[:, None], pad], axis=-1)

    zf = emb(pos, np.ones((L,)))
    zb = emb(np.where(pos > 0, L - pos, 0.0), (pos > 0).astype(np.float64))
    z = _const(np.stack([zf, zb]))
    w1p = jnp.concatenate([w1, jnp.zeros((HYENA_EMB_PAD - HYENA_EMB_DIM, HYENA_WIDTH), F32)], axis=0)
    w4s = jnp.stack([w4[:, :D_HYENA], w4[:, D_HYENA:]])
    deltas = _const(np.linspace(HYENA_MIN_DECAY, HYENA_MAX_DECAY, D_HYENA).reshape(1, D_HYENA))
    tr = _row_tile(L, 512)
    W = HYENA_WIDTH
    full = lambda shape: pl.BlockSpec(shape, lambda s, i: (0,) * len(shape))
    return pl.pallas_call(
        _filter_kernel,
        out_shape=(jax.ShapeDtypeStruct((2, L, D_HYENA), F32), jax.ShapeDtypeStruct((2, 1, D_HYENA), F32)),
        grid=(2, L // tr),
        in_specs=[pl.BlockSpec((1, tr, HYENA_EMB_PAD), lambda s, i: (s, i, 0)),
                  full((HYENA_EMB_PAD, W)), full((1, W)), full((W, W)), full((1, W)),
                  full((W, W)), full((1, W)),
                  pl.BlockSpec((1, W, D_HYENA), lambda s, i: (s, 0, 0)),
                  full((1, W)), full((1, D_HYENA))],
        out_specs=(pl.BlockSpec((1, tr, D_HYENA), lambda s, i: (s, i, 0)),
                   pl.BlockSpec((1, 1, D_HYENA), lambda s, i: (s, 0, 0))),
        compiler_params=_cparams(("arbitrary", "arbitrary")),
        name="hyena_filter",
    )(z, w1p, b1.reshape(1, W), w2, b2.reshape(1, W), w3, b3.reshape(1, W), w4s, freq.reshape(1, W), deltas)


def _hy_fwd_kernel(c_ref, s_ref, u_ref, *rest, raw, tf):
    u = u_ref[0]
    ure = _dot(c_ref[...], u)
    uim = -_dot(s_ref[...], u)
    if raw:
        re_ref, im_ref = rest
        re_ref[0] = ure
        im_ref[0] = uim
        return
    kre_ref, kim_ref, re_ref, im_ref = rest
    kre = kre_ref[...]
    kim = kim_ref[...]
    row = lax.broadcasted_iota(jnp.int32, (tf, 1), 0) + pl.program_id(1) * tf
    packed = row == 0
    a = uim * kim
    re_ref[0] = (ure * kre - jnp.where(packed, 0.0, a)).astype(BF16)
    im_ref[0] = jnp.where(packed, a, ure * kim + uim * kre).astype(BF16)


def _hy_fwd(u, cb, sbf, kre=None, kim=None):
    B, L, C = u.shape
    tf = _row_tile(L, 512)
    raw = kre is None
    mat = pl.BlockSpec((tf, L), lambda b, i: (i, 0))
    tile = pl.BlockSpec((1, tf, C), lambda b, i: (b, i, 0))
    in_specs = [mat, mat, pl.BlockSpec((1, L, C), lambda b, i: (b, 0, 0))]
    args = [cb, sbf, u]
    if not raw:
        in_specs += [pl.BlockSpec((tf, C), lambda b, i: (i, 0))] * 2
        args += [kre, kim]
    dt = F32 if raw else BF16
    return pl.pallas_call(
        functools.partial(_hy_fwd_kernel, raw=raw, tf=tf),
        out_shape=(jax.ShapeDtypeStruct((B, L, C), dt), jax.ShapeDtypeStruct((B, L, C), dt)),
        grid=(B, L // tf),
        in_specs=in_specs,
        out_specs=(tile, tile),
        compiler_params=_cparams(("parallel", "arbitrary")),
        name="hyena_fwd_raw" if raw else "hyena_fwd",
    )(*args)


def _spec_combine_kernel(re_ref, im_ref, n_ref, kre_ref, kim_ref, *, tf, n_fft):
    row = lax.broadcasted_iota(jnp.int32, (tf, 1), 0) + pl.program_id(0) * tf
    sgn = (1 - 2 * (row & 1)).astype(F32)
    wgt = jnp.where(row == 0, 1.0 / n_fft, 2.0 / n_fft)
    s = wgt / (n_ref[0] + n_ref[1])
    kre_ref[...] = (re_ref[0] + sgn * re_ref[1]) * s
    kim_ref[...] = (im_ref[0] + sgn * im_ref[1]) * s


def _spec_combine(re, im, norms):
    _, L, C = re.shape
    tf = _row_tile(L, 512)
    pair = pl.BlockSpec((2, tf, C), lambda i: (0, i, 0))
    tile = pl.BlockSpec((tf, C), lambda i: (i, 0))
    return pl.pallas_call(
        functools.partial(_spec_combine_kernel, tf=tf, n_fft=2 * L),
        out_shape=(jax.ShapeDtypeStruct((L, C), F32), jax.ShapeDtypeStruct((L, C), F32)),
        grid=(L // tf,),
        in_specs=[pair, pair, pl.BlockSpec((2, 1, C), lambda i: (0, 0, 0))],
        out_specs=(tile, tile),
        compiler_params=_cparams(("arbitrary",)),
        name="hyena_spec_combine",
    )(re, im, norms)


def _hy_inv_kernel(c_ref, s_ref, re_ref, im_ref, x0_ref, u_ref, b_ref, o_ref):
    y = _dot(c_ref[...], re_ref[0]) - _dot(s_ref[...], im_ref[0])
    o_ref[0] = (x0_ref[0] * (y + u_ref[0].astype(F32) * b_ref[...])).astype(BF16)


def _hy_inv(yre, yim, x0, u, bias, cb, sbi):
    B, L, C = u.shape
    tt = _row_tile(L, 512)
    mat = pl.BlockSpec((tt, L), lambda b, i: (i, 0))
    whole = pl.BlockSpec((1, L, C), lambda b, i: (b, 0, 0))
    tile = pl.BlockSpec((1, tt, C), lambda b, i: (b, i, 0))
    return pl.pallas_call(
        _hy_inv_kernel,
        out_shape=jax.ShapeDtypeStruct((B, L, C), BF16),
        grid=(B, L // tt),
        in_specs=[mat, mat, whole, whole, tile, tile, pl.BlockSpec((1, C), lambda b, i: (0, 0))],
        out_specs=tile,
        compiler_params=_cparams(("parallel", "arbitrary")),
        name="hyena_inv",
    )(cb, sbi, yre, yim, x0, u, bias.reshape(1, C))


def _fft_a_kernel(t_ref, x_ref, o_ref):
    r2, kr, c = x_ref.shape[1:]
    y = _dot(t_ref[...], x_ref[0].reshape(r2 * kr, c))
    o_ref[0] = y.reshape(2, y.shape[0] // (2 * kr), kr, c).astype(BF16)


def _hy_fft_a(x4, t1):
    Bx, R, _, C = x4.shape
    n_hi = t1.shape[0] // (2 * FFT_KRON)
    return pl.pallas_call(
        _fft_a_kernel,
        out_shape=jax.ShapeDtypeStruct((Bx, 2, n_hi, FFT_RADIX, C), BF16),
        grid=(Bx, FFT_RADIX // FFT_KRON),
        in_specs=[pl.BlockSpec(t1.shape, lambda b, i: (0, 0)),
                  pl.BlockSpec((1, R, FFT_KRON, C), lambda b, i: (b, 0, i, 0))],
        out_specs=pl.BlockSpec((1, 2, n_hi, FFT_KRON, C), lambda b, i: (b, 0, 0, i, 0)),
        compiler_params=_cparams(("parallel", "arbitrary")),
        name="hyena_fft_a",
    )(t1, x4)


def _fft_b_kernel(gf_ref, a_ref, *rest, raw, n_fft):
    nb = a_ref.shape[0]
    gf = gf_ref[0]
    if raw:
        n_ref, o_ref = rest
        s = 1.0 / (n_fft * (n_ref[0] + n_ref[1]))
    else:
        gi_ref, k_ref, o_ref = rest
        gi = gi_ref[0]
        kr = k_ref[0, 0, 0]
        ki = k_ref[0, 1, 0]
    for b in range(nb):
        a = a_ref[b, :, 0].reshape(2 * FFT_RADIX, D_HYENA)
        x = _dot(gf, a)
        if raw:
            o_ref[b, :, 0] = (x * s).reshape(2, FFT_RADIX, D_HYENA)
            continue
        xr = x[:FFT_RADIX]
        xi = x[FFT_RADIX:]
        y = jnp.concatenate([xr * kr - xi * ki, xr * ki + xi * kr], axis=0).astype(BF16)
        o_ref[b, :, 0] = _dot(gi, y).reshape(2, FFT_RADIX, D_HYENA).astype(BF16)


def _hy_fft_b(a5, gf, gi=None, kspec=None, norms=None):
    Bx, _, n_hi, _, C = a5.shape
    raw = kspec is None
    blk = pl.BlockSpec((Bx, 2, 1, FFT_RADIX, C), lambda k: (0, 0, k, 0, 0))
    tab = pl.BlockSpec((1, 2 * FFT_RADIX, 2 * FFT_RADIX), lambda k: (k, 0, 0))
    if raw:
        in_specs = [tab, blk, pl.BlockSpec((2, 1, C), lambda k: (0, 0, 0))]
        args = (gf, a5, norms)
    else:
        kblk = pl.BlockSpec((1, 2, 1, FFT_RADIX, C), lambda k: (0, 0, k, 0, 0))
        in_specs = [tab, blk, tab, kblk]
        args = (gf, a5, gi, kspec)
    return pl.pallas_call(
        functools.partial(_fft_b_kernel, raw=raw, n_fft=n_hi * FFT_RADIX),
        out_shape=jax.ShapeDtypeStruct(a5.shape, F32 if raw else BF16),
        grid=(n_hi,),
        in_specs=in_specs,
        out_specs=blk,
        compiler_params=_cparams(("arbitrary",)),
        name="hyena_fft_spec" if raw else "hyena_fft_b",
    )(*args)


def _fft_c_kernel(t_ref, v_ref, x0_ref, u_ref, b_ref, o_ref):
    _, n_hi, kr, c = v_ref.shape[1:]
    y = _dot(t_ref[...], v_ref[0].reshape(2 * n_hi * kr, c))
    y = y.reshape(y.shape[0] // kr, kr, c)
    o_ref[0] = (x0_ref[0] * (y + u_ref[0].astype(F32) * b_ref[...])).astype(BF16)


def _hy_fft_c(v5, t3, x0, u, bias):
    B, _, n_hi, _, C = v5.shape
    R = t3.shape[0] // FFT_KRON
    tile = pl.BlockSpec((1, R, FFT_KRON, C), lambda b, i: (b, 0, i, 0))
    return pl.pallas_call(
        _fft_c_kernel,
        out_shape=jax.ShapeDtypeStruct((B, R, FFT_RADIX, C), BF16),
        grid=(B, FFT_RADIX // FFT_KRON),
        in_specs=[pl.BlockSpec(t3.shape, lambda b, i: (0, 0)),
                  pl.BlockSpec((1, 2, n_hi, FFT_KRON, C), lambda b, i: (b, 0, 0, i, 0)),
                  tile, tile, pl.BlockSpec((1, C), lambda b, i: (0, 0))],
        out_specs=tile,
        compiler_params=_cparams(("parallel", "arbitrary")),
        name="hyena_fft_c",
    )(t3, v5, x0, u, bias.reshape(1, C))


def _ffn_kernel(xc_ref, xp_ref, xn_ref, g1_ref, wo_ref, g_ref, sh_ref, sc_ref, gate_ref, wu_ref, cw_ref, cb_ref,
                wd_ref, *rest, tm, n_y):
    y_refs = rest[:3 * n_y]
    o_ref, yext_ref, hn_ref, act_ref = rest[3 * n_y:]
    i = pl.program_id(1)
    nt = pl.num_programs(1)
    off = 0
    for k in range(n_y):
        yc_ref, yp_ref, yn_ref = y_refs[3 * k:3 * k + 3]
        c = yc_ref.shape[-1]
        yext_ref[0:HALO, off:off + c] = yp_ref[0].astype(BF16)
        yext_ref[HALO:HALO + tm, off:off + c] = yc_ref[0].astype(BF16)
        yext_ref[HALO + tm:, off:off + c] = yn_ref[0].astype(BF16)
        off += c
    mix = g1_ref[0] * _dot(yext_ref[...], wo_ref[...])
    x1 = xc_ref[0] + mix[HALO:HALO + tm]
    _fill_hn_rows(hn_ref, x1, xp_ref[0] + mix[:HALO], xn_ref[0] + mix[HALO + tm:], g_ref[...], sh_ref[0], sc_ref[0],
                  tm, i, nt)
    for c in range(0, D_FF, FFN_CHUNK):
        cols = slice(c, c + FFN_CHUNK)
        gx = _dot(hn_ref[...], wu_ref[:, cols])
        vx = _dot(hn_ref[HALO:HALO + tm, :], wu_ref[:, D_FF + c:D_FF + c + FFN_CHUNK])
        cv = _dwconv3_rows(gx, cw_ref[:, cols], tm) + cb_ref[:, cols]
        act_ref[:, cols] = (_silu(cv) * vx).astype(BF16)
    o_ref[0] = x1 + gate_ref[0] * _dot(act_ref[...], wd_ref[...])


def _mix_ffn(x, ys, w_o, gate1, g, shift, scale, gate2, w_up, conv_w, conv_b, w_down):
    B, L, D = x.shape
    tm = _row_tile(L, 512)
    cur, prev, nxt = _halo_specs(tm, D, L)
    const = lambda shape: pl.BlockSpec(shape, lambda b, i: (0,) * len(shape), pipeline_mode=pl.Buffered(1))
    y_specs, y_args = [], []
    for y in ys:
        y_specs += list(_halo_specs(tm, y.shape[-1], L))
        y_args += [y, y, y]
    return pl.pallas_call(
        functools.partial(_ffn_kernel, tm=tm, n_y=len(ys)),
        out_shape=jax.ShapeDtypeStruct((B, L, D), F32),
        grid=(B, L // tm),
        in_specs=[cur, prev, nxt, _mod_spec(gate1), const(w_o.shape), const((1, D)), _mod_spec(shift),
                  _mod_spec(scale), _mod_spec(gate2), const((D, 2 * D_FF)), const((3, D_FF)), const((1, D_FF)),
                  const((D_FF, D))] + y_specs,
        out_specs=pl.BlockSpec((1, tm, D), lambda b, i: (b, i, 0)),
        scratch_shapes=[pltpu.VMEM((tm + 2 * HALO, w_o.shape[0]), BF16), pltpu.VMEM((tm + 2 * HALO, D), BF16),
                        pltpu.VMEM((tm, D_FF), BF16)],
        compiler_params=_cparams(("parallel", "arbitrary")),
        name="mix_ffn",
    )(x, x, x, gate1, w_o, g.reshape(1, D), shift, scale, gate2, w_up, conv_w, conv_b.reshape(1, D_FF), w_down,
      *y_args)


def _rms(v, g, n):
    return v * lax.rsqrt(jnp.sum(v * v, axis=-1, keepdims=True) * (1.0 / n) + EPS) * g


def _rope_gain_tables(gain, cos, sa, sb):
    rot = slice(NOPE, QK_DIM)
    gr = gain[rot]
    return gain[:NOPE], gr * cos[rot], pltpu.roll(gr, 8, 0) * sa[rot], pltpu.roll(gr, ROPE - 8, 0) * sb[rot]


def _rope_t(yr, gc, gsa, gsb):
    return yr * gc + pltpu.roll(yr, 8, 0) * gsa + pltpu.roll(yr, ROPE - 8, 0) * gsb


def _sumsq(x):
    return jnp.sum(x * x, axis=0, keepdims=True)


def _mla_proj_kernel(x_ref, g_ref, sh_ref, sc_ref, win_ref, qan_ref, wuq_ref, qn_ref, kvn_ref, wuk_ref,
                     wuv_ref, kn_ref, cos_ref, sa_ref, sb_ref, *outs, with_q):
    hn = _norm_mod(x_ref[0], g_ref[...], sh_ref[0], sc_ref[0]).astype(BF16)
    a = _dot(hn, win_ref[...])
    tm = a.shape[0]
    cos = cos_ref[...]
    sa = sa_ref[...]
    sb = sb_ref[...]
    pad = jnp.zeros((HEAD_PAD - QK_DIM, tm), ATTN_DTYPE)
    if with_q:
        qt_ref, k_ref, vt_ref = outs
        qat = _rms(a[:, :Q_LORA], qan_ref[...], Q_LORA).T.astype(BF16)
        qft = _dot(wuq_ref[...], qat)
        gn, gc, gsa, gsb = _rope_gain_tables(qn_ref[...], cos, sa, sb)
        for h in range(MLA_HEADS):
            base = h * HEAD_PAD
            xn = qft[base:base + NOPE]
            xr = qft[base + NOPE:base + QK_DIM]
            r = lax.rsqrt((_sumsq(xn) + _sumsq(xr)) * (1.0 / QK_DIM) + EPS)
            qt_ref[0, base:base + NOPE, :] = (xn * r * gn).astype(ATTN_DTYPE)
            qt_ref[0, base + NOPE:base + QK_DIM, :] = _rope_t(xr * r, gc, gsa, gsb).astype(ATTN_DTYPE)
            qt_ref[0, base + QK_DIM:base + HEAD_PAD, :] = pad
    else:
        k_ref, vt_ref = outs
    ckvt = _rms(a[:, Q_LORA:Q_LORA + KV_LORA], kvn_ref[...], KV_LORA).T.astype(BF16)
    kft = _dot(wuk_ref[...], ckvt)
    vt_ref[0] = _dot(wuv_ref[...], ckvt).astype(ATTN_DTYPE)
    kpe = a[:, Q_LORA + KV_LORA:].T[:ROPE]
    gn, gc, gsa, gsb = _rope_gain_tables(kn_ref[...], cos, sa, sb)
    kpe_rot = _rope_t(kpe, gc, gsa, gsb)
    kpe_ss = _sumsq(kpe)
    for h in range(MLA_HEADS):
        base = h * HEAD_PAD
        xn = kft[base:base + NOPE]
        r = lax.rsqrt((_sumsq(xn) + kpe_ss) * (1.0 / QK_DIM) + EPS)
        kt = jnp.concatenate([xn * r * gn, kpe_rot * r, jnp.zeros((HEAD_PAD - QK_DIM, tm), F32)], axis=0)
        k_ref[0, :, base:base + HEAD_PAD] = kt.T.astype(ATTN_DTYPE)


def _mla_proj(x, g, shift, scale, wts, rope_tabs, with_q):
    B, L, D = x.shape
    tm = _row_tile(L, 512)
    win, qan, wuq, qn, kvn, wuk, wuv, kn = wts
    cos, sa, sb = rope_tabs
    HP = MLA_HEADS * HEAD_PAD
    HV = MLA_HEADS * V_DIM
    full = lambda a: pl.BlockSpec(a.shape, lambda b, i: (0,) * a.ndim)
    tab = pl.BlockSpec((HEAD_PAD, tm), lambda b, i: (0, i))
    out_shape = [jax.ShapeDtypeStruct((B, L, HP), ATTN_DTYPE), jax.ShapeDtypeStruct((B, HV, L), ATTN_DTYPE)]
    out_specs = [pl.BlockSpec((1, tm, HP), lambda b, i: (b, i, 0)),
                 pl.BlockSpec((1, HV, tm), lambda b, i: (b, 0, i))]
    if with_q:
        out_shape = [jax.ShapeDtypeStruct((B, HP, L), ATTN_DTYPE)] + out_shape
        out_specs = [pl.BlockSpec((1, HP, tm), lambda b, i: (b, 0, i))] + out_specs
    g2 = g.reshape(1, D)
    gain_tab = lambda v: jnp.broadcast_to(v.reshape(HEAD_PAD, 1), (HEAD_PAD, tm))
    qn = gain_tab(qn)
    kn = gain_tab(kn)
    return pl.pallas_call(
        functools.partial(_mla_proj_kernel, with_q=with_q),
        out_shape=tuple(out_shape),
        grid=(B, L // tm),
        in_specs=[pl.BlockSpec((1, tm, D), lambda b, i: (b, i, 0)), full(g2), _mod_spec(shift), _mod_spec(scale),
                  full(win), full(qan), full(wuq), full(qn), full(kvn), full(wuk), full(wuv), full(kn),
                  tab, tab, tab],
        out_specs=tuple(out_specs),
        compiler_params=_cparams(("parallel", "arbitrary")),
        name="mla_proj_q" if with_q else "mla_proj_kv",
    )(x, g2, shift, scale, win, qan, wuq, qn, kvn, wuk, wuv, kn, cos, sa, sb)


def _attn_kernel(qt_ref, kx_ref, kc_ref, vtx_ref, vtc_ref, o_ref, *scratch):
    t = pl.program_id(0)
    io = (qt_ref, kx_ref, kc_ref, vtx_ref, vtc_ref, o_ref)
    set_a, set_b = scratch[:5], scratch[5:]

    @pl.when(t == 0)
    def _():
        for ref in scratch:
            ref[...] = jnp.zeros(ref.shape, ref.dtype)

    @pl.when(t % 2 == 0)
    def _():
        _attn_step(*io, set_a, set_b)

    @pl.when(t % 2 == 1)
    def _():
        _attn_step(*io, set_b, set_a)


def _attn_step(qt_ref, kx_ref, kc_ref, vtx_ref, vtc_ref, o_ref, cur, prev):
    wx_ref, wc_ref, wm_ref, px_ref, pc_ref = cur
    rx_ref, rc_ref, rm_ref, qx_ref, qc_ref = prev
    tq = qt_ref.shape[2]
    off = rm_ref[...][None]

    def probs(s_ref):
        n = s_ref.shape[0]
        s = s_ref[...].reshape(n // 16, 16, 2 * tq)
        return jnp.exp2(s - off).reshape(n, 2 * tq).astype(ATTN_DTYPE)

    qx_ref[...] = probs(rx_ref)
    qc_ref[...] = probs(rc_ref)

    q0 = qt_ref[0, :HEAD_PAD, :]
    q1 = qt_ref[0, HEAD_PAD:, :]
    zq = jnp.zeros_like(q0)
    qbd = jnp.concatenate([jnp.concatenate([q0, zq], axis=1), jnp.concatenate([zq, q1], axis=1)], axis=0)
    nx = _dot(kx_ref[0], qbd).astype(BF16)
    nc = _dot(kc_ref[0], qbd).astype(BF16)
    wx_ref[...] = nx
    wc_ref[...] = nc
    m = jnp.maximum(jnp.max(nx, axis=0, keepdims=True), jnp.max(nc, axis=0, keepdims=True)).astype(F32)
    wm_ref[...] = jnp.broadcast_to((m - ATTN_P_SHIFT).astype(BF16), wm_ref.shape)

    vtx = vtx_ref[0]
    vtc = vtc_ref[0]
    halves = []
    for j in range(2):
        own = slice(j * V_DIM, (j + 1) * V_DIM)
        oth = slice((1 - j) * V_DIM, (2 - j) * V_DIM)
        cols = slice(j * tq, (j + 1) * tq)
        parts_x = [vtx[own], jnp.ones_like(vtx[oth])]
        parts_c = [vtc[own], jnp.ones_like(vtc[oth])]
        if j == 1:
            parts_x.reverse()
            parts_c.reverse()
        r = (_dot(jnp.concatenate(parts_x, axis=0), px_ref[:, cols])
             + _dot(jnp.concatenate(parts_c, axis=0), pc_ref[:, cols]))
        halves.append(r[own] * (1.0 / r[oth][0:1]))
    o_ref[0] = jnp.concatenate(halves, axis=0).T.astype(BF16)


def _attention(qt, kx, kc, vtx, vtc):
    B, HP, L = qt.shape
    Lc = kc.shape[1]
    tq = _row_tile(L, 512)
    nhp = MLA_HEADS // 2
    nq = L // tq
    n_tiles = B * nhp * nq

    def tile(t):
        t = jnp.clip(t, 0, n_tiles - 1)
        return t // (nhp * nq), (t // nq) % nhp, t % nq

    def logits_side(f):
        return lambda t: f(*tile(t))

    def value_side(f):
        return lambda t: f(*tile(t - 2))

    scratch_set = [pltpu.VMEM((L, 2 * tq), BF16), pltpu.VMEM((Lc, 2 * tq), BF16), pltpu.VMEM((16, 2 * tq), BF16),
                   pltpu.VMEM((L, 2 * tq), ATTN_DTYPE), pltpu.VMEM((Lc, 2 * tq), ATTN_DTYPE)]
    return pl.pallas_call(
        _attn_kernel,
        out_shape=jax.ShapeDtypeStruct((B, L, MLA_HEADS * V_DIM), BF16),
        grid=(n_tiles + 2,),
        in_specs=[pl.BlockSpec((1, 2 * HEAD_PAD, tq), logits_side(lambda b, h, i: (b, h, i))),
                  pl.BlockSpec((1, L, 2 * HEAD_PAD), logits_side(lambda b, h, i: (b, 0, h))),
                  pl.BlockSpec((1, Lc, 2 * HEAD_PAD), logits_side(lambda b, h, i: (b, 0, h))),
                  pl.BlockSpec((1, 2 * V_DIM, L), value_side(lambda b, h, i: (b, h, 0))),
                  pl.BlockSpec((1, 2 * V_DIM, Lc), value_side(lambda b, h, i: (b, h, 0)))],
        out_specs=pl.BlockSpec((1, tq, 2 * V_DIM), value_side(lambda b, h, i: (b, i, h))),
        scratch_shapes=scratch_set * 2,
        compiler_params=_cparams(("arbitrary",)),
        name="mla_attention",
    )(qt, kx, kc, vtx, vtc)


def _const(a, dtype=F32):
    return jnp.asarray(np.asarray(a, np.float32)).astype(dtype)


def _angle(num, period):
    return (num % period).astype(np.float64) * (2.0 * math.pi / period)


def _trig_matrix(n, period):
    r = np.arange(n, dtype=np.int64)
    a = _angle(r[:, None] * r[None, :], period)
    return np.cos(a), np.sin(a)


def _dft_tables(L):
    cl, sl = _trig_matrix(L, L)
    cb, sb = _trig_matrix(L, 2 * L)
    idx = np.arange(L)
    alt = (1 - 2 * (idx & 1)).astype(np.float64)
    sbf = np.where(idx[:, None] == 0, -alt[None, :], sb)
    sbi = np.where(idx[None, :] == 0, -alt[:, None], sb)
    return tuple(_const(m, BF16) for m in (cl, sl, cb, sbf, sbi))


def _fourier_fft_tables(L):
    n_hi = L // FFT_RADIX
    i = np.arange(FFT_RADIX, dtype=np.int64)
    a = _angle(i[:, None] * i[None, :], FFT_RADIX)
    cs = np.concatenate([np.cos(a), np.sin(a)], axis=0)
    ng = FFT_RADIX // 8
    g = np.arange(ng, dtype=np.int64)[:, None, None, None]
    k1 = np.arange(n_hi, dtype=np.int64)[None, :, None, None]
    j = np.arange(8, dtype=np.int64)[None, None, :, None]
    n1 = np.arange(n_hi, dtype=np.int64)[None, None, None, :]
    ang = _angle(n1 * (FFT_RADIX * k1 + 8 * g + j), L)
    eye = np.eye(8)
    blocks = [t[:, :, :, :, None] * eye[None, None, :, None, :] for t in (np.cos(ang), np.sin(ang))]
    tmat = np.stack(blocks, axis=3)
    return _const(cs, BF16), _const(tmat.reshape(ng, n_hi * 8, 2 * n_hi * 8), BF16)


def _hyena_fft_tables(L):
    n_fft = 2 * L
    n_hi = n_fft // FFT_RADIX
    k2 = np.arange(n_hi, dtype=np.int64)
    a1 = _angle(k2[:, None] * k2[None, :], n_hi)
    c1, s1 = np.cos(a1), np.sin(a1)
    eye = np.eye(FFT_KRON)
    kron = lambda m: (m[:, None, :, None] * eye[None, :, None, :]).reshape(m.shape[0] * FFT_KRON, m.shape[1] * FFT_KRON)
    t1 = _const(np.concatenate([kron(c1), kron(-s1)], axis=0), BF16)
    t1_half = t1[:, :n_hi // 2 * FFT_KRON]
    t3 = t1_half.T
    r = np.arange(FFT_RADIX, dtype=np.int64)
    k = k2[:, None, None] + n_hi * r[None, :, None]
    a2 = _angle(k * r[None, None, :], n_fft)
    c2, s2 = _const(np.cos(a2), BF16), _const(np.sin(a2), BF16)
    gf = jnp.concatenate([jnp.concatenate([c2, s2], axis=2), jnp.concatenate([-s2, c2], axis=2)], axis=1)
    c2t, s2t = jnp.swapaxes(c2, 1, 2), jnp.swapaxes(s2, 1, 2)
    gi = jnp.concatenate([jnp.concatenate([c2t, -s2t], axis=2), jnp.concatenate([s2t, c2t], axis=2)], axis=1)
    return t1, t1_half, t3, gf, gi


def _group_dft_matrix():
    g = FOURIER_GROUP_DIM
    ng = D_FOURIER // g
    c, s = _trig_matrix(g, g)
    eye = np.eye(ng)
    return _const(np.concatenate([np.kron(eye, c), -np.kron(eye, s)], axis=1), BF16)


def _rope_tables(L):
    pos = np.arange(L)
    rows = (pos // GRID_W).astype(np.float64)
    cols = (pos % GRID_W).astype(np.float64)
    nf = ROPE // 4
    inv_freq = ROPE_THETA ** (-np.arange(nf, dtype=np.float64) / nf)
    ar = rows[:, None] * inv_freq[None, :]
    ac = cols[:, None] * inv_freq[None, :]
    one = np.ones((L, NOPE))
    z_n = np.zeros((L, NOPE))
    z_f = np.zeros((L, nf))
    tail = np.zeros((L, HEAD_PAD - QK_DIM))
    cos = np.concatenate([one, np.cos(ar), np.cos(ar), np.cos(ac), np.cos(ac), tail], axis=1)
    sa = np.concatenate([z_n, z_f, np.sin(ar), z_f, np.sin(ac), tail], axis=1)
    sb = np.concatenate([z_n, -np.sin(ar), z_f, -np.sin(ac), z_f, tail], axis=1)
    return _const(cos.T), _const(sa.T), _const(sb.T)


def _identity_rope_tables(L):
    return jnp.ones((HEAD_PAD, L), F32), jnp.zeros((HEAD_PAD, L), F32), jnp.zeros((HEAD_PAD, L), F32)


def _pad_heads(w, per_head, keep):
    K = w.shape[0]
    w = w.reshape(K, MLA_HEADS, per_head)[:, :, :keep]
    w = jnp.pad(w, ((0, 0), (0, 0), (0, HEAD_PAD - keep)))
    return w.reshape(K, MLA_HEADS * HEAD_PAD)


def _pad_lanes(v, n):
    return jnp.pad(v, (0, n - v.shape[0])).reshape(1, n)


def _fourier_hyena_mixer(x, g, shift, scale, w_in, conv_w, conv_b, gmat, taps, norms, hy_bias):
    B, L, _ = x.shape
    if (L // FFT_RADIX) % 16 == 0:
        z, u, x0 = _fh_in(x, g, shift, scale, w_in, conv_w, conv_b, gmat, n_hi=L // FFT_RADIX)
        cs, tmat = _fourier_fft_tables(L)
        y_f = _fourier_seq_fft(z, cs, tmat)
        t1, t1_half, t3, gf, gi = _hyena_fft_tables(L)
        n_hi = 2 * L // FFT_RADIX
        rows = lambda a, r: a.reshape(a.shape[0], r, FFT_RADIX, D_HYENA)
        kspec = _hy_fft_b(_hy_fft_a(rows(taps.astype(BF16).reshape(1, 2 * L, D_HYENA), n_hi), t1), gf, norms=norms)
        v = _hy_fft_b(_hy_fft_a(rows(u, n_hi // 2), t1_half), gf, gi, kspec)
        y_h = _hy_fft_c(v, t3, rows(x0, n_hi // 2), rows(u, n_hi // 2), hy_bias)
        return y_f, y_h.reshape(B, L, D_HYENA)
    z, u, x0 = _fh_in(x, g, shift, scale, w_in, conv_w, conv_b, gmat)
    cl, sl, cb, sbf, sbi = _dft_tables(L)
    k_re, k_im = _hy_fwd(taps.astype(BF16), cb, sbf)
    kre, kim = _spec_combine(k_re, k_im, norms)
    y_f = _fourier_seq(z, cl, sl)
    yre, yim = _hy_fwd(u, cb, sbf, kre, kim)
    y_h = _hy_inv(yre, yim, x0, u, hy_bias, cb, sbi)
    return y_f, y_h


def kernel(x, c, ctx, c_ctx, norm1, norm2, w_mod, b_mod, ffn_w_up, ffn_conv_w, ffn_conv_b, ffn_w_down,
           fh_w_in, fh_w_out, hy_conv_w, hy_conv_b, hy_filt_w1, hy_filt_b1, hy_filt_w2, hy_filt_b2,
           hy_filt_w3, hy_filt_b3, hy_filt_w4, hy_freq, hy_bias, mla_w_in, mla_q_a_norm, mla_w_uq,
           mla_kv_a_norm, mla_w_ukv, mla_q_norm, mla_k_norm, mla_w_o):
    B, L, D = x.shape
    Lc = ctx.shape[1]
    bf = lambda a: a.astype(BF16)

    n_cond = -(-(B + 1) // 8) * 8
    cond = jnp.concatenate([c, c_ctx[None, :], jnp.zeros((n_cond - B - 1, D), F32)], axis=0)

    def mods(i):
        m = _adaln(cond, w_mod[i], b_mod[i])
        mx = [m[:B, j * D:(j + 1) * D].reshape(B, 1, D) for j in range(6)]
        mc = [m[B:B + 1, j * D:(j + 1) * D].reshape(1, 1, D) for j in range(6)]
        return mx, mc

    (sh1, sc1, g1, sh2, sc2, g2), (csh1, csc1, cg1, csh2, csc2, cg2) = mods(0)
    gmat = _group_dft_matrix()
    w_in0 = bf(fh_w_in[0])
    w_out0 = bf(fh_w_out[0])
    ffn0 = (bf(ffn_w_up[0]), ffn_conv_w[0], ffn_conv_b[0], bf(ffn_w_down[0]))
    filt = (hy_filt_w1[0], hy_filt_b1[0], hy_filt_w2[0], hy_filt_b2[0], hy_filt_w3[0], hy_filt_b3[0],
            hy_filt_w4[0], hy_freq[0])

    def mixer_layer(h, n_seq, m1, m2):
        s1, c1, gt1 = m1
        s2, c2, gt2 = m2
        taps, norms = _hyena_filter(n_seq, *filt)
        y_f, y_h = _fourier_hyena_mixer(h, norm1[0], s1, c1, w_in0, hy_conv_w[0], hy_conv_b[0], gmat, taps,
                                        norms, hy_bias[0])
        return _mix_ffn(h, [y_f, y_h], w_out0, gt1, norm2[0], s2, c2, gt2, *ffn0)

    x = mixer_layer(x, L, (sh1, sc1, g1), (sh2, sc2, g2))
    ctx = mixer_layer(ctx, Lc, (csh1, csc1, cg1), (csh2, csc2, cg2))

    (sh1, sc1, g1, sh2, sc2, g2), (csh1, csc1, _, _, _, _) = mods(1)
    w_in1 = bf(jnp.pad(mla_w_in[0], ((0, 0), (0, MLA_IN_PAD - mla_w_in.shape[2]))))
    wuq = bf(_pad_heads(mla_w_uq[0], QK_DIM, QK_DIM).T)
    wuk = bf(_pad_heads(mla_w_ukv[0], NOPE + V_DIM, NOPE).T)
    wuv = bf(mla_w_ukv[0].reshape(KV_LORA, MLA_HEADS, NOPE + V_DIM)[:, :, NOPE:].reshape(KV_LORA, MLA_HEADS * V_DIM).T)
    half_scale = math.sqrt(QK_DIM ** -0.5 * math.log2(math.e))
    wts = (w_in1, mla_q_a_norm[0].reshape(1, Q_LORA), wuq, _pad_lanes(mla_q_norm[0] * half_scale, HEAD_PAD),
           mla_kv_a_norm[0].reshape(1, KV_LORA), wuk, wuv, _pad_lanes(mla_k_norm[0] * half_scale, HEAD_PAD))
    qt, kx, vtx = _mla_proj(x, norm1[1], sh1, sc1, wts, _rope_tables(L), True)
    kc, vtc = _mla_proj(ctx, norm1[1], csh1, csc1, wts, _identity_rope_tables(Lc), False)
    o = _attention(qt, kx, kc, vtx, vtc)
    ffn1 = (bf(ffn_w_up[1]), ffn_conv_w[1], ffn_conv_b[1], bf(ffn_w_down[1]))
    return _mix_ffn(x, [o], bf(mla_w_o[0]), g1, norm2[1], sh2, sc2, g2, *ffn1)
```

```python
import functools
import math

import jax
import jax.numpy as jnp
import numpy as np
from jax import lax
from jax.experimental import pallas as pl
from jax.experimental.pallas import tpu as pltpu

F32 = jnp.float32
BF16 = jnp.bfloat16
HIGHEST = lax.Precision.HIGHEST

EPS = 1e-6
D_MODEL = 1024
D_FF = 2816
D_FOURIER = 512
FOURIER_GROUP_DIM = 128
D_HYENA = 512
HYENA_EMB_DIM = 33
HYENA_EMB_PAD = 64
HYENA_BANDS = 16
HYENA_WIDTH = 64
HYENA_MIN_DECAY = math.log(1e-2) / 0.3
HYENA_MAX_DECAY = math.log(1e-2) / 1.5
MLA_HEADS = 16
Q_LORA = 256
KV_LORA = 128
NOPE = 64
ROPE = 32
QK_DIM = NOPE + ROPE
V_DIM = 64
HEAD_PAD = 128
MLA_IN_PAD = 512
GRID_W = 64
ROPE_THETA = 10000.0

ATTN_DTYPE = jnp.float8_e4m3fn
ATTN_P_SHIFT = 8.0
FFN_CHUNK = 256
FFT_RADIX = 128
FFT_KRON = 16
HALO = 16
VMEM_LIMIT = 56 * 1024 * 1024


def _cparams(sem, flags=None):
    return pltpu.CompilerParams(dimension_semantics=sem, vmem_limit_bytes=VMEM_LIMIT, flags=flags)


def _dot(a, b):
    return jnp.dot(a, b, preferred_element_type=F32)


def _norm_mod(x, g, shift, scale):
    ms = jnp.mean(x * x, axis=-1, keepdims=True)
    return (x * lax.rsqrt(ms + EPS) * g) * (1.0 + scale) + shift


def _silu(x):
    return x * (1.0 / (1.0 + jnp.exp(-x)))


def _row_tile(L, want):
    t = min(L, want)
    assert L % t == 0 and t % HALO == 0
    return t


def _mod_kernel(c_ref, w_ref, b_ref, o_ref):
    s = _silu(c_ref[...])
    o_ref[...] = jnp.dot(s, w_ref[...], precision=HIGHEST, preferred_element_type=F32) + b_ref[...]


def _adaln(cond, w_mod, b_mod):
    R, D = cond.shape
    n = w_mod.shape[1]
    tn = 768
    return pl.pallas_call(
        _mod_kernel,
        out_shape=jax.ShapeDtypeStruct((R, n), F32),
        grid=(n // tn,),
        in_specs=[pl.BlockSpec((R, D), lambda j: (0, 0)),
                  pl.BlockSpec((D, tn), lambda j: (0, j)),
                  pl.BlockSpec((1, tn), lambda j: (0, j))],
        out_specs=pl.BlockSpec((R, tn), lambda j: (0, j)),
        compiler_params=_cparams(("arbitrary",)),
        name="adaln_mod",
    )(cond, w_mod, b_mod.reshape(1, n))


def _halo_specs(tm, D, L):
    nb = L // HALO
    per = tm // HALO
    cur = pl.BlockSpec((1, tm, D), lambda b, i, *_: (b, i, 0))
    prev = pl.BlockSpec((1, HALO, D), lambda b, i, *_: (b, jnp.maximum(i * per - 1, 0), 0))
    nxt = pl.BlockSpec((1, HALO, D), lambda b, i, *_: (b, jnp.minimum((i + 1) * per, nb - 1), 0))
    return cur, prev, nxt


def _mod_spec(arr):
    D = arr.shape[-1]
    if arr.shape[0] == 1:
        return pl.BlockSpec((1, 1, D), lambda b, *_: (0, 0, 0))
    return pl.BlockSpec((1, 1, D), lambda b, *_: (b, 0, 0))


def _fill_hn_rows(hn_ref, xc, xp, xn, g, shift, scale, tm, i, nt):
    hp = _norm_mod(xp, g, shift, scale)
    hn_ref[0:HALO, :] = jnp.where(i > 0, hp, 0.0).astype(BF16)
    hn_ref[HALO:HALO + tm, :] = _norm_mod(xc, g, shift, scale).astype(BF16)
    hx = _norm_mod(xn, g, shift, scale)
    hn_ref[HALO + tm:, :] = jnp.where(i < nt - 1, hx, 0.0).astype(BF16)


def _fill_hn(hn_ref, xc_ref, xp_ref, xn_ref, g, shift, scale, tm, i, nt):
    _fill_hn_rows(hn_ref, xc_ref[0], xp_ref[0], xn_ref[0], g, shift, scale, tm, i, nt)


def _dwconv3_rows(p, w, tm):
    n = tm + 2 * HALO
    up = pltpu.roll(p, 1, 0)
    dn = pltpu.roll(p, n - 1, 0)
    c = up * w[0:1] + p * w[1:2] + dn * w[2:3]
    return c[HALO:HALO + tm]


def _fh_in_kernel(xc_ref, xp_ref, xn_ref, g_ref, sh_ref, sc_ref, w_ref, cw_ref, cb_ref, gm_ref, *rest, tm, n_hi):
    if n_hi:
        perm_ref, z_ref, u_ref, x0_ref, hn_ref = rest
    else:
        z_ref, u_ref, x0_ref, hn_ref = rest
    i = pl.program_id(1)
    nt = pl.num_programs(1)
    _fill_hn(hn_ref, xc_ref, xp_ref, xn_ref, g_ref[...], sh_ref[0], sc_ref[0], tm, i, nt)
    proj = _dot(hn_ref[...], w_ref[...])
    uf = proj[HALO:HALO + tm, :D_FOURIER].astype(BF16)
    if n_hi:
        uf = _dot(perm_ref[...], uf).astype(BF16)
        z = _dot(uf, gm_ref[...]).astype(BF16)
        z_ref[0] = z.reshape(n_hi, tm // n_hi, 2 * D_FOURIER)
    else:
        z_ref[0] = _dot(uf, gm_ref[...]).astype(BF16)
    c = _dwconv3_rows(proj[:, D_FOURIER:], cw_ref[...], tm) + cb_ref[...]
    x0 = c[:, :D_HYENA]
    x1 = c[:, D_HYENA:2 * D_HYENA]
    v = c[:, 2 * D_HYENA:]
    x0_ref[0] = x0
    u_ref[0] = (v * x1).astype(BF16)


def _fh_in(x, g, shift, scale, w_in, conv_w, conv_b, gmat, n_hi=0):
    B, L, D = x.shape
    tm = _row_tile(L, 512)
    cur, prev, nxt = _halo_specs(tm, D, L)
    n_in = w_in.shape[1]
    nh = 3 * D_HYENA
    full = lambda shape: pl.BlockSpec(shape, lambda b, i: (0,) * len(shape))
    out_tile = lambda c: pl.BlockSpec((1, tm, c), lambda b, i: (b, i, 0))
    in_specs = [cur, prev, nxt, full((1, D)), _mod_spec(shift), _mod_spec(scale),
                full((D, n_in)), full((3, nh)), full((1, nh)), full((D_FOURIER, 2 * D_FOURIER))]
    args = [x, x, x, g.reshape(1, D), shift, scale, w_in, conv_w, conv_b.reshape(1, nh), gmat]
    if n_hi:
        per = tm // n_hi
        r = np.arange(tm)
        src = n_hi * (r % per) + r // per
        args.append(_const(src[:, None] == r[None, :], BF16))
        in_specs.append(full((tm, tm)))
        z_shape = jax.ShapeDtypeStruct((B, n_hi, L // n_hi, 2 * D_FOURIER), BF16)
        z_spec = pl.BlockSpec((1, n_hi, per, 2 * D_FOURIER), lambda b, i: (b, 0, i, 0))
    else:
        z_shape = jax.ShapeDtypeStruct((B, L, 2 * D_FOURIER), BF16)
        z_spec = out_tile(2 * D_FOURIER)
    return pl.pallas_call(
        functools.partial(_fh_in_kernel, tm=tm, n_hi=n_hi),
        out_shape=(z_shape, jax.ShapeDtypeStruct((B, L, D_HYENA), BF16), jax.ShapeDtypeStruct((B, L, D_HYENA), F32)),
        grid=(B, L // tm),
        in_specs=in_specs,
        out_specs=(z_spec, out_tile(D_HYENA), out_tile(D_HYENA)),
        scratch_shapes=[pltpu.VMEM((tm + 2 * HALO, D), BF16)],
        compiler_params=_cparams(("parallel", "arbitrary")),
        name="fh_in",
    )(*args)


def _fourier_kernel(c_ref, s_ref, z_ref, o_ref, *, scale):
    z = z_ref[0]
    y = _dot(c_ref[...], z[:, :D_FOURIER]) + _dot(s_ref[...], z[:, D_FOURIER:])
    o_ref[0] = (y * scale).astype(BF16)


def _fourier_seq(z, cl, sl):
    B, L, _ = z.shape
    tf = _row_tile(L, 512)
    scale = 1.0 / math.sqrt(L * FOURIER_GROUP_DIM)
    return pl.pallas_call(
        functools.partial(_fourier_kernel, scale=scale),
        out_shape=jax.ShapeDtypeStruct((B, L, D_FOURIER), BF16),
        grid=(B, L // tf),
        in_specs=[pl.BlockSpec((tf, L), lambda b, i: (i, 0)),
                  pl.BlockSpec((tf, L), lambda b, i: (i, 0)),
                  pl.BlockSpec((1, L, 2 * D_FOURIER), lambda b, i: (b, 0, 0))],
        out_specs=pl.BlockSpec((1, tf, D_FOURIER), lambda b, i: (b, i, 0)),
        compiler_params=_cparams(("parallel", "arbitrary")),
        name="fourier_seq",
    )(cl, sl, z)


def _fft_f1_kernel(cs_ref, z_ref, o_ref, *, nc):
    for j in range(nc):
        cs = _dot(cs_ref[...], z_ref[0, j])
        cz = cs[:FFT_RADIX]
        sz = cs[FFT_RADIX:]
        o_ref[0, 0, j] = (cz[:, :D_FOURIER] + sz[:, D_FOURIER:]).astype(BF16)
        o_ref[0, 1, j] = (cz[:, D_FOURIER:] - sz[:, :D_FOURIER]).astype(BF16)


def _fft_f2_kernel(t_ref, b_ref, o_ref, *, scale):
    n_hi = b_ref.shape[2]
    blk = b_ref[0].reshape(2 * n_hi * 8, D_FOURIER)
    y = _dot(t_ref[0], blk) * scale
    o_ref[0, :, 0] = y.reshape(n_hi, 8, D_FOURIER).astype(BF16)


def _fourier_seq_fft(zp, cs, tmat):
    B, n_hi, _, _ = zp.shape
    L = n_hi * FFT_RADIX
    nc = 8
    ng = FFT_RADIX // 8
    bc = pl.pallas_call(
        functools.partial(_fft_f1_kernel, nc=nc),
        out_shape=jax.ShapeDtypeStruct((B, 2, n_hi, FFT_RADIX, D_FOURIER), BF16),
        grid=(B, n_hi // nc),
        in_specs=[pl.BlockSpec((2 * FFT_RADIX, FFT_RADIX), lambda b, i: (0, 0)),
                  pl.BlockSpec((1, nc, FFT_RADIX, 2 * D_FOURIER), lambda b, i: (b, i, 0, 0))],
        out_specs=pl.BlockSpec((1, 2, nc, FFT_RADIX, D_FOURIER), lambda b, i: (b, 0, i, 0, 0)),
        compiler_params=_cparams(("parallel", "arbitrary")),
        name="fourier_fft1",
    )(cs, zp)
    scale = 1.0 / math.sqrt(L * FOURIER_GROUP_DIM)
    y = pl.pallas_call(
        functools.partial(_fft_f2_kernel, scale=scale),
        out_shape=jax.ShapeDtypeStruct((B, n_hi, ng, 8, D_FOURIER), BF16),
        grid=(B, ng),
        in_specs=[pl.BlockSpec((1, n_hi * 8, 2 * n_hi * 8), lambda b, g: (g, 0, 0)),
                  pl.BlockSpec((1, 2, n_hi, 8, D_FOURIER), lambda b, g: (b, 0, 0, g, 0))],
        out_specs=pl.BlockSpec((1, n_hi, 1, 8, D_FOURIER), lambda b, g: (b, 0, g, 0, 0)),
        compiler_params=_cparams(("parallel", "arbitrary")),
        name="fourier_fft2",
    )(tmat, bc)
    return y.reshape(B, L, D_FOURIER)


def _filter_kernel(z_ref, w1_ref, b1_ref, w2_ref, b2_ref, w3_ref, b3_ref, w4_ref, f_ref, dl_ref,
                   k_ref, n_ref):
    i = pl.program_id(1)
    z = z_ref[0]
    f = f_ref[...]
    hd = lambda a, w: jnp.dot(a, w, precision=HIGHEST, preferred_element_type=F32)
    h = jnp.sin(f * (hd(z, w1_ref[...]) + b1_ref[...]))
    h = jnp.sin(f * (hd(h, w2_ref[...]) + b2_ref[...]))
    h = jnp.sin(f * (hd(h, w3_ref[...]) + b3_ref[...]))
    h = hd(h, w4_ref[0])
    t = z[:, 0:1]
    valid = z[:, HYENA_EMB_DIM:HYENA_EMB_DIM + 1]
    k = h * jnp.exp(-t * jnp.abs(dl_ref[...])) * valid
    k_ref[0] = k

    @pl.when(i == 0)
    def _():
        n_ref[...] = jnp.zeros_like(n_ref)

    n_ref[0] += jnp.sum(jnp.abs(k), axis=0, keepdims=True)


def _hyena_filter(L, w1, b1, w2, b2, w3, b3, w4, freq):
    pos = np.arange(L, dtype=np.float64)
    bands = np.linspace(1e-4, HYENA_BANDS - 1, HYENA_BANDS)

    def emb(p, valid):
        t = p / max(L - 1, 1)
        ang = (2.0 * math.pi / L) * p[:, None] * bands[None, :]
        pad = np.zeros((L, HYENA_EMB_PAD - HYENA_EMB_DIM - 1))
        return np.concatenate([t[:, None], np.cos(ang), -np.sin(ang), valid[:, None], pad], axis=-1)

    zf = emb(pos, np.ones((L,)))
    zb = emb(np.where(pos > 0, L - pos, 0.0), (pos > 0).astype(np.float64))
    z = _const(np.stack([zf, zb]))
    w1p = jnp.concatenate([w1, jnp.zeros((HYENA_EMB_PAD - HYENA_EMB_DIM, HYENA_WIDTH), F32)], axis=0)
    w4s = jnp.stack([w4[:, :D_HYENA], w4[:, D_HYENA:]])
    deltas = _const(np.linspace(HYENA_MIN_DECAY, HYENA_MAX_DECAY, D_HYENA).reshape(1, D_HYENA))
    tr = _row_tile(L, 512)
    W = HYENA_WIDTH
    full = lambda shape: pl.BlockSpec(shape, lambda s, i: (0,) * len(shape))
    return pl.pallas_call(
        _filter_kernel,
        out_shape=(jax.ShapeDtypeStruct((2, L, D_HYENA), F32), jax.ShapeDtypeStruct((2, 1, D_HYENA), F32)),
        grid=(2, L // tr),
        in_specs=[pl.BlockSpec((1, tr, HYENA_EMB_PAD), lambda s, i: (s, i, 0)),
                  full((HYENA_EMB_PAD, W)), full((1, W)), full((W, W)), full((1, W)),
                  full((W, W)), full((1, W)),
                  pl.BlockSpec((1, W, D_HYENA), lambda s, i: (s, 0, 0)),
                  full((1, W)), full((1, D_HYENA))],
        out_specs=(pl.BlockSpec((1, tr, D_HYENA), lambda s, i: (s, i, 0)),
                   pl.BlockSpec((1, 1, D_HYENA), lambda s, i: (s, 0, 0))),
        compiler_params=_cparams(("arbitrary", "arbitrary")),
        name="hyena_filter",
    )(z, w1p, b1.reshape(1, W), w2, b2.reshape(1, W), w3, b3.reshape(1, W), w4s, freq.reshape(1, W), deltas)


def _hy_fwd_kernel(c_ref, s_ref, u_ref, *rest, raw, tf):
    u = u_ref[0]
    ure = _dot(c_ref[...], u)
    uim = -_dot(s_ref[...], u)
    if raw:
        re_ref, im_ref = rest
        re_ref[0] = ure
        im_ref[0] = uim
        return
    kre_ref, kim_ref, re_ref, im_ref = rest
    kre = kre_ref[...]
    kim = kim_ref[...]
    row = lax.broadcasted_iota(jnp.int32, (tf, 1), 0) + pl.program_id(1) * tf
    packed = row == 0
    a = uim * kim
    re_ref[0] = (ure * kre - jnp.where(packed, 0.0, a)).astype(BF16)
    im_ref[0] = jnp.where(packed, a, ure * kim + uim * kre).astype(BF16)


def _hy_fwd(u, cb, sbf, kre=None, kim=None):
    B, L, C = u.shape
    tf = _row_tile(L, 512)
    raw = kre is None
    mat = pl.BlockSpec((tf, L), lambda b, i: (i, 0))
    tile = pl.BlockSpec((1, tf, C), lambda b, i: (b, i, 0))
    in_specs = [mat, mat, pl.BlockSpec((1, L, C), lambda b, i: (b, 0, 0))]
    args = [cb, sbf, u]
    if not raw:
        in_specs += [pl.BlockSpec((tf, C), lambda b, i: (i, 0))] * 2
        args += [kre, kim]
    dt = F32 if raw else BF16
    return pl.pallas_call(
        functools.partial(_hy_fwd_kernel, raw=raw, tf=tf),
        out_shape=(jax.ShapeDtypeStruct((B, L, C), dt), jax.ShapeDtypeStruct((B, L, C), dt)),
        grid=(B, L // tf),
        in_specs=in_specs,
        out_specs=(tile, tile),
        compiler_params=_cparams(("parallel", "arbitrary")),
        name="hyena_fwd_raw" if raw else "hyena_fwd",
    )(*args)


def _spec_combine_kernel(re_ref, im_ref, n_ref, kre_ref, kim_ref, *, tf, n_fft):
    row = lax.broadcasted_iota(jnp.int32, (tf, 1), 0) + pl.program_id(0) * tf
    sgn = (1 - 2 * (row & 1)).astype(F32)
    wgt = jnp.where(row == 0, 1.0 / n_fft, 2.0 / n_fft)
    s = wgt / (n_ref[0] + n_ref[1])
    kre_ref[...] = (re_ref[0] + sgn * re_ref[1]) * s
    kim_ref[...] = (im_ref[0] + sgn * im_ref[1]) * s


def _spec_combine(re, im, norms):
    _, L, C = re.shape
    tf = _row_tile(L, 512)
    pair = pl.BlockSpec((2, tf, C), lambda i: (0, i, 0))
    tile = pl.BlockSpec((tf, C), lambda i: (i, 0))
    return pl.pallas_call(
        functools.partial(_spec_combine_kernel, tf=tf, n_fft=2 * L),
        out_shape=(jax.ShapeDtypeStruct((L, C), F32), jax.ShapeDtypeStruct((L, C), F32)),
        grid=(L // tf,),
        in_specs=[pair, pair, pl.BlockSpec((2, 1, C), lambda i: (0, 0, 0))],
        out_specs=(tile, tile),
        compiler_params=_cparams(("arbitrary",)),
        name="hyena_spec_combine",
    )(re, im, norms)


def _hy_inv_kernel(c_ref, s_ref, re_ref, im_ref, x0_ref, u_ref, b_ref, o_ref):
    y = _dot(c_ref[...], re_ref[0]) - _dot(s_ref[...], im_ref[0])
    o_ref[0] = (x0_ref[0] * (y + u_ref[0].astype(F32) * b_ref[...])).astype(BF16)


def _hy_inv(yre, yim, x0, u, bias, cb, sbi):
    B, L, C = u.shape
    tt = _row_tile(L, 512)
    mat = pl.BlockSpec((tt, L), lambda b, i: (i, 0))
    whole = pl.BlockSpec((1, L, C), lambda b, i: (b, 0, 0))
    tile = pl.BlockSpec((1, tt, C), lambda b, i: (b, i, 0))
    return pl.pallas_call(
        _hy_inv_kernel,
        out_shape=jax.ShapeDtypeStruct((B, L, C), BF16),
        grid=(B, L // tt),
        in_specs=[mat, mat, whole, whole, tile, tile, pl.BlockSpec((1, C), lambda b, i: (0, 0))],
        out_specs=tile,
        compiler_params=_cparams(("parallel", "arbitrary")),
        name="hyena_inv",
    )(cb, sbi, yre, yim, x0, u, bias.reshape(1, C))


def _fft_a_kernel(t_ref, x_ref, o_ref):
    r2, kr, c = x_ref.shape[1:]
    y = _dot(t_ref[...], x_ref[0].reshape(r2 * kr, c))
    o_ref[0] = y.reshape(2, y.shape[0] // (2 * kr), kr, c).astype(BF16)


def _hy_fft_a(x4, t1):
    Bx, R, _, C = x4.shape
    n_hi = t1.shape[0] // (2 * FFT_KRON)
    return pl.pallas_call(
        _fft_a_kernel,
        out_shape=jax.ShapeDtypeStruct((Bx, 2, n_hi, FFT_RADIX, C), BF16),
        grid=(Bx, FFT_RADIX // FFT_KRON),
        in_specs=[pl.BlockSpec(t1.shape, lambda b, i: (0, 0)),
                  pl.BlockSpec((1, R, FFT_KRON, C), lambda b, i: (b, 0, i, 0))],
        out_specs=pl.BlockSpec((1, 2, n_hi, FFT_KRON, C), lambda b, i: (b, 0, 0, i, 0)),
        compiler_params=_cparams(("parallel", "arbitrary")),
        name="hyena_fft_a",
    )(t1, x4)


def _fft_b_kernel(gf_ref, a_ref, *rest, raw, n_fft):
    nb = a_ref.shape[0]
    gf = gf_ref[0]
    if raw:
        n_ref, o_ref = rest
        s = 1.0 / (n_fft * (n_ref[0] + n_ref[1]))
    else:
        gi_ref, k_ref, o_ref = rest
        gi = gi_ref[0]
        kr = k_ref[0, 0, 0]
        ki = k_ref[0, 1, 0]
    for b in range(nb):
        a = a_ref[b, :, 0].reshape(2 * FFT_RADIX, D_HYENA)
        x = _dot(gf, a)
        if raw:
            o_ref[b, :, 0] = (x * s).reshape(2, FFT_RADIX, D_HYENA)
            continue
        xr = x[:FFT_RADIX]
        xi = x[FFT_RADIX:]
        y = jnp.concatenate([xr * kr - xi * ki, xr * ki + xi * kr], axis=0).astype(BF16)
        o_ref[b, :, 0] = _dot(gi, y).reshape(2, FFT_RADIX, D_HYENA).astype(BF16)


def _hy_fft_b(a5, gf, gi=None, kspec=None, norms=None):
    Bx, _, n_hi, _, C = a5.shape
    raw = kspec is None
    blk = pl.BlockSpec((Bx, 2, 1, FFT_RADIX, C), lambda k: (0, 0, k, 0, 0))
    tab = pl.BlockSpec((1, 2 * FFT_RADIX, 2 * FFT_RADIX), lambda k: (k, 0, 0))
    if raw:
        in_specs = [tab, blk, pl.BlockSpec((2, 1, C), lambda k: (0, 0, 0))]
        args = (gf, a5, norms)
    else:
        kblk = pl.BlockSpec((1, 2, 1, FFT_RADIX, C), lambda k: (0, 0, k, 0, 0))
        in_specs = [tab, blk, tab, kblk]
        args = (gf, a5, gi, kspec)
    return pl.pallas_call(
        functools.partial(_fft_b_kernel, raw=raw, n_fft=n_hi * FFT_RADIX),
        out_shape=jax.ShapeDtypeStruct(a5.shape, F32 if raw else BF16),
        grid=(n_hi,),
        in_specs=in_specs,
        out_specs=blk,
        compiler_params=_cparams(("arbitrary",)),
        name="hyena_fft_spec" if raw else "hyena_fft_b",
    )(*args)


def _fft_c_kernel(t_ref, v_ref, x0_ref, u_ref, b_ref, o_ref):
    _, n_hi, kr, c = v_ref.shape[1:]
    y = _dot(t_ref[...], v_ref[0].reshape(2 * n_hi * kr, c))
    y = y.reshape(y.shape[0] // kr, kr, c)
    o_ref[0] = (x0_ref[0] * (y + u_ref[0].astype(F32) * b_ref[...])).astype(BF16)


def _hy_fft_c(v5, t3, x0, u, bias):
    B, _, n_hi, _, C = v5.shape
    R = t3.shape[0] // FFT_KRON
    tile = pl.BlockSpec((1, R, FFT_KRON, C), lambda b, i: (b, 0, i, 0))
    return pl.pallas_call(
        _fft_c_kernel,
        out_shape=jax.ShapeDtypeStruct((B, R, FFT_RADIX, C), BF16),
        grid=(B, FFT_RADIX // FFT_KRON),
        in_specs=[pl.BlockSpec(t3.shape, lambda b, i: (0, 0)),
                  pl.BlockSpec((1, 2, n_hi, FFT_KRON, C), lambda b, i: (b, 0, 0, i, 0)),
                  tile, tile, pl.BlockSpec((1, C), lambda b, i: (0, 0))],
        out_specs=tile,
        compiler_params=_cparams(("parallel", "arbitrary")),
        name="hyena_fft_c",
    )(t3, v5, x0, u, bias.reshape(1, C))


def _ffn_kernel(xc_ref, xp_ref, xn_ref, g1_ref, wo_ref, g_ref, sh_ref, sc_ref, gate_ref, wu_ref, cw_ref, cb_ref,
                wd_ref, *rest, tm, n_y):
    y_refs = rest[:3 * n_y]
    o_ref, yext_ref, hn_ref, act_ref = rest[3 * n_y:]
    i = pl.program_id(1)
    nt = pl.num_programs(1)
    off = 0
    for k in range(n_y):
        yc_ref, yp_ref, yn_ref = y_refs[3 * k:3 * k + 3]
        c = yc_ref.shape[-1]
        yext_ref[0:HALO, off:off + c] = yp_ref[0].astype(BF16)
        yext_ref[HALO:HALO + tm, off:off + c] = yc_ref[0].astype(BF16)
        yext_ref[HALO + tm:, off:off + c] = yn_ref[0].astype(BF16)
        off += c
    mix = g1_ref[0] * _dot(yext_ref[...], wo_ref[...])
    x1 = xc_ref[0] + mix[HALO:HALO + tm]
    _fill_hn_rows(hn_ref, x1, xp_ref[0] + mix[:HALO], xn_ref[0] + mix[HALO + tm:], g_ref[...], sh_ref[0], sc_ref[0],
                  tm, i, nt)
    for c in range(0, D_FF, FFN_CHUNK):
        cols = slice(c, c + FFN_CHUNK)
        gx = _dot(hn_ref[...], wu_ref[:, cols])
        vx = _dot(hn_ref[HALO:HALO + tm, :], wu_ref[:, D_FF + c:D_FF + c + FFN_CHUNK])
        cv = _dwconv3_rows(gx, cw_ref[:, cols], tm) + cb_ref[:, cols]
        act_ref[:, cols] = (_silu(cv) * vx).astype(BF16)
    o_ref[0] = x1 + gate_ref[0] * _dot(act_ref[...], wd_ref[...])


def _mix_ffn(x, ys, w_o, gate1, g, shift, scale, gate2, w_up, conv_w, conv_b, w_down):
    B, L, D = x.shape
    tm = _row_tile(L, 512)
    cur, prev, nxt = _halo_specs(tm, D, L)
    const = lambda shape: pl.BlockSpec(shape, lambda b, i: (0,) * len(shape), pipeline_mode=pl.Buffered(1))
    y_specs, y_args = [], []
    for y in ys:
        y_specs += list(_halo_specs(tm, y.shape[-1], L))
        y_args += [y, y, y]
    return pl.pallas_call(
        functools.partial(_ffn_kernel, tm=tm, n_y=len(ys)),
        out_shape=jax.ShapeDtypeStruct((B, L, D), F32),
        grid=(B, L // tm),
        in_specs=[cur, prev, nxt, _mod_spec(gate1), const(w_o.shape), const((1, D)), _mod_spec(shift),
                  _mod_spec(scale), _mod_spec(gate2), const((D, 2 * D_FF)), const((3, D_FF)), const((1, D_FF)),
                  const((D_FF, D))] + y_specs,
        out_specs=pl.BlockSpec((1, tm, D), lambda b, i: (b, i, 0)),
        scratch_shapes=[pltpu.VMEM((tm + 2 * HALO, w_o.shape[0]), BF16), pltpu.VMEM((tm + 2 * HALO, D), BF16),
                        pltpu.VMEM((tm, D_FF), BF16)],
        compiler_params=_cparams(("parallel", "arbitrary")),
        name="mix_ffn",
    )(x, x, x, gate1, w_o, g.reshape(1, D), shift, scale, gate2, w_up, conv_w, conv_b.reshape(1, D_FF), w_down,
      *y_args)


def _rms(v, g, n):
    return v * lax.rsqrt(jnp.sum(v * v, axis=-1, keepdims=True) * (1.0 / n) + EPS) * g


def _rope_gain_tables(gain, cos, sa, sb):
    rot = slice(NOPE, QK_DIM)
    gr = gain[rot]
    return gain[:NOPE], gr * cos[rot], pltpu.roll(gr, 8, 0) * sa[rot], pltpu.roll(gr, ROPE - 8, 0) * sb[rot]


def _rope_t(yr, gc, gsa, gsb):
    return yr * gc + pltpu.roll(yr, 8, 0) * gsa + pltpu.roll(yr, ROPE - 8, 0) * gsb


def _sumsq(x):
    return jnp.sum(x * x, axis=0, keepdims=True)


def _mla_proj_kernel(x_ref, g_ref, sh_ref, sc_ref, win_ref, qan_ref, wuq_ref, qn_ref, kvn_ref, wuk_ref,
                     wuv_ref, kn_ref, cos_ref, sa_ref, sb_ref, *outs, with_q):
    hn = _norm_mod(x_ref[0], g_ref[...], sh_ref[0], sc_ref[0]).astype(BF16)
    a = _dot(hn, win_ref[...])
    tm = a.shape[0]
    cos = cos_ref[...]
    sa = sa_ref[...]
    sb = sb_ref[...]
    pad = jnp.zeros((HEAD_PAD - QK_DIM, tm), ATTN_DTYPE)
    if with_q:
        qt_ref, k_ref, vt_ref = outs
        qat = _rms(a[:, :Q_LORA], qan_ref[...], Q_LORA).T.astype(BF16)
        qft = _dot(wuq_ref[...], qat)
        gn, gc, gsa, gsb = _rope_gain_tables(qn_ref[...], cos, sa, sb)
        for h in range(MLA_HEADS):
            base = h * HEAD_PAD
            xn = qft[base:base + NOPE]
            xr = qft[base + NOPE:base + QK_DIM]
            r = lax.rsqrt((_sumsq(xn) + _sumsq(xr)) * (1.0 / QK_DIM) + EPS)
            qt_ref[0, base:base + NOPE, :] = (xn * r * gn).astype(ATTN_DTYPE)
            qt_ref[0, base + NOPE:base + QK_DIM, :] = _rope_t(xr * r, gc, gsa, gsb).astype(ATTN_DTYPE)
            qt_ref[0, base + QK_DIM:base + HEAD_PAD, :] = pad
    else:
        k_ref, vt_ref = outs
    ckvt = _rms(a[:, Q_LORA:Q_LORA + KV_LORA], kvn_ref[...], KV_LORA).T.astype(BF16)
    kft = _dot(wuk_ref[...], ckvt)
    vt_ref[0] = _dot(wuv_ref[...], ckvt).astype(ATTN_DTYPE)
    kpe = a[:, Q_LORA + KV_LORA:].T[:ROPE]
    gn, gc, gsa, gsb = _rope_gain_tables(kn_ref[...], cos, sa, sb)
    kpe_rot = _rope_t(kpe, gc, gsa, gsb)
    kpe_ss = _sumsq(kpe)
    for h in range(MLA_HEADS):
        base = h * HEAD_PAD
        xn = kft[base:base + NOPE]
        r = lax.rsqrt((_sumsq(xn) + kpe_ss) * (1.0 / QK_DIM) + EPS)
        kt = jnp.concatenate([xn * r * gn, kpe_rot * r, jnp.zeros((HEAD_PAD - QK_DIM, tm), F32)], axis=0)
        k_ref[0, :, base:base + HEAD_PAD] = kt.T.astype(ATTN_DTYPE)


def _mla_proj(x, g, shift, scale, wts, rope_tabs, with_q):
    B, L, D = x.shape
    tm = _row_tile(L, 512)
    win, qan, wuq, qn, kvn, wuk, wuv, kn = wts
    cos, sa, sb = rope_tabs
    HP = MLA_HEADS * HEAD_PAD
    HV = MLA_HEADS * V_DIM
    full = lambda a: pl.BlockSpec(a.shape, lambda b, i: (0,) * a.ndim)
    tab = pl.BlockSpec((HEAD_PAD, tm), lambda b, i: (0, i))
    out_shape = [jax.ShapeDtypeStruct((B, L, HP), ATTN_DTYPE), jax.ShapeDtypeStruct((B, HV, L), ATTN_DTYPE)]
    out_specs = [pl.BlockSpec((1, tm, HP), lambda b, i: (b, i, 0)),
                 pl.BlockSpec((1, HV, tm), lambda b, i: (b, 0, i))]
    if with_q:
        out_shape = [jax.ShapeDtypeStruct((B, HP, L), ATTN_DTYPE)] + out_shape
        out_specs = [pl.BlockSpec((1, HP, tm), lambda b, i: (b, 0, i))] + out_specs
    g2 = g.reshape(1, D)
    gain_tab = lambda v: jnp.broadcast_to(v.reshape(HEAD_PAD, 1), (HEAD_PAD, tm))
    qn = gain_tab(qn)
    kn = gain_tab(kn)
    return pl.pallas_call(
        functools.partial(_mla_proj_kernel, with_q=with_q),
        out_shape=tuple(out_shape),
        grid=(B, L // tm),
        in_specs=[pl.BlockSpec((1, tm, D), lambda b, i: (b, i, 0)), full(g2), _mod_spec(shift), _mod_spec(scale),
                  full(win), full(qan), full(wuq), full(qn), full(kvn), full(wuk), full(wuv), full(kn),
                  tab, tab, tab],
        out_specs=tuple(out_specs),
        compiler_params=_cparams(("parallel", "arbitrary")),
        name="mla_proj_q" if with_q else "mla_proj_kv",
    )(x, g2, shift, scale, win, qan, wuq, qn, kvn, wuk, wuv, kn, cos, sa, sb)


def _attn_kernel(qt_ref, kx_ref, kc_ref, vtx_ref, vtc_ref, o_ref, *scratch):
    t = pl.program_id(0)
    io = (qt_ref, kx_ref, kc_ref, vtx_ref, vtc_ref, o_ref)
    set_a, set_b = scratch[:5], scratch[5:]

    @pl.when(t == 0)
    def _():
        for ref in scratch:
            ref[...] = jnp.zeros(ref.shape, ref.dtype)

    @pl.when(t % 2 == 0)
    def _():
        _attn_step(*io, set_a, set_b)

    @pl.when(t % 2 == 1)
    def _():
        _attn_step(*io, set_b, set_a)


def _attn_step(qt_ref, kx_ref, kc_ref, vtx_ref, vtc_ref, o_ref, cur, prev):
    wx_ref, wc_ref, wm_ref, px_ref, pc_ref = cur
    rx_ref, rc_ref, rm_ref, qx_ref, qc_ref = prev
    tq = qt_ref.shape[2]
    off = rm_ref[...][None]

    def probs(s_ref):
        n = s_ref.shape[0]
        s = s_ref[...].reshape(n // 16, 16, 2 * tq)
        return jnp.exp2(s - off).reshape(n, 2 * tq).astype(ATTN_DTYPE)

    qx_ref[...] = probs(rx_ref)
    qc_ref[...] = probs(rc_ref)

    q0 = qt_ref[0, :HEAD_PAD, :]
    q1 = qt_ref[0, HEAD_PAD:, :]
    zq = jnp.zeros_like(q0)
    qbd = jnp.concatenate([jnp.concatenate([q0, zq], axis=1), jnp.concatenate([zq, q1], axis=1)], axis=0)
    nx = _dot(kx_ref[0], qbd).astype(BF16)
    nc = _dot(kc_ref[0], qbd).astype(BF16)
    wx_ref[...] = nx
    wc_ref[...] = nc
    m = jnp.maximum(jnp.max(nx, axis=0, keepdims=True), jnp.max(nc, axis=0, keepdims=True)).astype(F32)
    wm_ref[...] = jnp.broadcast_to((m - ATTN_P_SHIFT).astype(BF16), wm_ref.shape)

    vtx = vtx_ref[0]
    vtc = vtc_ref[0]
    halves = []
    for j in range(2):
        own = slice(j * V_DIM, (j + 1) * V_DIM)
        oth = slice((1 - j) * V_DIM, (2 - j) * V_DIM)
        cols = slice(j * tq, (j + 1) * tq)
        parts_x = [vtx[own], jnp.ones_like(vtx[oth])]
        parts_c = [vtc[own], jnp.ones_like(vtc[oth])]
        if j == 1:
            parts_x.reverse()
            parts_c.reverse()
        r = (_dot(jnp.concatenate(parts_x, axis=0), px_ref[:, cols])
             + _dot(jnp.concatenate(parts_c, axis=0), pc_ref[:, cols]))
        halves.append(r[own] * (1.0 / r[oth][0:1]))
    o_ref[0] = jnp.concatenate(halves, axis=0).T.astype(BF16)


def _attention(qt, kx, kc, vtx, vtc):
    B, HP, L = qt.shape
    Lc = kc.shape[1]
    tq = _row_tile(L, 256)
    nhp = MLA_HEADS // 2
    nq = L // tq
    n_tiles = B * nhp * nq

    def tile(t):
        t = jnp.clip(t, 0, n_tiles - 1)
        return t // (nhp * nq), (t // nq) % nhp, t % nq

    def logits_side(f):
        return lambda t: f(*tile(t))

    def value_side(f):
        return lambda t: f(*tile(t - 2))

    scratch_set = [pltpu.VMEM((L, 2 * tq), BF16), pltpu.VMEM((Lc, 2 * tq), BF16), pltpu.VMEM((16, 2 * tq), BF16),
                   pltpu.VMEM((L, 2 * tq), ATTN_DTYPE), pltpu.VMEM((Lc, 2 * tq), ATTN_DTYPE)]
    return pl.pallas_call(
        _attn_kernel,
        out_shape=jax.ShapeDtypeStruct((B, L, MLA_HEADS * V_DIM), BF16),
        grid=(n_tiles + 2,),
        in_specs=[pl.BlockSpec((1, 2 * HEAD_PAD, tq), logits_side(lambda b, h, i: (b, h, i))),
                  pl.BlockSpec((1, L, 2 * HEAD_PAD), logits_side(lambda b, h, i: (b, 0, h))),
                  pl.BlockSpec((1, Lc, 2 * HEAD_PAD), logits_side(lambda b, h, i: (b, 0, h))),
                  pl.BlockSpec((1, 2 * V_DIM, L), value_side(lambda b, h, i: (b, h, 0))),
                  pl.BlockSpec((1, 2 * V_DIM, Lc), value_side(lambda b, h, i: (b, h, 0)))],
        out_specs=pl.BlockSpec((1, tq, 2 * V_DIM), value_side(lambda b, h, i: (b, i, h))),
        scratch_shapes=scratch_set * 2,
        compiler_params=_cparams(("arbitrary",)),
        name="mla_attention",
    )(qt, kx, kc, vtx, vtc)


def _const(a, dtype=F32):
    return jnp.asarray(np.asarray(a, np.float32)).astype(dtype)


def _angle(num, period):
    return (num % period).astype(np.float64) * (2.0 * math.pi / period)


def _trig_matrix(n, period):
    r = np.arange(n, dtype=np.int64)
    a = _angle(r[:, None] * r[None, :], period)
    return np.cos(a), np.sin(a)


def _dft_tables(L):
    cl, sl = _trig_matrix(L, L)
    cb, sb = _trig_matrix(L, 2 * L)
    idx = np.arange(L)
    alt = (1 - 2 * (idx & 1)).astype(np.float64)
    sbf = np.where(idx[:, None] == 0, -alt[None, :], sb)
    sbi = np.where(idx[None, :] == 0, -alt[:, None], sb)
    return tuple(_const(m, BF16) for m in (cl, sl, cb, sbf, sbi))


def _fourier_fft_tables(L):
    n_hi = L // FFT_RADIX
    i = np.arange(FFT_RADIX, dtype=np.int64)
    a = _angle(i[:, None] * i[None, :], FFT_RADIX)
    cs = np.concatenate([np.cos(a), np.sin(a)], axis=0)
    ng = FFT_RADIX // 8
    g = np.arange(ng, dtype=np.int64)[:, None, None, None]
    k1 = np.arange(n_hi, dtype=np.int64)[None, :, None, None]
    j = np.arange(8, dtype=np.int64)[None, None, :, None]
    n1 = np.arange(n_hi, dtype=np.int64)[None, None, None, :]
    ang = _angle(n1 * (FFT_RADIX * k1 + 8 * g + j), L)
    eye = np.eye(8)
    blocks = [t[:, :, :, :, None] * eye[None, None, :, None, :] for t in (np.cos(ang), np.sin(ang))]
    tmat = np.stack(blocks, axis=3)
    return _const(cs, BF16), _const(tmat.reshape(ng, n_hi * 8, 2 * n_hi * 8), BF16)


def _hyena_fft_tables(L):
    n_fft = 2 * L
    n_hi = n_fft // FFT_RADIX
    k2 = np.arange(n_hi, dtype=np.int64)
    a1 = _angle(k2[:, None] * k2[None, :], n_hi)
    c1, s1 = np.cos(a1), np.sin(a1)
    eye = np.eye(FFT_KRON)
    kron = lambda m: (m[:, None, :, None] * eye[None, :, None, :]).reshape(m.shape[0] * FFT_KRON, m.shape[1] * FFT_KRON)
    t1 = _const(np.concatenate([kron(c1), kron(-s1)], axis=0), BF16)
    t1_half = t1[:, :n_hi // 2 * FFT_KRON]
    t3 = t1_half.T
    r = np.arange(FFT_RADIX, dtype=np.int64)
    k = k2[:, None, None] + n_hi * r[None, :, None]
    a2 = _angle(k * r[None, None, :], n_fft)
    c2, s2 = _const(np.cos(a2), BF16), _const(np.sin(a2), BF16)
    gf = jnp.concatenate([jnp.concatenate([c2, s2], axis=2), jnp.concatenate([-s2, c2], axis=2)], axis=1)
    c2t, s2t = jnp.swapaxes(c2, 1, 2), jnp.swapaxes(s2, 1, 2)
    gi = jnp.concatenate([jnp.concatenate([c2t, -s2t], axis=2), jnp.concatenate([s2t, c2t], axis=2)], axis=1)
    return t1, t1_half, t3, gf, gi


def _group_dft_matrix():
    g = FOURIER_GROUP_DIM
    ng = D_FOURIER // g
    c, s = _trig_matrix(g, g)
    eye = np.eye(ng)
    return _const(np.concatenate([np.kron(eye, c), -np.kron(eye, s)], axis=1), BF16)


def _rope_tables(L):
    pos = np.arange(L)
    rows = (pos // GRID_W).astype(np.float64)
    cols = (pos % GRID_W).astype(np.float64)
    nf = ROPE // 4
    inv_freq = ROPE_THETA ** (-np.arange(nf, dtype=np.float64) / nf)
    ar = rows[:, None] * inv_freq[None, :]
    ac = cols[:, None] * inv_freq[None, :]
    one = np.ones((L, NOPE))
    z_n = np.zeros((L, NOPE))
    z_f = np.zeros((L, nf))
    tail = np.zeros((L, HEAD_PAD - QK_DIM))
    cos = np.concatenate([one, np.cos(ar), np.cos(ar), np.cos(ac), np.cos(ac), tail], axis=1)
    sa = np.concatenate([z_n, z_f, np.sin(ar), z_f, np.sin(ac), tail], axis=1)
    sb = np.concatenate([z_n, -np.sin(ar), z_f, -np.sin(ac), z_f, tail], axis=1)
    return _const(cos.T), _const(sa.T), _const(sb.T)


def _identity_rope_tables(L):
    return jnp.ones((HEAD_PAD, L), F32), jnp.zeros((HEAD_PAD, L), F32), jnp.zeros((HEAD_PAD, L), F32)


def _pad_heads(w, per_head, keep):
    K = w.shape[0]
    w = w.reshape(K, MLA_HEADS, per_head)[:, :, :keep]
    w = jnp.pad(w, ((0, 0), (0, 0), (0, HEAD_PAD - keep)))
    return w.reshape(K, MLA_HEADS * HEAD_PAD)


def _pad_lanes(v, n):
    return jnp.pad(v, (0, n - v.shape[0])).reshape(1, n)


def _fourier_hyena_mixer(x, g, shift, scale, w_in, conv_w, conv_b, gmat, taps, norms, hy_bias):
    B, L, _ = x.shape
    if (L // FFT_RADIX) % 16 == 0:
        z, u, x0 = _fh_in(x, g, shift, scale, w_in, conv_w, conv_b, gmat, n_hi=L // FFT_RADIX)
        cs, tmat = _fourier_fft_tables(L)
        y_f = _fourier_seq_fft(z, cs, tmat)
        t1, t1_half, t3, gf, gi = _hyena_fft_tables(L)
        n_hi = 2 * L // FFT_RADIX
        rows = lambda a, r: a.reshape(a.shape[0], r, FFT_RADIX, D_HYENA)
        kspec = _hy_fft_b(_hy_fft_a(rows(taps.astype(BF16).reshape(1, 2 * L, D_HYENA), n_hi), t1), gf, norms=norms)
        v = _hy_fft_b(_hy_fft_a(rows(u, n_hi // 2), t1_half), gf, gi, kspec)
        y_h = _hy_fft_c(v, t3, rows(x0, n_hi // 2), rows(u, n_hi // 2), hy_bias)
        return y_f, y_h.reshape(B, L, D_HYENA)
    z, u, x0 = _fh_in(x, g, shift, scale, w_in, conv_w, conv_b, gmat)
    cl, sl, cb, sbf, sbi = _dft_tables(L)
    k_re, k_im = _hy_fwd(taps.astype(BF16), cb, sbf)
    kre, kim = _spec_combine(k_re, k_im, norms)
    y_f = _fourier_seq(z, cl, sl)
    yre, yim = _hy_fwd(u, cb, sbf, kre, kim)
    y_h = _hy_inv(yre, yim, x0, u, hy_bias, cb, sbi)
    return y_f, y_h


def kernel(x, c, ctx, c_ctx, norm1, norm2, w_mod, b_mod, ffn_w_up, ffn_conv_w, ffn_conv_b, ffn_w_down,
           fh_w_in, fh_w_out, hy_conv_w, hy_conv_b, hy_filt_w1, hy_filt_b1, hy_filt_w2, hy_filt_b2,
           hy_filt_w3, hy_filt_b3, hy_filt_w4, hy_freq, hy_bias, mla_w_in, mla_q_a_norm, mla_w_uq,
           mla_kv_a_norm, mla_w_ukv, mla_q_norm, mla_k_norm, mla_w_o):
    B, L, D = x.shape
    Lc = ctx.shape[1]
    bf = lambda a: a.astype(BF16)

    n_cond = -(-(B + 1) // 8) * 8
    cond = jnp.concatenate([c, c_ctx[None, :], jnp.zeros((n_cond - B - 1, D), F32)], axis=0)

    def mods(i):
        m = _adaln(cond, w_mod[i], b_mod[i])
        mx = [m[:B, j * D:(j + 1) * D].reshape(B, 1, D) for j in range(6)]
        mc = [m[B:B + 1, j * D:(j + 1) * D].reshape(1, 1, D) for j in range(6)]
        return mx, mc

    (sh1, sc1, g1, sh2, sc2, g2), (csh1, csc1, cg1, csh2, csc2, cg2) = mods(0)
    gmat = _group_dft_matrix()
    w_in0 = bf(fh_w_in[0])
    w_out0 = bf(fh_w_out[0])
    ffn0 = (bf(ffn_w_up[0]), ffn_conv_w[0], ffn_conv_b[0], bf(ffn_w_down[0]))
    filt = (hy_filt_w1[0], hy_filt_b1[0], hy_filt_w2[0], hy_filt_b2[0], hy_filt_w3[0], hy_filt_b3[0],
            hy_filt_w4[0], hy_freq[0])

    def mixer_layer(h, n_seq, m1, m2):
        s1, c1, gt1 = m1
        s2, c2, gt2 = m2
        taps, norms = _hyena_filter(n_seq, *filt)
        y_f, y_h = _fourier_hyena_mixer(h, norm1[0], s1, c1, w_in0, hy_conv_w[0], hy_conv_b[0], gmat, taps,
                                        norms, hy_bias[0])
        return _mix_ffn(h, [y_f, y_h], w_out0, gt1, norm2[0], s2, c2, gt2, *ffn0)

    x = mixer_layer(x, L, (sh1, sc1, g1), (sh2, sc2, g2))
    ctx = mixer_layer(ctx, Lc, (csh1, csc1, cg1), (csh2, csc2, cg2))

    (sh1, sc1, g1, sh2, sc2, g2), (csh1, csc1, _, _, _, _) = mods(1)
    w_in1 = bf(jnp.pad(mla_w_in[0], ((0, 0), (0, MLA_IN_PAD - mla_w_in.shape[2]))))
    wuq = bf(_pad_heads(mla_w_uq[0], QK_DIM, QK_DIM).T)
    wuk = bf(_pad_heads(mla_w_ukv[0], NOPE + V_DIM, NOPE).T)
    wuv = bf(mla_w_ukv[0].reshape(KV_LORA, MLA_HEADS, NOPE + V_DIM)[:, :, NOPE:].reshape(KV_LORA, MLA_HEADS * V_DIM).T)
    half_scale = math.sqrt(QK_DIM ** -0.5 * math.log2(math.e))
    wts = (w_in1, mla_q_a_norm[0].reshape(1, Q_LORA), wuq, _pad_lanes(mla_q_norm[0] * half_scale, HEAD_PAD),
           mla_kv_a_norm[0].reshape(1, KV_LORA), wuk, wuv, _pad_lanes(mla_k_norm[0] * half_scale, HEAD_PAD))
    qt, kx, vtx = _mla_proj(x, norm1[1], sh1, sc1, wts, _rope_tables(L), True)
    kc, vtc = _mla_proj(ctx, norm1[1], csh1, csc1, wts, _identity_rope_tables(Lc), False)
    o = _attention(qt, kx, kc, vtx, vtc)
    ffn1 = (bf(ffn_w_up[1]), ffn_conv_w[1], ffn_conv_b[1], bf(ffn_w_down[1]))
    return _mix_ffn(x, [o], bf(mla_w_o[0]), g1, norm2[1], sh2, sc2, g2, *ffn1)
```

```python
import functools
import math

import jax
import jax.numpy as jnp
import numpy as np
from jax import lax
from jax.experimental import pallas as pl
from jax.experimental.pallas import tpu as pltpu

F32 = jnp.float32
BF16 = jnp.bfloat16
HIGHEST = lax.Precision.HIGHEST

EPS = 1e-6
D_MODEL = 1024
D_FF = 2816
D_FOURIER = 512
FOURIER_GROUP_DIM = 128
D_HYENA = 512
HYENA_EMB_DIM = 33
HYENA_EMB_PAD = 64
HYENA_BANDS = 16
HYENA_WIDTH = 64
HYENA_MIN_DECAY = math.log(1e-2) / 0.3
HYENA_MAX_DECAY = math.log(1e-2) / 1.5
MLA_HEADS = 16
Q_LORA = 256
KV_LORA = 128
NOPE = 64
ROPE = 32
QK_DIM = NOPE + ROPE
V_DIM = 64
HEAD_PAD = 128
MLA_IN_PAD = 512
GRID_W = 64
ROPE_THETA = 10000.0

ATTN_DTYPE = jnp.float8_e4m3fn
ATTN_P_SHIFT = 8.0
FFN_CHUNK = 256
FFT_RADIX = 128
FFT_KRON = 16
HALO = 16
VMEM_LIMIT = 56 * 1024 * 1024


def _cparams(sem, flags=None):
    return pltpu.CompilerParams(dimension_semantics=sem, vmem_limit_bytes=VMEM_LIMIT, flags=flags)


def _dot(a, b):
    return jnp.dot(a, b, preferred_element_type=F32)


def _norm_mod(x, g, shift, scale):
    ms = jnp.mean(x * x, axis=-1, keepdims=True)
    return (x * lax.rsqrt(ms + EPS) * g) * (1.0 + scale) + shift


def _silu(x):
    return x * (1.0 / (1.0 + jnp.exp(-x)))


def _row_tile(L, want):
    t = min(L, want)
    assert L % t == 0 and t % HALO == 0
    return t


def _mod_kernel(c_ref, w_ref, b_ref, o_ref):
    s = _silu(c_ref[...])
    o_ref[...] = jnp.dot(s, w_ref[...], precision=HIGHEST, preferred_element_type=F32) + b_ref[...]


def _adaln(cond, w_mod, b_mod):
    R, D = cond.shape
    n = w_mod.shape[1]
    tn = 768
    return pl.pallas_call(
        _mod_kernel,
        out_shape=jax.ShapeDtypeStruct((R, n), F32),
        grid=(n // tn,),
        in_specs=[pl.BlockSpec((R, D), lambda j: (0, 0)),
                  pl.BlockSpec((D, tn), lambda j: (0, j)),
                  pl.BlockSpec((1, tn), lambda j: (0, j))],
        out_specs=pl.BlockSpec((R, tn), lambda j: (0, j)),
        compiler_params=_cparams(("arbitrary",)),
        name="adaln_mod",
    )(cond, w_mod, b_mod.reshape(1, n))


def _halo_specs(tm, D, L):
    nb = L // HALO
    per = tm // HALO
    cur = pl.BlockSpec((1, tm, D), lambda b, i, *_: (b, i, 0))
    prev = pl.BlockSpec((1, HALO, D), lambda b, i, *_: (b, jnp.maximum(i * per - 1, 0), 0))
    nxt = pl.BlockSpec((1, HALO, D), lambda b, i, *_: (b, jnp.minimum((i + 1) * per, nb - 1), 0))
    return cur, prev, nxt


def _mod_spec(arr):
    D = arr.shape[-1]
    if arr.shape[0] == 1:
        return pl.BlockSpec((1, 1, D), lambda b, *_: (0, 0, 0))
    return pl.BlockSpec((1, 1, D), lambda b, *_: (b, 0, 0))


def _fill_hn_rows(hn_ref, xc, xp, xn, g, shift, scale, tm, i, nt):
    hp = _norm_mod(xp, g, shift, scale)
    hn_ref[0:HALO, :] = jnp.where(i > 0, hp, 0.0).astype(BF16)
    hn_ref[HALO:HALO + tm, :] = _norm_mod(xc, g, shift, scale).astype(BF16)
    hx = _norm_mod(xn, g, shift, scale)
    hn_ref[HALO + tm:, :] = jnp.where(i < nt - 1, hx, 0.0).astype(BF16)


def _fill_hn(hn_ref, xc_ref, xp_ref, xn_ref, g, shift, scale, tm, i, nt):
    _fill_hn_rows(hn_ref, xc_ref[0], xp_ref[0], xn_ref[0], g, shift, scale, tm, i, nt)


def _dwconv3_rows(p, w, tm):
    n = tm + 2 * HALO
    up = pltpu.roll(p, 1, 0)
    dn = pltpu.roll(p, n - 1, 0)
    c = up * w[0:1] + p * w[1:2] + dn * w[2:3]
    return c[HALO:HALO + tm]


def _fh_in_kernel(xc_ref, xp_ref, xn_ref, g_ref, sh_ref, sc_ref, w_ref, cw_ref, cb_ref, gm_ref, *rest, tm, n_hi):
    if n_hi:
        perm_ref, z_ref, u_ref, x0_ref, hn_ref = rest
    else:
        z_ref, u_ref, x0_ref, hn_ref = rest
    i = pl.program_id(1)
    nt = pl.num_programs(1)
    _fill_hn(hn_ref, xc_ref, xp_ref, xn_ref, g_ref[...], sh_ref[0], sc_ref[0], tm, i, nt)
    proj = _dot(hn_ref[...], w_ref[...])
    uf = proj[HALO:HALO + tm, :D_FOURIER].astype(BF16)
    if n_hi:
        uf = _dot(perm_ref[...], uf).astype(BF16)
        z = _dot(uf, gm_ref[...]).astype(BF16)
        z_ref[0] = z.reshape(n_hi, tm // n_hi, 2 * D_FOURIER)
    else:
        z_ref[0] = _dot(uf, gm_ref[...]).astype(BF16)
    c = _dwconv3_rows(proj[:, D_FOURIER:], cw_ref[...], tm) + cb_ref[...]
    x0 = c[:, :D_HYENA]
    x1 = c[:, D_HYENA:2 * D_HYENA]
    v = c[:, 2 * D_HYENA:]
    x0_ref[0] = x0
    u_ref[0] = (v * x1).astype(BF16)


def _fh_in(x, g, shift, scale, w_in, conv_w, conv_b, gmat, n_hi=0):
    B, L, D = x.shape
    tm = _row_tile(L, 512)
    cur, prev, nxt = _halo_specs(tm, D, L)
    n_in = w_in.shape[1]
    nh = 3 * D_HYENA
    full = lambda shape: pl.BlockSpec(shape, lambda b, i: (0,) * len(shape))
    out_tile = lambda c: pl.BlockSpec((1, tm, c), lambda b, i: (b, i, 0))
    in_specs = [cur, prev, nxt, full((1, D)), _mod_spec(shift), _mod_spec(scale),
                full((D, n_in)), full((3, nh)), full((1, nh)), full((D_FOURIER, 2 * D_FOURIER))]
    args = [x, x, x, g.reshape(1, D), shift, scale, w_in, conv_w, conv_b.reshape(1, nh), gmat]
    if n_hi:
        per = tm // n_hi
        r = np.arange(tm)
        src = n_hi * (r % per) + r // per
        args.append(_const(src[:, None] == r[None, :], BF16))
        in_specs.append(full((tm, tm)))
        z_shape = jax.ShapeDtypeStruct((B, n_hi, L // n_hi, 2 * D_FOURIER), BF16)
        z_spec = pl.BlockSpec((1, n_hi, per, 2 * D_FOURIER), lambda b, i: (b, 0, i, 0))
    else:
        z_shape = jax.ShapeDtypeStruct((B, L, 2 * D_FOURIER), BF16)
        z_spec = out_tile(2 * D_FOURIER)
    return pl.pallas_call(
        functools.partial(_fh_in_kernel, tm=tm, n_hi=n_hi),
        out_shape=(z_shape, jax.ShapeDtypeStruct((B, L, D_HYENA), BF16), jax.ShapeDtypeStruct((B, L, D_HYENA), F32)),
        grid=(B, L // tm),
        in_specs=in_specs,
        out_specs=(z_spec, out_tile(D_HYENA), out_tile(D_HYENA)),
        scratch_shapes=[pltpu.VMEM((tm + 2 * HALO, D), BF16)],
        compiler_params=_cparams(("parallel", "arbitrary")),
        name="fh_in",
    )(*args)


def _fourier_kernel(c_ref, s_ref, z_ref, o_ref, *, scale):
    z = z_ref[0]
    y = _dot(c_ref[...], z[:, :D_FOURIER]) + _dot(s_ref[...], z[:, D_FOURIER:])
    o_ref[0] = (y * scale).astype(BF16)


def _fourier_seq(z, cl, sl):
    B, L, _ = z.shape
    tf = _row_tile(L, 512)
    scale = 1.0 / math.sqrt(L * FOURIER_GROUP_DIM)
    return pl.pallas_call(
        functools.partial(_fourier_kernel, scale=scale),
        out_shape=jax.ShapeDtypeStruct((B, L, D_FOURIER), BF16),
        grid=(B, L // tf),
        in_specs=[pl.BlockSpec((tf, L), lambda b, i: (i, 0)),
                  pl.BlockSpec((tf, L), lambda b, i: (i, 0)),
                  pl.BlockSpec((1, L, 2 * D_FOURIER), lambda b, i: (b, 0, 0))],
        out_specs=pl.BlockSpec((1, tf, D_FOURIER), lambda b, i: (b, i, 0)),
        compiler_params=_cparams(("parallel", "arbitrary")),
        name="fourier_seq",
    )(cl, sl, z)


def _fft_f1_kernel(cs_ref, z_ref, o_ref, *, nc):
    for j in range(nc):
        cs = _dot(cs_ref[...], z_ref[0, j])
        cz = cs[:FFT_RADIX]
        sz = cs[FFT_RADIX:]
        o_ref[0, 0, j] = (cz[:, :D_FOURIER] + sz[:, D_FOURIER:]).astype(BF16)
        o_ref[0, 1, j] = (cz[:, D_FOURIER:] - sz[:, :D_FOURIER]).astype(BF16)


def _fft_f2_kernel(t_ref, b_ref, o_ref, *, scale):
    n_hi = b_ref.shape[2]
    blk = b_ref[0].reshape(2 * n_hi * 8, D_FOURIER)
    y = _dot(t_ref[0], blk) * scale
    o_ref[0, :, 0] = y.reshape(n_hi, 8, D_FOURIER).astype(BF16)


def _fourier_seq_fft(zp, cs, tmat):
    B, n_hi, _, _ = zp.shape
    L = n_hi * FFT_RADIX
    nc = 8
    ng = FFT_RADIX // 8
    bc = pl.pallas_call(
        functools.partial(_fft_f1_kernel, nc=nc),
        out_shape=jax.ShapeDtypeStruct((B, 2, n_hi, FFT_RADIX, D_FOURIER), BF16),
        grid=(B, n_hi // nc),
        in_specs=[pl.BlockSpec((2 * FFT_RADIX, FFT_RADIX), lambda b, i: (0, 0)),
                  pl.BlockSpec((1, nc, FFT_RADIX, 2 * D_FOURIER), lambda b, i: (b, i, 0, 0))],
        out_specs=pl.BlockSpec((1, 2, nc, FFT_RADIX, D_FOURIER), lambda b, i: (b, 0, i, 0, 0)),
        compiler_params=_cparams(("parallel", "arbitrary")),
        name="fourier_fft1",
    )(cs, zp)
    scale = 1.0 / math.sqrt(L * FOURIER_GROUP_DIM)
    y = pl.pallas_call(
        functools.partial(_fft_f2_kernel, scale=scale),
        out_shape=jax.ShapeDtypeStruct((B, n_hi, ng, 8, D_FOURIER), BF16),
        grid=(B, ng),
        in_specs=[pl.BlockSpec((1, n_hi * 8, 2 * n_hi * 8), lambda b, g: (g, 0, 0)),
                  pl.BlockSpec((1, 2, n_hi, 8, D_FOURIER), lambda b, g: (b, 0, 0, g, 0))],
        out_specs=pl.BlockSpec((1, n_hi, 1, 8, D_FOURIER), lambda b, g: (b, 0, g, 0, 0)),
        compiler_params=_cparams(("parallel", "arbitrary")),
        name="fourier_fft2",
    )(tmat, bc)
    return y.reshape(B, L, D_FOURIER)


def _filter_kernel(z_ref, w1_ref, b1_ref, w2_ref, b2_ref, w3_ref, b3_ref, w4_ref, f_ref, dl_ref,
                   k_ref, n_ref):
    i = pl.program_id(1)
    z = z_ref[0]
    f = f_ref[...]
    hd = lambda a, w: jnp.dot(a, w, precision=HIGHEST, preferred_element_type=F32)
    h = jnp.sin(f * (hd(z, w1_ref[...]) + b1_ref[...]))
    h = jnp.sin(f * (hd(h, w2_ref[...]) + b2_ref[...]))
    h = jnp.sin(f * (hd(h, w3_ref[...]) + b3_ref[...]))
    h = hd(h, w4_ref[0])
    t = z[:, 0:1]
    valid = z[:, HYENA_EMB_DIM:HYENA_EMB_DIM + 1]
    k = h * jnp.exp(-t * jnp.abs(dl_ref[...])) * valid
    k_ref[0] = k

    @pl.when(i == 0)
    def _():
        n_ref[...] = jnp.zeros_like(n_ref)

    n_ref[0] += jnp.sum(jnp.abs(k), axis=0, keepdims=True)


def _hyena_filter(L, w1, b1, w2, b2, w3, b3, w4, freq):
    pos = np.arange(L, dtype=np.float64)
    bands = np.linspace(1e-4, HYENA_BANDS - 1, HYENA_BANDS)

    def emb(p, valid):
        t = p / max(L - 1, 1)
        ang = (2.0 * math.pi / L) * p[:, None] * bands[None, :]
        pad = np.zeros((L, HYENA_EMB_PAD - HYENA_EMB_DIM - 1))
        return np.concatenate([t[:, None], np.cos(ang), -np.sin(ang), valid[:, None], pad], axis=-1)

    zf = emb(pos, np.ones((L,)))
    zb = emb(np.where(pos > 0, L - pos, 0.0), (pos > 0).astype(np.float64))
    z = _const(np.stack([zf, zb]))
    w1p = jnp.concatenate([w1, jnp.zeros((HYENA_EMB_PAD - HYENA_EMB_DIM, HYENA_WIDTH), F32)], axis=0)
    w4s = jnp.stack([w4[:, :D_HYENA], w4[:, D_HYENA:]])
    deltas = _const(np.linspace(HYENA_MIN_DECAY, HYENA_MAX_DECAY, D_HYENA).reshape(1, D_HYENA))
    tr = _row_tile(L, 512)
    W = HYENA_WIDTH
    full = lambda shape: pl.BlockSpec(shape, lambda s, i: (0,) * len(shape))
    return pl.pallas_call(
        _filter_kernel,
        out_shape=(jax.ShapeDtypeStruct((2, L, D_HYENA), F32), jax.ShapeDtypeStruct((2, 1, D_HYENA), F32)),
        grid=(2, L // tr),
        in_specs=[pl.BlockSpec((1, tr, HYENA_EMB_PAD), lambda s, i: (s, i, 0)),
                  full((HYENA_EMB_PAD, W)), full((1, W)), full((W, W)), full((1, W)),
                  full((W, W)), full((1, W)),
                  pl.BlockSpec((1, W, D_HYENA), lambda s, i: (s, 0, 0)),
                  full((1, W)), full((1, D_HYENA))],
        out_specs=(pl.BlockSpec((1, tr, D_HYENA), lambda s, i: (s, i, 0)),
                   pl.BlockSpec((1, 1, D_HYENA), lambda s, i: (s, 0, 0))),
        compiler_params=_cparams(("arbitrary", "arbitrary")),
        name="hyena_filter",
    )(z, w1p, b1.reshape(1, W), w2, b2.reshape(1, W), w3, b3.reshape(1, W), w4s, freq.reshape(1, W), deltas)


def _hy_fwd_kernel(c_ref, s_ref, u_ref, *rest, raw, tf):
    u = u_ref[0]
    ure = _dot(c_ref[...], u)
    uim = -_dot(s_ref[...], u)
    if raw:
        re_ref, im_ref = rest
        re_ref[0] = ure
        im_ref[0] = uim
        return
    kre_ref, kim_ref, re_ref, im_ref = rest
    kre = kre_ref[...]
    kim = kim_ref[...]
    row = lax.broadcasted_iota(jnp.int32, (tf, 1), 0) + pl.program_id(1) * tf
    packed = row == 0
    a = uim * kim
    re_ref[0] = (ure * kre - jnp.where(packed, 0.0, a)).astype(BF16)
    im_ref[0] = jnp.where(packed, a, ure * kim + uim * kre).astype(BF16)


def _hy_fwd(u, cb, sbf, kre=None, kim=None):
    B, L, C = u.shape
    tf = _row_tile(L, 512)
    raw = kre is None
    mat = pl.BlockSpec((tf, L), lambda b, i: (i, 0))
    tile = pl.BlockSpec((1, tf, C), lambda b, i: (b, i, 0))
    in_specs = [mat, mat, pl.BlockSpec((1, L, C), lambda b, i: (b, 0, 0))]
    args = [cb, sbf, u]
    if not raw:
        in_specs += [pl.BlockSpec((tf, C), lambda b, i: (i, 0))] * 2
        args += [kre, kim]
    dt = F32 if raw else BF16
    return pl.pallas_call(
        functools.partial(_hy_fwd_kernel, raw=raw, tf=tf),
        out_shape=(jax.ShapeDtypeStruct((B, L, C), dt), jax.ShapeDtypeStruct((B, L, C), dt)),
        grid=(B, L // tf),
        in_specs=in_specs,
        out_specs=(tile, tile),
        compiler_params=_cparams(("parallel", "arbitrary")),
        name="hyena_fwd_raw" if raw else "hyena_fwd",
    )(*args)


def _spec_combine_kernel(re_ref, im_ref, n_ref, kre_ref, kim_ref, *, tf, n_fft):
    row = lax.broadcasted_iota(jnp.int32, (tf, 1), 0) + pl.program_id(0) * tf
    sgn = (1 - 2 * (row & 1)).astype(F32)
    wgt = jnp.where(row == 0, 1.0 / n_fft, 2.0 / n_fft)
    s = wgt / (n_ref[0] + n_ref[1])
    kre_ref[...] = (re_ref[0] + sgn * re_ref[1]) * s
    kim_ref[...] = (im_ref[0] + sgn * im_ref[1]) * s


def _spec_combine(re, im, norms):
    _, L, C = re.shape
    tf = _row_tile(L, 512)
    pair = pl.BlockSpec((2, tf, C), lambda i: (0, i, 0))
    tile = pl.BlockSpec((tf, C), lambda i: (i, 0))
    return pl.pallas_call(
        functools.partial(_spec_combine_kernel, tf=tf, n_fft=2 * L),
        out_shape=(jax.ShapeDtypeStruct((L, C), F32), jax.ShapeDtypeStruct((L, C), F32)),
        grid=(L // tf,),
        in_specs=[pair, pair, pl.BlockSpec((2, 1, C), lambda i: (0, 0, 0))],
        out_specs=(tile, tile),
        compiler_params=_cparams(("arbitrary",)),
        name="hyena_spec_combine",
    )(re, im, norms)


def _hy_inv_kernel(c_ref, s_ref, re_ref, im_ref, x0_ref, u_ref, b_ref, o_ref):
    y = _dot(c_ref[...], re_ref[0]) - _dot(s_ref[...], im_ref[0])
    o_ref[0] = (x0_ref[0] * (y + u_ref[0].astype(F32) * b_ref[...])).astype(BF16)


def _hy_inv(yre, yim, x0, u, bias, cb, sbi):
    B, L, C = u.shape
    tt = _row_tile(L, 512)
    mat = pl.BlockSpec((tt, L), lambda b, i: (i, 0))
    whole = pl.BlockSpec((1, L, C), lambda b, i: (b, 0, 0))
    tile = pl.BlockSpec((1, tt, C), lambda b, i: (b, i, 0))
    return pl.pallas_call(
        _hy_inv_kernel,
        out_shape=jax.ShapeDtypeStruct((B, L, C), BF16),
        grid=(B, L // tt),
        in_specs=[mat, mat, whole, whole, tile, tile, pl.BlockSpec((1, C), lambda b, i: (0, 0))],
        out_specs=tile,
        compiler_params=_cparams(("parallel", "arbitrary")),
        name="hyena_inv",
    )(cb, sbi, yre, yim, x0, u, bias.reshape(1, C))


def _fft_a_kernel(t_ref, x_ref, o_ref):
    r2, kr, c = x_ref.shape[1:]
    y = _dot(t_ref[...], x_ref[0].reshape(r2 * kr, c))
    o_ref[0] = y.reshape(2, y.shape[0] // (2 * kr), kr, c).astype(BF16)


def _hy_fft_a(x4, t1):
    Bx, R, _, C = x4.shape
    n_hi = t1.shape[0] // (2 * FFT_KRON)
    return pl.pallas_call(
        _fft_a_kernel,
        out_shape=jax.ShapeDtypeStruct((Bx, 2, n_hi, FFT_RADIX, C), BF16),
        grid=(Bx, FFT_RADIX // FFT_KRON),
        in_specs=[pl.BlockSpec(t1.shape, lambda b, i: (0, 0)),
                  pl.BlockSpec((1, R, FFT_KRON, C), lambda b, i: (b, 0, i, 0))],
        out_specs=pl.BlockSpec((1, 2, n_hi, FFT_KRON, C), lambda b, i: (b, 0, 0, i, 0)),
        compiler_params=_cparams(("parallel", "arbitrary")),
        name="hyena_fft_a",
    )(t1, x4)


def _fft_b_kernel(gf_ref, a_ref, *rest, raw, n_fft):
    nb = a_ref.shape[0]
    gf = gf_ref[0]
    if raw:
        n_ref, o_ref = rest
        s = 1.0 / (n_fft * (n_ref[0] + n_ref[1]))
    else:
        gi_ref, k_ref, o_ref = rest
        gi = gi_ref[0]
        kr = k_ref[0, 0, 0]
        ki = k_ref[0, 1, 0]
    for b in range(nb):
        a = a_ref[b, :, 0].reshape(2 * FFT_RADIX, D_HYENA)
        x = _dot(gf, a)
        if raw:
            o_ref[b, :, 0] = (x * s).reshape(2, FFT_RADIX, D_HYENA)
            continue
        xr = x[:FFT_RADIX]
        xi = x[FFT_RADIX:]
        y = jnp.concatenate([xr * kr - xi * ki, xr * ki + xi * kr], axis=0).astype(BF16)
        o_ref[b, :, 0] = _dot(gi, y).reshape(2, FFT_RADIX, D_HYENA).astype(BF16)


def _hy_fft_b(a5, gf, gi=None, kspec=None, norms=None):
    Bx, _, n_hi, _, C = a5.shape
    raw = kspec is None
    blk = pl.BlockSpec((Bx, 2, 1, FFT_RADIX, C), lambda k: (0, 0, k, 0, 0))
    tab = pl.BlockSpec((1, 2 * FFT_RADIX, 2 * FFT_RADIX), lambda k: (k, 0, 0))
    if raw:
        in_specs = [tab, blk, pl.BlockSpec((2, 1, C), lambda k: (0, 0, 0))]
        args = (gf, a5, norms)
    else:
        kblk = pl.BlockSpec((1, 2, 1, FFT_RADIX, C), lambda k: (0, 0, k, 0, 0))
        in_specs = [tab, blk, tab, kblk]
        args = (gf, a5, gi, kspec)
    return pl.pallas_call(
        functools.partial(_fft_b_kernel, raw=raw, n_fft=n_hi * FFT_RADIX),
        out_shape=jax.ShapeDtypeStruct(a5.shape, F32 if raw else BF16),
        grid=(n_hi,),
        in_specs=in_specs,
        out_specs=blk,
        compiler_params=_cparams(("arbitrary",)),
        name="hyena_fft_spec" if raw else "hyena_fft_b",
    )(*args)


def _fft_c_kernel(t_ref, v_ref, x0_ref, u_ref, b_ref, o_ref):
    _, n_hi, kr, c = v_ref.shape[1:]
    y = _dot(t_ref[...], v_ref[0].reshape(2 * n_hi * kr, c))
    y = y.reshape(y.shape[0] // kr, kr, c)
    o_ref[0] = (x0_ref[0] * (y + u_ref[0].astype(F32) * b_ref[...])).astype(BF16)


def _hy_fft_c(v5, t3, x0, u, bias):
    B, _, n_hi, _, C = v5.shape
    R = t3.shape[0] // FFT_KRON
    tile = pl.BlockSpec((1, R, FFT_KRON, C), lambda b, i: (b, 0, i, 0))
    return pl.pallas_call(
        _fft_c_kernel,
        out_shape=jax.ShapeDtypeStruct((B, R, FFT_RADIX, C), BF16),
        grid=(B, FFT_RADIX // FFT_KRON),
        in_specs=[pl.BlockSpec(t3.shape, lambda b, i: (0, 0)),
                  pl.BlockSpec((1, 2, n_hi, FFT_KRON, C), lambda b, i: (b, 0, 0, i, 0)),
                  tile, tile, pl.BlockSpec((1, C), lambda b, i: (0, 0))],
        out_specs=tile,
        compiler_params=_cparams(("parallel", "arbitrary")),
        name="hyena_fft_c",
    )(t3, v5, x0, u, bias.reshape(1, C))


def _ffn_kernel(xc_ref, xp_ref, xn_ref, g1_ref, wo_ref, g_ref, sh_ref, sc_ref, gate_ref, wu_ref, cw_ref, cb_ref,
                wd_ref, *rest, tm, n_y):
    y_refs = rest[:3 * n_y]
    o_ref, yext_ref, hn_ref, act_ref = rest[3 * n_y:]
    i = pl.program_id(1)
    nt = pl.num_programs(1)
    off = 0
    for k in range(n_y):
        yc_ref, yp_ref, yn_ref = y_refs[3 * k:3 * k + 3]
        c = yc_ref.shape[-1]
        yext_ref[0:HALO, off:off + c] = yp_ref[0].astype(BF16)
        yext_ref[HALO:HALO + tm, off:off + c] = yc_ref[0].astype(BF16)
        yext_ref[HALO + tm:, off:off + c] = yn_ref[0].astype(BF16)
        off += c
    mix = g1_ref[0] * _dot(yext_ref[...], wo_ref[...])
    x1 = xc_ref[0] + mix[HALO:HALO + tm]
    _fill_hn_rows(hn_ref, x1, xp_ref[0] + mix[:HALO], xn_ref[0] + mix[HALO + tm:], g_ref[...], sh_ref[0], sc_ref[0],
                  tm, i, nt)
    for c in range(0, D_FF, FFN_CHUNK):
        cols = slice(c, c + FFN_CHUNK)
        gx = _dot(hn_ref[...], wu_ref[:, cols])
        vx = _dot(hn_ref[HALO:HALO + tm, :], wu_ref[:, D_FF + c:D_FF + c + FFN_CHUNK])
        cv = _dwconv3_rows(gx, cw_ref[:, cols], tm) + cb_ref[:, cols]
        act_ref[:, cols] = (_silu(cv) * vx).astype(BF16)
    o_ref[0] = x1 + gate_ref[0] * _dot(act_ref[...], wd_ref[...])


def _mix_ffn(x, ys, w_o, gate1, g, shift, scale, gate2, w_up, conv_w, conv_b, w_down):
    B, L, D = x.shape
    tm = _row_tile(L, 512)
    cur, prev, nxt = _halo_specs(tm, D, L)
    const = lambda shape: pl.BlockSpec(shape, lambda b, i: (0,) * len(shape), pipeline_mode=pl.Buffered(1))
    y_specs, y_args = [], []
    for y in ys:
        y_specs += list(_halo_specs(tm, y.shape[-1], L))
        y_args += [y, y, y]
    return pl.pallas_call(
        functools.partial(_ffn_kernel, tm=tm, n_y=len(ys)),
        out_shape=jax.ShapeDtypeStruct((B, L, D), F32),
        grid=(B, L // tm),
        in_specs=[cur, prev, nxt, _mod_spec(gate1), const(w_o.shape), const((1, D)), _mod_spec(shift),
                  _mod_spec(scale), _mod_spec(gate2), const((D, 2 * D_FF)), const((3, D_FF)), const((1, D_FF)),
                  const((D_FF, D))] + y_specs,
        out_specs=pl.BlockSpec((1, tm, D), lambda b, i: (b, i, 0)),
        scratch_shapes=[pltpu.VMEM((tm + 2 * HALO, w_o.shape[0]), BF16), pltpu.VMEM((tm + 2 * HALO, D), BF16),
                        pltpu.VMEM((tm, D_FF), BF16)],
        compiler_params=_cparams(("parallel", "arbitrary")),
        name="mix_ffn",
    )(x, x, x, gate1, w_o, g.reshape(1, D), shift, scale, gate2, w_up, conv_w, conv_b.reshape(1, D_FF), w_down,
      *y_args)


def _rms(v, g, n):
    return v * lax.rsqrt(jnp.sum(v * v, axis=-1, keepdims=True) * (1.0 / n) + EPS) * g


def _rope_gain_tables(gain, cos, sa, sb):
    rot = slice(NOPE, QK_DIM)
    gr = gain[rot]
    return gain[:NOPE], gr * cos[rot], pltpu.roll(gr, 8, 0) * sa[rot], pltpu.roll(gr, ROPE - 8, 0) * sb[rot]


def _rope_t(yr, gc, gsa, gsb):
    return yr * gc + pltpu.roll(yr, 8, 0) * gsa + pltpu.roll(yr, ROPE - 8, 0) * gsb


def _sumsq(x):
    return jnp.sum(x * x, axis=0, keepdims=True)


def _mla_proj_kernel(x_ref, g_ref, sh_ref, sc_ref, win_ref, qan_ref, wuq_ref, qn_ref, kvn_ref, wuk_ref,
                     wuv_ref, kn_ref, cos_ref, sa_ref, sb_ref, *outs, with_q):
    hn = _norm_mod(x_ref[0], g_ref[...], sh_ref[0], sc_ref[0]).astype(BF16)
    a = _dot(hn, win_ref[...])
    tm = a.shape[0]
    cos = cos_ref[...]
    sa = sa_ref[...]
    sb = sb_ref[...]
    pad = jnp.zeros((HEAD_PAD - QK_DIM, tm), ATTN_DTYPE)
    if with_q:
        qt_ref, k_ref, vt_ref = outs
        qat = _rms(a[:, :Q_LORA], qan_ref[...], Q_LORA).T.astype(BF16)
        qft = _dot(wuq_ref[...], qat)
        gn, gc, gsa, gsb = _rope_gain_tables(qn_ref[...], cos, sa, sb)
        for h in range(MLA_HEADS):
            base = h * HEAD_PAD
            xn = qft[base:base + NOPE]
            xr = qft[base + NOPE:base + QK_DIM]
            r = lax.rsqrt((_sumsq(xn) + _sumsq(xr)) * (1.0 / QK_DIM) + EPS)
            qt_ref[0, base:base + NOPE, :] = (xn * r * gn).astype(ATTN_DTYPE)
            qt_ref[0, base + NOPE:base + QK_DIM, :] = _rope_t(xr * r, gc, gsa, gsb).astype(ATTN_DTYPE)
            qt_ref[0, base + QK_DIM:base + HEAD_PAD, :] = pad
    else:
        k_ref, vt_ref = outs
    ckvt = _rms(a[:, Q_LORA:Q_LORA + KV_LORA], kvn_ref[...], KV_LORA).T.astype(BF16)
    kft = _dot(wuk_ref[...], ckvt)
    vt_ref[0] = _dot(wuv_ref[...], ckvt).astype(ATTN_DTYPE)
    kpe = a[:, Q_LORA + KV_LORA:].T[:ROPE]
    gn, gc, gsa, gsb = _rope_gain_tables(kn_ref[...], cos, sa, sb)
    kpe_rot = _rope_t(kpe, gc, gsa, gsb)
    kpe_ss = _sumsq(kpe)
    for h in range(MLA_HEADS):
        base = h * HEAD_PAD
        xn = kft[base:base + NOPE]
        r = lax.rsqrt((_sumsq(xn) + kpe_ss) * (1.0 / QK_DIM) + EPS)
        kt = jnp.concatenate([xn * r * gn, kpe_rot * r, jnp.zeros((HEAD_PAD - QK_DIM, tm), F32)], axis=0)
        k_ref[0, :, base:base + HEAD_PAD] = kt.T.astype(ATTN_DTYPE)


def _mla_proj(x, g, shift, scale, wts, rope_tabs, with_q):
    B, L, D = x.shape
    tm = _row_tile(L, 512)
    win, qan, wuq, qn, kvn, wuk, wuv, kn = wts
    cos, sa, sb = rope_tabs
    HP = MLA_HEADS * HEAD_PAD
    HV = MLA_HEADS * V_DIM
    full = lambda a: pl.BlockSpec(a.shape, lambda b, i: (0,) * a.ndim)
    tab = pl.BlockSpec((HEAD_PAD, tm), lambda b, i: (0, i))
    out_shape = [jax.ShapeDtypeStruct((B, L, HP), ATTN_DTYPE), jax.ShapeDtypeStruct((B, HV, L), ATTN_DTYPE)]
    out_specs = [pl.BlockSpec((1, tm, HP), lambda b, i: (b, i, 0)),
                 pl.BlockSpec((1, HV, tm), lambda b, i: (b, 0, i))]
    if with_q:
        out_shape = [jax.ShapeDtypeStruct((B, HP, L), ATTN_DTYPE)] + out_shape
        out_specs = [pl.BlockSpec((1, HP, tm), lambda b, i: (b, 0, i))] + out_specs
    g2 = g.reshape(1, D)
    gain_tab = lambda v: jnp.broadcast_to(v.reshape(HEAD_PAD, 1), (HEAD_PAD, tm))
    qn = gain_tab(qn)
    kn = gain_tab(kn)
    return pl.pallas_call(
        functools.partial(_mla_proj_kernel, with_q=with_q),
        out_shape=tuple(out_shape),
        grid=(B, L // tm),
        in_specs=[pl.BlockSpec((1, tm, D), lambda b, i: (b, i, 0)), full(g2), _mod_spec(shift), _mod_spec(scale),
                  full(win), full(qan), full(wuq), full(qn), full(kvn), full(wuk), full(wuv), full(kn),
                  tab, tab, tab],
        out_specs=tuple(out_specs),
        compiler_params=_cparams(("parallel", "arbitrary")),
        name="mla_proj_q" if with_q else "mla_proj_kv",
    )(x, g2, shift, scale, win, qan, wuq, qn, kvn, wuk, wuv, kn, cos, sa, sb)


def _attn_kernel(qt_ref, kx_ref, kc_ref, vtx_ref, vtc_ref, o_ref, ax_ref, ac_ref, am_ref, bx_ref, bc_ref, bm_ref,
                 px_ref, pc_ref):
    t = pl.program_id(0)
    io = (qt_ref, kx_ref, kc_ref, vtx_ref, vtc_ref, o_ref, px_ref, pc_ref)

    @pl.when(t == 0)
    def _():
        bx_ref[...] = jnp.zeros(bx_ref.shape, BF16)
        bc_ref[...] = jnp.zeros(bc_ref.shape, BF16)
        bm_ref[...] = jnp.zeros(bm_ref.shape, BF16)

    @pl.when(t % 2 == 0)
    def _():
        _attn_step(*io, (ax_ref, ac_ref, am_ref), (bx_ref, bc_ref, bm_ref))

    @pl.when(t % 2 == 1)
    def _():
        _attn_step(*io, (bx_ref, bc_ref, bm_ref), (ax_ref, ac_ref, am_ref))


def _attn_step(qt_ref, kx_ref, kc_ref, vtx_ref, vtc_ref, o_ref, px_ref, pc_ref, parked, ready):
    wx_ref, wc_ref, wm_ref = parked
    rx_ref, rc_ref, rm_ref = ready
    tq = qt_ref.shape[2]
    off = rm_ref[...][None]

    def probs(s_ref):
        n = s_ref.shape[0]
        s = s_ref[...].reshape(n // 16, 16, 2 * tq)
        return jnp.exp2(s - off).reshape(n, 2 * tq).astype(ATTN_DTYPE)

    px_ref[...] = probs(rx_ref)
    pc_ref[...] = probs(rc_ref)

    q0 = qt_ref[0, :HEAD_PAD, :]
    q1 = qt_ref[0, HEAD_PAD:, :]
    zq = jnp.zeros_like(q0)
    qbd = jnp.concatenate([jnp.concatenate([q0, zq], axis=1), jnp.concatenate([zq, q1], axis=1)], axis=0)
    nx = _dot(kx_ref[0], qbd).astype(BF16)
    nc = _dot(kc_ref[0], qbd).astype(BF16)
    wx_ref[...] = nx
    wc_ref[...] = nc
    m = jnp.maximum(jnp.max(nx, axis=0, keepdims=True), jnp.max(nc, axis=0, keepdims=True)).astype(F32)
    wm_ref[...] = jnp.broadcast_to((m - ATTN_P_SHIFT).astype(BF16), wm_ref.shape)

    vtx = vtx_ref[0]
    vtc = vtc_ref[0]
    halves = []
    for j in range(2):
        own = slice(j * V_DIM, (j + 1) * V_DIM)
        oth = slice((1 - j) * V_DIM, (2 - j) * V_DIM)
        cols = slice(j * tq, (j + 1) * tq)
        parts_x = [vtx[own], jnp.ones_like(vtx[oth])]
        parts_c = [vtc[own], jnp.ones_like(vtc[oth])]
        if j == 1:
            parts_x.reverse()
            parts_c.reverse()
        r = (_dot(jnp.concatenate(parts_x, axis=0), px_ref[:, cols])
             + _dot(jnp.concatenate(parts_c, axis=0), pc_ref[:, cols]))
        halves.append(r[own] * (1.0 / r[oth][0:1]))
    o_ref[0] = jnp.concatenate(halves, axis=0).T.astype(BF16)


def _attention(qt, kx, kc, vtx, vtc):
    B, HP, L = qt.shape
    Lc = kc.shape[1]
    tq = _row_tile(L, 512)
    nhp = MLA_HEADS // 2
    nq = L // tq
    n_tiles = B * nhp * nq

    def tile(t):
        t = jnp.clip(t, 0, n_tiles - 1)
        return t // (nhp * nq), (t // nq) % nhp, t % nq

    def logits_side(f):
        return lambda t: f(*tile(t))

    def value_side(f):
        return lambda t: f(*tile(t - 1))

    return pl.pallas_call(
        _attn_kernel,
        out_shape=jax.ShapeDtypeStruct((B, L, MLA_HEADS * V_DIM), BF16),
        grid=(n_tiles + 1,),
        in_specs=[pl.BlockSpec((1, 2 * HEAD_PAD, tq), logits_side(lambda b, h, i: (b, h, i))),
                  pl.BlockSpec((1, L, 2 * HEAD_PAD), logits_side(lambda b, h, i: (b, 0, h))),
                  pl.BlockSpec((1, Lc, 2 * HEAD_PAD), logits_side(lambda b, h, i: (b, 0, h))),
                  pl.BlockSpec((1, 2 * V_DIM, L), value_side(lambda b, h, i: (b, h, 0))),
                  pl.BlockSpec((1, 2 * V_DIM, Lc), value_side(lambda b, h, i: (b, h, 0)))],
        out_specs=pl.BlockSpec((1, tq, 2 * V_DIM), value_side(lambda b, h, i: (b, i, h))),
        scratch_shapes=[pltpu.VMEM((L, 2 * tq), BF16), pltpu.VMEM((Lc, 2 * tq), BF16), pltpu.VMEM((16, 2 * tq), BF16)] * 2
                       + [pltpu.VMEM((L, 2 * tq), ATTN_DTYPE), pltpu.VMEM((Lc, 2 * tq), ATTN_DTYPE)],
        compiler_params=_cparams(("arbitrary",)),
        name="mla_attention",
    )(qt, kx, kc, vtx, vtc)


def _const(a, dtype=F32):
    return jnp.asarray(np.asarray(a, np.float32)).astype(dtype)


def _angle(num, period):
    return (num % period).astype(np.float64) * (2.0 * math.pi / period)


def _trig_matrix(n, period):
    r = np.arange(n, dtype=np.int64)
    a = _angle(r[:, None] * r[None, :], period)
    return np.cos(a), np.sin(a)


def _dft_tables(L):
    cl, sl = _trig_matrix(L, L)
    cb, sb = _trig_matrix(L, 2 * L)
    idx = np.arange(L)
    alt = (1 - 2 * (idx & 1)).astype(np.float64)
    sbf = np.where(idx[:, None] == 0, -alt[None, :], sb)
    sbi = np.where(idx[None, :] == 0, -alt[:, None], sb)
    return tuple(_const(m, BF16) for m in (cl, sl, cb, sbf, sbi))


def _fourier_fft_tables(L):
    n_hi = L // FFT_RADIX
    i = np.arange(FFT_RADIX, dtype=np.int64)
    a = _angle(i[:, None] * i[None, :], FFT_RADIX)
    cs = np.concatenate([np.cos(a), np.sin(a)], axis=0)
    ng = FFT_RADIX // 8
    g = np.arange(ng, dtype=np.int64)[:, None, None, None]
    k1 = np.arange(n_hi, dtype=np.int64)[None, :, None, None]
    j = np.arange(8, dtype=np.int64)[None, None, :, None]
    n1 = np.arange(n_hi, dtype=np.int64)[None, None, None, :]
    ang = _angle(n1 * (FFT_RADIX * k1 + 8 * g + j), L)
    eye = np.eye(8)
    blocks = [t[:, :, :, :, None] * eye[None, None, :, None, :] for t in (np.cos(ang), np.sin(ang))]
    tmat = np.stack(blocks, axis=3)
    return _const(cs, BF16), _const(tmat.reshape(ng, n_hi * 8, 2 * n_hi * 8), BF16)


def _hyena_fft_tables(L):
    n_fft = 2 * L
    n_hi = n_fft // FFT_RADIX
    k2 = np.arange(n_hi, dtype=np.int64)
    a1 = _angle(k2[:, None] * k2[None, :], n_hi)
    c1, s1 = np.cos(a1), np.sin(a1)
    eye = np.eye(FFT_KRON)
    kron = lambda m: (m[:, None, :, None] * eye[None, :, None, :]).reshape(m.shape[0] * FFT_KRON, m.shape[1] * FFT_KRON)
    t1 = _const(np.concatenate([kron(c1), kron(-s1)], axis=0), BF16)
    t1_half = t1[:, :n_hi // 2 * FFT_KRON]
    t3 = t1_half.T
    r = np.arange(FFT_RADIX, dtype=np.int64)
    k = k2[:, None, None] + n_hi * r[None, :, None]
    a2 = _angle(k * r[None, None, :], n_fft)
    c2, s2 = _const(np.cos(a2), BF16), _const(np.sin(a2), BF16)
    gf = jnp.concatenate([jnp.concatenate([c2, s2], axis=2), jnp.concatenate([-s2, c2], axis=2)], axis=1)
    c2t, s2t = jnp.swapaxes(c2, 1, 2), jnp.swapaxes(s2, 1, 2)
    gi = jnp.concatenate([jnp.concatenate([c2t, -s2t], axis=2), jnp.concatenate([s2t, c2t], axis=2)], axis=1)
    return t1, t1_half, t3, gf, gi


def _group_dft_matrix():
    g = FOURIER_GROUP_DIM
    ng = D_FOURIER // g
    c, s = _trig_matrix(g, g)
    eye = np.eye(ng)
    return _const(np.concatenate([np.kron(eye, c), -np.kron(eye, s)], axis=1), BF16)


def _rope_tables(L):
    pos = np.arange(L)
    rows = (pos // GRID_W).astype(np.float64)
    cols = (pos % GRID_W).astype(np.float64)
    nf = ROPE // 4
    inv_freq = ROPE_THETA ** (-np.arange(nf, dtype=np.float64) / nf)
    ar = rows[:, None] * inv_freq[None, :]
    ac = cols[:, None] * inv_freq[None, :]
    one = np.ones((L, NOPE))
    z_n = np.zeros((L, NOPE))
    z_f = np.zeros((L, nf))
    tail = np.zeros((L, HEAD_PAD - QK_DIM))
    cos = np.concatenate([one, np.cos(ar), np.cos(ar), np.cos(ac), np.cos(ac), tail], axis=1)
    sa = np.concatenate([z_n, z_f, np.sin(ar), z_f, np.sin(ac), tail], axis=1)
    sb = np.concatenate([z_n, -np.sin(ar), z_f, -np.sin(ac), z_f, tail], axis=1)
    return _const(cos.T), _const(sa.T), _const(sb.T)


def _identity_rope_tables(L):
    return jnp.ones((HEAD_PAD, L), F32), jnp.zeros((HEAD_PAD, L), F32), jnp.zeros((HEAD_PAD, L), F32)


def _pad_heads(w, per_head, keep):
    K = w.shape[0]
    w = w.reshape(K, MLA_HEADS, per_head)[:, :, :keep]
    w = jnp.pad(w, ((0, 0), (0, 0), (0, HEAD_PAD - keep)))
    return w.reshape(K, MLA_HEADS * HEAD_PAD)


def _pad_lanes(v, n):
    return jnp.pad(v, (0, n - v.shape[0])).reshape(1, n)


def _fourier_hyena_mixer(x, g, shift, scale, w_in, conv_w, conv_b, gmat, taps, norms, hy_bias):
    B, L, _ = x.shape
    if (L // FFT_RADIX) % 16 == 0:
        z, u, x0 = _fh_in(x, g, shift, scale, w_in, conv_w, conv_b, gmat, n_hi=L // FFT_RADIX)
        cs, tmat = _fourier_fft_tables(L)
        y_f = _fourier_seq_fft(z, cs, tmat)
        t1, t1_half, t3, gf, gi = _hyena_fft_tables(L)
        n_hi = 2 * L // FFT_RADIX
        rows = lambda a, r: a.reshape(a.shape[0], r, FFT_RADIX, D_HYENA)
        kspec = _hy_fft_b(_hy_fft_a(rows(taps.astype(BF16).reshape(1, 2 * L, D_HYENA), n_hi), t1), gf, norms=norms)
        v = _hy_fft_b(_hy_fft_a(rows(u, n_hi // 2), t1_half), gf, gi, kspec)
        y_h = _hy_fft_c(v, t3, rows(x0, n_hi // 2), rows(u, n_hi // 2), hy_bias)
        return y_f, y_h.reshape(B, L, D_HYENA)
    z, u, x0 = _fh_in(x, g, shift, scale, w_in, conv_w, conv_b, gmat)
    cl, sl, cb, sbf, sbi = _dft_tables(L)
    k_re, k_im = _hy_fwd(taps.astype(BF16), cb, sbf)
    kre, kim = _spec_combine(k_re, k_im, norms)
    y_f = _fourier_seq(z, cl, sl)
    yre, yim = _hy_fwd(u, cb, sbf, kre, kim)
    y_h = _hy_inv(yre, yim, x0, u, hy_bias, cb, sbi)
    return y_f, y_h


def kernel(x, c, ctx, c_ctx, norm1, norm2, w_mod, b_mod, ffn_w_up, ffn_conv_w, ffn_conv_b, ffn_w_down,
           fh_w_in, fh_w_out, hy_conv_w, hy_conv_b, hy_filt_w1, hy_filt_b1, hy_filt_w2, hy_filt_b2,
           hy_filt_w3, hy_filt_b3, hy_filt_w4, hy_freq, hy_bias, mla_w_in, mla_q_a_norm, mla_w_uq,
           mla_kv_a_norm, mla_w_ukv, mla_q_norm, mla_k_norm, mla_w_o):
    B, L, D = x.shape
    Lc = ctx.shape[1]
    bf = lambda a: a.astype(BF16)

    n_cond = -(-(B + 1) // 8) * 8
    cond = jnp.concatenate([c, c_ctx[None, :], jnp.zeros((n_cond - B - 1, D), F32)], axis=0)

    def mods(i):
        m = _adaln(cond, w_mod[i], b_mod[i])
        mx = [m[:B, j * D:(j + 1) * D].reshape(B, 1, D) for j in range(6)]
        mc = [m[B:B + 1, j * D:(j + 1) * D].reshape(1, 1, D) for j in range(6)]
        return mx, mc

    (sh1, sc1, g1, sh2, sc2, g2), (csh1, csc1, cg1, csh2, csc2, cg2) = mods(0)
    gmat = _group_dft_matrix()
    w_in0 = bf(fh_w_in[0])
    w_out0 = bf(fh_w_out[0])
    ffn0 = (bf(ffn_w_up[0]), ffn_conv_w[0], ffn_conv_b[0], bf(ffn_w_down[0]))
    filt = (hy_filt_w1[0], hy_filt_b1[0], hy_filt_w2[0], hy_filt_b2[0], hy_filt_w3[0], hy_filt_b3[0],
            hy_filt_w4[0], hy_freq[0])

    def mixer_layer(h, n_seq, m1, m2):
        s1, c1, gt1 = m1
        s2, c2, gt2 = m2
        taps, norms = _hyena_filter(n_seq, *filt)
        y_f, y_h = _fourier_hyena_mixer(h, norm1[0], s1, c1, w_in0, hy_conv_w[0], hy_conv_b[0], gmat, taps,
                                        norms, hy_bias[0])
        return _mix_ffn(h, [y_f, y_h], w_out0, gt1, norm2[0], s2, c2, gt2, *ffn0)

    x = mixer_layer(x, L, (sh1, sc1, g1), (sh2, sc2, g2))
    ctx = mixer_layer(ctx, Lc, (csh1, csc1, cg1), (csh2, csc2, cg2))

    (sh1, sc1, g1, sh2, sc2, g2), (csh1, csc1, _, _, _, _) = mods(1)
    w_in1 = bf(jnp.pad(mla_w_in[0], ((0, 0), (0, MLA_IN_PAD - mla_w_in.shape[2]))))
    wuq = bf(_pad_heads(mla_w_uq[0], QK_DIM, QK_DIM).T)
    wuk = bf(_pad_heads(mla_w_ukv[0], NOPE + V_DIM, NOPE).T)
    wuv = bf(mla_w_ukv[0].reshape(KV_LORA, MLA_HEADS, NOPE + V_DIM)[:, :, NOPE:].reshape(KV_LORA, MLA_HEADS * V_DIM).T)
    half_scale = math.sqrt(QK_DIM ** -0.5 * math.log2(math.e))
    wts = (w_in1, mla_q_a_norm[0].reshape(1, Q_LORA), wuq, _pad_lanes(mla_q_norm[0] * half_scale, HEAD_PAD),
           mla_kv_a_norm[0].reshape(1, KV_LORA), wuk, wuv, _pad_lanes(mla_k_norm[0] * half_scale, HEAD_PAD))
    qt, kx, vtx = _mla_proj(x, norm1[1], sh1, sc1, wts, _rope_tables(L), True)
    kc, vtc = _mla_proj(ctx, norm1[1], csh1, csc1, wts, _identity_rope_tables(Lc), False)
    o = _attention(qt, kx, kc, vtx, vtc)
    ffn1 = (bf(ffn_w_up[1]), ffn_conv_w[1], ffn_conv_b[1], bf(ffn_w_down[1]))
    return _mix_ffn(x, [o], bf(mla_w_o[0]), g1, norm2[1], sh2, sc2, g2, *ffn1)
```

```python
import functools
import math

import jax
import jax.numpy as jnp
import numpy as np
from jax import lax
from jax.experimental import pallas as pl
from jax.experimental.pallas import tpu as pltpu

F32 = jnp.float32
BF16 = jnp.bfloat16
HIGHEST = lax.Precision.HIGHEST

EPS = 1e-6
D_MODEL = 1024
D_FF = 2816
D_FOURIER = 512
FOURIER_GROUP_DIM = 128
D_HYENA = 512
HYENA_EMB_DIM = 33
HYENA_EMB_PAD = 64
HYENA_BANDS = 16
HYENA_WIDTH = 64
HYENA_MIN_DECAY = math.log(1e-2) / 0.3
HYENA_MAX_DECAY = math.log(1e-2) / 1.5
MLA_HEADS = 16
Q_LORA = 256
KV_LORA = 128
NOPE = 64
ROPE = 32
QK_DIM = NOPE + ROPE
V_DIM = 64
HEAD_PAD = 128
MLA_IN_PAD = 512
GRID_W = 64
ROPE_THETA = 10000.0

ATTN_DTYPE = jnp.float8_e4m3fn
ATTN_P_SHIFT = 8.0
FFN_CHUNK = 256
FFT_RADIX = 128
FFT_KRON = 16
HALO = 16
VMEM_LIMIT = 56 * 1024 * 1024


def _cparams(sem, flags=None):
    return pltpu.CompilerParams(dimension_semantics=sem, vmem_limit_bytes=VMEM_LIMIT, flags=flags)


def _dot(a, b):
    return jnp.dot(a, b, preferred_element_type=F32)


def _norm_mod(x, g, shift, scale):
    ms = jnp.mean(x * x, axis=-1, keepdims=True)
    return (x * lax.rsqrt(ms + EPS) * g) * (1.0 + scale) + shift


def _silu(x):
    return x * (1.0 / (1.0 + jnp.exp(-x)))


def _row_tile(L, want):
    t = min(L, want)
    assert L % t == 0 and t % HALO == 0
    return t


def _mod_kernel(c_ref, w_ref, b_ref, o_ref):
    s = _silu(c_ref[...])
    o_ref[...] = jnp.dot(s, w_ref[...], precision=HIGHEST, preferred_element_type=F32) + b_ref[...]


def _adaln(cond, w_mod, b_mod):
    R, D = cond.shape
    n = w_mod.shape[1]
    tn = 768
    return pl.pallas_call(
        _mod_kernel,
        out_shape=jax.ShapeDtypeStruct((R, n), F32),
        grid=(n // tn,),
        in_specs=[pl.BlockSpec((R, D), lambda j: (0, 0)),
                  pl.BlockSpec((D, tn), lambda j: (0, j)),
                  pl.BlockSpec((1, tn), lambda j: (0, j))],
        out_specs=pl.BlockSpec((R, tn), lambda j: (0, j)),
        compiler_params=_cparams(("arbitrary",)),
        name="adaln_mod",
    )(cond, w_mod, b_mod.reshape(1, n))


def _halo_specs(tm, D, L):
    nb = L // HALO
    per = tm // HALO
    cur = pl.BlockSpec((1, tm, D), lambda b, i, *_: (b, i, 0))
    prev = pl.BlockSpec((1, HALO, D), lambda b, i, *_: (b, jnp.maximum(i * per - 1, 0), 0))
    nxt = pl.BlockSpec((1, HALO, D), lambda b, i, *_: (b, jnp.minimum((i + 1) * per, nb - 1), 0))
    return cur, prev, nxt


def _mod_spec(arr):
    D = arr.shape[-1]
    if arr.shape[0] == 1:
        return pl.BlockSpec((1, 1, D), lambda b, *_: (0, 0, 0))
    return pl.BlockSpec((1, 1, D), lambda b, *_: (b, 0, 0))


def _fill_hn_rows(hn_ref, xc, xp, xn, g, shift, scale, tm, i, nt):
    hp = _norm_mod(xp, g, shift, scale)
    hn_ref[0:HALO, :] = jnp.where(i > 0, hp, 0.0).astype(BF16)
    hn_ref[HALO:HALO + tm, :] = _norm_mod(xc, g, shift, scale).astype(BF16)
    hx = _norm_mod(xn, g, shift, scale)
    hn_ref[HALO + tm:, :] = jnp.where(i < nt - 1, hx, 0.0).astype(BF16)


def _fill_hn(hn_ref, xc_ref, xp_ref, xn_ref, g, shift, scale, tm, i, nt):
    _fill_hn_rows(hn_ref, xc_ref[0], xp_ref[0], xn_ref[0], g, shift, scale, tm, i, nt)


def _dwconv3_rows(p, w, tm):
    n = tm + 2 * HALO
    up = pltpu.roll(p, 1, 0)
    dn = pltpu.roll(p, n - 1, 0)
    c = up * w[0:1] + p * w[1:2] + dn * w[2:3]
    return c[HALO:HALO + tm]


def _fh_in_kernel(xc_ref, xp_ref, xn_ref, g_ref, sh_ref, sc_ref, w_ref, cw_ref, cb_ref, gm_ref, *rest, tm, n_hi):
    if n_hi:
        perm_ref, z_ref, u_ref, x0_ref, hn_ref = rest
    else:
        z_ref, u_ref, x0_ref, hn_ref = rest
    i = pl.program_id(1)
    nt = pl.num_programs(1)
    _fill_hn(hn_ref, xc_ref, xp_ref, xn_ref, g_ref[...], sh_ref[0], sc_ref[0], tm, i, nt)
    proj = _dot(hn_ref[...], w_ref[...])
    uf = proj[HALO:HALO + tm, :D_FOURIER].astype(BF16)
    if n_hi:
        uf = _dot(perm_ref[...], uf).astype(BF16)
        z = _dot(uf, gm_ref[...]).astype(BF16)
        z_ref[0] = z.reshape(n_hi, tm // n_hi, 2 * D_FOURIER)
    else:
        z_ref[0] = _dot(uf, gm_ref[...]).astype(BF16)
    c = _dwconv3_rows(proj[:, D_FOURIER:], cw_ref[...], tm) + cb_ref[...]
    x0 = c[:, :D_HYENA]
    x1 = c[:, D_HYENA:2 * D_HYENA]
    v = c[:, 2 * D_HYENA:]
    x0_ref[0] = x0
    u_ref[0] = (v * x1).astype(BF16)


def _fh_in(x, g, shift, scale, w_in, conv_w, conv_b, gmat, n_hi=0):
    B, L, D = x.shape
    tm = _row_tile(L, 512)
    cur, prev, nxt = _halo_specs(tm, D, L)
    n_in = w_in.shape[1]
    nh = 3 * D_HYENA
    full = lambda shape: pl.BlockSpec(shape, lambda b, i: (0,) * len(shape))
    out_tile = lambda c: pl.BlockSpec((1, tm, c), lambda b, i: (b, i, 0))
    in_specs = [cur, prev, nxt, full((1, D)), _mod_spec(shift), _mod_spec(scale),
                full((D, n_in)), full((3, nh)), full((1, nh)), full((D_FOURIER, 2 * D_FOURIER))]
    args = [x, x, x, g.reshape(1, D), shift, scale, w_in, conv_w, conv_b.reshape(1, nh), gmat]
    if n_hi:
        per = tm // n_hi
        r = np.arange(tm)
        src = n_hi * (r % per) + r // per
        args.append(_const(src[:, None] == r[None, :], BF16))
        in_specs.append(full((tm, tm)))
        z_shape = jax.ShapeDtypeStruct((B, n_hi, L // n_hi, 2 * D_FOURIER), BF16)
        z_spec = pl.BlockSpec((1, n_hi, per, 2 * D_FOURIER), lambda b, i: (b, 0, i, 0))
    else:
        z_shape = jax.ShapeDtypeStruct((B, L, 2 * D_FOURIER), BF16)
        z_spec = out_tile(2 * D_FOURIER)
    return pl.pallas_call(
        functools.partial(_fh_in_kernel, tm=tm, n_hi=n_hi),
        out_shape=(z_shape, jax.ShapeDtypeStruct((B, L, D_HYENA), BF16), jax.ShapeDtypeStruct((B, L, D_HYENA), F32)),
        grid=(B, L // tm),
        in_specs=in_specs,
        out_specs=(z_spec, out_tile(D_HYENA), out_tile(D_HYENA)),
        scratch_shapes=[pltpu.VMEM((tm + 2 * HALO, D), BF16)],
        compiler_params=_cparams(("parallel", "arbitrary")),
        name="fh_in",
    )(*args)


def _fourier_kernel(c_ref, s_ref, z_ref, o_ref, *, scale):
    z = z_ref[0]
    y = _dot(c_ref[...], z[:, :D_FOURIER]) + _dot(s_ref[...], z[:, D_FOURIER:])
    o_ref[0] = (y * scale).astype(BF16)


def _fourier_seq(z, cl, sl):
    B, L, _ = z.shape
    tf = _row_tile(L, 512)
    scale = 1.0 / math.sqrt(L * FOURIER_GROUP_DIM)
    return pl.pallas_call(
        functools.partial(_fourier_kernel, scale=scale),
        out_shape=jax.ShapeDtypeStruct((B, L, D_FOURIER), BF16),
        grid=(B, L // tf),
        in_specs=[pl.BlockSpec((tf, L), lambda b, i: (i, 0)),
                  pl.BlockSpec((tf, L), lambda b, i: (i, 0)),
                  pl.BlockSpec((1, L, 2 * D_FOURIER), lambda b, i: (b, 0, 0))],
        out_specs=pl.BlockSpec((1, tf, D_FOURIER), lambda b, i: (b, i, 0)),
        compiler_params=_cparams(("parallel", "arbitrary")),
        name="fourier_seq",
    )(cl, sl, z)


def _fft_f1_kernel(cs_ref, z_ref, o_ref, *, nc):
    for j in range(nc):
        cs = _dot(cs_ref[...], z_ref[0, j])
        cz = cs[:FFT_RADIX]
        sz = cs[FFT_RADIX:]
        o_ref[0, 0, j] = (cz[:, :D_FOURIER] + sz[:, D_FOURIER:]).astype(BF16)
        o_ref[0, 1, j] = (cz[:, D_FOURIER:] - sz[:, :D_FOURIER]).astype(BF16)


def _fft_f2_kernel(t_ref, b_ref, o_ref, *, scale):
    n_hi = b_ref.shape[2]
    blk = b_ref[0].reshape(2 * n_hi * 8, D_FOURIER)
    y = _dot(t_ref[0], blk) * scale
    o_ref[0, :, 0] = y.reshape(n_hi, 8, D_FOURIER).astype(BF16)


def _fourier_seq_fft(zp, cs, tmat):
    B, n_hi, _, _ = zp.shape
    L = n_hi * FFT_RADIX
    nc = 8
    ng = FFT_RADIX // 8
    bc = pl.pallas_call(
        functools.partial(_fft_f1_kernel, nc=nc),
        out_shape=jax.ShapeDtypeStruct((B, 2, n_hi, FFT_RADIX, D_FOURIER), BF16),
        grid=(B, n_hi // nc),
        in_specs=[pl.BlockSpec((2 * FFT_RADIX, FFT_RADIX), lambda b, i: (0, 0)),
                  pl.BlockSpec((1, nc, FFT_RADIX, 2 * D_FOURIER), lambda b, i: (b, i, 0, 0))],
        out_specs=pl.BlockSpec((1, 2, nc, FFT_RADIX, D_FOURIER), lambda b, i: (b, 0, i, 0, 0)),
        compiler_params=_cparams(("parallel", "arbitrary")),
        name="fourier_fft1",
    )(cs, zp)
    scale = 1.0 / math.sqrt(L * FOURIER_GROUP_DIM)
    y = pl.pallas_call(
        functools.partial(_fft_f2_kernel, scale=scale),
        out_shape=jax.ShapeDtypeStruct((B, n_hi, ng, 8, D_FOURIER), BF16),
        grid=(B, ng),
        in_specs=[pl.BlockSpec((1, n_hi * 8, 2 * n_hi * 8), lambda b, g: (g, 0, 0)),
                  pl.BlockSpec((1, 2, n_hi, 8, D_FOURIER), lambda b, g: (b, 0, 0, g, 0))],
        out_specs=pl.BlockSpec((1, n_hi, 1, 8, D_FOURIER), lambda b, g: (b, 0, g, 0, 0)),
        compiler_params=_cparams(("parallel", "arbitrary")),
        name="fourier_fft2",
    )(tmat, bc)
    return y.reshape(B, L, D_FOURIER)


def _filter_kernel(z_ref, w1_ref, b1_ref, w2_ref, b2_ref, w3_ref, b3_ref, w4_ref, f_ref, dl_ref,
                   k_ref, n_ref):
    i = pl.program_id(1)
    z = z_ref[0]
    f = f_ref[...]
    hd = lambda a, w: jnp.dot(a, w, precision=HIGHEST, preferred_element_type=F32)
    h = jnp.sin(f * (hd(z, w1_ref[...]) + b1_ref[...]))
    h = jnp.sin(f * (hd(h, w2_ref[...]) + b2_ref[...]))
    h = jnp.sin(f * (hd(h, w3_ref[...]) + b3_ref[...]))
    h = hd(h, w4_ref[0])
    t = z[:, 0:1]
    valid = z[:, HYENA_EMB_DIM:HYENA_EMB_DIM + 1]
    k = h * jnp.exp(-t * jnp.abs(dl_ref[...])) * valid
    k_ref[0] = k

    @pl.when(i == 0)
    def _():
        n_ref[...] = jnp.zeros_like(n_ref)

    n_ref[0] += jnp.sum(jnp.abs(k), axis=0, keepdims=True)


def _hyena_filter(L, w1, b1, w2, b2, w3, b3, w4, freq):
    pos = np.arange(L, dtype=np.float64)
    bands = np.linspace(1e-4, HYENA_BANDS - 1, HYENA_BANDS)

    def emb(p, valid):
        t = p / max(L - 1, 1)
        ang = (2.0 * math.pi / L) * p[:, None] * bands[None, :]
        pad = np.zeros((L, HYENA_EMB_PAD - HYENA_EMB_DIM - 1))
        return np.concatenate([t[:, None], np.cos(ang), -np.sin(ang), valid[:, None], pad], axis=-1)

    zf = emb(pos, np.ones((L,)))
    zb = emb(np.where(pos > 0, L - pos, 0.0), (pos > 0).astype(np.float64))
    z = _const(np.stack([zf, zb]))
    w1p = jnp.concatenate([w1, jnp.zeros((HYENA_EMB_PAD - HYENA_EMB_DIM, HYENA_WIDTH), F32)], axis=0)
    w4s = jnp.stack([w4[:, :D_HYENA], w4[:, D_HYENA:]])
    deltas = _const(np.linspace(HYENA_MIN_DECAY, HYENA_MAX_DECAY, D_HYENA).reshape(1, D_HYENA))
    tr = _row_tile(L, 512)
    W = HYENA_WIDTH
    full = lambda shape: pl.BlockSpec(shape, lambda s, i: (0,) * len(shape))
    return pl.pallas_call(
        _filter_kernel,
        out_shape=(jax.ShapeDtypeStruct((2, L, D_HYENA), F32), jax.ShapeDtypeStruct((2, 1, D_HYENA), F32)),
        grid=(2, L // tr),
        in_specs=[pl.BlockSpec((1, tr, HYENA_EMB_PAD), lambda s, i: (s, i, 0)),
                  full((HYENA_EMB_PAD, W)), full((1, W)), full((W, W)), full((1, W)),
                  full((W, W)), full((1, W)),
                  pl.BlockSpec((1, W, D_HYENA), lambda s, i: (s, 0, 0)),
                  full((1, W)), full((1, D_HYENA))],
        out_specs=(pl.BlockSpec((1, tr, D_HYENA), lambda s, i: (s, i, 0)),
                   pl.BlockSpec((1, 1, D_HYENA), lambda s, i: (s, 0, 0))),
        compiler_params=_cparams(("arbitrary", "arbitrary")),
        name="hyena_filter",
    )(z, w1p, b1.reshape(1, W), w2, b2.reshape(1, W), w3, b3.reshape(1, W), w4s, freq.reshape(1, W), deltas)


def _hy_fwd_kernel(c_ref, s_ref, u_ref, *rest, raw, tf):
    u = u_ref[0]
    ure = _dot(c_ref[...], u)
    uim = -_dot(s_ref[...], u)
    if raw:
        re_ref, im_ref = rest
        re_ref[0] = ure
        im_ref[0] = uim
        return
    kre_ref, kim_ref, re_ref, im_ref = rest
    kre = kre_ref[...]
    kim = kim_ref[...]
    row = lax.broadcasted_iota(jnp.int32, (tf, 1), 0) + pl.program_id(1) * tf
    packed = row == 0
    a = uim * kim
    re_ref[0] = (ure * kre - jnp.where(packed, 0.0, a)).astype(BF16)
    im_ref[0] = jnp.where(packed, a, ure * kim + uim * kre).astype(BF16)


def _hy_fwd(u, cb, sbf, kre=None, kim=None):
    B, L, C = u.shape
    tf = _row_tile(L, 512)
    raw = kre is None
    mat = pl.BlockSpec((tf, L), lambda b, i: (i, 0))
    tile = pl.BlockSpec((1, tf, C), lambda b, i: (b, i, 0))
    in_specs = [mat, mat, pl.BlockSpec((1, L, C), lambda b, i: (b, 0, 0))]
    args = [cb, sbf, u]
    if not raw:
        in_specs += [pl.BlockSpec((tf, C), lambda b, i: (i, 0))] * 2
        args += [kre, kim]
    dt = F32 if raw else BF16
    return pl.pallas_call(
        functools.partial(_hy_fwd_kernel, raw=raw, tf=tf),
        out_shape=(jax.ShapeDtypeStruct((B, L, C), dt), jax.ShapeDtypeStruct((B, L, C), dt)),
        grid=(B, L // tf),
        in_specs=in_specs,
        out_specs=(tile, tile),
        compiler_params=_cparams(("parallel", "arbitrary")),
        name="hyena_fwd_raw" if raw else "hyena_fwd",
    )(*args)


def _spec_combine_kernel(re_ref, im_ref, n_ref, kre_ref, kim_ref, *, tf, n_fft):
    row = lax.broadcasted_iota(jnp.int32, (tf, 1), 0) + pl.program_id(0) * tf
    sgn = (1 - 2 * (row & 1)).astype(F32)
    wgt = jnp.where(row == 0, 1.0 / n_fft, 2.0 / n_fft)
    s = wgt / (n_ref[0] + n_ref[1])
    kre_ref[...] = (re_ref[0] + sgn * re_ref[1]) * s
    kim_ref[...] = (im_ref[0] + sgn * im_ref[1]) * s


def _spec_combine(re, im, norms):
    _, L, C = re.shape
    tf = _row_tile(L, 512)
    pair = pl.BlockSpec((2, tf, C), lambda i: (0, i, 0))
    tile = pl.BlockSpec((tf, C), lambda i: (i, 0))
    return pl.pallas_call(
        functools.partial(_spec_combine_kernel, tf=tf, n_fft=2 * L),
        out_shape=(jax.ShapeDtypeStruct((L, C), F32), jax.ShapeDtypeStruct((L, C), F32)),
        grid=(L // tf,),
        in_specs=[pair, pair, pl.BlockSpec((2, 1, C), lambda i: (0, 0, 0))],
        out_specs=(tile, tile),
        compiler_params=_cparams(("arbitrary",)),
        name="hyena_spec_combine",
    )(re, im, norms)


def _hy_inv_kernel(c_ref, s_ref, re_ref, im_ref, x0_ref, u_ref, b_ref, o_ref):
    y = _dot(c_ref[...], re_ref[0]) - _dot(s_ref[...], im_ref[0])
    o_ref[0] = (x0_ref[0] * (y + u_ref[0].astype(F32) * b_ref[...])).astype(BF16)


def _hy_inv(yre, yim, x0, u, bias, cb, sbi):
    B, L, C = u.shape
    tt = _row_tile(L, 512)
    mat = pl.BlockSpec((tt, L), lambda b, i: (i, 0))
    whole = pl.BlockSpec((1, L, C), lambda b, i: (b, 0, 0))
    tile = pl.BlockSpec((1, tt, C), lambda b, i: (b, i, 0))
    return pl.pallas_call(
        _hy_inv_kernel,
        out_shape=jax.ShapeDtypeStruct((B, L, C), BF16),
        grid=(B, L // tt),
        in_specs=[mat, mat, whole, whole, tile, tile, pl.BlockSpec((1, C), lambda b, i: (0, 0))],
        out_specs=tile,
        compiler_params=_cparams(("parallel", "arbitrary")),
        name="hyena_inv",
    )(cb, sbi, yre, yim, x0, u, bias.reshape(1, C))


def _fft_a_kernel(t_ref, x_ref, o_ref):
    r2, kr, c = x_ref.shape[1:]
    y = _dot(t_ref[...], x_ref[0].reshape(r2 * kr, c))
    o_ref[0] = y.reshape(2, y.shape[0] // (2 * kr), kr, c).astype(BF16)


def _hy_fft_a(x4, t1):
    Bx, R, _, C = x4.shape
    n_hi = t1.shape[0] // (2 * FFT_KRON)
    return pl.pallas_call(
        _fft_a_kernel,
        out_shape=jax.ShapeDtypeStruct((Bx, 2, n_hi, FFT_RADIX, C), BF16),
        grid=(Bx, FFT_RADIX // FFT_KRON),
        in_specs=[pl.BlockSpec(t1.shape, lambda b, i: (0, 0)),
                  pl.BlockSpec((1, R, FFT_KRON, C), lambda b, i: (b, 0, i, 0))],
        out_specs=pl.BlockSpec((1, 2, n_hi, FFT_KRON, C), lambda b, i: (b, 0, 0, i, 0)),
        compiler_params=_cparams(("parallel", "arbitrary")),
        name="hyena_fft_a",
    )(t1, x4)


def _fft_b_kernel(gf_ref, a_ref, *rest, raw, n_fft):
    nb = a_ref.shape[0]
    gf = gf_ref[0]
    if raw:
        n_ref, o_ref = rest
        s = 1.0 / (n_fft * (n_ref[0] + n_ref[1]))
    else:
        gi_ref, k_ref, o_ref = rest
        gi = gi_ref[0]
        kr = k_ref[0, 0, 0]
        ki = k_ref[0, 1, 0]
    for b in range(nb):
        a = a_ref[b, :, 0].reshape(2 * FFT_RADIX, D_HYENA)
        x = _dot(gf, a)
        if raw:
            o_ref[b, :, 0] = (x * s).reshape(2, FFT_RADIX, D_HYENA)
            continue
        xr = x[:FFT_RADIX]
        xi = x[FFT_RADIX:]
        y = jnp.concatenate([xr * kr - xi * ki, xr * ki + xi * kr], axis=0).astype(BF16)
        o_ref[b, :, 0] = _dot(gi, y).reshape(2, FFT_RADIX, D_HYENA).astype(BF16)


def _hy_fft_b(a5, gf, gi=None, kspec=None, norms=None):
    Bx, _, n_hi, _, C = a5.shape
    raw = kspec is None
    blk = pl.BlockSpec((Bx, 2, 1, FFT_RADIX, C), lambda k: (0, 0, k, 0, 0))
    tab = pl.BlockSpec((1, 2 * FFT_RADIX, 2 * FFT_RADIX), lambda k: (k, 0, 0))
    if raw:
        in_specs = [tab, blk, pl.BlockSpec((2, 1, C), lambda k: (0, 0, 0))]
        args = (gf, a5, norms)
    else:
        kblk = pl.BlockSpec((1, 2, 1, FFT_RADIX, C), lambda k: (0, 0, k, 0, 0))
        in_specs = [tab, blk, tab, kblk]
        args = (gf, a5, gi, kspec)
    return pl.pallas_call(
        functools.partial(_fft_b_kernel, raw=raw, n_fft=n_hi * FFT_RADIX),
        out_shape=jax.ShapeDtypeStruct(a5.shape, F32 if raw else BF16),
        grid=(n_hi,),
        in_specs=in_specs,
        out_specs=blk,
        compiler_params=_cparams(("arbitrary",)),
        name="hyena_fft_spec" if raw else "hyena_fft_b",
    )(*args)


def _fft_c_kernel(t_ref, v_ref, x0_ref, u_ref, b_ref, o_ref):
    _, n_hi, kr, c = v_ref.shape[1:]
    y = _dot(t_ref[...], v_ref[0].reshape(2 * n_hi * kr, c))
    y = y.reshape(y.shape[0] // kr, kr, c)
    o_ref[0] = (x0_ref[0] * (y + u_ref[0].astype(F32) * b_ref[...])).astype(BF16)


def _hy_fft_c(v5, t3, x0, u, bias):
    B, _, n_hi, _, C = v5.shape
    R = t3.shape[0] // FFT_KRON
    tile = pl.BlockSpec((1, R, FFT_KRON, C), lambda b, i: (b, 0, i, 0))
    return pl.pallas_call(
        _fft_c_kernel,
        out_shape=jax.ShapeDtypeStruct((B, R, FFT_RADIX, C), BF16),
        grid=(B, FFT_RADIX // FFT_KRON),
        in_specs=[pl.BlockSpec(t3.shape, lambda b, i: (0, 0)),
                  pl.BlockSpec((1, 2, n_hi, FFT_KRON, C), lambda b, i: (b, 0, 0, i, 0)),
                  tile, tile, pl.BlockSpec((1, C), lambda b, i: (0, 0))],
        out_specs=tile,
        compiler_params=_cparams(("parallel", "arbitrary")),
        name="hyena_fft_c",
    )(t3, v5, x0, u, bias.reshape(1, C))


def _ffn_kernel(xc_ref, xp_ref, xn_ref, g1_ref, wo_ref, g_ref, sh_ref, sc_ref, gate_ref, wu_ref, cw_ref, cb_ref,
                wd_ref, *rest, tm, n_y):
    y_refs = rest[:3 * n_y]
    o_ref, yext_ref, hn_ref, act_ref = rest[3 * n_y:]
    i = pl.program_id(1)
    nt = pl.num_programs(1)
    off = 0
    for k in range(n_y):
        yc_ref, yp_ref, yn_ref = y_refs[3 * k:3 * k + 3]
        c = yc_ref.shape[-1]
        yext_ref[0:HALO, off:off + c] = yp_ref[0].astype(BF16)
        yext_ref[HALO:HALO + tm, off:off + c] = yc_ref[0].astype(BF16)
        yext_ref[HALO + tm:, off:off + c] = yn_ref[0].astype(BF16)
        off += c
    mix = g1_ref[0] * _dot(yext_ref[...], wo_ref[...])
    x1 = xc_ref[0] + mix[HALO:HALO + tm]
    _fill_hn_rows(hn_ref, x1, xp_ref[0] + mix[:HALO], xn_ref[0] + mix[HALO + tm:], g_ref[...], sh_ref[0], sc_ref[0],
                  tm, i, nt)
    for c in range(0, D_FF, FFN_CHUNK):
        cols = slice(c, c + FFN_CHUNK)
        gx = _dot(hn_ref[...], wu_ref[:, cols])
        vx = _dot(hn_ref[HALO:HALO + tm, :], wu_ref[:, D_FF + c:D_FF + c + FFN_CHUNK])
        cv = _dwconv3_rows(gx, cw_ref[:, cols], tm) + cb_ref[:, cols]
        act_ref[:, cols] = (_silu(cv) * vx).astype(BF16)
    o_ref[0] = x1 + gate_ref[0] * _dot(act_ref[...], wd_ref[...])


def _mix_ffn(x, ys, w_o, gate1, g, shift, scale, gate2, w_up, conv_w, conv_b, w_down):
    B, L, D = x.shape
    tm = _row_tile(L, 1024)
    cur, prev, nxt = _halo_specs(tm, D, L)
    const = lambda shape: pl.BlockSpec(shape, lambda b, i: (0,) * len(shape), pipeline_mode=pl.Buffered(1))
    y_specs, y_args = [], []
    for y in ys:
        y_specs += list(_halo_specs(tm, y.shape[-1], L))
        y_args += [y, y, y]
    return pl.pallas_call(
        functools.partial(_ffn_kernel, tm=tm, n_y=len(ys)),
        out_shape=jax.ShapeDtypeStruct((B, L, D), F32),
        grid=(B, L // tm),
        in_specs=[cur, prev, nxt, _mod_spec(gate1), const(w_o.shape), const((1, D)), _mod_spec(shift),
                  _mod_spec(scale), _mod_spec(gate2), const((D, 2 * D_FF)), const((3, D_FF)), const((1, D_FF)),
                  const((D_FF, D))] + y_specs,
        out_specs=pl.BlockSpec((1, tm, D), lambda b, i: (b, i, 0)),
        scratch_shapes=[pltpu.VMEM((tm + 2 * HALO, w_o.shape[0]), BF16), pltpu.VMEM((tm + 2 * HALO, D), BF16),
                        pltpu.VMEM((tm, D_FF), BF16)],
        compiler_params=_cparams(("parallel", "arbitrary")),
        name="mix_ffn",
    )(x, x, x, gate1, w_o, g.reshape(1, D), shift, scale, gate2, w_up, conv_w, conv_b.reshape(1, D_FF), w_down,
      *y_args)


def _rms(v, g, n):
    return v * lax.rsqrt(jnp.sum(v * v, axis=-1, keepdims=True) * (1.0 / n) + EPS) * g


def _rope_gain_tables(gain, cos, sa, sb):
    rot = slice(NOPE, QK_DIM)
    gr = gain[rot]
    return gain[:NOPE], gr * cos[rot], pltpu.roll(gr, 8, 0) * sa[rot], pltpu.roll(gr, ROPE - 8, 0) * sb[rot]


def _rope_t(yr, gc, gsa, gsb):
    return yr * gc + pltpu.roll(yr, 8, 0) * gsa + pltpu.roll(yr, ROPE - 8, 0) * gsb


def _sumsq(x):
    return jnp.sum(x * x, axis=0, keepdims=True)


def _mla_proj_kernel(x_ref, g_ref, sh_ref, sc_ref, win_ref, qan_ref, wuq_ref, qn_ref, kvn_ref, wuk_ref,
                     wuv_ref, kn_ref, cos_ref, sa_ref, sb_ref, *outs, with_q):
    hn = _norm_mod(x_ref[0], g_ref[...], sh_ref[0], sc_ref[0]).astype(BF16)
    a = _dot(hn, win_ref[...])
    tm = a.shape[0]
    cos = cos_ref[...]
    sa = sa_ref[...]
    sb = sb_ref[...]
    pad = jnp.zeros((HEAD_PAD - QK_DIM, tm), ATTN_DTYPE)
    if with_q:
        qt_ref, k_ref, vt_ref = outs
        qat = _rms(a[:, :Q_LORA], qan_ref[...], Q_LORA).T.astype(BF16)
        qft = _dot(wuq_ref[...], qat)
        gn, gc, gsa, gsb = _rope_gain_tables(qn_ref[...], cos, sa, sb)
        for h in range(MLA_HEADS):
            base = h * HEAD_PAD
            xn = qft[base:base + NOPE]
            xr = qft[base + NOPE:base + QK_DIM]
            r = lax.rsqrt((_sumsq(xn) + _sumsq(xr)) * (1.0 / QK_DIM) + EPS)
            qt_ref[0, base:base + NOPE, :] = (xn * r * gn).astype(ATTN_DTYPE)
            qt_ref[0, base + NOPE:base + QK_DIM, :] = _rope_t(xr * r, gc, gsa, gsb).astype(ATTN_DTYPE)
            qt_ref[0, base + QK_DIM:base + HEAD_PAD, :] = pad
    else:
        k_ref, vt_ref = outs
    ckvt = _rms(a[:, Q_LORA:Q_LORA + KV_LORA], kvn_ref[...], KV_LORA).T.astype(BF16)
    kft = _dot(wuk_ref[...], ckvt)
    vt_ref[0] = _dot(wuv_ref[...], ckvt).astype(ATTN_DTYPE)
    kpe = a[:, Q_LORA + KV_LORA:].T[:ROPE]
    gn, gc, gsa, gsb = _rope_gain_tables(kn_ref[...], cos, sa, sb)
    kpe_rot = _rope_t(kpe, gc, gsa, gsb)
    kpe_ss = _sumsq(kpe)
    for h in range(MLA_HEADS):
        base = h * HEAD_PAD
        xn = kft[base:base + NOPE]
        r = lax.rsqrt((_sumsq(xn) + kpe_ss) * (1.0 / QK_DIM) + EPS)
        kt = jnp.concatenate([xn * r * gn, kpe_rot * r, jnp.zeros((HEAD_PAD - QK_DIM, tm), F32)], axis=0)
        k_ref[0, :, base:base + HEAD_PAD] = kt.T.astype(ATTN_DTYPE)


def _mla_proj(x, g, shift, scale, wts, rope_tabs, with_q):
    B, L, D = x.shape
    tm = _row_tile(L, 512)
    win, qan, wuq, qn, kvn, wuk, wuv, kn = wts
    cos, sa, sb = rope_tabs
    HP = MLA_HEADS * HEAD_PAD
    HV = MLA_HEADS * V_DIM
    full = lambda a: pl.BlockSpec(a.shape, lambda b, i: (0,) * a.ndim)
    tab = pl.BlockSpec((HEAD_PAD, tm), lambda b, i: (0, i))
    out_shape = [jax.ShapeDtypeStruct((B, L, HP), ATTN_DTYPE), jax.ShapeDtypeStruct((B, HV, L), ATTN_DTYPE)]
    out_specs = [pl.BlockSpec((1, tm, HP), lambda b, i: (b, i, 0)),
                 pl.BlockSpec((1, HV, tm), lambda b, i: (b, 0, i))]
    if with_q:
        out_shape = [jax.ShapeDtypeStruct((B, HP, L), ATTN_DTYPE)] + out_shape
        out_specs = [pl.BlockSpec((1, HP, tm), lambda b, i: (b, 0, i))] + out_specs
    g2 = g.reshape(1, D)
    gain_tab = lambda v: jnp.broadcast_to(v.reshape(HEAD_PAD, 1), (HEAD_PAD, tm))
    qn = gain_tab(qn)
    kn = gain_tab(kn)
    return pl.pallas_call(
        functools.partial(_mla_proj_kernel, with_q=with_q),
        out_shape=tuple(out_shape),
        grid=(B, L // tm),
        in_specs=[pl.BlockSpec((1, tm, D), lambda b, i: (b, i, 0)), full(g2), _mod_spec(shift), _mod_spec(scale),
                  full(win), full(qan), full(wuq), full(qn), full(kvn), full(wuk), full(wuv), full(kn),
                  tab, tab, tab],
        out_specs=tuple(out_specs),
        compiler_params=_cparams(("parallel", "arbitrary")),
        name="mla_proj_q" if with_q else "mla_proj_kv",
    )(x, g2, shift, scale, win, qan, wuq, qn, kvn, wuk, wuv, kn, cos, sa, sb)


def _attn_kernel(qt_ref, kx_ref, kc_ref, vtx_ref, vtc_ref, o_ref, ax_ref, ac_ref, am_ref, bx_ref, bc_ref, bm_ref,
                 px_ref, pc_ref):
    t = pl.program_id(0)
    io = (qt_ref, kx_ref, kc_ref, vtx_ref, vtc_ref, o_ref, px_ref, pc_ref)

    @pl.when(t == 0)
    def _():
        bx_ref[...] = jnp.zeros(bx_ref.shape, BF16)
        bc_ref[...] = jnp.zeros(bc_ref.shape, BF16)
        bm_ref[...] = jnp.zeros(bm_ref.shape, BF16)

    @pl.when(t % 2 == 0)
    def _():
        _attn_step(*io, (ax_ref, ac_ref, am_ref), (bx_ref, bc_ref, bm_ref))

    @pl.when(t % 2 == 1)
    def _():
        _attn_step(*io, (bx_ref, bc_ref, bm_ref), (ax_ref, ac_ref, am_ref))


def _attn_step(qt_ref, kx_ref, kc_ref, vtx_ref, vtc_ref, o_ref, px_ref, pc_ref, parked, ready):
    wx_ref, wc_ref, wm_ref = parked
    rx_ref, rc_ref, rm_ref = ready
    tq = qt_ref.shape[2]
    off = rm_ref[...][None]

    def probs(s_ref):
        n = s_ref.shape[0]
        s = s_ref[...].reshape(n // 16, 16, 2 * tq)
        return jnp.exp2(s - off).reshape(n, 2 * tq).astype(ATTN_DTYPE)

    px_ref[...] = probs(rx_ref)
    pc_ref[...] = probs(rc_ref)

    q0 = qt_ref[0, :HEAD_PAD, :]
    q1 = qt_ref[0, HEAD_PAD:, :]
    zq = jnp.zeros_like(q0)
    qbd = jnp.concatenate([jnp.concatenate([q0, zq], axis=1), jnp.concatenate([zq, q1], axis=1)], axis=0)
    nx = _dot(kx_ref[0], qbd).astype(BF16)
    nc = _dot(kc_ref[0], qbd).astype(BF16)
    wx_ref[...] = nx
    wc_ref[...] = nc
    m = jnp.maximum(jnp.max(nx, axis=0, keepdims=True), jnp.max(nc, axis=0, keepdims=True)).astype(F32)
    wm_ref[...] = jnp.broadcast_to((m - ATTN_P_SHIFT).astype(BF16), wm_ref.shape)

    vtx = vtx_ref[0]
    vtc = vtc_ref[0]
    halves = []
    for j in range(2):
        own = slice(j * V_DIM, (j + 1) * V_DIM)
        oth = slice((1 - j) * V_DIM, (2 - j) * V_DIM)
        cols = slice(j * tq, (j + 1) * tq)
        parts_x = [vtx[own], jnp.ones_like(vtx[oth])]
        parts_c = [vtc[own], jnp.ones_like(vtc[oth])]
        if j == 1:
            parts_x.reverse()
            parts_c.reverse()
        r = (_dot(jnp.concatenate(parts_x, axis=0), px_ref[:, cols])
             + _dot(jnp.concatenate(parts_c, axis=0), pc_ref[:, cols]))
        halves.append(r[own] * (1.0 / r[oth][0:1]))
    o_ref[0] = jnp.concatenate(halves, axis=0).T.astype(BF16)


def _attention(qt, kx, kc, vtx, vtc):
    B, HP, L = qt.shape
    Lc = kc.shape[1]
    tq = _row_tile(L, 1024)
    nhp = MLA_HEADS // 2
    nq = L // tq
    n_tiles = B * nhp * nq

    def tile(t):
        t = jnp.clip(t, 0, n_tiles - 1)
        return t // (nhp * nq), (t // nq) % nhp, t % nq

    def logits_side(f):
        return lambda t: f(*tile(t))

    def value_side(f):
        return lambda t: f(*tile(t - 1))

    return pl.pallas_call(
        _attn_kernel,
        out_shape=jax.ShapeDtypeStruct((B, L, MLA_HEADS * V_DIM), BF16),
        grid=(n_tiles + 1,),
        in_specs=[pl.BlockSpec((1, 2 * HEAD_PAD, tq), logits_side(lambda b, h, i: (b, h, i))),
                  pl.BlockSpec((1, L, 2 * HEAD_PAD), logits_side(lambda b, h, i: (b, 0, h))),
                  pl.BlockSpec((1, Lc, 2 * HEAD_PAD), logits_side(lambda b, h, i: (b, 0, h))),
                  pl.BlockSpec((1, 2 * V_DIM, L), value_side(lambda b, h, i: (b, h, 0))),
                  pl.BlockSpec((1, 2 * V_DIM, Lc), value_side(lambda b, h, i: (b, h, 0)))],
        out_specs=pl.BlockSpec((1, tq, 2 * V_DIM), value_side(lambda b, h, i: (b, i, h))),
        scratch_shapes=[pltpu.VMEM((L, 2 * tq), BF16), pltpu.VMEM((Lc, 2 * tq), BF16), pltpu.VMEM((16, 2 * tq), BF16)] * 2
                       + [pltpu.VMEM((L, 2 * tq), ATTN_DTYPE), pltpu.VMEM((Lc, 2 * tq), ATTN_DTYPE)],
        compiler_params=_cparams(("arbitrary",)),
        name="mla_attention",
    )(qt, kx, kc, vtx, vtc)


def _const(a, dtype=F32):
    return jnp.asarray(np.asarray(a, np.float32)).astype(dtype)


def _angle(num, period):
    return (num % period).astype(np.float64) * (2.0 * math.pi / period)


def _trig_matrix(n, period):
    r = np.arange(n, dtype=np.int64)
    a = _angle(r[:, None] * r[None, :], period)
    return np.cos(a), np.sin(a)


def _dft_tables(L):
    cl, sl = _trig_matrix(L, L)
    cb, sb = _trig_matrix(L, 2 * L)
    idx = np.arange(L)
    alt = (1 - 2 * (idx & 1)).astype(np.float64)
    sbf = np.where(idx[:, None] == 0, -alt[None, :], sb)
    sbi = np.where(idx[None, :] == 0, -alt[:, None], sb)
    return tuple(_const(m, BF16) for m in (cl, sl, cb, sbf, sbi))


def _fourier_fft_tables(L):
    n_hi = L // FFT_RADIX
    i = np.arange(FFT_RADIX, dtype=np.int64)
    a = _angle(i[:, None] * i[None, :], FFT_RADIX)
    cs = np.concatenate([np.cos(a), np.sin(a)], axis=0)
    ng = FFT_RADIX // 8
    g = np.arange(ng, dtype=np.int64)[:, None, None, None]
    k1 = np.arange(n_hi, dtype=np.int64)[None, :, None, None]
    j = np.arange(8, dtype=np.int64)[None, None, :, None]
    n1 = np.arange(n_hi, dtype=np.int64)[None, None, None, :]
    ang = _angle(n1 * (FFT_RADIX * k1 + 8 * g + j), L)
    eye = np.eye(8)
    blocks = [t[:, :, :, :, None] * eye[None, None, :, None, :] for t in (np.cos(ang), np.sin(ang))]
    tmat = np.stack(blocks, axis=3)
    return _const(cs, BF16), _const(tmat.reshape(ng, n_hi * 8, 2 * n_hi * 8), BF16)


def _hyena_fft_tables(L):
    n_fft = 2 * L
    n_hi = n_fft // FFT_RADIX
    k2 = np.arange(n_hi, dtype=np.int64)
    a1 = _angle(k2[:, None] * k2[None, :], n_hi)
    c1, s1 = np.cos(a1), np.sin(a1)
    eye = np.eye(FFT_KRON)
    kron = lambda m: (m[:, None, :, None] * eye[None, :, None, :]).reshape(m.shape[0] * FFT_KRON, m.shape[1] * FFT_KRON)
    t1 = _const(np.concatenate([kron(c1), kron(-s1)], axis=0), BF16)
    t1_half = t1[:, :n_hi // 2 * FFT_KRON]
    t3 = t1_half.T
    r = np.arange(FFT_RADIX, dtype=np.int64)
    k = k2[:, None, None] + n_hi * r[None, :, None]
    a2 = _angle(k * r[None, None, :], n_fft)
    c2, s2 = _const(np.cos(a2), BF16), _const(np.sin(a2), BF16)
    gf = jnp.concatenate([jnp.concatenate([c2, s2], axis=2), jnp.concatenate([-s2, c2], axis=2)], axis=1)
    c2t, s2t = jnp.swapaxes(c2, 1, 2), jnp.swapaxes(s2, 1, 2)
    gi = jnp.concatenate([jnp.concatenate([c2t, -s2t], axis=2), jnp.concatenate([s2t, c2t], axis=2)], axis=1)
    return t1, t1_half, t3, gf, gi


def _group_dft_matrix():
    g = FOURIER_GROUP_DIM
    ng = D_FOURIER // g
    c, s = _trig_matrix(g, g)
    eye = np.eye(ng)
    return _const(np.concatenate([np.kron(eye, c), -np.kron(eye, s)], axis=1), BF16)


def _rope_tables(L):
    pos = np.arange(L)
    rows = (pos // GRID_W).astype(np.float64)
    cols = (pos % GRID_W).astype(np.float64)
    nf = ROPE // 4
    inv_freq = ROPE_THETA ** (-np.arange(nf, dtype=np.float64) / nf)
    ar = rows[:, None] * inv_freq[None, :]
    ac = cols[:, None] * inv_freq[None, :]
    one = np.ones((L, NOPE))
    z_n = np.zeros((L, NOPE))
    z_f = np.zeros((L, nf))
    tail = np.zeros((L, HEAD_PAD - QK_DIM))
    cos = np.concatenate([one, np.cos(ar), np.cos(ar), np.cos(ac), np.cos(ac), tail], axis=1)
    sa = np.concatenate([z_n, z_f, np.sin(ar), z_f, np.sin(ac), tail], axis=1)
    sb = np.concatenate([z_n, -np.sin(ar), z_f, -np.sin(ac), z_f, tail], axis=1)
    return _const(cos.T), _const(sa.T), _const(sb.T)


def _identity_rope_tables(L):
    return jnp.ones((HEAD_PAD, L), F32), jnp.zeros((HEAD_PAD, L), F32), jnp.zeros((HEAD_PAD, L), F32)


def _pad_heads(w, per_head, keep):
    K = w.shape[0]
    w = w.reshape(K, MLA_HEADS, per_head)[:, :, :keep]
    w = jnp.pad(w, ((0, 0), (0, 0), (0, HEAD_PAD - keep)))
    return w.reshape(K, MLA_HEADS * HEAD_PAD)


def _pad_lanes(v, n):
    return jnp.pad(v, (0, n - v.shape[0])).reshape(1, n)


def _fourier_hyena_mixer(x, g, shift, scale, w_in, conv_w, conv_b, gmat, taps, norms, hy_bias):
    B, L, _ = x.shape
    if (L // FFT_RADIX) % 16 == 0:
        z, u, x0 = _fh_in(x, g, shift, scale, w_in, conv_w, conv_b, gmat, n_hi=L // FFT_RADIX)
        cs, tmat = _fourier_fft_tables(L)
        y_f = _fourier_seq_fft(z, cs, tmat)
        t1, t1_half, t3, gf, gi = _hyena_fft_tables(L)
        n_hi = 2 * L // FFT_RADIX
        rows = lambda a, r: a.reshape(a.shape[0], r, FFT_RADIX, D_HYENA)
        kspec = _hy_fft_b(_hy_fft_a(rows(taps.astype(BF16).reshape(1, 2 * L, D_HYENA), n_hi), t1), gf, norms=norms)
        v = _hy_fft_b(_hy_fft_a(rows(u, n_hi // 2), t1_half), gf, gi, kspec)
        y_h = _hy_fft_c(v, t3, rows(x0, n_hi // 2), rows(u, n_hi // 2), hy_bias)
        return y_f, y_h.reshape(B, L, D_HYENA)
    z, u, x0 = _fh_in(x, g, shift, scale, w_in, conv_w, conv_b, gmat)
    cl, sl, cb, sbf, sbi = _dft_tables(L)
    k_re, k_im = _hy_fwd(taps.astype(BF16), cb, sbf)
    kre, kim = _spec_combine(k_re, k_im, norms)
    y_f = _fourier_seq(z, cl, sl)
    yre, yim = _hy_fwd(u, cb, sbf, kre, kim)
    y_h = _hy_inv(yre, yim, x0, u, hy_bias, cb, sbi)
    return y_f, y_h


def kernel(x, c, ctx, c_ctx, norm1, norm2, w_mod, b_mod, ffn_w_up, ffn_conv_w, ffn_conv_b, ffn_w_down,
           fh_w_in, fh_w_out, hy_conv_w, hy_conv_b, hy_filt_w1, hy_filt_b1, hy_filt_w2, hy_filt_b2,
           hy_filt_w3, hy_filt_b3, hy_filt_w4, hy_freq, hy_bias, mla_w_in, mla_q_a_norm, mla_w_uq,
           mla_kv_a_norm, mla_w_ukv, mla_q_norm, mla_k_norm, mla_w_o):
    B, L, D = x.shape
    Lc = ctx.shape[1]
    bf = lambda a: a.astype(BF16)

    n_cond = -(-(B + 1) // 8) * 8
    cond = jnp.concatenate([c, c_ctx[None, :], jnp.zeros((n_cond - B - 1, D), F32)], axis=0)

    def mods(i):
        m = _adaln(cond, w_mod[i], b_mod[i])
        mx = [m[:B, j * D:(j + 1) * D].reshape(B, 1, D) for j in range(6)]
        mc = [m[B:B + 1, j * D:(j + 1) * D].reshape(1, 1, D) for j in range(6)]
        return mx, mc

    (sh1, sc1, g1, sh2, sc2, g2), (csh1, csc1, cg1, csh2, csc2, cg2) = mods(0)
    gmat = _group_dft_matrix()
    w_in0 = bf(fh_w_in[0])
    w_out0 = bf(fh_w_out[0])
    ffn0 = (bf(ffn_w_up[0]), ffn_conv_w[0], ffn_conv_b[0], bf(ffn_w_down[0]))
    filt = (hy_filt_w1[0], hy_filt_b1[0], hy_filt_w2[0], hy_filt_b2[0], hy_filt_w3[0], hy_filt_b3[0],
            hy_filt_w4[0], hy_freq[0])

    def mixer_layer(h, n_seq, m1, m2):
        s1, c1, gt1 = m1
        s2, c2, gt2 = m2
        taps, norms = _hyena_filter(n_seq, *filt)
        y_f, y_h = _fourier_hyena_mixer(h, norm1[0], s1, c1, w_in0, hy_conv_w[0], hy_conv_b[0], gmat, taps,
                                        norms, hy_bias[0])
        return _mix_ffn(h, [y_f, y_h], w_out0, gt1, norm2[0], s2, c2, gt2, *ffn0)

    x = mixer_layer(x, L, (sh1, sc1, g1), (sh2, sc2, g2))
    ctx = mixer_layer(ctx, Lc, (csh1, csc1, cg1), (csh2, csc2, cg2))

    (sh1, sc1, g1, sh2, sc2, g2), (csh1, csc1, _, _, _, _) = mods(1)
    w_in1 = bf(jnp.pad(mla_w_in[0], ((0, 0), (0, MLA_IN_PAD - mla_w_in.shape[2]))))
    wuq = bf(_pad_heads(mla_w_uq[0], QK_DIM, QK_DIM).T)
    wuk = bf(_pad_heads(mla_w_ukv[0], NOPE + V_DIM, NOPE).T)
    wuv = bf(mla_w_ukv[0].reshape(KV_LORA, MLA_HEADS, NOPE + V_DIM)[:, :, NOPE:].reshape(KV_LORA, MLA_HEADS * V_DIM).T)
    half_scale = math.sqrt(QK_DIM ** -0.5 * math.log2(math.e))
    wts = (w_in1, mla_q_a_norm[0].reshape(1, Q_LORA), wuq, _pad_lanes(mla_q_norm[0] * half_scale, HEAD_PAD),
           mla_kv_a_norm[0].reshape(1, KV_LORA), wuk, wuv, _pad_lanes(mla_k_norm[0] * half_scale, HEAD_PAD))
    qt, kx, vtx = _mla_proj(x, norm1[1], sh1, sc1, wts, _rope_tables(L), True)
    kc, vtc = _mla_proj(ctx, norm1[1], csh1, csc1, wts, _identity_rope_tables(Lc), False)
    o = _attention(qt, kx, kc, vtx, vtc)
    ffn1 = (bf(ffn_w_up[1]), ffn_conv_w[1], ffn_conv_b[1], bf(ffn_w_down[1]))
    return _mix_ffn(x, [o], bf(mla_w_o[0]), g1, norm2[1], sh2, sc2, g2, *ffn1)
```

```python
import functools
import math

import jax
import jax.numpy as jnp
import numpy as np
from jax import lax
from jax.experimental import pallas as pl
from jax.experimental.pallas import tpu as pltpu

F32 = jnp.float32
BF16 = jnp.bfloat16
HIGHEST = lax.Precision.HIGHEST

EPS = 1e-6
D_MODEL = 1024
D_FF = 2816
D_FOURIER = 512
FOURIER_GROUP_DIM = 128
D_HYENA = 512
HYENA_EMB_DIM = 33
HYENA_EMB_PAD = 64
HYENA_BANDS = 16
HYENA_WIDTH = 64
HYENA_MIN_DECAY = math.log(1e-2) / 0.3
HYENA_MAX_DECAY = math.log(1e-2) / 1.5
MLA_HEADS = 16
Q_LORA = 256
KV_LORA = 128
NOPE = 64
ROPE = 32
QK_DIM = NOPE + ROPE
V_DIM = 64
HEAD_PAD = 128
MLA_IN_PAD = 512
GRID_W = 64
ROPE_THETA = 10000.0

ATTN_DTYPE = jnp.float8_e4m3fn
ATTN_P_SHIFT = 8.0
FFN_CHUNK = 256
FFT_RADIX = 128
FFT_KRON = 16
HALO = 16
VMEM_LIMIT = 56 * 1024 * 1024


def _cparams(sem, flags=None):
    return pltpu.CompilerParams(dimension_semantics=sem, vmem_limit_bytes=VMEM_LIMIT, flags=flags)


def _dot(a, b):
    return jnp.dot(a, b, preferred_element_type=F32)


def _norm_mod(x, g, shift, scale):
    ms = jnp.mean(x * x, axis=-1, keepdims=True)
    return (x * lax.rsqrt(ms + EPS) * g) * (1.0 + scale) + shift


def _silu(x):
    return x * (1.0 / (1.0 + jnp.exp(-x)))


def _row_tile(L, want):
    t = min(L, want)
    assert L % t == 0 and t % HALO == 0
    return t


def _mod_kernel(c_ref, w_ref, b_ref, o_ref):
    s = _silu(c_ref[...])
    o_ref[...] = jnp.dot(s, w_ref[0], precision=HIGHEST, preferred_element_type=F32) + b_ref[0]


def _adaln(cond, w_mod, b_mod, layer):
    R, D = cond.shape
    n = w_mod.shape[2]
    tn = 768
    return pl.pallas_call(
        _mod_kernel,
        out_shape=jax.ShapeDtypeStruct((R, n), F32),
        grid=(n // tn,),
        in_specs=[pl.BlockSpec((R, D), lambda j: (0, 0)),
                  pl.BlockSpec((1, D, tn), lambda j: (layer, 0, j)),
                  pl.BlockSpec((1, 1, tn), lambda j: (layer, 0, j))],
        out_specs=pl.BlockSpec((R, tn), lambda j: (0, j)),
        compiler_params=_cparams(("arbitrary",)),
        name="adaln_mod",
    )(cond, w_mod, b_mod.reshape(b_mod.shape[0], 1, n))


def _halo_specs(tm, D, L):
    nb = L // HALO
    per = tm // HALO
    cur = pl.BlockSpec((1, tm, D), lambda b, i, *_: (b, i, 0))
    prev = pl.BlockSpec((1, HALO, D), lambda b, i, *_: (b, jnp.maximum(i * per - 1, 0), 0))
    nxt = pl.BlockSpec((1, HALO, D), lambda b, i, *_: (b, jnp.minimum((i + 1) * per, nb - 1), 0))
    return cur, prev, nxt


def _mod_spec(arr):
    D = arr.shape[-1]
    if arr.shape[0] == 1:
        return pl.BlockSpec((1, 1, D), lambda b, *_: (0, 0, 0))
    return pl.BlockSpec((1, 1, D), lambda b, *_: (b, 0, 0))


def _fill_hn_rows(hn_ref, xc, xp, xn, g, shift, scale, tm, i, nt):
    hp = _norm_mod(xp, g, shift, scale)
    hn_ref[0:HALO, :] = jnp.where(i > 0, hp, 0.0).astype(BF16)
    hn_ref[HALO:HALO + tm, :] = _norm_mod(xc, g, shift, scale).astype(BF16)
    hx = _norm_mod(xn, g, shift, scale)
    hn_ref[HALO + tm:, :] = jnp.where(i < nt - 1, hx, 0.0).astype(BF16)


def _fill_hn(hn_ref, xc_ref, xp_ref, xn_ref, g, shift, scale, tm, i, nt):
    _fill_hn_rows(hn_ref, xc_ref[0], xp_ref[0], xn_ref[0], g, shift, scale, tm, i, nt)


def _dwconv3_rows(p, w, tm):
    n = tm + 2 * HALO
    up = pltpu.roll(p, 1, 0)
    dn = pltpu.roll(p, n - 1, 0)
    c = up * w[0:1] + p * w[1:2] + dn * w[2:3]
    return c[HALO:HALO + tm]


def _fh_in_kernel(xc_ref, xp_ref, xn_ref, g_ref, sh_ref, sc_ref, w_ref, cw_ref, cb_ref, gm_ref, *rest, tm, n_hi):
    if n_hi:
        perm_ref, z_ref, u_ref, x0_ref, hn_ref = rest
    else:
        z_ref, u_ref, x0_ref, hn_ref = rest
    i = pl.program_id(1)
    nt = pl.num_programs(1)
    _fill_hn(hn_ref, xc_ref, xp_ref, xn_ref, g_ref[...], sh_ref[0], sc_ref[0], tm, i, nt)
    proj = _dot(hn_ref[...], w_ref[...])
    uf = proj[HALO:HALO + tm, :D_FOURIER].astype(BF16)
    if n_hi:
        uf = _dot(perm_ref[...], uf).astype(BF16)
        z = _dot(uf, gm_ref[...]).astype(BF16)
        z_ref[0] = z.reshape(n_hi, tm // n_hi, 2 * D_FOURIER)
    else:
        z_ref[0] = _dot(uf, gm_ref[...]).astype(BF16)
    c = _dwconv3_rows(proj[:, D_FOURIER:], cw_ref[...], tm) + cb_ref[...]
    x0 = c[:, :D_HYENA]
    x1 = c[:, D_HYENA:2 * D_HYENA]
    v = c[:, 2 * D_HYENA:]
    x0_ref[0] = x0
    u_ref[0] = (v * x1).astype(BF16)


def _fh_in(x, g, shift, scale, w_in, conv_w, conv_b, gmat, n_hi=0):
    B, L, D = x.shape
    tm = _row_tile(L, 512)
    cur, prev, nxt = _halo_specs(tm, D, L)
    n_in = w_in.shape[1]
    nh = 3 * D_HYENA
    full = lambda shape: pl.BlockSpec(shape, lambda b, i: (0,) * len(shape))
    out_tile = lambda c: pl.BlockSpec((1, tm, c), lambda b, i: (b, i, 0))
    in_specs = [cur, prev, nxt, full((1, D)), _mod_spec(shift), _mod_spec(scale),
                full((D, n_in)), full((3, nh)), full((1, nh)), full((D_FOURIER, 2 * D_FOURIER))]
    args = [x, x, x, g.reshape(1, D), shift, scale, w_in, conv_w, conv_b.reshape(1, nh), gmat]
    if n_hi:
        per = tm // n_hi
        r = np.arange(tm)
        src = n_hi * (r % per) + r // per
        args.append(_const(src[:, None] == r[None, :], BF16))
        in_specs.append(full((tm, tm)))
        z_shape = jax.ShapeDtypeStruct((B, n_hi, L // n_hi, 2 * D_FOURIER), BF16)
        z_spec = pl.BlockSpec((1, n_hi, per, 2 * D_FOURIER), lambda b, i: (b, 0, i, 0))
    else:
        z_shape = jax.ShapeDtypeStruct((B, L, 2 * D_FOURIER), BF16)
        z_spec = out_tile(2 * D_FOURIER)
    return pl.pallas_call(
        functools.partial(_fh_in_kernel, tm=tm, n_hi=n_hi),
        out_shape=(z_shape, jax.ShapeDtypeStruct((B, L, D_HYENA), BF16), jax.ShapeDtypeStruct((B, L, D_HYENA), F32)),
        grid=(B, L // tm),
        in_specs=in_specs,
        out_specs=(z_spec, out_tile(D_HYENA), out_tile(D_HYENA)),
        scratch_shapes=[pltpu.VMEM((tm + 2 * HALO, D), BF16)],
        compiler_params=_cparams(("parallel", "arbitrary")),
        name="fh_in",
    )(*args)


def _fourier_kernel(c_ref, s_ref, z_ref, o_ref, *, scale):
    z = z_ref[0]
    y = _dot(c_ref[...], z[:, :D_FOURIER]) + _dot(s_ref[...], z[:, D_FOURIER:])
    o_ref[0] = (y * scale).astype(BF16)


def _fourier_seq(z, cl, sl):
    B, L, _ = z.shape
    tf = _row_tile(L, 512)
    scale = 1.0 / math.sqrt(L * FOURIER_GROUP_DIM)
    return pl.pallas_call(
        functools.partial(_fourier_kernel, scale=scale),
        out_shape=jax.ShapeDtypeStruct((B, L, D_FOURIER), BF16),
        grid=(B, L // tf),
        in_specs=[pl.BlockSpec((tf, L), lambda b, i: (i, 0)),
                  pl.BlockSpec((tf, L), lambda b, i: (i, 0)),
                  pl.BlockSpec((1, L, 2 * D_FOURIER), lambda b, i: (b, 0, 0))],
        out_specs=pl.BlockSpec((1, tf, D_FOURIER), lambda b, i: (b, i, 0)),
        compiler_params=_cparams(("parallel", "arbitrary")),
        name="fourier_seq",
    )(cl, sl, z)


def _fft_f1_kernel(cs_ref, z_ref, o_ref, *, nc):
    for j in range(nc):
        cs = _dot(cs_ref[...], z_ref[0, j])
        cz = cs[:FFT_RADIX]
        sz = cs[FFT_RADIX:]
        o_ref[0, 0, j] = (cz[:, :D_FOURIER] + sz[:, D_FOURIER:]).astype(BF16)
        o_ref[0, 1, j] = (cz[:, D_FOURIER:] - sz[:, :D_FOURIER]).astype(BF16)


def _fft_f2_kernel(t_ref, b_ref, o_ref, *, scale):
    n_hi = b_ref.shape[2]
    blk = b_ref[0].reshape(2 * n_hi * 8, D_FOURIER)
    y = _dot(t_ref[0], blk) * scale
    o_ref[0, :, 0] = y.reshape(n_hi, 8, D_FOURIER).astype(BF16)


def _fourier_seq_fft(zp, cs, tmat):
    B, n_hi, _, _ = zp.shape
    L = n_hi * FFT_RADIX
    nc = 8
    ng = FFT_RADIX // 8
    bc = pl.pallas_call(
        functools.partial(_fft_f1_kernel, nc=nc),
        out_shape=jax.ShapeDtypeStruct((B, 2, n_hi, FFT_RADIX, D_FOURIER), BF16),
        grid=(B, n_hi // nc),
        in_specs=[pl.BlockSpec((2 * FFT_RADIX, FFT_RADIX), lambda b, i: (0, 0)),
                  pl.BlockSpec((1, nc, FFT_RADIX, 2 * D_FOURIER), lambda b, i: (b, i, 0, 0))],
        out_specs=pl.BlockSpec((1, 2, nc, FFT_RADIX, D_FOURIER), lambda b, i: (b, 0, i, 0, 0)),
        compiler_params=_cparams(("parallel", "arbitrary")),
        name="fourier_fft1",
    )(cs, zp)
    scale = 1.0 / math.sqrt(L * FOURIER_GROUP_DIM)
    y = pl.pallas_call(
        functools.partial(_fft_f2_kernel, scale=scale),
        out_shape=jax.ShapeDtypeStruct((B, n_hi, ng, 8, D_FOURIER), BF16),
        grid=(B, ng),
        in_specs=[pl.BlockSpec((1, n_hi * 8, 2 * n_hi * 8), lambda b, g: (g, 0, 0)),
                  pl.BlockSpec((1, 2, n_hi, 8, D_FOURIER), lambda b, g: (b, 0, 0, g, 0))],
        out_specs=pl.BlockSpec((1, n_hi, 1, 8, D_FOURIER), lambda b, g: (b, 0, g, 0, 0)),
        compiler_params=_cparams(("parallel", "arbitrary")),
        name="fourier_fft2",
    )(tmat, bc)
    return y.reshape(B, L, D_FOURIER)


def _filter_kernel(z_ref, w1_ref, b1_ref, w2_ref, b2_ref, w3_ref, b3_ref, w4_ref, f_ref, dl_ref,
                   k_ref, n_ref):
    i = pl.program_id(1)
    z = z_ref[0]
    f = f_ref[...]
    hd = lambda a, w: jnp.dot(a, w, precision=HIGHEST, preferred_element_type=F32)
    h = jnp.sin(f * (hd(z, w1_ref[...]) + b1_ref[...]))
    h = jnp.sin(f * (hd(h, w2_ref[...]) + b2_ref[...]))
    h = jnp.sin(f * (hd(h, w3_ref[...]) + b3_ref[...]))
    h = hd(h, w4_ref[0])
    t = z[:, 0:1]
    valid = z[:, HYENA_EMB_DIM:HYENA_EMB_DIM + 1]
    k = h * jnp.exp(-t * jnp.abs(dl_ref[...])) * valid
    k_ref[0] = k

    @pl.when(i == 0)
    def _():
        n_ref[...] = jnp.zeros_like(n_ref)

    n_ref[0] += jnp.sum(jnp.abs(k), axis=0, keepdims=True)


def _hyena_filter(L, w1, b1, w2, b2, w3, b3, w4, freq):
    pos = np.arange(L, dtype=np.float64)
    bands = np.linspace(1e-4, HYENA_BANDS - 1, HYENA_BANDS)

    def emb(p, valid):
        t = p / max(L - 1, 1)
        ang = (2.0 * math.pi / L) * p[:, None] * bands[None, :]
        pad = np.zeros((L, HYENA_EMB_PAD - HYENA_EMB_DIM - 1))
        return np.concatenate([t[:, None], np.cos(ang), -np.sin(ang), valid[:, None], pad], axis=-1)

    zf = emb(pos, np.ones((L,)))
    zb = emb(np.where(pos > 0, L - pos, 0.0), (pos > 0).astype(np.float64))
    z = _const(np.stack([zf, zb]))
    w1p = jnp.concatenate([w1, jnp.zeros((HYENA_EMB_PAD - HYENA_EMB_DIM, HYENA_WIDTH), F32)], axis=0)
    w4s = jnp.stack([w4[:, :D_HYENA], w4[:, D_HYENA:]])
    deltas = _const(np.linspace(HYENA_MIN_DECAY, HYENA_MAX_DECAY, D_HYENA).reshape(1, D_HYENA))
    tr = _row_tile(L, 512)
    W = HYENA_WIDTH
    full = lambda shape: pl.BlockSpec(shape, lambda s, i: (0,) * len(shape))
    return pl.pallas_call(
        _filter_kernel,
        out_shape=(jax.ShapeDtypeStruct((2, L, D_HYENA), F32), jax.ShapeDtypeStruct((2, 1, D_HYENA), F32)),
        grid=(2, L // tr),
        in_specs=[pl.BlockSpec((1, tr, HYENA_EMB_PAD), lambda s, i: (s, i, 0)),
                  full((HYENA_EMB_PAD, W)), full((1, W)), full((W, W)), full((1, W)),
                  full((W, W)), full((1, W)),
                  pl.BlockSpec((1, W, D_HYENA), lambda s, i: (s, 0, 0)),
                  full((1, W)), full((1, D_HYENA))],
        out_specs=(pl.BlockSpec((1, tr, D_HYENA), lambda s, i: (s, i, 0)),
                   pl.BlockSpec((1, 1, D_HYENA), lambda s, i: (s, 0, 0))),
        compiler_params=_cparams(("arbitrary", "arbitrary")),
        name="hyena_filter",
    )(z, w1p, b1.reshape(1, W), w2, b2.reshape(1, W), w3, b3.reshape(1, W), w4s, freq.reshape(1, W), deltas)


def _hy_fwd_kernel(c_ref, s_ref, u_ref, *rest, raw, tf):
    u = u_ref[0]
    ure = _dot(c_ref[...], u)
    uim = -_dot(s_ref[...], u)
    if raw:
        re_ref, im_ref = rest
        re_ref[0] = ure
        im_ref[0] = uim
        return
    kre_ref, kim_ref, re_ref, im_ref = rest
    kre = kre_ref[...]
    kim = kim_ref[...]
    row = lax.broadcasted_iota(jnp.int32, (tf, 1), 0) + pl.program_id(1) * tf
    packed = row == 0
    a = uim * kim
    re_ref[0] = (ure * kre - jnp.where(packed, 0.0, a)).astype(BF16)
    im_ref[0] = jnp.where(packed, a, ure * kim + uim * kre).astype(BF16)


def _hy_fwd(u, cb, sbf, kre=None, kim=None):
    B, L, C = u.shape
    tf = _row_tile(L, 512)
    raw = kre is None
    mat = pl.BlockSpec((tf, L), lambda b, i: (i, 0))
    tile = pl.BlockSpec((1, tf, C), lambda b, i: (b, i, 0))
    in_specs = [mat, mat, pl.BlockSpec((1, L, C), lambda b, i: (b, 0, 0))]
    args = [cb, sbf, u]
    if not raw:
        in_specs += [pl.BlockSpec((tf, C), lambda b, i: (i, 0))] * 2
        args += [kre, kim]
    dt = F32 if raw else BF16
    return pl.pallas_call(
        functools.partial(_hy_fwd_kernel, raw=raw, tf=tf),
        out_shape=(jax.ShapeDtypeStruct((B, L, C), dt), jax.ShapeDtypeStruct((B, L, C), dt)),
        grid=(B, L // tf),
        in_specs=in_specs,
        out_specs=(tile, tile),
        compiler_params=_cparams(("parallel", "arbitrary")),
        name="hyena_fwd_raw" if raw else "hyena_fwd",
    )(*args)


def _spec_combine_kernel(re_ref, im_ref, n_ref, kre_ref, kim_ref, *, tf, n_fft):
    row = lax.broadcasted_iota(jnp.int32, (tf, 1), 0) + pl.program_id(0) * tf
    sgn = (1 - 2 * (row & 1)).astype(F32)
    wgt = jnp.where(row == 0, 1.0 / n_fft, 2.0 / n_fft)
    s = wgt / (n_ref[0] + n_ref[1])
    kre_ref[...] = (re_ref[0] + sgn * re_ref[1]) * s
    kim_ref[...] = (im_ref[0] + sgn * im_ref[1]) * s


def _spec_combine(re, im, norms):
    _, L, C = re.shape
    tf = _row_tile(L, 512)
    pair = pl.BlockSpec((2, tf, C), lambda i: (0, i, 0))
    tile = pl.BlockSpec((tf, C), lambda i: (i, 0))
    return pl.pallas_call(
        functools.partial(_spec_combine_kernel, tf=tf, n_fft=2 * L),
        out_shape=(jax.ShapeDtypeStruct((L, C), F32), jax.ShapeDtypeStruct((L, C), F32)),
        grid=(L // tf,),
        in_specs=[pair, pair, pl.BlockSpec((2, 1, C), lambda i: (0, 0, 0))],
        out_specs=(tile, tile),
        compiler_params=_cparams(("arbitrary",)),
        name="hyena_spec_combine",
    )(re, im, norms)


def _hy_inv_kernel(c_ref, s_ref, re_ref, im_ref, x0_ref, u_ref, b_ref, o_ref):
    y = _dot(c_ref[...], re_ref[0]) - _dot(s_ref[...], im_ref[0])
    o_ref[0] = (x0_ref[0] * (y + u_ref[0].astype(F32) * b_ref[...])).astype(BF16)


def _hy_inv(yre, yim, x0, u, bias, cb, sbi):
    B, L, C = u.shape
    tt = _row_tile(L, 512)
    mat = pl.BlockSpec((tt, L), lambda b, i: (i, 0))
    whole = pl.BlockSpec((1, L, C), lambda b, i: (b, 0, 0))
    tile = pl.BlockSpec((1, tt, C), lambda b, i: (b, i, 0))
    return pl.pallas_call(
        _hy_inv_kernel,
        out_shape=jax.ShapeDtypeStruct((B, L, C), BF16),
        grid=(B, L // tt),
        in_specs=[mat, mat, whole, whole, tile, tile, pl.BlockSpec((1, C), lambda b, i: (0, 0))],
        out_specs=tile,
        compiler_params=_cparams(("parallel", "arbitrary")),
        name="hyena_inv",
    )(cb, sbi, yre, yim, x0, u, bias.reshape(1, C))


def _fft_a_kernel(t_ref, x_ref, o_ref):
    r2, kr, c = x_ref.shape[1:]
    y = _dot(t_ref[...], x_ref[0].reshape(r2 * kr, c))
    o_ref[0] = y.reshape(2, y.shape[0] // (2 * kr), kr, c).astype(BF16)


def _hy_fft_a(x4, t1):
    Bx, R, _, C = x4.shape
    n_hi = t1.shape[0] // (2 * FFT_KRON)
    return pl.pallas_call(
        _fft_a_kernel,
        out_shape=jax.ShapeDtypeStruct((Bx, 2, n_hi, FFT_RADIX, C), BF16),
        grid=(Bx, FFT_RADIX // FFT_KRON),
        in_specs=[pl.BlockSpec(t1.shape, lambda b, i: (0, 0)),
                  pl.BlockSpec((1, R, FFT_KRON, C), lambda b, i: (b, 0, i, 0))],
        out_specs=pl.BlockSpec((1, 2, n_hi, FFT_KRON, C), lambda b, i: (b, 0, 0, i, 0)),
        compiler_params=_cparams(("parallel", "arbitrary")),
        name="hyena_fft_a",
    )(t1, x4)


def _fft_b_kernel(gf_ref, a_ref, *rest, raw, n_fft):
    nb = a_ref.shape[0]
    gf = gf_ref[0]
    if raw:
        n_ref, o_ref = rest
        s = 1.0 / (n_fft * (n_ref[0] + n_ref[1]))
    else:
        gi_ref, k_ref, o_ref = rest
        gi = gi_ref[0]
        kr = k_ref[0, 0, 0]
        ki = k_ref[0, 1, 0]
    for b in range(nb):
        a = a_ref[b, :, 0].reshape(2 * FFT_RADIX, D_HYENA)
        x = _dot(gf, a)
        if raw:
            o_ref[b, :, 0] = (x * s).reshape(2, FFT_RADIX, D_HYENA)
            continue
        xr = x[:FFT_RADIX]
        xi = x[FFT_RADIX:]
        y = jnp.concatenate([xr * kr - xi * ki, xr * ki + xi * kr], axis=0).astype(BF16)
        o_ref[b, :, 0] = _dot(gi, y).reshape(2, FFT_RADIX, D_HYENA).astype(BF16)


def _hy_fft_b(a5, gf, gi=None, kspec=None, norms=None):
    Bx, _, n_hi, _, C = a5.shape
    raw = kspec is None
    blk = pl.BlockSpec((Bx, 2, 1, FFT_RADIX, C), lambda k: (0, 0, k, 0, 0))
    tab = pl.BlockSpec((1, 2 * FFT_RADIX, 2 * FFT_RADIX), lambda k: (k, 0, 0))
    if raw:
        in_specs = [tab, blk, pl.BlockSpec((2, 1, C), lambda k: (0, 0, 0))]
        args = (gf, a5, norms)
    else:
        kblk = pl.BlockSpec((1, 2, 1, FFT_RADIX, C), lambda k: (0, 0, k, 0, 0))
        in_specs = [tab, blk, tab, kblk]
        args = (gf, a5, gi, kspec)
    return pl.pallas_call(
        functools.partial(_fft_b_kernel, raw=raw, n_fft=n_hi * FFT_RADIX),
        out_shape=jax.ShapeDtypeStruct(a5.shape, F32 if raw else BF16),
        grid=(n_hi,),
        in_specs=in_specs,
        out_specs=blk,
        compiler_params=_cparams(("arbitrary",)),
        name="hyena_fft_spec" if raw else "hyena_fft_b",
    )(*args)


def _fft_c_kernel(t_ref, v_ref, x0_ref, u_ref, b_ref, o_ref):
    _, n_hi, kr, c = v_ref.shape[1:]
    y = _dot(t_ref[...], v_ref[0].reshape(2 * n_hi * kr, c))
    y = y.reshape(y.shape[0] // kr, kr, c)
    o_ref[0] = (x0_ref[0] * (y + u_ref[0].astype(F32) * b_ref[...])).astype(BF16)


def _hy_fft_c(v5, t3, x0, u, bias):
    B, _, n_hi, _, C = v5.shape
    R = t3.shape[0] // FFT_KRON
    tile = pl.BlockSpec((1, R, FFT_KRON, C), lambda b, i: (b, 0, i, 0))
    return pl.pallas_call(
        _fft_c_kernel,
        out_shape=jax.ShapeDtypeStruct((B, R, FFT_RADIX, C), BF16),
        grid=(B, FFT_RADIX // FFT_KRON),
        in_specs=[pl.BlockSpec(t3.shape, lambda b, i: (0, 0)),
                  pl.BlockSpec((1, 2, n_hi, FFT_KRON, C), lambda b, i: (b, 0, 0, i, 0)),
                  tile, tile, pl.BlockSpec((1, C), lambda b, i: (0, 0))],
        out_specs=tile,
        compiler_params=_cparams(("parallel", "arbitrary")),
        name="hyena_fft_c",
    )(t3, v5, x0, u, bias.reshape(1, C))


def _ffn_kernel(xc_ref, xp_ref, xn_ref, g1_ref, wo_ref, g_ref, sh_ref, sc_ref, gate_ref, wu_ref, cw_ref, cb_ref,
                wd_ref, *rest, tm, n_y):
    y_refs = rest[:3 * n_y]
    o_ref, yext_ref, hn_ref, act_ref = rest[3 * n_y:]
    i = pl.program_id(1)
    nt = pl.num_programs(1)
    off = 0
    for k in range(n_y):
        yc_ref, yp_ref, yn_ref = y_refs[3 * k:3 * k + 3]
        c = yc_ref.shape[-1]
        yext_ref[0:HALO, off:off + c] = yp_ref[0].astype(BF16)
        yext_ref[HALO:HALO + tm, off:off + c] = yc_ref[0].astype(BF16)
        yext_ref[HALO + tm:, off:off + c] = yn_ref[0].astype(BF16)
        off += c
    mix = g1_ref[0] * _dot(yext_ref[...], wo_ref[...])
    x1 = xc_ref[0] + mix[HALO:HALO + tm]
    _fill_hn_rows(hn_ref, x1, xp_ref[0] + mix[:HALO], xn_ref[0] + mix[HALO + tm:], g_ref[...], sh_ref[0], sc_ref[0],
                  tm, i, nt)
    for c in range(0, D_FF, FFN_CHUNK):
        cols = slice(c, c + FFN_CHUNK)
        gx = _dot(hn_ref[...], wu_ref[:, cols])
        vx = _dot(hn_ref[HALO:HALO + tm, :], wu_ref[:, D_FF + c:D_FF + c + FFN_CHUNK])
        cv = _dwconv3_rows(gx, cw_ref[:, cols], tm) + cb_ref[:, cols]
        act_ref[:, cols] = (_silu(cv) * vx).astype(BF16)
    o_ref[0] = x1 + gate_ref[0] * _dot(act_ref[...], wd_ref[...])


def _mix_ffn(x, ys, w_o, gate1, g, shift, scale, gate2, w_up, conv_w, conv_b, w_down):
    B, L, D = x.shape
    tm = _row_tile(L, 1024)
    cur, prev, nxt = _halo_specs(tm, D, L)
    const = lambda shape: pl.BlockSpec(shape, lambda b, i: (0,) * len(shape), pipeline_mode=pl.Buffered(1))
    y_specs, y_args = [], []
    for y in ys:
        y_specs += list(_halo_specs(tm, y.shape[-1], L))
        y_args += [y, y, y]
    return pl.pallas_call(
        functools.partial(_ffn_kernel, tm=tm, n_y=len(ys)),
        out_shape=jax.ShapeDtypeStruct((B, L, D), F32),
        grid=(B, L // tm),
        in_specs=[cur, prev, nxt, _mod_spec(gate1), const(w_o.shape), const((1, D)), _mod_spec(shift),
                  _mod_spec(scale), _mod_spec(gate2), const((D, 2 * D_FF)), const((3, D_FF)), const((1, D_FF)),
                  const((D_FF, D))] + y_specs,
        out_specs=pl.BlockSpec((1, tm, D), lambda b, i: (b, i, 0)),
        scratch_shapes=[pltpu.VMEM((tm + 2 * HALO, w_o.shape[0]), BF16), pltpu.VMEM((tm + 2 * HALO, D), BF16),
                        pltpu.VMEM((tm, D_FF), BF16)],
        compiler_params=_cparams(("parallel", "arbitrary")),
        name="mix_ffn",
    )(x, x, x, gate1, w_o, g.reshape(1, D), shift, scale, gate2, w_up, conv_w, conv_b.reshape(1, D_FF), w_down,
      *y_args)


def _rms(v, g, n):
    return v * lax.rsqrt(jnp.sum(v * v, axis=-1, keepdims=True) * (1.0 / n) + EPS) * g


def _rope_gain_tables(gain, cos, sa, sb):
    rot = slice(NOPE, QK_DIM)
    gr = gain[rot]
    return gain[:NOPE], gr * cos[rot], pltpu.roll(gr, 8, 0) * sa[rot], pltpu.roll(gr, ROPE - 8, 0) * sb[rot]


def _rope_t(yr, gc, gsa, gsb):
    return yr * gc + pltpu.roll(yr, 8, 0) * gsa + pltpu.roll(yr, ROPE - 8, 0) * gsb


def _sumsq(x):
    return jnp.sum(x * x, axis=0, keepdims=True)


def _mla_proj_kernel(x_ref, g_ref, sh_ref, sc_ref, win_ref, qan_ref, wuq_ref, qn_ref, kvn_ref, wuk_ref,
                     wuv_ref, kn_ref, cos_ref, sa_ref, sb_ref, *outs, with_q):
    hn = _norm_mod(x_ref[0], g_ref[...], sh_ref[0], sc_ref[0]).astype(BF16)
    a = _dot(hn, win_ref[...])
    tm = a.shape[0]
    cos = cos_ref[...]
    sa = sa_ref[...]
    sb = sb_ref[...]
    pad = jnp.zeros((HEAD_PAD - QK_DIM, tm), ATTN_DTYPE)
    if with_q:
        qt_ref, k_ref, vt_ref = outs
        qat = _rms(a[:, :Q_LORA], qan_ref[...], Q_LORA).T.astype(BF16)
        qft = _dot(wuq_ref[...], qat)
        gn, gc, gsa, gsb = _rope_gain_tables(qn_ref[...], cos, sa, sb)
        for h in range(MLA_HEADS):
            base = h * HEAD_PAD
            xn = qft[base:base + NOPE]
            xr = qft[base + NOPE:base + QK_DIM]
            r = lax.rsqrt((_sumsq(xn) + _sumsq(xr)) * (1.0 / QK_DIM) + EPS)
            qt_ref[0, base:base + NOPE, :] = (xn * r * gn).astype(ATTN_DTYPE)
            qt_ref[0, base + NOPE:base + QK_DIM, :] = _rope_t(xr * r, gc, gsa, gsb).astype(ATTN_DTYPE)
            qt_ref[0, base + QK_DIM:base + HEAD_PAD, :] = pad
    else:
        k_ref, vt_ref = outs
    ckvt = _rms(a[:, Q_LORA:Q_LORA + KV_LORA], kvn_ref[...], KV_LORA).T.astype(BF16)
    kft = _dot(wuk_ref[...], ckvt)
    vt_ref[0] = _dot(wuv_ref[...], ckvt).astype(ATTN_DTYPE)
    kpe = a[:, Q_LORA + KV_LORA:].T[:ROPE]
    gn, gc, gsa, gsb = _rope_gain_tables(kn_ref[...], cos, sa, sb)
    kpe_rot = _rope_t(kpe, gc, gsa, gsb)
    kpe_ss = _sumsq(kpe)
    for h in range(MLA_HEADS):
        base = h * HEAD_PAD
        xn = kft[base:base + NOPE]
        r = lax.rsqrt((_sumsq(xn) + kpe_ss) * (1.0 / QK_DIM) + EPS)
        kt = jnp.concatenate([xn * r * gn, kpe_rot * r, jnp.zeros((HEAD_PAD - QK_DIM, tm), F32)], axis=0)
        k_ref[0, :, base:base + HEAD_PAD] = kt.T.astype(ATTN_DTYPE)


def _mla_proj(x, g, shift, scale, wts, rope_tabs, with_q):
    B, L, D = x.shape
    tm = _row_tile(L, 512)
    win, qan, wuq, qn, kvn, wuk, wuv, kn = wts
    cos, sa, sb = rope_tabs
    HP = MLA_HEADS * HEAD_PAD
    HV = MLA_HEADS * V_DIM
    full = lambda a: pl.BlockSpec(a.shape, lambda b, i: (0,) * a.ndim)
    tab = pl.BlockSpec((HEAD_PAD, tm), lambda b, i: (0, i))
    out_shape = [jax.ShapeDtypeStruct((B, L, HP), ATTN_DTYPE), jax.ShapeDtypeStruct((B, HV, L), ATTN_DTYPE)]
    out_specs = [pl.BlockSpec((1, tm, HP), lambda b, i: (b, i, 0)),
                 pl.BlockSpec((1, HV, tm), lambda b, i: (b, 0, i))]
    if with_q:
        out_shape = [jax.ShapeDtypeStruct((B, HP, L), ATTN_DTYPE)] + out_shape
        out_specs = [pl.BlockSpec((1, HP, tm), lambda b, i: (b, 0, i))] + out_specs
    g2 = g.reshape(1, D)
    gain_tab = lambda v: jnp.broadcast_to(v.reshape(HEAD_PAD, 1), (HEAD_PAD, tm))
    qn = gain_tab(qn)
    kn = gain_tab(kn)
    return pl.pallas_call(
        functools.partial(_mla_proj_kernel, with_q=with_q),
        out_shape=tuple(out_shape),
        grid=(B, L // tm),
        in_specs=[pl.BlockSpec((1, tm, D), lambda b, i: (b, i, 0)), full(g2), _mod_spec(shift), _mod_spec(scale),
                  full(win), full(qan), full(wuq), full(qn), full(kvn), full(wuk), full(wuv), full(kn),
                  tab, tab, tab],
        out_specs=tuple(out_specs),
        compiler_params=_cparams(("parallel", "arbitrary")),
        name="mla_proj_q" if with_q else "mla_proj_kv",
    )(x, g2, shift, scale, win, qan, wuq, qn, kvn, wuk, wuv, kn, cos, sa, sb)


def _attn_kernel(qt_ref, kx_ref, kc_ref, vtx_ref, vtc_ref, o_ref, ax_ref, ac_ref, am_ref, bx_ref, bc_ref, bm_ref,
                 px_ref, pc_ref):
    t = pl.program_id(0)
    io = (qt_ref, kx_ref, kc_ref, vtx_ref, vtc_ref, o_ref, px_ref, pc_ref)

    @pl.when(t == 0)
    def _():
        bx_ref[...] = jnp.zeros(bx_ref.shape, BF16)
        bc_ref[...] = jnp.zeros(bc_ref.shape, BF16)
        bm_ref[...] = jnp.zeros(bm_ref.shape, BF16)

    @pl.when(t % 2 == 0)
    def _():
        _attn_step(*io, (ax_ref, ac_ref, am_ref), (bx_ref, bc_ref, bm_ref))

    @pl.when(t % 2 == 1)
    def _():
        _attn_step(*io, (bx_ref, bc_ref, bm_ref), (ax_ref, ac_ref, am_ref))


def _attn_step(qt_ref, kx_ref, kc_ref, vtx_ref, vtc_ref, o_ref, px_ref, pc_ref, parked, ready):
    wx_ref, wc_ref, wm_ref = parked
    rx_ref, rc_ref, rm_ref = ready
    tq = qt_ref.shape[2]
    off = rm_ref[...][None]

    def probs(s_ref):
        n = s_ref.shape[0]
        s = s_ref[...].reshape(n // 16, 16, 2 * tq)
        return jnp.exp2(s - off).reshape(n, 2 * tq).astype(ATTN_DTYPE)

    px_ref[...] = probs(rx_ref)
    pc_ref[...] = probs(rc_ref)

    q0 = qt_ref[0, :HEAD_PAD, :]
    q1 = qt_ref[0, HEAD_PAD:, :]
    zq = jnp.zeros_like(q0)
    qbd = jnp.concatenate([jnp.concatenate([q0, zq], axis=1), jnp.concatenate([zq, q1], axis=1)], axis=0)
    nx = _dot(kx_ref[0], qbd).astype(BF16)
    nc = _dot(kc_ref[0], qbd).astype(BF16)
    wx_ref[...] = nx
    wc_ref[...] = nc
    m = jnp.maximum(jnp.max(nx, axis=0, keepdims=True), jnp.max(nc, axis=0, keepdims=True)).astype(F32)
    wm_ref[...] = jnp.broadcast_to((m - ATTN_P_SHIFT).astype(BF16), wm_ref.shape)

    vtx = vtx_ref[0]
    vtc = vtc_ref[0]
    halves = []
    for j in range(2):
        own = slice(j * V_DIM, (j + 1) * V_DIM)
        oth = slice((1 - j) * V_DIM, (2 - j) * V_DIM)
        cols = slice(j * tq, (j + 1) * tq)
        parts_x = [vtx[own], jnp.ones_like(vtx[oth])]
        parts_c = [vtc[own], jnp.ones_like(vtc[oth])]
        if j == 1:
            parts_x.reverse()
            parts_c.reverse()
        r = (_dot(jnp.concatenate(parts_x, axis=0), px_ref[:, cols])
             + _dot(jnp.concatenate(parts_c, axis=0), pc_ref[:, cols]))
        halves.append(r[own] * (1.0 / r[oth][0:1]))
    o_ref[0] = jnp.concatenate(halves, axis=0).T.astype(BF16)


def _attention(qt, kx, kc, vtx, vtc):
    B, HP, L = qt.shape
    Lc = kc.shape[1]
    tq = _row_tile(L, 1024)
    nhp = MLA_HEADS // 2
    nq = L // tq
    n_tiles = B * nhp * nq

    def tile(t):
        t = jnp.clip(t, 0, n_tiles - 1)
        return t // (nhp * nq), (t // nq) % nhp, t % nq

    def logits_side(f):
        return lambda t: f(*tile(t))

    def value_side(f):
        return lambda t: f(*tile(t - 1))

    return pl.pallas_call(
        _attn_kernel,
        out_shape=jax.ShapeDtypeStruct((B, L, MLA_HEADS * V_DIM), BF16),
        grid=(n_tiles + 1,),
        in_specs=[pl.BlockSpec((1, 2 * HEAD_PAD, tq), logits_side(lambda b, h, i: (b, h, i))),
                  pl.BlockSpec((1, L, 2 * HEAD_PAD), logits_side(lambda b, h, i: (b, 0, h))),
                  pl.BlockSpec((1, Lc, 2 * HEAD_PAD), logits_side(lambda b, h, i: (b, 0, h))),
                  pl.BlockSpec((1, 2 * V_DIM, L), value_side(lambda b, h, i: (b, h, 0))),
                  pl.BlockSpec((1, 2 * V_DIM, Lc), value_side(lambda b, h, i: (b, h, 0)))],
        out_specs=pl.BlockSpec((1, tq, 2 * V_DIM), value_side(lambda b, h, i: (b, i, h))),
        scratch_shapes=[pltpu.VMEM((L, 2 * tq), BF16), pltpu.VMEM((Lc, 2 * tq), BF16), pltpu.VMEM((16, 2 * tq), BF16)] * 2
                       + [pltpu.VMEM((L, 2 * tq), ATTN_DTYPE), pltpu.VMEM((Lc, 2 * tq), ATTN_DTYPE)],
        compiler_params=_cparams(("arbitrary",)),
        name="mla_attention",
    )(qt, kx, kc, vtx, vtc)


def _const(a, dtype=F32):
    return jnp.asarray(np.asarray(a, np.float32)).astype(dtype)


def _angle(num, period):
    return (num % period).astype(np.float64) * (2.0 * math.pi / period)


def _trig_matrix(n, period):
    r = np.arange(n, dtype=np.int64)
    a = _angle(r[:, None] * r[None, :], period)
    return np.cos(a), np.sin(a)


def _dft_tables(L):
    cl, sl = _trig_matrix(L, L)
    cb, sb = _trig_matrix(L, 2 * L)
    idx = np.arange(L)
    alt = (1 - 2 * (idx & 1)).astype(np.float64)
    sbf = np.where(idx[:, None] == 0, -alt[None, :], sb)
    sbi = np.where(idx[None, :] == 0, -alt[:, None], sb)
    return tuple(_const(m, BF16) for m in (cl, sl, cb, sbf, sbi))


def _fourier_fft_tables(L):
    n_hi = L // FFT_RADIX
    i = np.arange(FFT_RADIX, dtype=np.int64)
    a = _angle(i[:, None] * i[None, :], FFT_RADIX)
    cs = np.concatenate([np.cos(a), np.sin(a)], axis=0)
    ng = FFT_RADIX // 8
    g = np.arange(ng, dtype=np.int64)[:, None, None, None]
    k1 = np.arange(n_hi, dtype=np.int64)[None, :, None, None]
    j = np.arange(8, dtype=np.int64)[None, None, :, None]
    n1 = np.arange(n_hi, dtype=np.int64)[None, None, None, :]
    ang = _angle(n1 * (FFT_RADIX * k1 + 8 * g + j), L)
    eye = np.eye(8)
    blocks = [t[:, :, :, :, None] * eye[None, None, :, None, :] for t in (np.cos(ang), np.sin(ang))]
    tmat = np.stack(blocks, axis=3)
    return _const(cs, BF16), _const(tmat.reshape(ng, n_hi * 8, 2 * n_hi * 8), BF16)


def _hyena_fft_tables(L):
    n_fft = 2 * L
    n_hi = n_fft // FFT_RADIX
    k2 = np.arange(n_hi, dtype=np.int64)
    a1 = _angle(k2[:, None] * k2[None, :], n_hi)
    c1, s1 = np.cos(a1), np.sin(a1)
    eye = np.eye(FFT_KRON)
    kron = lambda m: (m[:, None, :, None] * eye[None, :, None, :]).reshape(m.shape[0] * FFT_KRON, m.shape[1] * FFT_KRON)
    t1 = _const(np.concatenate([kron(c1), kron(-s1)], axis=0), BF16)
    t1_half = t1[:, :n_hi // 2 * FFT_KRON]
    t3 = t1_half.T
    r = np.arange(FFT_RADIX, dtype=np.int64)
    k = k2[:, None, None] + n_hi * r[None, :, None]
    a2 = _angle(k * r[None, None, :], n_fft)
    c2, s2 = _const(np.cos(a2), BF16), _const(np.sin(a2), BF16)
    gf = jnp.concatenate([jnp.concatenate([c2, s2], axis=2), jnp.concatenate([-s2, c2], axis=2)], axis=1)
    c2t, s2t = jnp.swapaxes(c2, 1, 2), jnp.swapaxes(s2, 1, 2)
    gi = jnp.concatenate([jnp.concatenate([c2t, -s2t], axis=2), jnp.concatenate([s2t, c2t], axis=2)], axis=1)
    return t1, t1_half, t3, gf, gi


def _group_dft_matrix():
    g = FOURIER_GROUP_DIM
    ng = D_FOURIER // g
    c, s = _trig_matrix(g, g)
    eye = np.eye(ng)
    return _const(np.concatenate([np.kron(eye, c), -np.kron(eye, s)], axis=1), BF16)


def _rope_tables(L):
    pos = np.arange(L)
    rows = (pos // GRID_W).astype(np.float64)
    cols = (pos % GRID_W).astype(np.float64)
    nf = ROPE // 4
    inv_freq = ROPE_THETA ** (-np.arange(nf, dtype=np.float64) / nf)
    ar = rows[:, None] * inv_freq[None, :]
    ac = cols[:, None] * inv_freq[None, :]
    one = np.ones((L, NOPE))
    z_n = np.zeros((L, NOPE))
    z_f = np.zeros((L, nf))
    tail = np.zeros((L, HEAD_PAD - QK_DIM))
    cos = np.concatenate([one, np.cos(ar), np.cos(ar), np.cos(ac), np.cos(ac), tail], axis=1)
    sa = np.concatenate([z_n, z_f, np.sin(ar), z_f, np.sin(ac), tail], axis=1)
    sb = np.concatenate([z_n, -np.sin(ar), z_f, -np.sin(ac), z_f, tail], axis=1)
    return _const(cos.T), _const(sa.T), _const(sb.T)


def _identity_rope_tables(L):
    return jnp.ones((HEAD_PAD, L), F32), jnp.zeros((HEAD_PAD, L), F32), jnp.zeros((HEAD_PAD, L), F32)


def _pad_heads(w, per_head, keep):
    K = w.shape[0]
    w = w.reshape(K, MLA_HEADS, per_head)[:, :, :keep]
    w = jnp.pad(w, ((0, 0), (0, 0), (0, HEAD_PAD - keep)))
    return w.reshape(K, MLA_HEADS * HEAD_PAD)


def _pad_lanes(v, n):
    return jnp.pad(v, (0, n - v.shape[0])).reshape(1, n)


def _fourier_hyena_mixer(x, g, shift, scale, w_in, conv_w, conv_b, gmat, taps, norms, hy_bias):
    B, L, _ = x.shape
    if (L // FFT_RADIX) % 16 == 0:
        z, u, x0 = _fh_in(x, g, shift, scale, w_in, conv_w, conv_b, gmat, n_hi=L // FFT_RADIX)
        cs, tmat = _fourier_fft_tables(L)
        y_f = _fourier_seq_fft(z, cs, tmat)
        t1, t1_half, t3, gf, gi = _hyena_fft_tables(L)
        n_hi = 2 * L // FFT_RADIX
        rows = lambda a, r: a.reshape(a.shape[0], r, FFT_RADIX, D_HYENA)
        kspec = _hy_fft_b(_hy_fft_a(rows(taps.astype(BF16).reshape(1, 2 * L, D_HYENA), n_hi), t1), gf, norms=norms)
        v = _hy_fft_b(_hy_fft_a(rows(u, n_hi // 2), t1_half), gf, gi, kspec)
        y_h = _hy_fft_c(v, t3, rows(x0, n_hi // 2), rows(u, n_hi // 2), hy_bias)
        return y_f, y_h.reshape(B, L, D_HYENA)
    z, u, x0 = _fh_in(x, g, shift, scale, w_in, conv_w, conv_b, gmat)
    cl, sl, cb, sbf, sbi = _dft_tables(L)
    k_re, k_im = _hy_fwd(taps.astype(BF16), cb, sbf)
    kre, kim = _spec_combine(k_re, k_im, norms)
    y_f = _fourier_seq(z, cl, sl)
    yre, yim = _hy_fwd(u, cb, sbf, kre, kim)
    y_h = _hy_inv(yre, yim, x0, u, hy_bias, cb, sbi)
    return y_f, y_h


def kernel(x, c, ctx, c_ctx, norm1, norm2, w_mod, b_mod, ffn_w_up, ffn_conv_w, ffn_conv_b, ffn_w_down,
           fh_w_in, fh_w_out, hy_conv_w, hy_conv_b, hy_filt_w1, hy_filt_b1, hy_filt_w2, hy_filt_b2,
           hy_filt_w3, hy_filt_b3, hy_filt_w4, hy_freq, hy_bias, mla_w_in, mla_q_a_norm, mla_w_uq,
           mla_kv_a_norm, mla_w_ukv, mla_q_norm, mla_k_norm, mla_w_o):
    B, L, D = x.shape
    Lc = ctx.shape[1]
    bf = lambda a: a.astype(BF16)

    n_cond = -(-(B + 1) // 8) * 8
    cond = jnp.concatenate([c, c_ctx[None, :], jnp.zeros((n_cond - B - 1, D), F32)], axis=0)

    def mods(i):
        m = _adaln(cond, w_mod, b_mod, i)
        mx = [m[:B, j * D:(j + 1) * D].reshape(B, 1, D) for j in range(6)]
        mc = [m[B:B + 1, j * D:(j + 1) * D].reshape(1, 1, D) for j in range(6)]
        return mx, mc

    (sh1, sc1, g1, sh2, sc2, g2), (csh1, csc1, cg1, csh2, csc2, cg2) = mods(0)
    gmat = _group_dft_matrix()
    w_in0 = bf(fh_w_in[0])
    w_out0 = bf(fh_w_out[0])
    ffn0 = (bf(ffn_w_up[0]), ffn_conv_w[0], ffn_conv_b[0], bf(ffn_w_down[0]))
    filt = (hy_filt_w1[0], hy_filt_b1[0], hy_filt_w2[0], hy_filt_b2[0], hy_filt_w3[0], hy_filt_b3[0],
            hy_filt_w4[0], hy_freq[0])

    def mixer_layer(h, n_seq, m1, m2):
        s1, c1, gt1 = m1
        s2, c2, gt2 = m2
        taps, norms = _hyena_filter(n_seq, *filt)
        y_f, y_h = _fourier_hyena_mixer(h, norm1[0], s1, c1, w_in0, hy_conv_w[0], hy_conv_b[0], gmat, taps,
                                        norms, hy_bias[0])
        return _mix_ffn(h, [y_f, y_h], w_out0, gt1, norm2[0], s2, c2, gt2, *ffn0)

    x = mixer_layer(x, L, (sh1, sc1, g1), (sh2, sc2, g2))
    ctx = mixer_layer(ctx, Lc, (csh1, csc1, cg1), (csh2, csc2, cg2))

    (sh1, sc1, g1, sh2, sc2, g2), (csh1, csc1, _, _, _, _) = mods(1)
    w_in1 = bf(jnp.pad(mla_w_in[0], ((0, 0), (0, MLA_IN_PAD - mla_w_in.shape[2]))))
    wuq = bf(_pad_heads(mla_w_uq[0], QK_DIM, QK_DIM).T)
    wuk = bf(_pad_heads(mla_w_ukv[0], NOPE + V_DIM, NOPE).T)
    wuv = bf(mla_w_ukv[0].reshape(KV_LORA, MLA_HEADS, NOPE + V_DIM)[:, :, NOPE:].reshape(KV_LORA, MLA_HEADS * V_DIM).T)
    half_scale = math.sqrt(QK_DIM ** -0.5 * math.log2(math.e))
    wts = (w_in1, mla_q_a_norm[0].reshape(1, Q_LORA), wuq, _pad_lanes(mla_q_norm[0] * half_scale, HEAD_PAD),
           mla_kv_a_norm[0].reshape(1, KV_LORA), wuk, wuv, _pad_lanes(mla_k_norm[0] * half_scale, HEAD_PAD))
    qt, kx, vtx = _mla_proj(x, norm1[1], sh1, sc1, wts, _rope_tables(L), True)
    kc, vtc = _mla_proj(ctx, norm1[1], csh1, csc1, wts, _identity_rope_tables(Lc), False)
    o = _attention(qt, kx, kc, vtx, vtc)
    ffn1 = (bf(ffn_w_up[1]), ffn_conv_w[1], ffn_conv_b[1], bf(ffn_w_down[1]))
    return _mix_ffn(x, [o], bf(mla_w_o[0]), g1, norm2[1], sh2, sc2, g2, *ffn1)
```

```python
import functools
import math

import jax
import jax.numpy as jnp
import numpy as np
from jax import lax
from jax.experimental import pallas as pl
from jax.experimental.pallas import tpu as pltpu

F32 = jnp.float32
BF16 = jnp.bfloat16
HIGHEST = lax.Precision.HIGHEST

EPS = 1e-6
D_MODEL = 1024
D_FF = 2816
D_FOURIER = 512
FOURIER_GROUP_DIM = 128
D_HYENA = 512
HYENA_EMB_DIM = 33
HYENA_EMB_PAD = 64
HYENA_BANDS = 16
HYENA_WIDTH = 64
HYENA_MIN_DECAY = math.log(1e-2) / 0.3
HYENA_MAX_DECAY = math.log(1e-2) / 1.5
MLA_HEADS = 16
Q_LORA = 256
KV_LORA = 128
NOPE = 64
ROPE = 32
QK_DIM = NOPE + ROPE
V_DIM = 64
HEAD_PAD = 128
MLA_IN_PAD = 512
GRID_W = 64
ROPE_THETA = 10000.0

ATTN_DTYPE = jnp.float8_e4m3fn
ATTN_P_SHIFT = 8.0
FFN_CHUNK = 256
FFT_RADIX = 128
FFT_KRON = 16
HALO = 16
VMEM_LIMIT = 56 * 1024 * 1024


def _cparams(sem, flags=None):
    return pltpu.CompilerParams(dimension_semantics=sem, vmem_limit_bytes=VMEM_LIMIT, flags=flags)


def _dot(a, b):
    return jnp.dot(a, b, preferred_element_type=F32)


def _norm_mod(x, g, shift, scale):
    ms = jnp.mean(x * x, axis=-1, keepdims=True)
    return (x * lax.rsqrt(ms + EPS) * g) * (1.0 + scale) + shift


def _silu(x):
    return x * (1.0 / (1.0 + jnp.exp(-x)))


def _row_tile(L, want):
    t = min(L, want)
    assert L % t == 0 and t % HALO == 0
    return t


def _mod_kernel(c_ref, w_ref, b_ref, o_ref):
    s = _silu(c_ref[...])
    o_ref[...] = jnp.dot(s, w_ref[0], precision=HIGHEST, preferred_element_type=F32) + b_ref[0]


def _adaln(cond, w_mod, b_mod, layer):
    R, D = cond.shape
    n = w_mod.shape[2]
    tn = 768
    return pl.pallas_call(
        _mod_kernel,
        out_shape=jax.ShapeDtypeStruct((R, n), F32),
        grid=(n // tn,),
        in_specs=[pl.BlockSpec((R, D), lambda j: (0, 0)),
                  pl.BlockSpec((1, D, tn), lambda j: (layer, 0, j)),
                  pl.BlockSpec((1, 1, tn), lambda j: (layer, 0, j))],
        out_specs=pl.BlockSpec((R, tn), lambda j: (0, j)),
        compiler_params=_cparams(("arbitrary",)),
        name="adaln_mod",
    )(cond, w_mod, b_mod.reshape(b_mod.shape[0], 1, n))


def _halo_specs(tm, D, L):
    nb = L // HALO
    per = tm // HALO
    cur = pl.BlockSpec((1, tm, D), lambda b, i, *_: (b, i, 0))
    prev = pl.BlockSpec((1, HALO, D), lambda b, i, *_: (b, jnp.maximum(i * per - 1, 0), 0))
    nxt = pl.BlockSpec((1, HALO, D), lambda b, i, *_: (b, jnp.minimum((i + 1) * per, nb - 1), 0))
    return cur, prev, nxt


def _mod_spec(arr):
    D = arr.shape[-1]
    if arr.shape[0] == 1:
        return pl.BlockSpec((1, 1, D), lambda b, *_: (0, 0, 0))
    return pl.BlockSpec((1, 1, D), lambda b, *_: (b, 0, 0))


def _fill_hn_rows(hn_ref, xc, xp, xn, g, shift, scale, tm, i, nt):
    hp = _norm_mod(xp, g, shift, scale)
    hn_ref[0:HALO, :] = jnp.where(i > 0, hp, 0.0).astype(BF16)
    hn_ref[HALO:HALO + tm, :] = _norm_mod(xc, g, shift, scale).astype(BF16)
    hx = _norm_mod(xn, g, shift, scale)
    hn_ref[HALO + tm:, :] = jnp.where(i < nt - 1, hx, 0.0).astype(BF16)


def _fill_hn(hn_ref, xc_ref, xp_ref, xn_ref, g, shift, scale, tm, i, nt):
    _fill_hn_rows(hn_ref, xc_ref[0], xp_ref[0], xn_ref[0], g, shift, scale, tm, i, nt)


def _dwconv3_rows(p, w, tm):
    n = tm + 2 * HALO
    up = pltpu.roll(p, 1, 0)
    dn = pltpu.roll(p, n - 1, 0)
    c = up * w[0:1] + p * w[1:2] + dn * w[2:3]
    return c[HALO:HALO + tm]


def _fh_in_kernel(xc_ref, xp_ref, xn_ref, g_ref, sh_ref, sc_ref, w_ref, cw_ref, cb_ref, gm_ref, *rest, tm, n_hi):
    if n_hi:
        perm_ref, z_ref, u_ref, x0_ref, hn_ref = rest
    else:
        z_ref, u_ref, x0_ref, hn_ref = rest
    i = pl.program_id(1)
    nt = pl.num_programs(1)
    _fill_hn(hn_ref, xc_ref, xp_ref, xn_ref, g_ref[...], sh_ref[0], sc_ref[0], tm, i, nt)
    proj = _dot(hn_ref[...], w_ref[...])
    uf = proj[HALO:HALO + tm, :D_FOURIER].astype(BF16)
    if n_hi:
        uf = _dot(perm_ref[...], uf).astype(BF16)
        z = _dot(uf, gm_ref[...]).astype(BF16)
        z_ref[0] = z.reshape(n_hi, tm // n_hi, 2 * D_FOURIER)
    else:
        z_ref[0] = _dot(uf, gm_ref[...]).astype(BF16)
    c = _dwconv3_rows(proj[:, D_FOURIER:], cw_ref[...], tm) + cb_ref[...]
    x0 = c[:, :D_HYENA]
    x1 = c[:, D_HYENA:2 * D_HYENA]
    v = c[:, 2 * D_HYENA:]
    x0_ref[0] = x0
    u_ref[0] = (v * x1).astype(BF16)


def _fh_in(x, g, shift, scale, w_in, conv_w, conv_b, gmat, n_hi=0):
    B, L, D = x.shape
    tm = _row_tile(L, 512)
    cur, prev, nxt = _halo_specs(tm, D, L)
    n_in = w_in.shape[1]
    nh = 3 * D_HYENA
    full = lambda shape: pl.BlockSpec(shape, lambda b, i: (0,) * len(shape))
    out_tile = lambda c: pl.BlockSpec((1, tm, c), lambda b, i: (b, i, 0))
    in_specs = [cur, prev, nxt, full((1, D)), _mod_spec(shift), _mod_spec(scale),
                full((D, n_in)), full((3, nh)), full((1, nh)), full((D_FOURIER, 2 * D_FOURIER))]
    args = [x, x, x, g.reshape(1, D), shift, scale, w_in, conv_w, conv_b.reshape(1, nh), gmat]
    if n_hi:
        per = tm // n_hi
        r = np.arange(tm)
        src = n_hi * (r % per) + r // per
        args.append(_const(src[:, None] == r[None, :], BF16))
        in_specs.append(full((tm, tm)))
        z_shape = jax.ShapeDtypeStruct((B, n_hi, L // n_hi, 2 * D_FOURIER), BF16)
        z_spec = pl.BlockSpec((1, n_hi, per, 2 * D_FOURIER), lambda b, i: (b, 0, i, 0))
    else:
        z_shape = jax.ShapeDtypeStruct((B, L, 2 * D_FOURIER), BF16)
        z_spec = out_tile(2 * D_FOURIER)
    return pl.pallas_call(
        functools.partial(_fh_in_kernel, tm=tm, n_hi=n_hi),
        out_shape=(z_shape, jax.ShapeDtypeStruct((B, L, D_HYENA), BF16), jax.ShapeDtypeStruct((B, L, D_HYENA), F32)),
        grid=(B, L // tm),
        in_specs=in_specs,
        out_specs=(z_spec, out_tile(D_HYENA), out_tile(D_HYENA)),
        scratch_shapes=[pltpu.VMEM((tm + 2 * HALO, D), BF16)],
        compiler_params=_cparams(("parallel", "arbitrary")),
        name="fh_in",
    )(*args)


def _fourier_kernel(c_ref, s_ref, z_ref, o_ref, *, scale):
    z = z_ref[0]
    y = _dot(c_ref[...], z[:, :D_FOURIER]) + _dot(s_ref[...], z[:, D_FOURIER:])
    o_ref[0] = (y * scale).astype(BF16)


def _fourier_seq(z, cl, sl):
    B, L, _ = z.shape
    tf = _row_tile(L, 512)
    scale = 1.0 / math.sqrt(L * FOURIER_GROUP_DIM)
    return pl.pallas_call(
        functools.partial(_fourier_kernel, scale=scale),
        out_shape=jax.ShapeDtypeStruct((B, L, D_FOURIER), BF16),
        grid=(B, L // tf),
        in_specs=[pl.BlockSpec((tf, L), lambda b, i: (i, 0)),
                  pl.BlockSpec((tf, L), lambda b, i: (i, 0)),
                  pl.BlockSpec((1, L, 2 * D_FOURIER), lambda b, i: (b, 0, 0))],
        out_specs=pl.BlockSpec((1, tf, D_FOURIER), lambda b, i: (b, i, 0)),
        compiler_params=_cparams(("parallel", "arbitrary")),
        name="fourier_seq",
    )(cl, sl, z)


def _fft_f1_kernel(cs_ref, z_ref, o_ref, *, nc):
    for j in range(nc):
        cs = _dot(cs_ref[...], z_ref[0, j])
        cz = cs[:FFT_RADIX]
        sz = cs[FFT_RADIX:]
        o_ref[0, 0, j] = (cz[:, :D_FOURIER] + sz[:, D_FOURIER:]).astype(BF16)
        o_ref[0, 1, j] = (cz[:, D_FOURIER:] - sz[:, :D_FOURIER]).astype(BF16)


def _fft_f2_kernel(*refs, scale, n_grp):
    t_refs, b_refs, o_ref = refs[:n_grp], refs[n_grp:2 * n_grp], refs[2 * n_grp]
    for j in range(n_grp):
        n_hi = b_refs[j].shape[2]
        blk = b_refs[j][0].reshape(2 * n_hi * 8, D_FOURIER)
        y = _dot(t_refs[j][0], blk) * scale
        o_ref[0, :, j] = y.reshape(n_hi, 8, D_FOURIER).astype(BF16)


def _fourier_seq_fft(zp, cs, tmat):
    B, n_hi, _, _ = zp.shape
    L = n_hi * FFT_RADIX
    nc = 8
    ng = FFT_RADIX // 8
    bc = pl.pallas_call(
        functools.partial(_fft_f1_kernel, nc=nc),
        out_shape=jax.ShapeDtypeStruct((B, 2, n_hi, FFT_RADIX, D_FOURIER), BF16),
        grid=(B, n_hi // nc),
        in_specs=[pl.BlockSpec((2 * FFT_RADIX, FFT_RADIX), lambda b, i: (0, 0)),
                  pl.BlockSpec((1, nc, FFT_RADIX, 2 * D_FOURIER), lambda b, i: (b, i, 0, 0))],
        out_specs=pl.BlockSpec((1, 2, nc, FFT_RADIX, D_FOURIER), lambda b, i: (b, 0, i, 0, 0)),
        compiler_params=_cparams(("parallel", "arbitrary")),
        name="fourier_fft1",
    )(cs, zp)
    scale = 1.0 / math.sqrt(L * FOURIER_GROUP_DIM)
    n_grp = 4
    t_specs = [pl.BlockSpec((1, n_hi * 8, 2 * n_hi * 8), lambda b, g, j=j: (g * n_grp + j, 0, 0)) for j in range(n_grp)]
    b_specs = [pl.BlockSpec((1, 2, n_hi, 8, D_FOURIER), lambda b, g, j=j: (b, 0, 0, g * n_grp + j, 0))
               for j in range(n_grp)]
    y = pl.pallas_call(
        functools.partial(_fft_f2_kernel, scale=scale, n_grp=n_grp),
        out_shape=jax.ShapeDtypeStruct((B, n_hi, ng, 8, D_FOURIER), BF16),
        grid=(B, ng // n_grp),
        in_specs=t_specs + b_specs,
        out_specs=pl.BlockSpec((1, n_hi, n_grp, 8, D_FOURIER), lambda b, g: (b, 0, g, 0, 0)),
        compiler_params=_cparams(("parallel", "arbitrary")),
        name="fourier_fft2",
    )(*([tmat] * n_grp + [bc] * n_grp))
    return y.reshape(B, L, D_FOURIER)


def _filter_kernel(z_ref, w1_ref, b1_ref, w2_ref, b2_ref, w3_ref, b3_ref, w4_ref, f_ref, dl_ref,
                   k_ref, n_ref):
    i = pl.program_id(1)
    z = z_ref[0]
    f = f_ref[...]
    hd = lambda a, w: jnp.dot(a, w, precision=HIGHEST, preferred_element_type=F32)
    h = jnp.sin(f * (hd(z, w1_ref[...]) + b1_ref[...]))
    h = jnp.sin(f * (hd(h, w2_ref[...]) + b2_ref[...]))
    h = jnp.sin(f * (hd(h, w3_ref[...]) + b3_ref[...]))
    h = hd(h, w4_ref[0])
    t = z[:, 0:1]
    valid = z[:, HYENA_EMB_DIM:HYENA_EMB_DIM + 1]
    k = h * jnp.exp(-t * jnp.abs(dl_ref[...])) * valid
    k_ref[0] = k

    @pl.when(i == 0)
    def _():
        n_ref[...] = jnp.zeros_like(n_ref)

    n_ref[0] += jnp.sum(jnp.abs(k), axis=0, keepdims=True)


def _hyena_filter(L, w1, b1, w2, b2, w3, b3, w4, freq):
    pos = np.arange(L, dtype=np.float64)
    bands = np.linspace(1e-4, HYENA_BANDS - 1, HYENA_BANDS)

    def emb(p, valid):
        t = p / max(L - 1, 1)
        ang = (2.0 * math.pi / L) * p[:, None] * bands[None, :]
        pad = np.zeros((L, HYENA_EMB_PAD - HYENA_EMB_DIM - 1))
        return np.concatenate([t[:, None], np.cos(ang), -np.sin(ang), valid[:, None], pad], axis=-1)

    zf = emb(pos, np.ones((L,)))
    zb = emb(np.where(pos > 0, L - pos, 0.0), (pos > 0).astype(np.float64))
    z = _const(np.stack([zf, zb]))
    w1p = jnp.concatenate([w1, jnp.zeros((HYENA_EMB_PAD - HYENA_EMB_DIM, HYENA_WIDTH), F32)], axis=0)
    w4s = jnp.stack([w4[:, :D_HYENA], w4[:, D_HYENA:]])
    deltas = _const(np.linspace(HYENA_MIN_DECAY, HYENA_MAX_DECAY, D_HYENA).reshape(1, D_HYENA))
    tr = _row_tile(L, 512)
    W = HYENA_WIDTH
    full = lambda shape: pl.BlockSpec(shape, lambda s, i: (0,) * len(shape))
    return pl.pallas_call(
        _filter_kernel,
        out_shape=(jax.ShapeDtypeStruct((2, L, D_HYENA), F32), jax.ShapeDtypeStruct((2, 1, D_HYENA), F32)),
        grid=(2, L // tr),
        in_specs=[pl.BlockSpec((1, tr, HYENA_EMB_PAD), lambda s, i: (s, i, 0)),
                  full((HYENA_EMB_PAD, W)), full((1, W)), full((W, W)), full((1, W)),
                  full((W, W)), full((1, W)),
                  pl.BlockSpec((1, W, D_HYENA), lambda s, i: (s, 0, 0)),
                  full((1, W)), full((1, D_HYENA))],
        out_specs=(pl.BlockSpec((1, tr, D_HYENA), lambda s, i: (s, i, 0)),
                   pl.BlockSpec((1, 1, D_HYENA), lambda s, i: (s, 0, 0))),
        compiler_params=_cparams(("arbitrary", "arbitrary")),
        name="hyena_filter",
    )(z, w1p, b1.reshape(1, W), w2, b2.reshape(1, W), w3, b3.reshape(1, W), w4s, freq.reshape(1, W), deltas)


def _hy_fwd_kernel(c_ref, s_ref, u_ref, *rest, raw, tf):
    u = u_ref[0]
    ure = _dot(c_ref[...], u)
    uim = -_dot(s_ref[...], u)
    if raw:
        re_ref, im_ref = rest
        re_ref[0] = ure
        im_ref[0] = uim
        return
    kre_ref, kim_ref, re_ref, im_ref = rest
    kre = kre_ref[...]
    kim = kim_ref[...]
    row = lax.broadcasted_iota(jnp.int32, (tf, 1), 0) + pl.program_id(1) * tf
    packed = row == 0
    a = uim * kim
    re_ref[0] = (ure * kre - jnp.where(packed, 0.0, a)).astype(BF16)
    im_ref[0] = jnp.where(packed, a, ure * kim + uim * kre).astype(BF16)


def _hy_fwd(u, cb, sbf, kre=None, kim=None):
    B, L, C = u.shape
    tf = _row_tile(L, 512)
    raw = kre is None
    mat = pl.BlockSpec((tf, L), lambda b, i: (i, 0))
    tile = pl.BlockSpec((1, tf, C), lambda b, i: (b, i, 0))
    in_specs = [mat, mat, pl.BlockSpec((1, L, C), lambda b, i: (b, 0, 0))]
    args = [cb, sbf, u]
    if not raw:
        in_specs += [pl.BlockSpec((tf, C), lambda b, i: (i, 0))] * 2
        args += [kre, kim]
    dt = F32 if raw else BF16
    return pl.pallas_call(
        functools.partial(_hy_fwd_kernel, raw=raw, tf=tf),
        out_shape=(jax.ShapeDtypeStruct((B, L, C), dt), jax.ShapeDtypeStruct((B, L, C), dt)),
        grid=(B, L // tf),
        in_specs=in_specs,
        out_specs=(tile, tile),
        compiler_params=_cparams(("parallel", "arbitrary")),
        name="hyena_fwd_raw" if raw else "hyena_fwd",
    )(*args)


def _spec_combine_kernel(re_ref, im_ref, n_ref, kre_ref, kim_ref, *, tf, n_fft):
    row = lax.broadcasted_iota(jnp.int32, (tf, 1), 0) + pl.program_id(0) * tf
    sgn = (1 - 2 * (row & 1)).astype(F32)
    wgt = jnp.where(row == 0, 1.0 / n_fft, 2.0 / n_fft)
    s = wgt / (n_ref[0] + n_ref[1])
    kre_ref[...] = (re_ref[0] + sgn * re_ref[1]) * s
    kim_ref[...] = (im_ref[0] + sgn * im_ref[1]) * s


def _spec_combine(re, im, norms):
    _, L, C = re.shape
    tf = _row_tile(L, 512)
    pair = pl.BlockSpec((2, tf, C), lambda i: (0, i, 0))
    tile = pl.BlockSpec((tf, C), lambda i: (i, 0))
    return pl.pallas_call(
        functools.partial(_spec_combine_kernel, tf=tf, n_fft=2 * L),
        out_shape=(jax.ShapeDtypeStruct((L, C), F32), jax.ShapeDtypeStruct((L, C), F32)),
        grid=(L // tf,),
        in_specs=[pair, pair, pl.BlockSpec((2, 1, C), lambda i: (0, 0, 0))],
        out_specs=(tile, tile),
        compiler_params=_cparams(("arbitrary",)),
        name="hyena_spec_combine",
    )(re, im, norms)


def _hy_inv_kernel(c_ref, s_ref, re_ref, im_ref, x0_ref, u_ref, b_ref, o_ref):
    y = _dot(c_ref[...], re_ref[0]) - _dot(s_ref[...], im_ref[0])
    o_ref[0] = (x0_ref[0] * (y + u_ref[0].astype(F32) * b_ref[...])).astype(BF16)


def _hy_inv(yre, yim, x0, u, bias, cb, sbi):
    B, L, C = u.shape
    tt = _row_tile(L, 512)
    mat = pl.BlockSpec((tt, L), lambda b, i: (i, 0))
    whole = pl.BlockSpec((1, L, C), lambda b, i: (b, 0, 0))
    tile = pl.BlockSpec((1, tt, C), lambda b, i: (b, i, 0))
    return pl.pallas_call(
        _hy_inv_kernel,
        out_shape=jax.ShapeDtypeStruct((B, L, C), BF16),
        grid=(B, L // tt),
        in_specs=[mat, mat, whole, whole, tile, tile, pl.BlockSpec((1, C), lambda b, i: (0, 0))],
        out_specs=tile,
        compiler_params=_cparams(("parallel", "arbitrary")),
        name="hyena_inv",
    )(cb, sbi, yre, yim, x0, u, bias.reshape(1, C))


def _fft_a_kernel(t_ref, x_ref, o_ref):
    r2, kr, c = x_ref.shape[1:]
    y = _dot(t_ref[...], x_ref[0].reshape(r2 * kr, c))
    o_ref[0] = y.reshape(2, y.shape[0] // (2 * kr), kr, c).astype(BF16)


def _hy_fft_a(x4, t1):
    Bx, R, _, C = x4.shape
    n_hi = t1.shape[0] // (2 * FFT_KRON)
    return pl.pallas_call(
        _fft_a_kernel,
        out_shape=jax.ShapeDtypeStruct((Bx, 2, n_hi, FFT_RADIX, C), BF16),
        grid=(Bx, FFT_RADIX // FFT_KRON),
        in_specs=[pl.BlockSpec(t1.shape, lambda b, i: (0, 0)),
                  pl.BlockSpec((1, R, FFT_KRON, C), lambda b, i: (b, 0, i, 0))],
        out_specs=pl.BlockSpec((1, 2, n_hi, FFT_KRON, C), lambda b, i: (b, 0, 0, i, 0)),
        compiler_params=_cparams(("parallel", "arbitrary")),
        name="hyena_fft_a",
    )(t1, x4)


def _fft_b_kernel(gf_ref, a_ref, *rest, raw, n_fft):
    nb, _, kb = a_ref.shape[:3]
    if raw:
        n_ref, o_ref = rest
        s = 1.0 / (n_fft * (n_ref[0] + n_ref[1]))
    else:
        gi_ref, k_ref, o_ref = rest
    for kk in range(kb):
        gf = gf_ref[kk]
        if not raw:
            gi = gi_ref[kk]
            kr = k_ref[0, 0, kk]
            ki = k_ref[0, 1, kk]
        for b in range(nb):
            a = a_ref[b, :, kk].reshape(2 * FFT_RADIX, D_HYENA)
            x = _dot(gf, a)
            if raw:
                o_ref[b, :, kk] = (x * s).reshape(2, FFT_RADIX, D_HYENA)
                continue
            xr = x[:FFT_RADIX]
            xi = x[FFT_RADIX:]
            y = jnp.concatenate([xr * kr - xi * ki, xr * ki + xi * kr], axis=0).astype(BF16)
            o_ref[b, :, kk] = _dot(gi, y).reshape(2, FFT_RADIX, D_HYENA).astype(BF16)


def _hy_fft_b(a5, gf, gi=None, kspec=None, norms=None):
    Bx, _, n_hi, _, C = a5.shape
    raw = kspec is None
    kb = 4 if raw else 2
    blk = pl.BlockSpec((Bx, 2, kb, FFT_RADIX, C), lambda k: (0, 0, k, 0, 0))
    tab = pl.BlockSpec((kb, 2 * FFT_RADIX, 2 * FFT_RADIX), lambda k: (k, 0, 0))
    if raw:
        in_specs = [tab, blk, pl.BlockSpec((2, 1, C), lambda k: (0, 0, 0))]
        args = (gf, a5, norms)
    else:
        kblk = pl.BlockSpec((1, 2, kb, FFT_RADIX, C), lambda k: (0, 0, k, 0, 0))
        in_specs = [tab, blk, tab, kblk]
        args = (gf, a5, gi, kspec)
    return pl.pallas_call(
        functools.partial(_fft_b_kernel, raw=raw, n_fft=n_hi * FFT_RADIX),
        out_shape=jax.ShapeDtypeStruct(a5.shape, F32 if raw else BF16),
        grid=(n_hi // kb,),
        in_specs=in_specs,
        out_specs=blk,
        compiler_params=_cparams(("arbitrary",)),
        name="hyena_fft_spec" if raw else "hyena_fft_b",
    )(*args)


def _fft_c_kernel(t_ref, v_ref, x0_ref, u_ref, b_ref, o_ref):
    _, n_hi, kr, c = v_ref.shape[1:]
    y = _dot(t_ref[...], v_ref[0].reshape(2 * n_hi * kr, c))
    y = y.reshape(y.shape[0] // kr, kr, c)
    o_ref[0] = (x0_ref[0] * (y + u_ref[0].astype(F32) * b_ref[...])).astype(BF16)


def _hy_fft_c(v5, t3, x0, u, bias):
    B, _, n_hi, _, C = v5.shape
    R = t3.shape[0] // FFT_KRON
    tile = pl.BlockSpec((1, R, FFT_KRON, C), lambda b, i: (b, 0, i, 0))
    return pl.pallas_call(
        _fft_c_kernel,
        out_shape=jax.ShapeDtypeStruct((B, R, FFT_RADIX, C), BF16),
        grid=(B, FFT_RADIX // FFT_KRON),
        in_specs=[pl.BlockSpec(t3.shape, lambda b, i: (0, 0)),
                  pl.BlockSpec((1, 2, n_hi, FFT_KRON, C), lambda b, i: (b, 0, 0, i, 0)),
                  tile, tile, pl.BlockSpec((1, C), lambda b, i: (0, 0))],
        out_specs=tile,
        compiler_params=_cparams(("parallel", "arbitrary")),
        name="hyena_fft_c",
    )(t3, v5, x0, u, bias.reshape(1, C))


def _ffn_kernel(xc_ref, xp_ref, xn_ref, g1_ref, wo_ref, g_ref, sh_ref, sc_ref, gate_ref, wu_ref, cw_ref, cb_ref,
                wd_ref, *rest, tm, n_y):
    y_refs = rest[:3 * n_y]
    o_ref, yext_ref, hn_ref, act_ref = rest[3 * n_y:]
    i = pl.program_id(1)
    nt = pl.num_programs(1)
    off = 0
    for k in range(n_y):
        yc_ref, yp_ref, yn_ref = y_refs[3 * k:3 * k + 3]
        c = yc_ref.shape[-1]
        yext_ref[0:HALO, off:off + c] = yp_ref[0].astype(BF16)
        yext_ref[HALO:HALO + tm, off:off + c] = yc_ref[0].astype(BF16)
        yext_ref[HALO + tm:, off:off + c] = yn_ref[0].astype(BF16)
        off += c
    mix = g1_ref[0] * _dot(yext_ref[...], wo_ref[...])
    x1 = xc_ref[0] + mix[HALO:HALO + tm]
    _fill_hn_rows(hn_ref, x1, xp_ref[0] + mix[:HALO], xn_ref[0] + mix[HALO + tm:], g_ref[...], sh_ref[0], sc_ref[0],
                  tm, i, nt)
    for c in range(0, D_FF, FFN_CHUNK):
        cols = slice(c, c + FFN_CHUNK)
        gx = _dot(hn_ref[...], wu_ref[:, cols])
        vx = _dot(hn_ref[HALO:HALO + tm, :], wu_ref[:, D_FF + c:D_FF + c + FFN_CHUNK])
        cv = _dwconv3_rows(gx, cw_ref[:, cols], tm) + cb_ref[:, cols]
        act_ref[:, cols] = (_silu(cv) * vx).astype(BF16)
    o_ref[0] = x1 + gate_ref[0] * _dot(act_ref[...], wd_ref[...])


def _mix_ffn(x, ys, w_o, gate1, g, shift, scale, gate2, w_up, conv_w, conv_b, w_down):
    B, L, D = x.shape
    tm = _row_tile(L, 1024)
    cur, prev, nxt = _halo_specs(tm, D, L)
    const = lambda shape: pl.BlockSpec(shape, lambda b, i: (0,) * len(shape), pipeline_mode=pl.Buffered(1))
    y_specs, y_args = [], []
    for y in ys:
        y_specs += list(_halo_specs(tm, y.shape[-1], L))
        y_args += [y, y, y]
    return pl.pallas_call(
        functools.partial(_ffn_kernel, tm=tm, n_y=len(ys)),
        out_shape=jax.ShapeDtypeStruct((B, L, D), F32),
        grid=(B, L // tm),
        in_specs=[cur, prev, nxt, _mod_spec(gate1), const(w_o.shape), const((1, D)), _mod_spec(shift),
                  _mod_spec(scale), _mod_spec(gate2), const((D, 2 * D_FF)), const((3, D_FF)), const((1, D_FF)),
                  const((D_FF, D))] + y_specs,
        out_specs=pl.BlockSpec((1, tm, D), lambda b, i: (b, i, 0)),
        scratch_shapes=[pltpu.VMEM((tm + 2 * HALO, w_o.shape[0]), BF16), pltpu.VMEM((tm + 2 * HALO, D), BF16),
                        pltpu.VMEM((tm, D_FF), BF16)],
        compiler_params=_cparams(("parallel", "arbitrary")),
        name="mix_ffn",
    )(x, x, x, gate1, w_o, g.reshape(1, D), shift, scale, gate2, w_up, conv_w, conv_b.reshape(1, D_FF), w_down,
      *y_args)


def _rms(v, g, n):
    return v * lax.rsqrt(jnp.sum(v * v, axis=-1, keepdims=True) * (1.0 / n) + EPS) * g


def _rope_gain_tables(gain, cos, sa, sb):
    rot = slice(NOPE, QK_DIM)
    gr = gain[rot]
    return gain[:NOPE], gr * cos[rot], pltpu.roll(gr, 8, 0) * sa[rot], pltpu.roll(gr, ROPE - 8, 0) * sb[rot]


def _rope_t(yr, gc, gsa, gsb):
    return yr * gc + pltpu.roll(yr, 8, 0) * gsa + pltpu.roll(yr, ROPE - 8, 0) * gsb


def _sumsq(x):
    return jnp.sum(x * x, axis=0, keepdims=True)


def _mla_proj_kernel(x_ref, g_ref, sh_ref, sc_ref, win_ref, qan_ref, wuq_ref, qn_ref, kvn_ref, wuk_ref,
                     wuv_ref, kn_ref, cos_ref, sa_ref, sb_ref, *outs, with_q):
    hn = _norm_mod(x_ref[0], g_ref[...], sh_ref[0], sc_ref[0]).astype(BF16)
    a = _dot(hn, win_ref[...])
    tm = a.shape[0]
    cos = cos_ref[...]
    sa = sa_ref[...]
    sb = sb_ref[...]
    pad = jnp.zeros((HEAD_PAD - QK_DIM, tm), ATTN_DTYPE)
    if with_q:
        qt_ref, k_ref, vt_ref = outs
        qat = _rms(a[:, :Q_LORA], qan_ref[...], Q_LORA).T.astype(BF16)
        qft = _dot(wuq_ref[...], qat)
        gn, gc, gsa, gsb = _rope_gain_tables(qn_ref[...], cos, sa, sb)
        for h in range(MLA_HEADS):
            base = h * HEAD_PAD
            xn = qft[base:base + NOPE]
            xr = qft[base + NOPE:base + QK_DIM]
            r = lax.rsqrt((_sumsq(xn) + _sumsq(xr)) * (1.0 / QK_DIM) + EPS)
            qt_ref[0, base:base + NOPE, :] = (xn * r * gn).astype(ATTN_DTYPE)
            qt_ref[0, base + NOPE:base + QK_DIM, :] = _rope_t(xr * r, gc, gsa, gsb).astype(ATTN_DTYPE)
            qt_ref[0, base + QK_DIM:base + HEAD_PAD, :] = pad
    else:
        k_ref, vt_ref = outs
    ckvt = _rms(a[:, Q_LORA:Q_LORA + KV_LORA], kvn_ref[...], KV_LORA).T.astype(BF16)
    kft = _dot(wuk_ref[...], ckvt)
    vt_ref[0] = _dot(wuv_ref[...], ckvt).astype(ATTN_DTYPE)
    kpe = a[:, Q_LORA + KV_LORA:].T[:ROPE]
    gn, gc, gsa, gsb = _rope_gain_tables(kn_ref[...], cos, sa, sb)
    kpe_rot = _rope_t(kpe, gc, gsa, gsb)
    kpe_ss = _sumsq(kpe)
    for h in range(MLA_HEADS):
        base = h * HEAD_PAD
        xn = kft[base:base + NOPE]
        r = lax.rsqrt((_sumsq(xn) + kpe_ss) * (1.0 / QK_DIM) + EPS)
        kt = jnp.concatenate([xn * r * gn, kpe_rot * r, jnp.zeros((HEAD_PAD - QK_DIM, tm), F32)], axis=0)
        k_ref[0, :, base:base + HEAD_PAD] = kt.T.astype(ATTN_DTYPE)


def _mla_proj(x, g, shift, scale, wts, rope_tabs, with_q):
    B, L, D = x.shape
    tm = _row_tile(L, 512)
    win, qan, wuq, qn, kvn, wuk, wuv, kn = wts
    cos, sa, sb = rope_tabs
    HP = MLA_HEADS * HEAD_PAD
    HV = MLA_HEADS * V_DIM
    full = lambda a: pl.BlockSpec(a.shape, lambda b, i: (0,) * a.ndim)
    tab = pl.BlockSpec((HEAD_PAD, tm), lambda b, i: (0, i))
    out_shape = [jax.ShapeDtypeStruct((B, L, HP), ATTN_DTYPE), jax.ShapeDtypeStruct((B, HV, L), ATTN_DTYPE)]
    out_specs = [pl.BlockSpec((1, tm, HP), lambda b, i: (b, i, 0)),
                 pl.BlockSpec((1, HV, tm), lambda b, i: (b, 0, i))]
    if with_q:
        out_shape = [jax.ShapeDtypeStruct((B, HP, L), ATTN_DTYPE)] + out_shape
        out_specs = [pl.BlockSpec((1, HP, tm), lambda b, i: (b, 0, i))] + out_specs
    g2 = g.reshape(1, D)
    gain_tab = lambda v: jnp.broadcast_to(v.reshape(HEAD_PAD, 1), (HEAD_PAD, tm))
    qn = gain_tab(qn)
    kn = gain_tab(kn)
    return pl.pallas_call(
        functools.partial(_mla_proj_kernel, with_q=with_q),
        out_shape=tuple(out_shape),
        grid=(B, L // tm),
        in_specs=[pl.BlockSpec((1, tm, D), lambda b, i: (b, i, 0)), full(g2), _mod_spec(shift), _mod_spec(scale),
                  full(win), full(qan), full(wuq), full(qn), full(kvn), full(wuk), full(wuv), full(kn),
                  tab, tab, tab],
        out_specs=tuple(out_specs),
        compiler_params=_cparams(("parallel", "arbitrary")),
        name="mla_proj_q" if with_q else "mla_proj_kv",
    )(x, g2, shift, scale, win, qan, wuq, qn, kvn, wuk, wuv, kn, cos, sa, sb)


def _attn_kernel(qt_ref, kx_ref, kc_ref, vtx_ref, vtc_ref, o_ref, ax_ref, ac_ref, am_ref, bx_ref, bc_ref, bm_ref,
                 px_ref, pc_ref):
    t = pl.program_id(0)
    io = (qt_ref, kx_ref, kc_ref, vtx_ref, vtc_ref, o_ref, px_ref, pc_ref)

    @pl.when(t == 0)
    def _():
        bx_ref[...] = jnp.zeros(bx_ref.shape, BF16)
        bc_ref[...] = jnp.zeros(bc_ref.shape, BF16)
        bm_ref[...] = jnp.zeros(bm_ref.shape, BF16)

    @pl.when(t % 2 == 0)
    def _():
        _attn_step(*io, (ax_ref, ac_ref, am_ref), (bx_ref, bc_ref, bm_ref))

    @pl.when(t % 2 == 1)
    def _():
        _attn_step(*io, (bx_ref, bc_ref, bm_ref), (ax_ref, ac_ref, am_ref))


def _attn_step(qt_ref, kx_ref, kc_ref, vtx_ref, vtc_ref, o_ref, px_ref, pc_ref, parked, ready):
    wx_ref, wc_ref, wm_ref = parked
    rx_ref, rc_ref, rm_ref = ready
    tq = qt_ref.shape[2]
    off = rm_ref[...][None]

    def probs(s_ref):
        n = s_ref.shape[0]
        s = s_ref[...].reshape(n // 16, 16, 2 * tq)
        return jnp.exp2(s - off).reshape(n, 2 * tq).astype(ATTN_DTYPE)

    px_ref[...] = probs(rx_ref)
    pc_ref[...] = probs(rc_ref)

    q0 = qt_ref[0, :HEAD_PAD, :]
    q1 = qt_ref[0, HEAD_PAD:, :]
    zq = jnp.zeros_like(q0)
    qbd = jnp.concatenate([jnp.concatenate([q0, zq], axis=1), jnp.concatenate([zq, q1], axis=1)], axis=0)
    nx = _dot(kx_ref[0], qbd).astype(BF16)
    nc = _dot(kc_ref[0], qbd).astype(BF16)
    wx_ref[...] = nx
    wc_ref[...] = nc
    m = jnp.maximum(jnp.max(nx, axis=0, keepdims=True), jnp.max(nc, axis=0, keepdims=True)).astype(F32)
    wm_ref[...] = jnp.broadcast_to((m - ATTN_P_SHIFT).astype(BF16), wm_ref.shape)

    vtx = vtx_ref[0]
    vtc = vtc_ref[0]
    halves = []
    for j in range(2):
        own = slice(j * V_DIM, (j + 1) * V_DIM)
        oth = slice((1 - j) * V_DIM, (2 - j) * V_DIM)
        cols = slice(j * tq, (j + 1) * tq)
        parts_x = [vtx[own], jnp.ones_like(vtx[oth])]
        parts_c = [vtc[own], jnp.ones_like(vtc[oth])]
        if j == 1:
            parts_x.reverse()
            parts_c.reverse()
        r = (_dot(jnp.concatenate(parts_x, axis=0), px_ref[:, cols])
             + _dot(jnp.concatenate(parts_c, axis=0), pc_ref[:, cols]))
        halves.append(r[own] * (1.0 / r[oth][0:1]))
    o_ref[0] = jnp.concatenate(halves, axis=0).T.astype(BF16)


def _attention(qt, kx, kc, vtx, vtc):
    B, HP, L = qt.shape
    Lc = kc.shape[1]
    tq = _row_tile(L, 1024)
    nhp = MLA_HEADS // 2
    nq = L // tq
    n_tiles = B * nhp * nq

    def tile(t):
        t = jnp.clip(t, 0, n_tiles - 1)
        return t // (nhp * nq), (t // nq) % nhp, t % nq

    def logits_side(f):
        return lambda t: f(*tile(t))

    def value_side(f):
        return lambda t: f(*tile(t - 1))

    return pl.pallas_call(
        _attn_kernel,
        out_shape=jax.ShapeDtypeStruct((B, L, MLA_HEADS * V_DIM), BF16),
        grid=(n_tiles + 1,),
        in_specs=[pl.BlockSpec((1, 2 * HEAD_PAD, tq), logits_side(lambda b, h, i: (b, h, i))),
                  pl.BlockSpec((1, L, 2 * HEAD_PAD), logits_side(lambda b, h, i: (b, 0, h))),
                  pl.BlockSpec((1, Lc, 2 * HEAD_PAD), logits_side(lambda b, h, i: (b, 0, h))),
                  pl.BlockSpec((1, 2 * V_DIM, L), value_side(lambda b, h, i: (b, h, 0))),
                  pl.BlockSpec((1, 2 * V_DIM, Lc), value_side(lambda b, h, i: (b, h, 0)))],
        out_specs=pl.BlockSpec((1, tq, 2 * V_DIM), value_side(lambda b, h, i: (b, i, h))),
        scratch_shapes=[pltpu.VMEM((L, 2 * tq), BF16), pltpu.VMEM((Lc, 2 * tq), BF16), pltpu.VMEM((16, 2 * tq), BF16)] * 2
                       + [pltpu.VMEM((L, 2 * tq), ATTN_DTYPE), pltpu.VMEM((Lc, 2 * tq), ATTN_DTYPE)],
        compiler_params=_cparams(("arbitrary",)),
        name="mla_attention",
    )(qt, kx, kc, vtx, vtc)


def _const(a, dtype=F32):
    return jnp.asarray(np.asarray(a, np.float32)).astype(dtype)


def _angle(num, period):
    return (num % period).astype(np.float64) * (2.0 * math.pi / period)


def _trig_matrix(n, period):
    r = np.arange(n, dtype=np.int64)
    a = _angle(r[:, None] * r[None, :], period)
    return np.cos(a), np.sin(a)


def _dft_tables(L):
    cl, sl = _trig_matrix(L, L)
    cb, sb = _trig_matrix(L, 2 * L)
    idx = np.arange(L)
    alt = (1 - 2 * (idx & 1)).astype(np.float64)
    sbf = np.where(idx[:, None] == 0, -alt[None, :], sb)
    sbi = np.where(idx[None, :] == 0, -alt[:, None], sb)
    return tuple(_const(m, BF16) for m in (cl, sl, cb, sbf, sbi))


def _fourier_fft_tables(L):
    n_hi = L // FFT_RADIX
    i = np.arange(FFT_RADIX, dtype=np.int64)
    a = _angle(i[:, None] * i[None, :], FFT_RADIX)
    cs = np.concatenate([np.cos(a), np.sin(a)], axis=0)
    ng = FFT_RADIX // 8
    g = np.arange(ng, dtype=np.int64)[:, None, None, None]
    k1 = np.arange(n_hi, dtype=np.int64)[None, :, None, None]
    j = np.arange(8, dtype=np.int64)[None, None, :, None]
    n1 = np.arange(n_hi, dtype=np.int64)[None, None, None, :]
    ang = _angle(n1 * (FFT_RADIX * k1 + 8 * g + j), L)
    eye = np.eye(8)
    blocks = [t[:, :, :, :, None] * eye[None, None, :, None, :] for t in (np.cos(ang), np.sin(ang))]
    tmat = np.stack(blocks, axis=3)
    return _const(cs, BF16), _const(tmat.reshape(ng, n_hi * 8, 2 * n_hi * 8), BF16)


def _hyena_fft_tables(L):
    n_fft = 2 * L
    n_hi = n_fft // FFT_RADIX
    k2 = np.arange(n_hi, dtype=np.int64)
    a1 = _angle(k2[:, None] * k2[None, :], n_hi)
    c1, s1 = np.cos(a1), np.sin(a1)
    eye = np.eye(FFT_KRON)
    kron = lambda m: (m[:, None, :, None] * eye[None, :, None, :]).reshape(m.shape[0] * FFT_KRON, m.shape[1] * FFT_KRON)
    t1 = _const(np.concatenate([kron(c1), kron(-s1)], axis=0), BF16)
    t1_half = t1[:, :n_hi // 2 * FFT_KRON]
    t3 = t1_half.T
    r = np.arange(FFT_RADIX, dtype=np.int64)
    k = k2[:, None, None] + n_hi * r[None, :, None]
    a2 = _angle(k * r[None, None, :], n_fft)
    c2, s2 = _const(np.cos(a2), BF16), _const(np.sin(a2), BF16)
    gf = jnp.concatenate([jnp.concatenate([c2, s2], axis=2), jnp.concatenate([-s2, c2], axis=2)], axis=1)
    c2t, s2t = jnp.swapaxes(c2, 1, 2), jnp.swapaxes(s2, 1, 2)
    gi = jnp.concatenate([jnp.concatenate([c2t, -s2t], axis=2), jnp.concatenate([s2t, c2t], axis=2)], axis=1)
    return t1, t1_half, t3, gf, gi


def _group_dft_matrix():
    g = FOURIER_GROUP_DIM
    ng = D_FOURIER // g
    c, s = _trig_matrix(g, g)
    eye = np.eye(ng)
    return _const(np.concatenate([np.kron(eye, c), -np.kron(eye, s)], axis=1), BF16)


def _rope_tables(L):
    pos = np.arange(L)
    rows = (pos // GRID_W).astype(np.float64)
    cols = (pos % GRID_W).astype(np.float64)
    nf = ROPE // 4
    inv_freq = ROPE_THETA ** (-np.arange(nf, dtype=np.float64) / nf)
    ar = rows[:, None] * inv_freq[None, :]
    ac = cols[:, None] * inv_freq[None, :]
    one = np.ones((L, NOPE))
    z_n = np.zeros((L, NOPE))
    z_f = np.zeros((L, nf))
    tail = np.zeros((L, HEAD_PAD - QK_DIM))
    cos = np.concatenate([one, np.cos(ar), np.cos(ar), np.cos(ac), np.cos(ac), tail], axis=1)
    sa = np.concatenate([z_n, z_f, np.sin(ar), z_f, np.sin(ac), tail], axis=1)
    sb = np.concatenate([z_n, -np.sin(ar), z_f, -np.sin(ac), z_f, tail], axis=1)
    return _const(cos.T), _const(sa.T), _const(sb.T)


def _identity_rope_tables(L):
    return jnp.ones((HEAD_PAD, L), F32), jnp.zeros((HEAD_PAD, L), F32), jnp.zeros((HEAD_PAD, L), F32)


def _pad_heads(w, per_head, keep):
    K = w.shape[0]
    w = w.reshape(K, MLA_HEADS, per_head)[:, :, :keep]
    w = jnp.pad(w, ((0, 0), (0, 0), (0, HEAD_PAD - keep)))
    return w.reshape(K, MLA_HEADS * HEAD_PAD)


def _pad_lanes(v, n):
    return jnp.pad(v, (0, n - v.shape[0])).reshape(1, n)


def _fourier_hyena_mixer(x, g, shift, scale, w_in, conv_w, conv_b, gmat, taps, norms, hy_bias):
    B, L, _ = x.shape
    if (L // FFT_RADIX) % 16 == 0:
        z, u, x0 = _fh_in(x, g, shift, scale, w_in, conv_w, conv_b, gmat, n_hi=L // FFT_RADIX)
        cs, tmat = _fourier_fft_tables(L)
        y_f = _fourier_seq_fft(z, cs, tmat)
        t1, t1_half, t3, gf, gi = _hyena_fft_tables(L)
        n_hi = 2 * L // FFT_RADIX
        rows = lambda a, r: a.reshape(a.shape[0], r, FFT_RADIX, D_HYENA)
        kspec = _hy_fft_b(_hy_fft_a(rows(taps.astype(BF16).reshape(1, 2 * L, D_HYENA), n_hi), t1), gf, norms=norms)
        v = _hy_fft_b(_hy_fft_a(rows(u, n_hi // 2), t1_half), gf, gi, kspec)
        y_h = _hy_fft_c(v, t3, rows(x0, n_hi // 2), rows(u, n_hi // 2), hy_bias)
        return y_f, y_h.reshape(B, L, D_HYENA)
    z, u, x0 = _fh_in(x, g, shift, scale, w_in, conv_w, conv_b, gmat)
    cl, sl, cb, sbf, sbi = _dft_tables(L)
    k_re, k_im = _hy_fwd(taps.astype(BF16), cb, sbf)
    kre, kim = _spec_combine(k_re, k_im, norms)
    y_f = _fourier_seq(z, cl, sl)
    yre, yim = _hy_fwd(u, cb, sbf, kre, kim)
    y_h = _hy_inv(yre, yim, x0, u, hy_bias, cb, sbi)
    return y_f, y_h


def kernel(x, c, ctx, c_ctx, norm1, norm2, w_mod, b_mod, ffn_w_up, ffn_conv_w, ffn_conv_b, ffn_w_down,
           fh_w_in, fh_w_out, hy_conv_w, hy_conv_b, hy_filt_w1, hy_filt_b1, hy_filt_w2, hy_filt_b2,
           hy_filt_w3, hy_filt_b3, hy_filt_w4, hy_freq, hy_bias, mla_w_in, mla_q_a_norm, mla_w_uq,
           mla_kv_a_norm, mla_w_ukv, mla_q_norm, mla_k_norm, mla_w_o):
    B, L, D = x.shape
    Lc = ctx.shape[1]
    bf = lambda a: a.astype(BF16)

    n_cond = -(-(B + 1) // 8) * 8
    cond = jnp.concatenate([c, c_ctx[None, :], jnp.zeros((n_cond - B - 1, D), F32)], axis=0)

    def mods(i):
        m = _adaln(cond, w_mod, b_mod, i)
        mx = [m[:B, j * D:(j + 1) * D].reshape(B, 1, D) for j in range(6)]
        mc = [m[B:B + 1, j * D:(j + 1) * D].reshape(1, 1, D) for j in range(6)]
        return mx, mc

    (sh1, sc1, g1, sh2, sc2, g2), (csh1, csc1, cg1, csh2, csc2, cg2) = mods(0)
    gmat = _group_dft_matrix()
    w_in0 = bf(fh_w_in[0])
    w_out0 = bf(fh_w_out[0])
    ffn0 = (bf(ffn_w_up[0]), ffn_conv_w[0], ffn_conv_b[0], bf(ffn_w_down[0]))
    filt = (hy_filt_w1[0], hy_filt_b1[0], hy_filt_w2[0], hy_filt_b2[0], hy_filt_w3[0], hy_filt_b3[0],
            hy_filt_w4[0], hy_freq[0])

    def mixer_layer(h, n_seq, m1, m2):
        s1, c1, gt1 = m1
        s2, c2, gt2 = m2
        taps, norms = _hyena_filter(n_seq, *filt)
        y_f, y_h = _fourier_hyena_mixer(h, norm1[0], s1, c1, w_in0, hy_conv_w[0], hy_conv_b[0], gmat, taps,
                                        norms, hy_bias[0])
        return _mix_ffn(h, [y_f, y_h], w_out0, gt1, norm2[0], s2, c2, gt2, *ffn0)

    x = mixer_layer(x, L, (sh1, sc1, g1), (sh2, sc2, g2))
    ctx = mixer_layer(ctx, Lc, (csh1, csc1, cg1), (csh2, csc2, cg2))

    (sh1, sc1, g1, sh2, sc2, g2), (csh1, csc1, _, _, _, _) = mods(1)
    w_in1 = bf(jnp.pad(mla_w_in[0], ((0, 0), (0, MLA_IN_PAD - mla_w_in.shape[2]))))
    wuq = bf(_pad_heads(mla_w_uq[0], QK_DIM, QK_DIM).T)
    wuk = bf(_pad_heads(mla_w_ukv[0], NOPE + V_DIM, NOPE).T)
    wuv = bf(mla_w_ukv[0].reshape(KV_LORA, MLA_HEADS, NOPE + V_DIM)[:, :, NOPE:].reshape(KV_LORA, MLA_HEADS * V_DIM).T)
    half_scale = math.sqrt(QK_DIM ** -0.5 * math.log2(math.e))
    wts = (w_in1, mla_q_a_norm[0].reshape(1, Q_LORA), wuq, _pad_lanes(mla_q_norm[0] * half_scale, HEAD_PAD),
           mla_kv_a_norm[0].reshape(1, KV_LORA), wuk, wuv, _pad_lanes(mla_k_norm[0] * half_scale, HEAD_PAD))
    qt, kx, vtx = _mla_proj(x, norm1[1], sh1, sc1, wts, _rope_tables(L), True)
    kc, vtc = _mla_proj(ctx, norm1[1], csh1, csc1, wts, _identity_rope_tables(Lc), False)
    o = _attention(qt, kx, kc, vtx, vtc)
    ffn1 = (bf(ffn_w_up[1]), ffn_conv_w[1], ffn_conv_b[1], bf(ffn_w_down[1]))
    return _mix_ffn(x, [o], bf(mla_w_o[0]), g1, norm2[1], sh2, sc2, g2, *ffn1)
```

```python
import functools
import math

import jax
import jax.numpy as jnp
import numpy as np
from jax import lax
from jax.experimental import pallas as pl
from jax.experimental.pallas import tpu as pltpu

F32 = jnp.float32
BF16 = jnp.bfloat16
HIGHEST = lax.Precision.HIGHEST

EPS = 1e-6
D_MODEL = 1024
D_FF = 2816
D_FOURIER = 512
FOURIER_GROUP_DIM = 128
D_HYENA = 512
HYENA_EMB_DIM = 33
HYENA_EMB_PAD = 64
HYENA_BANDS = 16
HYENA_WIDTH = 64
HYENA_MIN_DECAY = math.log(1e-2) / 0.3
HYENA_MAX_DECAY = math.log(1e-2) / 1.5
MLA_HEADS = 16
Q_LORA = 256
KV_LORA = 128
NOPE = 64
ROPE = 32
QK_DIM = NOPE + ROPE
V_DIM = 64
HEAD_PAD = 128
MLA_IN_PAD = 512
GRID_W = 64
ROPE_THETA = 10000.0

ATTN_DTYPE = jnp.float8_e4m3fn
ATTN_P_SHIFT = 8.0
FFN_CHUNK = 256
FFT_RADIX = 128
FFT_KRON = 16
FFT_ROWS = 2 * FFT_KRON
HALO = 16
VMEM_LIMIT = 56 * 1024 * 1024


def _cparams(sem, flags=None):
    return pltpu.CompilerParams(dimension_semantics=sem, vmem_limit_bytes=VMEM_LIMIT, flags=flags)


def _dot(a, b):
    return jnp.dot(a, b, preferred_element_type=F32)


def _norm_mod(x, g, shift, scale):
    ms = jnp.mean(x * x, axis=-1, keepdims=True)
    return (x * lax.rsqrt(ms + EPS) * g) * (1.0 + scale) + shift


def _silu(x):
    return x * (1.0 / (1.0 + jnp.exp(-x)))


def _row_tile(L, want):
    t = min(L, want)
    assert L % t == 0 and t % HALO == 0
    return t


def _mod_kernel(c_ref, w_ref, b_ref, o_ref):
    s = _silu(c_ref[...])
    o_ref[...] = jnp.dot(s, w_ref[0], precision=HIGHEST, preferred_element_type=F32) + b_ref[0]


def _adaln(cond, w_mod, b_mod, layer):
    R, D = cond.shape
    n = w_mod.shape[2]
    tn = 768
    return pl.pallas_call(
        _mod_kernel,
        out_shape=jax.ShapeDtypeStruct((R, n), F32),
        grid=(n // tn,),
        in_specs=[pl.BlockSpec((R, D), lambda j: (0, 0)),
                  pl.BlockSpec((1, D, tn), lambda j: (layer, 0, j)),
                  pl.BlockSpec((1, 1, tn), lambda j: (layer, 0, j))],
        out_specs=pl.BlockSpec((R, tn), lambda j: (0, j)),
        compiler_params=_cparams(("arbitrary",)),
        name="adaln_mod",
    )(cond, w_mod, b_mod.reshape(b_mod.shape[0], 1, n))


def _halo_specs(tm, D, L):
    nb = L // HALO
    per = tm // HALO
    cur = pl.BlockSpec((1, tm, D), lambda b, i, *_: (b, i, 0))
    prev = pl.BlockSpec((1, HALO, D), lambda b, i, *_: (b, jnp.maximum(i * per - 1, 0), 0))
    nxt = pl.BlockSpec((1, HALO, D), lambda b, i, *_: (b, jnp.minimum((i + 1) * per, nb - 1), 0))
    return cur, prev, nxt


def _mod_spec(arr):
    D = arr.shape[-1]
    if arr.shape[0] == 1:
        return pl.BlockSpec((1, 1, D), lambda b, *_: (0, 0, 0))
    return pl.BlockSpec((1, 1, D), lambda b, *_: (b, 0, 0))


def _fill_hn_rows(hn_ref, xc, xp, xn, g, shift, scale, tm, i, nt):
    hp = _norm_mod(xp, g, shift, scale)
    hn_ref[0:HALO, :] = jnp.where(i > 0, hp, 0.0).astype(BF16)
    hn_ref[HALO:HALO + tm, :] = _norm_mod(xc, g, shift, scale).astype(BF16)
    hx = _norm_mod(xn, g, shift, scale)
    hn_ref[HALO + tm:, :] = jnp.where(i < nt - 1, hx, 0.0).astype(BF16)


def _fill_hn(hn_ref, xc_ref, xp_ref, xn_ref, g, shift, scale, tm, i, nt):
    _fill_hn_rows(hn_ref, xc_ref[0], xp_ref[0], xn_ref[0], g, shift, scale, tm, i, nt)


def _dwconv3_rows(p, w, tm):
    n = tm + 2 * HALO
    up = pltpu.roll(p, 1, 0)
    dn = pltpu.roll(p, n - 1, 0)
    c = up * w[0:1] + p * w[1:2] + dn * w[2:3]
    return c[HALO:HALO + tm]


def _fh_in_kernel(xc_ref, xp_ref, xn_ref, g_ref, sh_ref, sc_ref, w_ref, cw_ref, cb_ref, gm_ref, *rest, tm, n_hi):
    if n_hi:
        perm_ref, z_ref, u_ref, x0_ref, hn_ref = rest
    else:
        z_ref, u_ref, x0_ref, hn_ref = rest
    i = pl.program_id(1)
    nt = pl.num_programs(1)
    _fill_hn(hn_ref, xc_ref, xp_ref, xn_ref, g_ref[...], sh_ref[0], sc_ref[0], tm, i, nt)
    proj = _dot(hn_ref[...], w_ref[...])
    uf = proj[HALO:HALO + tm, :D_FOURIER].astype(BF16)
    if n_hi:
        uf = _dot(perm_ref[...], uf).astype(BF16)
        z = _dot(uf, gm_ref[...]).astype(BF16)
        z_ref[0] = z.reshape(n_hi, tm // n_hi, 2 * D_FOURIER)
    else:
        z_ref[0] = _dot(uf, gm_ref[...]).astype(BF16)
    c = _dwconv3_rows(proj[:, D_FOURIER:], cw_ref[...], tm) + cb_ref[...]
    x0 = c[:, :D_HYENA]
    x1 = c[:, D_HYENA:2 * D_HYENA]
    v = c[:, 2 * D_HYENA:]
    x0_ref[0] = x0
    u_ref[0] = (v * x1).astype(BF16)


def _fh_in(x, g, shift, scale, w_in, conv_w, conv_b, gmat, n_hi=0):
    B, L, D = x.shape
    tm = _row_tile(L, 512)
    cur, prev, nxt = _halo_specs(tm, D, L)
    n_in = w_in.shape[1]
    nh = 3 * D_HYENA
    full = lambda shape: pl.BlockSpec(shape, lambda b, i: (0,) * len(shape))
    out_tile = lambda c: pl.BlockSpec((1, tm, c), lambda b, i: (b, i, 0))
    in_specs = [cur, prev, nxt, full((1, D)), _mod_spec(shift), _mod_spec(scale),
                full((D, n_in)), full((3, nh)), full((1, nh)), full((D_FOURIER, 2 * D_FOURIER))]
    args = [x, x, x, g.reshape(1, D), shift, scale, w_in, conv_w, conv_b.reshape(1, nh), gmat]
    if n_hi:
        per = tm // n_hi
        r = np.arange(tm)
        src = n_hi * (r % per) + r // per
        args.append(_const(src[:, None] == r[None, :], BF16))
        in_specs.append(full((tm, tm)))
        z_shape = jax.ShapeDtypeStruct((B, n_hi, L // n_hi, 2 * D_FOURIER), BF16)
        z_spec = pl.BlockSpec((1, n_hi, per, 2 * D_FOURIER), lambda b, i: (b, 0, i, 0))
    else:
        z_shape = jax.ShapeDtypeStruct((B, L, 2 * D_FOURIER), BF16)
        z_spec = out_tile(2 * D_FOURIER)
    return pl.pallas_call(
        functools.partial(_fh_in_kernel, tm=tm, n_hi=n_hi),
        out_shape=(z_shape, jax.ShapeDtypeStruct((B, L, D_HYENA), BF16), jax.ShapeDtypeStruct((B, L, D_HYENA), F32)),
        grid=(B, L // tm),
        in_specs=in_specs,
        out_specs=(z_spec, out_tile(D_HYENA), out_tile(D_HYENA)),
        scratch_shapes=[pltpu.VMEM((tm + 2 * HALO, D), BF16)],
        compiler_params=_cparams(("parallel", "arbitrary")),
        name="fh_in",
    )(*args)


def _fourier_kernel(c_ref, s_ref, z_ref, o_ref, *, scale):
    z = z_ref[0]
    y = _dot(c_ref[...], z[:, :D_FOURIER]) + _dot(s_ref[...], z[:, D_FOURIER:])
    o_ref[0] = (y * scale).astype(BF16)


def _fourier_seq(z, cl, sl):
    B, L, _ = z.shape
    tf = _row_tile(L, 512)
    scale = 1.0 / math.sqrt(L * FOURIER_GROUP_DIM)
    return pl.pallas_call(
        functools.partial(_fourier_kernel, scale=scale),
        out_shape=jax.ShapeDtypeStruct((B, L, D_FOURIER), BF16),
        grid=(B, L // tf),
        in_specs=[pl.BlockSpec((tf, L), lambda b, i: (i, 0)),
                  pl.BlockSpec((tf, L), lambda b, i: (i, 0)),
                  pl.BlockSpec((1, L, 2 * D_FOURIER), lambda b, i: (b, 0, 0))],
        out_specs=pl.BlockSpec((1, tf, D_FOURIER), lambda b, i: (b, i, 0)),
        compiler_params=_cparams(("parallel", "arbitrary")),
        name="fourier_seq",
    )(cl, sl, z)


def _fft_f1_kernel(cs_ref, z_ref, o_ref, *, nc):
    for j in range(nc):
        cs = _dot(cs_ref[...], z_ref[0, j])
        cz = cs[:FFT_RADIX]
        sz = cs[FFT_RADIX:]
        o_ref[0, 0, j] = (cz[:, :D_FOURIER] + sz[:, D_FOURIER:]).astype(BF16)
        o_ref[0, 1, j] = (cz[:, D_FOURIER:] - sz[:, :D_FOURIER]).astype(BF16)


def _fft_f2_kernel(*refs, scale, n_grp):
    t_refs, b_refs, o_ref = refs[:n_grp], refs[n_grp:2 * n_grp], refs[2 * n_grp]
    for j in range(n_grp):
        n_hi = b_refs[j].shape[2]
        blk = b_refs[j][0].reshape(2 * n_hi * 8, D_FOURIER)
        y = _dot(t_refs[j][0], blk) * scale
        o_ref[0, :, j] = y.reshape(n_hi, 8, D_FOURIER).astype(BF16)


def _fourier_seq_fft(zp, cs, tmat):
    B, n_hi, _, _ = zp.shape
    L = n_hi * FFT_RADIX
    nc = 8
    ng = FFT_RADIX // 8
    bc = pl.pallas_call(
        functools.partial(_fft_f1_kernel, nc=nc),
        out_shape=jax.ShapeDtypeStruct((B, 2, n_hi, FFT_RADIX, D_FOURIER), BF16),
        grid=(B, n_hi // nc),
        in_specs=[pl.BlockSpec((2 * FFT_RADIX, FFT_RADIX), lambda b, i: (0, 0)),
                  pl.BlockSpec((1, nc, FFT_RADIX, 2 * D_FOURIER), lambda b, i: (b, i, 0, 0))],
        out_specs=pl.BlockSpec((1, 2, nc, FFT_RADIX, D_FOURIER), lambda b, i: (b, 0, i, 0, 0)),
        compiler_params=_cparams(("parallel", "arbitrary")),
        name="fourier_fft1",
    )(cs, zp)
    scale = 1.0 / math.sqrt(L * FOURIER_GROUP_DIM)
    n_grp = 4
    t_specs = [pl.BlockSpec((1, n_hi * 8, 2 * n_hi * 8), lambda b, g, j=j: (g * n_grp + j, 0, 0)) for j in range(n_grp)]
    b_specs = [pl.BlockSpec((1, 2, n_hi, 8, D_FOURIER), lambda b, g, j=j: (b, 0, 0, g * n_grp + j, 0))
               for j in range(n_grp)]
    y = pl.pallas_call(
        functools.partial(_fft_f2_kernel, scale=scale, n_grp=n_grp),
        out_shape=jax.ShapeDtypeStruct((B, n_hi, ng, 8, D_FOURIER), BF16),
        grid=(B, ng // n_grp),
        in_specs=t_specs + b_specs,
        out_specs=pl.BlockSpec((1, n_hi, n_grp, 8, D_FOURIER), lambda b, g: (b, 0, g, 0, 0)),
        compiler_params=_cparams(("parallel", "arbitrary")),
        name="fourier_fft2",
    )(*([tmat] * n_grp + [bc] * n_grp))
    return y.reshape(B, L, D_FOURIER)


def _filter_kernel(z_ref, w1_ref, b1_ref, w2_ref, b2_ref, w3_ref, b3_ref, w4_ref, f_ref, dl_ref,
                   k_ref, n_ref):
    i = pl.program_id(1)
    z = z_ref[0]
    f = f_ref[...]
    hd = lambda a, w: jnp.dot(a, w, precision=HIGHEST, preferred_element_type=F32)
    h = jnp.sin(f * (hd(z, w1_ref[...]) + b1_ref[...]))
    h = jnp.sin(f * (hd(h, w2_ref[...]) + b2_ref[...]))
    h = jnp.sin(f * (hd(h, w3_ref[...]) + b3_ref[...]))
    h = hd(h, w4_ref[0])
    t = z[:, 0:1]
    valid = z[:, HYENA_EMB_DIM:HYENA_EMB_DIM + 1]
    k = h * jnp.exp(-t * jnp.abs(dl_ref[...])) * valid
    k_ref[0] = k

    @pl.when(i == 0)
    def _():
        n_ref[...] = jnp.zeros_like(n_ref)

    n_ref[0] += jnp.sum(jnp.abs(k), axis=0, keepdims=True)


def _hyena_filter(L, w1, b1, w2, b2, w3, b3, w4, freq):
    pos = np.arange(L, dtype=np.float64)
    bands = np.linspace(1e-4, HYENA_BANDS - 1, HYENA_BANDS)

    def emb(p, valid):
        t = p / max(L - 1, 1)
        ang = (2.0 * math.pi / L) * p[:, None] * bands[None, :]
        pad = np.zeros((L, HYENA_EMB_PAD - HYENA_EMB_DIM - 1))
        return np.concatenate([t[:, None], np.cos(ang), -np.sin(ang), valid[:, None], pad], axis=-1)

    zf = emb(pos, np.ones((L,)))
    zb = emb(np.where(pos > 0, L - pos, 0.0), (pos > 0).astype(np.float64))
    z = _const(np.stack([zf, zb]))
    w1p = jnp.concatenate([w1, jnp.zeros((HYENA_EMB_PAD - HYENA_EMB_DIM, HYENA_WIDTH), F32)], axis=0)
    w4s = jnp.stack([w4[:, :D_HYENA], w4[:, D_HYENA:]])
    deltas = _const(np.linspace(HYENA_MIN_DECAY, HYENA_MAX_DECAY, D_HYENA).reshape(1, D_HYENA))
    tr = _row_tile(L, 512)
    W = HYENA_WIDTH
    full = lambda shape: pl.BlockSpec(shape, lambda s, i: (0,) * len(shape))
    return pl.pallas_call(
        _filter_kernel,
        out_shape=(jax.ShapeDtypeStruct((2, L, D_HYENA), F32), jax.ShapeDtypeStruct((2, 1, D_HYENA), F32)),
        grid=(2, L // tr),
        in_specs=[pl.BlockSpec((1, tr, HYENA_EMB_PAD), lambda s, i: (s, i, 0)),
                  full((HYENA_EMB_PAD, W)), full((1, W)), full((W, W)), full((1, W)),
                  full((W, W)), full((1, W)),
                  pl.BlockSpec((1, W, D_HYENA), lambda s, i: (s, 0, 0)),
                  full((1, W)), full((1, D_HYENA))],
        out_specs=(pl.BlockSpec((1, tr, D_HYENA), lambda s, i: (s, i, 0)),
                   pl.BlockSpec((1, 1, D_HYENA), lambda s, i: (s, 0, 0))),
        compiler_params=_cparams(("arbitrary", "arbitrary")),
        name="hyena_filter",
    )(z, w1p, b1.reshape(1, W), w2, b2.reshape(1, W), w3, b3.reshape(1, W), w4s, freq.reshape(1, W), deltas)


def _hy_fwd_kernel(c_ref, s_ref, u_ref, *rest, raw, tf):
    u = u_ref[0]
    ure = _dot(c_ref[...], u)
    uim = -_dot(s_ref[...], u)
    if raw:
        re_ref, im_ref = rest
        re_ref[0] = ure
        im_ref[0] = uim
        return
    kre_ref, kim_ref, re_ref, im_ref = rest
    kre = kre_ref[...]
    kim = kim_ref[...]
    row = lax.broadcasted_iota(jnp.int32, (tf, 1), 0) + pl.program_id(1) * tf
    packed = row == 0
    a = uim * kim
    re_ref[0] = (ure * kre - jnp.where(packed, 0.0, a)).astype(BF16)
    im_ref[0] = jnp.where(packed, a, ure * kim + uim * kre).astype(BF16)


def _hy_fwd(u, cb, sbf, kre=None, kim=None):
    B, L, C = u.shape
    tf = _row_tile(L, 512)
    raw = kre is None
    mat = pl.BlockSpec((tf, L), lambda b, i: (i, 0))
    tile = pl.BlockSpec((1, tf, C), lambda b, i: (b, i, 0))
    in_specs = [mat, mat, pl.BlockSpec((1, L, C), lambda b, i: (b, 0, 0))]
    args = [cb, sbf, u]
    if not raw:
        in_specs += [pl.BlockSpec((tf, C), lambda b, i: (i, 0))] * 2
        args += [kre, kim]
    dt = F32 if raw else BF16
    return pl.pallas_call(
        functools.partial(_hy_fwd_kernel, raw=raw, tf=tf),
        out_shape=(jax.ShapeDtypeStruct((B, L, C), dt), jax.ShapeDtypeStruct((B, L, C), dt)),
        grid=(B, L // tf),
        in_specs=in_specs,
        out_specs=(tile, tile),
        compiler_params=_cparams(("parallel", "arbitrary")),
        name="hyena_fwd_raw" if raw else "hyena_fwd",
    )(*args)


def _spec_combine_kernel(re_ref, im_ref, n_ref, kre_ref, kim_ref, *, tf, n_fft):
    row = lax.broadcasted_iota(jnp.int32, (tf, 1), 0) + pl.program_id(0) * tf
    sgn = (1 - 2 * (row & 1)).astype(F32)
    wgt = jnp.where(row == 0, 1.0 / n_fft, 2.0 / n_fft)
    s = wgt / (n_ref[0] + n_ref[1])
    kre_ref[...] = (re_ref[0] + sgn * re_ref[1]) * s
    kim_ref[...] = (im_ref[0] + sgn * im_ref[1]) * s


def _spec_combine(re, im, norms):
    _, L, C = re.shape
    tf = _row_tile(L, 512)
    pair = pl.BlockSpec((2, tf, C), lambda i: (0, i, 0))
    tile = pl.BlockSpec((tf, C), lambda i: (i, 0))
    return pl.pallas_call(
        functools.partial(_spec_combine_kernel, tf=tf, n_fft=2 * L),
        out_shape=(jax.ShapeDtypeStruct((L, C), F32), jax.ShapeDtypeStruct((L, C), F32)),
        grid=(L // tf,),
        in_specs=[pair, pair, pl.BlockSpec((2, 1, C), lambda i: (0, 0, 0))],
        out_specs=(tile, tile),
        compiler_params=_cparams(("arbitrary",)),
        name="hyena_spec_combine",
    )(re, im, norms)


def _hy_inv_kernel(c_ref, s_ref, re_ref, im_ref, x0_ref, u_ref, b_ref, o_ref):
    y = _dot(c_ref[...], re_ref[0]) - _dot(s_ref[...], im_ref[0])
    o_ref[0] = (x0_ref[0] * (y + u_ref[0].astype(F32) * b_ref[...])).astype(BF16)


def _hy_inv(yre, yim, x0, u, bias, cb, sbi):
    B, L, C = u.shape
    tt = _row_tile(L, 512)
    mat = pl.BlockSpec((tt, L), lambda b, i: (i, 0))
    whole = pl.BlockSpec((1, L, C), lambda b, i: (b, 0, 0))
    tile = pl.BlockSpec((1, tt, C), lambda b, i: (b, i, 0))
    return pl.pallas_call(
        _hy_inv_kernel,
        out_shape=jax.ShapeDtypeStruct((B, L, C), BF16),
        grid=(B, L // tt),
        in_specs=[mat, mat, whole, whole, tile, tile, pl.BlockSpec((1, C), lambda b, i: (0, 0))],
        out_specs=tile,
        compiler_params=_cparams(("parallel", "arbitrary")),
        name="hyena_inv",
    )(cb, sbi, yre, yim, x0, u, bias.reshape(1, C))


def _fft_a_kernel(t_ref, x_ref, o_ref):
    r2, rows, c = x_ref.shape[1:]
    for j in range(rows // FFT_KRON):
        grp = slice(j * FFT_KRON, (j + 1) * FFT_KRON)
        y = _dot(t_ref[...], x_ref[0, :, grp, :].reshape(r2 * FFT_KRON, c))
        o_ref[0, :, :, grp, :] = y.reshape(2, y.shape[0] // (2 * FFT_KRON), FFT_KRON, c).astype(BF16)


def _hy_fft_a(x4, t1):
    Bx, R, _, C = x4.shape
    n_hi = t1.shape[0] // (2 * FFT_KRON)
    return pl.pallas_call(
        _fft_a_kernel,
        out_shape=jax.ShapeDtypeStruct((Bx, 2, n_hi, FFT_RADIX, C), BF16),
        grid=(Bx, FFT_RADIX // FFT_ROWS),
        in_specs=[pl.BlockSpec(t1.shape, lambda b, i: (0, 0)),
                  pl.BlockSpec((1, R, FFT_ROWS, C), lambda b, i: (b, 0, i, 0))],
        out_specs=pl.BlockSpec((1, 2, n_hi, FFT_ROWS, C), lambda b, i: (b, 0, 0, i, 0)),
        compiler_params=_cparams(("parallel", "arbitrary")),
        name="hyena_fft_a",
    )(t1, x4)


def _fft_b_kernel(gf_ref, a_ref, *rest, raw, n_fft):
    nb, _, kb = a_ref.shape[:3]
    if raw:
        n_ref, o_ref = rest
        s = 1.0 / (n_fft * (n_ref[0] + n_ref[1]))
    else:
        gi_ref, k_ref, o_ref = rest
    for kk in range(kb):
        gf = gf_ref[kk]
        if not raw:
            gi = gi_ref[kk]
            kr = k_ref[0, 0, kk]
            ki = k_ref[0, 1, kk]
        for b in range(nb):
            a = a_ref[b, :, kk].reshape(2 * FFT_RADIX, D_HYENA)
            x = _dot(gf, a)
            if raw:
                o_ref[b, :, kk] = (x * s).reshape(2, FFT_RADIX, D_HYENA)
                continue
            xr = x[:FFT_RADIX]
            xi = x[FFT_RADIX:]
            y = jnp.concatenate([xr * kr - xi * ki, xr * ki + xi * kr], axis=0).astype(BF16)
            o_ref[b, :, kk] = _dot(gi, y).reshape(2, FFT_RADIX, D_HYENA).astype(BF16)


def _hy_fft_b(a5, gf, gi=None, kspec=None, norms=None):
    Bx, _, n_hi, _, C = a5.shape
    raw = kspec is None
    kb = 4 if raw else 2
    blk = pl.BlockSpec((Bx, 2, kb, FFT_RADIX, C), lambda k: (0, 0, k, 0, 0))
    tab = pl.BlockSpec((kb, 2 * FFT_RADIX, 2 * FFT_RADIX), lambda k: (k, 0, 0))
    if raw:
        in_specs = [tab, blk, pl.BlockSpec((2, 1, C), lambda k: (0, 0, 0))]
        args = (gf, a5, norms)
    else:
        kblk = pl.BlockSpec((1, 2, kb, FFT_RADIX, C), lambda k: (0, 0, k, 0, 0))
        in_specs = [tab, blk, tab, kblk]
        args = (gf, a5, gi, kspec)
    return pl.pallas_call(
        functools.partial(_fft_b_kernel, raw=raw, n_fft=n_hi * FFT_RADIX),
        out_shape=jax.ShapeDtypeStruct(a5.shape, F32 if raw else BF16),
        grid=(n_hi // kb,),
        in_specs=in_specs,
        out_specs=blk,
        compiler_params=_cparams(("arbitrary",)),
        name="hyena_fft_spec" if raw else "hyena_fft_b",
    )(*args)


def _fft_c_kernel(t_ref, v_ref, x0_ref, u_ref, b_ref, o_ref):
    _, n_hi, rows, c = v_ref.shape[1:]
    for j in range(rows // FFT_KRON):
        grp = slice(j * FFT_KRON, (j + 1) * FFT_KRON)
        y = _dot(t_ref[...], v_ref[0, :, :, grp, :].reshape(2 * n_hi * FFT_KRON, c))
        y = y.reshape(y.shape[0] // FFT_KRON, FFT_KRON, c)
        o_ref[0, :, grp, :] = (x0_ref[0, :, grp, :] * (y + u_ref[0, :, grp, :].astype(F32) * b_ref[...])).astype(BF16)


def _hy_fft_c(v5, t3, x0, u, bias):
    B, _, n_hi, _, C = v5.shape
    R = t3.shape[0] // FFT_KRON
    tile = pl.BlockSpec((1, R, FFT_ROWS, C), lambda b, i: (b, 0, i, 0))
    return pl.pallas_call(
        _fft_c_kernel,
        out_shape=jax.ShapeDtypeStruct((B, R, FFT_RADIX, C), BF16),
        grid=(B, FFT_RADIX // FFT_ROWS),
        in_specs=[pl.BlockSpec(t3.shape, lambda b, i: (0, 0)),
                  pl.BlockSpec((1, 2, n_hi, FFT_ROWS, C), lambda b, i: (b, 0, 0, i, 0)),
                  tile, tile, pl.BlockSpec((1, C), lambda b, i: (0, 0))],
        out_specs=tile,
        compiler_params=_cparams(("parallel", "arbitrary")),
        name="hyena_fft_c",
    )(t3, v5, x0, u, bias.reshape(1, C))


def _ffn_kernel(xc_ref, xp_ref, xn_ref, g1_ref, wo_ref, g_ref, sh_ref, sc_ref, gate_ref, wu_ref, cw_ref, cb_ref,
                wd_ref, *rest, tm, n_y):
    y_refs = rest[:3 * n_y]
    o_ref, yext_ref, hn_ref, act_ref = rest[3 * n_y:]
    i = pl.program_id(1)
    nt = pl.num_programs(1)
    off = 0
    for k in range(n_y):
        yc_ref, yp_ref, yn_ref = y_refs[3 * k:3 * k + 3]
        c = yc_ref.shape[-1]
        yext_ref[0:HALO, off:off + c] = yp_ref[0].astype(BF16)
        yext_ref[HALO:HALO + tm, off:off + c] = yc_ref[0].astype(BF16)
        yext_ref[HALO + tm:, off:off + c] = yn_ref[0].astype(BF16)
        off += c
    mix = g1_ref[0] * _dot(yext_ref[...], wo_ref[...])
    x1 = xc_ref[0] + mix[HALO:HALO + tm]
    _fill_hn_rows(hn_ref, x1, xp_ref[0] + mix[:HALO], xn_ref[0] + mix[HALO + tm:], g_ref[...], sh_ref[0], sc_ref[0],
                  tm, i, nt)
    for c in range(0, D_FF, FFN_CHUNK):
        cols = slice(c, c + FFN_CHUNK)
        gx = _dot(hn_ref[...], wu_ref[:, cols])
        vx = _dot(hn_ref[HALO:HALO + tm, :], wu_ref[:, D_FF + c:D_FF + c + FFN_CHUNK])
        cv = _dwconv3_rows(gx, cw_ref[:, cols], tm) + cb_ref[:, cols]
        act_ref[:, cols] = (_silu(cv) * vx).astype(BF16)
    o_ref[0] = x1 + gate_ref[0] * _dot(act_ref[...], wd_ref[...])


def _mix_ffn(x, ys, w_o, gate1, g, shift, scale, gate2, w_up, conv_w, conv_b, w_down):
    B, L, D = x.shape
    tm = _row_tile(L, 1024)
    cur, prev, nxt = _halo_specs(tm, D, L)
    const = lambda shape: pl.BlockSpec(shape, lambda b, i: (0,) * len(shape), pipeline_mode=pl.Buffered(1))
    y_specs, y_args = [], []
    for y in ys:
        y_specs += list(_halo_specs(tm, y.shape[-1], L))
        y_args += [y, y, y]
    return pl.pallas_call(
        functools.partial(_ffn_kernel, tm=tm, n_y=len(ys)),
        out_shape=jax.ShapeDtypeStruct((B, L, D), F32),
        grid=(B, L // tm),
        in_specs=[cur, prev, nxt, _mod_spec(gate1), const(w_o.shape), const((1, D)), _mod_spec(shift),
                  _mod_spec(scale), _mod_spec(gate2), const((D, 2 * D_FF)), const((3, D_FF)), const((1, D_FF)),
                  const((D_FF, D))] + y_specs,
        out_specs=pl.BlockSpec((1, tm, D), lambda b, i: (b, i, 0)),
        scratch_shapes=[pltpu.VMEM((tm + 2 * HALO, w_o.shape[0]), BF16), pltpu.VMEM((tm + 2 * HALO, D), BF16),
                        pltpu.VMEM((tm, D_FF), BF16)],
        compiler_params=_cparams(("parallel", "arbitrary")),
        name="mix_ffn",
    )(x, x, x, gate1, w_o, g.reshape(1, D), shift, scale, gate2, w_up, conv_w, conv_b.reshape(1, D_FF), w_down,
      *y_args)


def _rms(v, g, n):
    return v * lax.rsqrt(jnp.sum(v * v, axis=-1, keepdims=True) * (1.0 / n) + EPS) * g


def _rope_gain_tables(gain, cos, sa, sb):
    rot = slice(NOPE, QK_DIM)
    gr = gain[rot]
    return gain[:NOPE], gr * cos[rot], pltpu.roll(gr, 8, 0) * sa[rot], pltpu.roll(gr, ROPE - 8, 0) * sb[rot]


def _rope_t(yr, gc, gsa, gsb):
    return yr * gc + pltpu.roll(yr, 8, 0) * gsa + pltpu.roll(yr, ROPE - 8, 0) * gsb


def _sumsq(x):
    return jnp.sum(x * x, axis=0, keepdims=True)


def _mla_proj_kernel(x_ref, g_ref, sh_ref, sc_ref, win_ref, qan_ref, wuq_ref, qn_ref, kvn_ref, wuk_ref,
                     wuv_ref, kn_ref, cos_ref, sa_ref, sb_ref, *outs, with_q):
    hn = _norm_mod(x_ref[0], g_ref[...], sh_ref[0], sc_ref[0]).astype(BF16)
    a = _dot(hn, win_ref[...])
    tm = a.shape[0]
    cos = cos_ref[...]
    sa = sa_ref[...]
    sb = sb_ref[...]
    pad = jnp.zeros((HEAD_PAD - QK_DIM, tm), ATTN_DTYPE)
    if with_q:
        qt_ref, k_ref, vt_ref = outs
        qat = _rms(a[:, :Q_LORA], qan_ref[...], Q_LORA).T.astype(BF16)
        qft = _dot(wuq_ref[...], qat)
        gn, gc, gsa, gsb = _rope_gain_tables(qn_ref[...], cos, sa, sb)
        for h in range(MLA_HEADS):
            base = h * HEAD_PAD
            xn = qft[base:base + NOPE]
            xr = qft[base + NOPE:base + QK_DIM]
            r = lax.rsqrt((_sumsq(xn) + _sumsq(xr)) * (1.0 / QK_DIM) + EPS)
            qt_ref[0, base:base + NOPE, :] = (xn * r * gn).astype(ATTN_DTYPE)
            qt_ref[0, base + NOPE:base + QK_DIM, :] = _rope_t(xr * r, gc, gsa, gsb).astype(ATTN_DTYPE)
            qt_ref[0, base + QK_DIM:base + HEAD_PAD, :] = pad
    else:
        k_ref, vt_ref = outs
    ckvt = _rms(a[:, Q_LORA:Q_LORA + KV_LORA], kvn_ref[...], KV_LORA).T.astype(BF16)
    kft = _dot(wuk_ref[...], ckvt)
    vt_ref[0] = _dot(wuv_ref[...], ckvt).astype(ATTN_DTYPE)
    kpe = a[:, Q_LORA + KV_LORA:].T[:ROPE]
    gn, gc, gsa, gsb = _rope_gain_tables(kn_ref[...], cos, sa, sb)
    kpe_rot = _rope_t(kpe, gc, gsa, gsb)
    kpe_ss = _sumsq(kpe)
    for h in range(MLA_HEADS):
        base = h * HEAD_PAD
        xn = kft[base:base + NOPE]
        r = lax.rsqrt((_sumsq(xn) + kpe_ss) * (1.0 / QK_DIM) + EPS)
        kt = jnp.concatenate([xn * r * gn, kpe_rot * r, jnp.zeros((HEAD_PAD - QK_DIM, tm), F32)], axis=0)
        k_ref[0, :, base:base + HEAD_PAD] = kt.T.astype(ATTN_DTYPE)


def _mla_proj(x, g, shift, scale, wts, rope_tabs, with_q):
    B, L, D = x.shape
    tm = _row_tile(L, 512)
    win, qan, wuq, qn, kvn, wuk, wuv, kn = wts
    cos, sa, sb = rope_tabs
    HP = MLA_HEADS * HEAD_PAD
    HV = MLA_HEADS * V_DIM
    full = lambda a: pl.BlockSpec(a.shape, lambda b, i: (0,) * a.ndim)
    tab = pl.BlockSpec((HEAD_PAD, tm), lambda b, i: (0, i))
    out_shape = [jax.ShapeDtypeStruct((B, L, HP), ATTN_DTYPE), jax.ShapeDtypeStruct((B, HV, L), ATTN_DTYPE)]
    out_specs = [pl.BlockSpec((1, tm, HP), lambda b, i: (b, i, 0)),
                 pl.BlockSpec((1, HV, tm), lambda b, i: (b, 0, i))]
    if with_q:
        out_shape = [jax.ShapeDtypeStruct((B, HP, L), ATTN_DTYPE)] + out_shape
        out_specs = [pl.BlockSpec((1, HP, tm), lambda b, i: (b, 0, i))] + out_specs
    g2 = g.reshape(1, D)
    gain_tab = lambda v: jnp.broadcast_to(v.reshape(HEAD_PAD, 1), (HEAD_PAD, tm))
    qn = gain_tab(qn)
    kn = gain_tab(kn)
    return pl.pallas_call(
        functools.partial(_mla_proj_kernel, with_q=with_q),
        out_shape=tuple(out_shape),
        grid=(B, L // tm),
        in_specs=[pl.BlockSpec((1, tm, D), lambda b, i: (b, i, 0)), full(g2), _mod_spec(shift), _mod_spec(scale),
                  full(win), full(qan), full(wuq), full(qn), full(kvn), full(wuk), full(wuv), full(kn),
                  tab, tab, tab],
        out_specs=tuple(out_specs),
        compiler_params=_cparams(("parallel", "arbitrary")),
        name="mla_proj_q" if with_q else "mla_proj_kv",
    )(x, g2, shift, scale, win, qan, wuq, qn, kvn, wuk, wuv, kn, cos, sa, sb)


def _attn_kernel(qt_ref, kx_ref, kc_ref, vtx_ref, vtc_ref, o_ref, ax_ref, ac_ref, am_ref, bx_ref, bc_ref, bm_ref,
                 px_ref, pc_ref):
    t = pl.program_id(0)
    io = (qt_ref, kx_ref, kc_ref, vtx_ref, vtc_ref, o_ref, px_ref, pc_ref)

    @pl.when(t == 0)
    def _():
        bx_ref[...] = jnp.zeros(bx_ref.shape, BF16)
        bc_ref[...] = jnp.zeros(bc_ref.shape, BF16)
        bm_ref[...] = jnp.zeros(bm_ref.shape, BF16)

    @pl.when(t % 2 == 0)
    def _():
        _attn_step(*io, (ax_ref, ac_ref, am_ref), (bx_ref, bc_ref, bm_ref))

    @pl.when(t % 2 == 1)
    def _():
        _attn_step(*io, (bx_ref, bc_ref, bm_ref), (ax_ref, ac_ref, am_ref))


def _attn_step(qt_ref, kx_ref, kc_ref, vtx_ref, vtc_ref, o_ref, px_ref, pc_ref, parked, ready):
    wx_ref, wc_ref, wm_ref = parked
    rx_ref, rc_ref, rm_ref = ready
    tq = qt_ref.shape[2]
    off = rm_ref[...][None]

    def probs(s_ref):
        n = s_ref.shape[0]
        s = s_ref[...].reshape(n // 16, 16, 2 * tq)
        return jnp.exp2(s - off).reshape(n, 2 * tq).astype(ATTN_DTYPE)

    px_ref[...] = probs(rx_ref)
    pc_ref[...] = probs(rc_ref)

    q0 = qt_ref[0, :HEAD_PAD, :]
    q1 = qt_ref[0, HEAD_PAD:, :]
    zq = jnp.zeros_like(q0)
    qbd = jnp.concatenate([jnp.concatenate([q0, zq], axis=1), jnp.concatenate([zq, q1], axis=1)], axis=0)
    nx = _dot(kx_ref[0], qbd).astype(BF16)
    nc = _dot(kc_ref[0], qbd).astype(BF16)
    wx_ref[...] = nx
    wc_ref[...] = nc
    m = jnp.maximum(jnp.max(nx, axis=0, keepdims=True), jnp.max(nc, axis=0, keepdims=True)).astype(F32)
    wm_ref[...] = jnp.broadcast_to((m - ATTN_P_SHIFT).astype(BF16), wm_ref.shape)

    vtx = vtx_ref[0]
    vtc = vtc_ref[0]
    halves = []
    for j in range(2):
        own = slice(j * V_DIM, (j + 1) * V_DIM)
        oth = slice((1 - j) * V_DIM, (2 - j) * V_DIM)
        cols = slice(j * tq, (j + 1) * tq)
        parts_x = [vtx[own], jnp.ones_like(vtx[oth])]
        parts_c = [vtc[own], jnp.ones_like(vtc[oth])]
        if j == 1:
            parts_x.reverse()
            parts_c.reverse()
        r = (_dot(jnp.concatenate(parts_x, axis=0), px_ref[:, cols])
             + _dot(jnp.concatenate(parts_c, axis=0), pc_ref[:, cols]))
        halves.append(r[own] * (1.0 / r[oth][0:1]))
    o_ref[0] = jnp.concatenate(halves, axis=0).T.astype(BF16)


def _attention(qt, kx, kc, vtx, vtc):
    B, HP, L = qt.shape
    Lc = kc.shape[1]
    tq = _row_tile(L, 1024)
    nhp = MLA_HEADS // 2
    nq = L // tq
    n_tiles = B * nhp * nq

    def tile(t):
        t = jnp.clip(t, 0, n_tiles - 1)
        return t // (nhp * nq), (t // nq) % nhp, t % nq

    def logits_side(f):
        return lambda t: f(*tile(t))

    def value_side(f):
        return lambda t: f(*tile(t - 1))

    return pl.pallas_call(
        _attn_kernel,
        out_shape=jax.ShapeDtypeStruct((B, L, MLA_HEADS * V_DIM), BF16),
        grid=(n_tiles + 1,),
        in_specs=[pl.BlockSpec((1, 2 * HEAD_PAD, tq), logits_side(lambda b, h, i: (b, h, i))),
                  pl.BlockSpec((1, L, 2 * HEAD_PAD), logits_side(lambda b, h, i: (b, 0, h))),
                  pl.BlockSpec((1, Lc, 2 * HEAD_PAD), logits_side(lambda b, h, i: (b, 0, h))),
                  pl.BlockSpec((1, 2 * V_DIM, L), value_side(lambda b, h, i: (b, h, 0))),
                  pl.BlockSpec((1, 2 * V_DIM, Lc), value_side(lambda b, h, i: (b, h, 0)))],
        out_specs=pl.BlockSpec((1, tq, 2 * V_DIM), value_side(lambda b, h, i: (b, i, h))),
        scratch_shapes=[pltpu.VMEM((L, 2 * tq), BF16), pltpu.VMEM((Lc, 2 * tq), BF16), pltpu.VMEM((16, 2 * tq), BF16)] * 2
                       + [pltpu.VMEM((L, 2 * tq), ATTN_DTYPE), pltpu.VMEM((Lc, 2 * tq), ATTN_DTYPE)],
        compiler_params=_cparams(("arbitrary",)),
        name="mla_attention",
    )(qt, kx, kc, vtx, vtc)


def _const(a, dtype=F32):
    return jnp.asarray(np.asarray(a, np.float32)).astype(dtype)


def _angle(num, period):
    return (num % period).astype(np.float64) * (2.0 * math.pi / period)


def _trig_matrix(n, period):
    r = np.arange(n, dtype=np.int64)
    a = _angle(r[:, None] * r[None, :], period)
    return np.cos(a), np.sin(a)


def _dft_tables(L):
    cl, sl = _trig_matrix(L, L)
    cb, sb = _trig_matrix(L, 2 * L)
    idx = np.arange(L)
    alt = (1 - 2 * (idx & 1)).astype(np.float64)
    sbf = np.where(idx[:, None] == 0, -alt[None, :], sb)
    sbi = np.where(idx[None, :] == 0, -alt[:, None], sb)
    return tuple(_const(m, BF16) for m in (cl, sl, cb, sbf, sbi))


def _fourier_fft_tables(L):
    n_hi = L // FFT_RADIX
    i = np.arange(FFT_RADIX, dtype=np.int64)
    a = _angle(i[:, None] * i[None, :], FFT_RADIX)
    cs = np.concatenate([np.cos(a), np.sin(a)], axis=0)
    ng = FFT_RADIX // 8
    g = np.arange(ng, dtype=np.int64)[:, None, None, None]
    k1 = np.arange(n_hi, dtype=np.int64)[None, :, None, None]
    j = np.arange(8, dtype=np.int64)[None, None, :, None]
    n1 = np.arange(n_hi, dtype=np.int64)[None, None, None, :]
    ang = _angle(n1 * (FFT_RADIX * k1 + 8 * g + j), L)
    eye = np.eye(8)
    blocks = [t[:, :, :, :, None] * eye[None, None, :, None, :] for t in (np.cos(ang), np.sin(ang))]
    tmat = np.stack(blocks, axis=3)
    return _const(cs, BF16), _const(tmat.reshape(ng, n_hi * 8, 2 * n_hi * 8), BF16)


def _hyena_fft_tables(L):
    n_fft = 2 * L
    n_hi = n_fft // FFT_RADIX
    k2 = np.arange(n_hi, dtype=np.int64)
    a1 = _angle(k2[:, None] * k2[None, :], n_hi)
    c1, s1 = np.cos(a1), np.sin(a1)
    eye = np.eye(FFT_KRON)
    kron = lambda m: (m[:, None, :, None] * eye[None, :, None, :]).reshape(m.shape[0] * FFT_KRON, m.shape[1] * FFT_KRON)
    t1 = _const(np.concatenate([kron(c1), kron(-s1)], axis=0), BF16)
    t1_half = t1[:, :n_hi // 2 * FFT_KRON]
    t3 = t1_half.T
    r = np.arange(FFT_RADIX, dtype=np.int64)
    k = k2[:, None, None] + n_hi * r[None, :, None]
    a2 = _angle(k * r[None, None, :], n_fft)
    c2, s2 = _const(np.cos(a2), BF16), _const(np.sin(a2), BF16)
    gf = jnp.concatenate([jnp.concatenate([c2, s2], axis=2), jnp.concatenate([-s2, c2], axis=2)], axis=1)
    c2t, s2t = jnp.swapaxes(c2, 1, 2), jnp.swapaxes(s2, 1, 2)
    gi = jnp.concatenate([jnp.concatenate([c2t, -s2t], axis=2), jnp.concatenate([s2t, c2t], axis=2)], axis=1)
    return t1, t1_half, t3, gf, gi


def _group_dft_matrix():
    g = FOURIER_GROUP_DIM
    ng = D_FOURIER // g
    c, s = _trig_matrix(g, g)
    eye = np.eye(ng)
    return _const(np.concatenate([np.kron(eye, c), -np.kron(eye, s)], axis=1), BF16)


def _rope_tables(L):
    pos = np.arange(L)
    rows = (pos // GRID_W).astype(np.float64)
    cols = (pos % GRID_W).astype(np.float64)
    nf = ROPE // 4
    inv_freq = ROPE_THETA ** (-np.arange(nf, dtype=np.float64) / nf)
    ar = rows[:, None] * inv_freq[None, :]
    ac = cols[:, None] * inv_freq[None, :]
    one = np.ones((L, NOPE))
    z_n = np.zeros((L, NOPE))
    z_f = np.zeros((L, nf))
    tail = np.zeros((L, HEAD_PAD - QK_DIM))
    cos = np.concatenate([one, np.cos(ar), np.cos(ar), np.cos(ac), np.cos(ac), tail], axis=1)
    sa = np.concatenate([z_n, z_f, np.sin(ar), z_f, np.sin(ac), tail], axis=1)
    sb = np.concatenate([z_n, -np.sin(ar), z_f, -np.sin(ac), z_f, tail], axis=1)
    return _const(cos.T), _const(sa.T), _const(sb.T)


def _identity_rope_tables(L):
    return jnp.ones((HEAD_PAD, L), F32), jnp.zeros((HEAD_PAD, L), F32), jnp.zeros((HEAD_PAD, L), F32)


def _pad_heads(w, per_head, keep):
    K = w.shape[0]
    w = w.reshape(K, MLA_HEADS, per_head)[:, :, :keep]
    w = jnp.pad(w, ((0, 0), (0, 0), (0, HEAD_PAD - keep)))
    return w.reshape(K, MLA_HEADS * HEAD_PAD)


def _pad_lanes(v, n):
    return jnp.pad(v, (0, n - v.shape[0])).reshape(1, n)


def _fourier_hyena_mixer(x, g, shift, scale, w_in, conv_w, conv_b, gmat, taps, norms, hy_bias):
    B, L, _ = x.shape
    if (L // FFT_RADIX) % 16 == 0:
        z, u, x0 = _fh_in(x, g, shift, scale, w_in, conv_w, conv_b, gmat, n_hi=L // FFT_RADIX)
        cs, tmat = _fourier_fft_tables(L)
        y_f = _fourier_seq_fft(z, cs, tmat)
        t1, t1_half, t3, gf, gi = _hyena_fft_tables(L)
        n_hi = 2 * L // FFT_RADIX
        rows = lambda a, r: a.reshape(a.shape[0], r, FFT_RADIX, D_HYENA)
        kspec = _hy_fft_b(_hy_fft_a(rows(taps.astype(BF16).reshape(1, 2 * L, D_HYENA), n_hi), t1), gf, norms=norms)
        v = _hy_fft_b(_hy_fft_a(rows(u, n_hi // 2), t1_half), gf, gi, kspec)
        y_h = _hy_fft_c(v, t3, rows(x0, n_hi // 2), rows(u, n_hi // 2), hy_bias)
        return y_f, y_h.reshape(B, L, D_HYENA)
    z, u, x0 = _fh_in(x, g, shift, scale, w_in, conv_w, conv_b, gmat)
    cl, sl, cb, sbf, sbi = _dft_tables(L)
    k_re, k_im = _hy_fwd(taps.astype(BF16), cb, sbf)
    kre, kim = _spec_combine(k_re, k_im, norms)
    y_f = _fourier_seq(z, cl, sl)
    yre, yim = _hy_fwd(u, cb, sbf, kre, kim)
    y_h = _hy_inv(yre, yim, x0, u, hy_bias, cb, sbi)
    return y_f, y_h


def kernel(x, c, ctx, c_ctx, norm1, norm2, w_mod, b_mod, ffn_w_up, ffn_conv_w, ffn_conv_b, ffn_w_down,
           fh_w_in, fh_w_out, hy_conv_w, hy_conv_b, hy_filt_w1, hy_filt_b1, hy_filt_w2, hy_filt_b2,
           hy_filt_w3, hy_filt_b3, hy_filt_w4, hy_freq, hy_bias, mla_w_in, mla_q_a_norm, mla_w_uq,
           mla_kv_a_norm, mla_w_ukv, mla_q_norm, mla_k_norm, mla_w_o):
    B, L, D = x.shape
    Lc = ctx.shape[1]
    bf = lambda a: a.astype(BF16)

    n_cond = -(-(B + 1) // 8) * 8
    cond = jnp.concatenate([c, c_ctx[None, :], jnp.zeros((n_cond - B - 1, D), F32)], axis=0)

    def mods(i):
        m = _adaln(cond, w_mod, b_mod, i)
        mx = [m[:B, j * D:(j + 1) * D].reshape(B, 1, D) for j in range(6)]
        mc = [m[B:B + 1, j * D:(j + 1) * D].reshape(1, 1, D) for j in range(6)]
        return mx, mc

    (sh1, sc1, g1, sh2, sc2, g2), (csh1, csc1, cg1, csh2, csc2, cg2) = mods(0)
    gmat = _group_dft_matrix()
    w_in0 = bf(fh_w_in[0])
    w_out0 = bf(fh_w_out[0])
    ffn0 = (bf(ffn_w_up[0]), ffn_conv_w[0], ffn_conv_b[0], bf(ffn_w_down[0]))
    filt = (hy_filt_w1[0], hy_filt_b1[0], hy_filt_w2[0], hy_filt_b2[0], hy_filt_w3[0], hy_filt_b3[0],
            hy_filt_w4[0], hy_freq[0])

    def mixer_layer(h, n_seq, m1, m2):
        s1, c1, gt1 = m1
        s2, c2, gt2 = m2
        taps, norms = _hyena_filter(n_seq, *filt)
        y_f, y_h = _fourier_hyena_mixer(h, norm1[0], s1, c1, w_in0, hy_conv_w[0], hy_conv_b[0], gmat, taps,
                                        norms, hy_bias[0])
        return _mix_ffn(h, [y_f, y_h], w_out0, gt1, norm2[0], s2, c2, gt2, *ffn0)

    x = mixer_layer(x, L, (sh1, sc1, g1), (sh2, sc2, g2))
    ctx = mixer_layer(ctx, Lc, (csh1, csc1, cg1), (csh2, csc2, cg2))

    (sh1, sc1, g1, sh2, sc2, g2), (csh1, csc1, _, _, _, _) = mods(1)
    w_in1 = bf(jnp.pad(mla_w_in[0], ((0, 0), (0, MLA_IN_PAD - mla_w_in.shape[2]))))
    wuq = bf(_pad_heads(mla_w_uq[0], QK_DIM, QK_DIM).T)
    wuk = bf(_pad_heads(mla_w_ukv[0], NOPE + V_DIM, NOPE).T)
    wuv = bf(mla_w_ukv[0].reshape(KV_LORA, MLA_HEADS, NOPE + V_DIM)[:, :, NOPE:].reshape(KV_LORA, MLA_HEADS * V_DIM).T)
    half_scale = math.sqrt(QK_DIM ** -0.5 * math.log2(math.e))
    wts = (w_in1, mla_q_a_norm[0].reshape(1, Q_LORA), wuq, _pad_lanes(mla_q_norm[0] * half_scale, HEAD_PAD),
           mla_kv_a_norm[0].reshape(1, KV_LORA), wuk, wuv, _pad_lanes(mla_k_norm[0] * half_scale, HEAD_PAD))
    qt, kx, vtx = _mla_proj(x, norm1[1], sh1, sc1, wts, _rope_tables(L), True)
    kc, vtc = _mla_proj(ctx, norm1[1], csh1, csc1, wts, _identity_rope_tables(Lc), False)
    o = _attention(qt, kx, kc, vtx, vtc)
    ffn1 = (bf(ffn_w_up[1]), ffn_conv_w[1], ffn_conv_b[1], bf(ffn_w_down[1]))
    return _mix_ffn(x, [o], bf(mla_w_o[0]), g1, norm2[1], sh2, sc2, g2, *ffn1)
```

```python
import functools
import math

import jax
import jax.numpy as jnp
import numpy as np
from jax import lax
from jax.experimental import pallas as pl
from jax.experimental.pallas import tpu as pltpu

F32 = jnp.float32
BF16 = jnp.bfloat16
HIGHEST = lax.Precision.HIGHEST

EPS = 1e-6
D_MODEL = 1024
D_FF = 2816
D_FOURIER = 512
FOURIER_GROUP_DIM = 128
D_HYENA = 512
HYENA_EMB_DIM = 33
HYENA_EMB_PAD = 64
HYENA_BANDS = 16
HYENA_WIDTH = 64
HYENA_MIN_DECAY = math.log(1e-2) / 0.3
HYENA_MAX_DECAY = math.log(1e-2) / 1.5
MLA_HEADS = 16
Q_LORA = 256
KV_LORA = 128
NOPE = 64
ROPE = 32
QK_DIM = NOPE + ROPE
V_DIM = 64
HEAD_PAD = 128
MLA_IN_PAD = 512
GRID_W = 64
ROPE_THETA = 10000.0

ATTN_DTYPE = jnp.float8_e4m3fn
ATTN_P_SHIFT = 8.0
FFN_CHUNK = 256
FFT_RADIX = 128
FFT_KRON = 16
FFT_ROWS = 2 * FFT_KRON
HALO = 16
VMEM_LIMIT = 56 * 1024 * 1024


def _cparams(sem, flags=None):
    return pltpu.CompilerParams(dimension_semantics=sem, vmem_limit_bytes=VMEM_LIMIT, flags=flags)


def _dot(a, b):
    return jnp.dot(a, b, preferred_element_type=F32)


def _norm_mod(x, g, shift, scale):
    ms = jnp.mean(x * x, axis=-1, keepdims=True)
    return (x * lax.rsqrt(ms + EPS) * g) * (1.0 + scale) + shift


def _silu(x):
    return x * (1.0 / (1.0 + jnp.exp(-x)))


def _row_tile(L, want):
    t = min(L, want)
    assert L % t == 0 and t % HALO == 0
    return t


def _mod_kernel(c_ref, w_ref, b_ref, o_ref):
    s = _silu(c_ref[...])
    o_ref[...] = jnp.dot(s, w_ref[0], precision=HIGHEST, preferred_element_type=F32) + b_ref[0]


def _adaln(cond, w_mod, b_mod, layer):
    R, D = cond.shape
    n = w_mod.shape[2]
    tn = 768
    return pl.pallas_call(
        _mod_kernel,
        out_shape=jax.ShapeDtypeStruct((R, n), F32),
        grid=(n // tn,),
        in_specs=[pl.BlockSpec((R, D), lambda j: (0, 0)),
                  pl.BlockSpec((1, D, tn), lambda j: (layer, 0, j)),
                  pl.BlockSpec((1, 1, tn), lambda j: (layer, 0, j))],
        out_specs=pl.BlockSpec((R, tn), lambda j: (0, j)),
        compiler_params=_cparams(("arbitrary",)),
        name="adaln_mod",
    )(cond, w_mod, b_mod.reshape(b_mod.shape[0], 1, n))


def _halo_specs(tm, D, L):
    nb = L // HALO
    per = tm // HALO
    cur = pl.BlockSpec((1, tm, D), lambda b, i, *_: (b, i, 0))
    prev = pl.BlockSpec((1, HALO, D), lambda b, i, *_: (b, jnp.maximum(i * per - 1, 0), 0))
    nxt = pl.BlockSpec((1, HALO, D), lambda b, i, *_: (b, jnp.minimum((i + 1) * per, nb - 1), 0))
    return cur, prev, nxt


def _mod_spec(arr):
    D = arr.shape[-1]
    if arr.shape[0] == 1:
        return pl.BlockSpec((1, 1, D), lambda b, *_: (0, 0, 0))
    return pl.BlockSpec((1, 1, D), lambda b, *_: (b, 0, 0))


def _fill_hn_rows(hn_ref, xc, xp, xn, g, shift, scale, tm, i, nt):
    hp = _norm_mod(xp, g, shift, scale)
    hn_ref[0:HALO, :] = jnp.where(i > 0, hp, 0.0).astype(BF16)
    hn_ref[HALO:HALO + tm, :] = _norm_mod(xc, g, shift, scale).astype(BF16)
    hx = _norm_mod(xn, g, shift, scale)
    hn_ref[HALO + tm:, :] = jnp.where(i < nt - 1, hx, 0.0).astype(BF16)


def _fill_hn(hn_ref, xc_ref, xp_ref, xn_ref, g, shift, scale, tm, i, nt):
    _fill_hn_rows(hn_ref, xc_ref[0], xp_ref[0], xn_ref[0], g, shift, scale, tm, i, nt)


def _dwconv3_rows(p, w, tm):
    n = tm + 2 * HALO
    up = pltpu.roll(p, 1, 0)
    dn = pltpu.roll(p, n - 1, 0)
    c = up * w[0:1] + p * w[1:2] + dn * w[2:3]
    return c[HALO:HALO + tm]


def _fh_in_kernel(xc_ref, xp_ref, xn_ref, g_ref, sh_ref, sc_ref, w_ref, cw_ref, cb_ref, gm_ref, *rest, tm, n_hi):
    if n_hi:
        perm_ref, z_ref, u_ref, x0_ref, hn_ref = rest
    else:
        z_ref, u_ref, x0_ref, hn_ref = rest
    i = pl.program_id(1)
    nt = pl.num_programs(1)
    _fill_hn(hn_ref, xc_ref, xp_ref, xn_ref, g_ref[...], sh_ref[0], sc_ref[0], tm, i, nt)
    proj = _dot(hn_ref[...], w_ref[...])
    uf = proj[HALO:HALO + tm, :D_FOURIER].astype(BF16)
    if n_hi:
        uf = _dot(perm_ref[...], uf).astype(BF16)
        z = _dot(uf, gm_ref[...]).astype(BF16)
        z_ref[0] = z.reshape(n_hi, tm // n_hi, 2 * D_FOURIER)
    else:
        z_ref[0] = _dot(uf, gm_ref[...]).astype(BF16)
    c = _dwconv3_rows(proj[:, D_FOURIER:], cw_ref[...], tm) + cb_ref[...]
    x0 = c[:, :D_HYENA]
    x1 = c[:, D_HYENA:2 * D_HYENA]
    v = c[:, 2 * D_HYENA:]
    x0_ref[0] = x0
    u_ref[0] = (v * x1).astype(BF16)


def _fh_in(x, g, shift, scale, w_in, conv_w, conv_b, gmat, n_hi=0):
    B, L, D = x.shape
    tm = _row_tile(L, 512)
    cur, prev, nxt = _halo_specs(tm, D, L)
    n_in = w_in.shape[1]
    nh = 3 * D_HYENA
    full = lambda shape: pl.BlockSpec(shape, lambda b, i: (0,) * len(shape))
    out_tile = lambda c: pl.BlockSpec((1, tm, c), lambda b, i: (b, i, 0))
    in_specs = [cur, prev, nxt, full((1, D)), _mod_spec(shift), _mod_spec(scale),
                full((D, n_in)), full((3, nh)), full((1, nh)), full((D_FOURIER, 2 * D_FOURIER))]
    args = [x, x, x, g.reshape(1, D), shift, scale, w_in, conv_w, conv_b.reshape(1, nh), gmat]
    if n_hi:
        per = tm // n_hi
        r = np.arange(tm)
        src = n_hi * (r % per) + r // per
        args.append(_const(src[:, None] == r[None, :], BF16))
        in_specs.append(full((tm, tm)))
        z_shape = jax.ShapeDtypeStruct((B, n_hi, L // n_hi, 2 * D_FOURIER), BF16)
        z_spec = pl.BlockSpec((1, n_hi, per, 2 * D_FOURIER), lambda b, i: (b, 0, i, 0))
    else:
        z_shape = jax.ShapeDtypeStruct((B, L, 2 * D_FOURIER), BF16)
        z_spec = out_tile(2 * D_FOURIER)
    return pl.pallas_call(
        functools.partial(_fh_in_kernel, tm=tm, n_hi=n_hi),
        out_shape=(z_shape, jax.ShapeDtypeStruct((B, L, D_HYENA), BF16), jax.ShapeDtypeStruct((B, L, D_HYENA), F32)),
        grid=(B, L // tm),
        in_specs=in_specs,
        out_specs=(z_spec, out_tile(D_HYENA), out_tile(D_HYENA)),
        scratch_shapes=[pltpu.VMEM((tm + 2 * HALO, D), BF16)],
        compiler_params=_cparams(("parallel", "arbitrary")),
        name="fh_in",
    )(*args)


def _fourier_kernel(c_ref, s_ref, z_ref, o_ref, *, scale):
    z = z_ref[0]
    y = _dot(c_ref[...], z[:, :D_FOURIER]) + _dot(s_ref[...], z[:, D_FOURIER:])
    o_ref[0] = (y * scale).astype(BF16)


def _fourier_seq(z, cl, sl):
    B, L, _ = z.shape
    tf = _row_tile(L, 512)
    scale = 1.0 / math.sqrt(L * FOURIER_GROUP_DIM)
    return pl.pallas_call(
        functools.partial(_fourier_kernel, scale=scale),
        out_shape=jax.ShapeDtypeStruct((B, L, D_FOURIER), BF16),
        grid=(B, L // tf),
        in_specs=[pl.BlockSpec((tf, L), lambda b, i: (i, 0)),
                  pl.BlockSpec((tf, L), lambda b, i: (i, 0)),
                  pl.BlockSpec((1, L, 2 * D_FOURIER), lambda b, i: (b, 0, 0))],
        out_specs=pl.BlockSpec((1, tf, D_FOURIER), lambda b, i: (b, i, 0)),
        compiler_params=_cparams(("parallel", "arbitrary")),
        name="fourier_seq",
    )(cl, sl, z)


def _fft_f1_kernel(cs_ref, z_ref, o_ref, *, nc):
    for j in range(nc):
        cs = _dot(cs_ref[...], z_ref[0, j])
        cz = cs[:FFT_RADIX]
        sz = cs[FFT_RADIX:]
        o_ref[0, 0, j] = (cz[:, :D_FOURIER] + sz[:, D_FOURIER:]).astype(BF16)
        o_ref[0, 1, j] = (cz[:, D_FOURIER:] - sz[:, :D_FOURIER]).astype(BF16)


def _fft_f2_kernel(*refs, scale, n_grp):
    t_refs, b_refs, o_ref = refs[:n_grp], refs[n_grp:2 * n_grp], refs[2 * n_grp]
    for j in range(n_grp):
        n_hi = b_refs[j].shape[2]
        blk = b_refs[j][0].reshape(2 * n_hi * 8, D_FOURIER)
        y = _dot(t_refs[j][0], blk) * scale
        o_ref[0, :, j] = y.reshape(n_hi, 8, D_FOURIER).astype(BF16)


def _fourier_seq_fft(zp, cs, tmat):
    B, n_hi, _, _ = zp.shape
    L = n_hi * FFT_RADIX
    nc = 8
    ng = FFT_RADIX // 8
    bc = pl.pallas_call(
        functools.partial(_fft_f1_kernel, nc=nc),
        out_shape=jax.ShapeDtypeStruct((B, 2, n_hi, FFT_RADIX, D_FOURIER), BF16),
        grid=(B, n_hi // nc),
        in_specs=[pl.BlockSpec((2 * FFT_RADIX, FFT_RADIX), lambda b, i: (0, 0)),
                  pl.BlockSpec((1, nc, FFT_RADIX, 2 * D_FOURIER), lambda b, i: (b, i, 0, 0))],
        out_specs=pl.BlockSpec((1, 2, nc, FFT_RADIX, D_FOURIER), lambda b, i: (b, 0, i, 0, 0)),
        compiler_params=_cparams(("parallel", "arbitrary")),
        name="fourier_fft1",
    )(cs, zp)
    scale = 1.0 / math.sqrt(L * FOURIER_GROUP_DIM)
    n_grp = 4
    t_specs = [pl.BlockSpec((1, n_hi * 8, 2 * n_hi * 8), lambda b, g, j=j: (g * n_grp + j, 0, 0)) for j in range(n_grp)]
    b_specs = [pl.BlockSpec((1, 2, n_hi, 8, D_FOURIER), lambda b, g, j=j: (b, 0, 0, g * n_grp + j, 0))
               for j in range(n_grp)]
    y = pl.pallas_call(
        functools.partial(_fft_f2_kernel, scale=scale, n_grp=n_grp),
        out_shape=jax.ShapeDtypeStruct((B, n_hi, ng, 8, D_FOURIER), BF16),
        grid=(B, ng // n_grp),
        in_specs=t_specs + b_specs,
        out_specs=pl.BlockSpec((1, n_hi, n_grp, 8, D_FOURIER), lambda b, g: (b, 0, g, 0, 0)),
        compiler_params=_cparams(("parallel", "arbitrary")),
        name="fourier_fft2",
    )(*([tmat] * n_grp + [bc] * n_grp))
    return y.reshape(B, L, D_FOURIER)


def _filter_kernel(z_ref, w1_ref, b1_ref, w2_ref, b2_ref, w3_ref, b3_ref, w4_ref, f_ref, dl_ref,
                   k_ref, n_ref):
    i = pl.program_id(1)
    z = z_ref[0]
    f = f_ref[...]
    hd = lambda a, w: jnp.dot(a, w, precision=HIGHEST, preferred_element_type=F32)
    h = jnp.sin(f * (hd(z, w1_ref[...]) + b1_ref[...]))
    h = jnp.sin(f * (hd(h, w2_ref[...]) + b2_ref[...]))
    h = jnp.sin(f * (hd(h, w3_ref[...]) + b3_ref[...]))
    h = hd(h, w4_ref[0])
    t = z[:, 0:1]
    valid = z[:, HYENA_EMB_DIM:HYENA_EMB_DIM + 1]
    k = h * jnp.exp(-t * jnp.abs(dl_ref[...])) * valid
    k_ref[0] = k

    @pl.when(i == 0)
    def _():
        n_ref[...] = jnp.zeros_like(n_ref)

    n_ref[0] += jnp.sum(jnp.abs(k), axis=0, keepdims=True)


def _hyena_filter(L, w1, b1, w2, b2, w3, b3, w4, freq):
    pos = np.arange(L, dtype=np.float64)
    bands = np.linspace(1e-4, HYENA_BANDS - 1, HYENA_BANDS)

    def emb(p, valid):
        t = p / max(L - 1, 1)
        ang = (2.0 * math.pi / L) * p[:, None] * bands[None, :]
        pad = np.zeros((L, HYENA_EMB_PAD - HYENA_EMB_DIM - 1))
        return np.concatenate([t[:, None], np.cos(ang), -np.sin(ang), valid[:, None], pad], axis=-1)

    zf = emb(pos, np.ones((L,)))
    zb = emb(np.where(pos > 0, L - pos, 0.0), (pos > 0).astype(np.float64))
    z = _const(np.stack([zf, zb]))
    w1p = jnp.concatenate([w1, jnp.zeros((HYENA_EMB_PAD - HYENA_EMB_DIM, HYENA_WIDTH), F32)], axis=0)
    w4s = jnp.stack([w4[:, :D_HYENA], w4[:, D_HYENA:]])
    deltas = _const(np.linspace(HYENA_MIN_DECAY, HYENA_MAX_DECAY, D_HYENA).reshape(1, D_HYENA))
    tr = _row_tile(L, 512)
    W = HYENA_WIDTH
    full = lambda shape: pl.BlockSpec(shape, lambda s, i: (0,) * len(shape))
    return pl.pallas_call(
        _filter_kernel,
        out_shape=(jax.ShapeDtypeStruct((2, L, D_HYENA), F32), jax.ShapeDtypeStruct((2, 1, D_HYENA), F32)),
        grid=(2, L // tr),
        in_specs=[pl.BlockSpec((1, tr, HYENA_EMB_PAD), lambda s, i: (s, i, 0)),
                  full((HYENA_EMB_PAD, W)), full((1, W)), full((W, W)), full((1, W)),
                  full((W, W)), full((1, W)),
                  pl.BlockSpec((1, W, D_HYENA), lambda s, i: (s, 0, 0)),
                  full((1, W)), full((1, D_HYENA))],
        out_specs=(pl.BlockSpec((1, tr, D_HYENA), lambda s, i: (s, i, 0)),
                   pl.BlockSpec((1, 1, D_HYENA), lambda s, i: (s, 0, 0))),
        compiler_params=_cparams(("arbitrary", "arbitrary")),
        name="hyena_filter",
    )(z, w1p, b1.reshape(1, W), w2, b2.reshape(1, W), w3, b3.reshape(1, W), w4s, freq.reshape(1, W), deltas)


def _hy_fwd_kernel(c_ref, s_ref, u_ref, *rest, raw, tf):
    u = u_ref[0]
    ure = _dot(c_ref[...], u)
    uim = -_dot(s_ref[...], u)
    if raw:
        re_ref, im_ref = rest
        re_ref[0] = ure
        im_ref[0] = uim
        return
    kre_ref, kim_ref, re_ref, im_ref = rest
    kre = kre_ref[...]
    kim = kim_ref[...]
    row = lax.broadcasted_iota(jnp.int32, (tf, 1), 0) + pl.program_id(1) * tf
    packed = row == 0
    a = uim * kim
    re_ref[0] = (ure * kre - jnp.where(packed, 0.0, a)).astype(BF16)
    im_ref[0] = jnp.where(packed, a, ure * kim + uim * kre).astype(BF16)


def _hy_fwd(u, cb, sbf, kre=None, kim=None):
    B, L, C = u.shape
    tf = _row_tile(L, 512)
    raw = kre is None
    mat = pl.BlockSpec((tf, L), lambda b, i: (i, 0))
    tile = pl.BlockSpec((1, tf, C), lambda b, i: (b, i, 0))
    in_specs = [mat, mat, pl.BlockSpec((1, L, C), lambda b, i: (b, 0, 0))]
    args = [cb, sbf, u]
    if not raw:
        in_specs += [pl.BlockSpec((tf, C), lambda b, i: (i, 0))] * 2
        args += [kre, kim]
    dt = F32 if raw else BF16
    return pl.pallas_call(
        functools.partial(_hy_fwd_kernel, raw=raw, tf=tf),
        out_shape=(jax.ShapeDtypeStruct((B, L, C), dt), jax.ShapeDtypeStruct((B, L, C), dt)),
        grid=(B, L // tf),
        in_specs=in_specs,
        out_specs=(tile, tile),
        compiler_params=_cparams(("parallel", "arbitrary")),
        name="hyena_fwd_raw" if raw else "hyena_fwd",
    )(*args)


def _spec_combine_kernel(re_ref, im_ref, n_ref, kre_ref, kim_ref, *, tf, n_fft):
    row = lax.broadcasted_iota(jnp.int32, (tf, 1), 0) + pl.program_id(0) * tf
    sgn = (1 - 2 * (row & 1)).astype(F32)
    wgt = jnp.where(row == 0, 1.0 / n_fft, 2.0 / n_fft)
    s = wgt / (n_ref[0] + n_ref[1])
    kre_ref[...] = (re_ref[0] + sgn * re_ref[1]) * s
    kim_ref[...] = (im_ref[0] + sgn * im_ref[1]) * s


def _spec_combine(re, im, norms):
    _, L, C = re.shape
    tf = _row_tile(L, 512)
    pair = pl.BlockSpec((2, tf, C), lambda i: (0, i, 0))
    tile = pl.BlockSpec((tf, C), lambda i: (i, 0))
    return pl.pallas_call(
        functools.partial(_spec_combine_kernel, tf=tf, n_fft=2 * L),
        out_shape=(jax.ShapeDtypeStruct((L, C), F32), jax.ShapeDtypeStruct((L, C), F32)),
        grid=(L // tf,),
        in_specs=[pair, pair, pl.BlockSpec((2, 1, C), lambda i: (0, 0, 0))],
        out_specs=(tile, tile),
        compiler_params=_cparams(("arbitrary",)),
        name="hyena_spec_combine",
    )(re, im, norms)


def _hy_inv_kernel(c_ref, s_ref, re_ref, im_ref, x0_ref, u_ref, b_ref, o_ref):
    y = _dot(c_ref[...], re_ref[0]) - _dot(s_ref[...], im_ref[0])
    o_ref[0] = (x0_ref[0] * (y + u_ref[0].astype(F32) * b_ref[...])).astype(BF16)


def _hy_inv(yre, yim, x0, u, bias, cb, sbi):
    B, L, C = u.shape
    tt = _row_tile(L, 512)
    mat = pl.BlockSpec((tt, L), lambda b, i: (i, 0))
    whole = pl.BlockSpec((1, L, C), lambda b, i: (b, 0, 0))
    tile = pl.BlockSpec((1, tt, C), lambda b, i: (b, i, 0))
    return pl.pallas_call(
        _hy_inv_kernel,
        out_shape=jax.ShapeDtypeStruct((B, L, C), BF16),
        grid=(B, L // tt),
        in_specs=[mat, mat, whole, whole, tile, tile, pl.BlockSpec((1, C), lambda b, i: (0, 0))],
        out_specs=tile,
        compiler_params=_cparams(("parallel", "arbitrary")),
        name="hyena_inv",
    )(cb, sbi, yre, yim, x0, u, bias.reshape(1, C))


def _fft_a_kernel(t_ref, x_ref, o_ref):
    r2, rows, c = x_ref.shape[1:]
    for j in range(rows // FFT_KRON):
        grp = slice(j * FFT_KRON, (j + 1) * FFT_KRON)
        y = _dot(t_ref[...], x_ref[0, :, grp, :].reshape(r2 * FFT_KRON, c))
        o_ref[0, :, :, grp, :] = y.reshape(2, y.shape[0] // (2 * FFT_KRON), FFT_KRON, c).astype(BF16)


def _hy_fft_a(x4, t1):
    Bx, R, _, C = x4.shape
    n_hi = t1.shape[0] // (2 * FFT_KRON)
    return pl.pallas_call(
        _fft_a_kernel,
        out_shape=jax.ShapeDtypeStruct((Bx, 2, n_hi, FFT_RADIX, C), BF16),
        grid=(Bx, FFT_RADIX // FFT_ROWS),
        in_specs=[pl.BlockSpec(t1.shape, lambda b, i: (0, 0)),
                  pl.BlockSpec((1, R, FFT_ROWS, C), lambda b, i: (b, 0, i, 0))],
        out_specs=pl.BlockSpec((1, 2, n_hi, FFT_ROWS, C), lambda b, i: (b, 0, 0, i, 0)),
        compiler_params=_cparams(("parallel", "arbitrary")),
        name="hyena_fft_a",
    )(t1, x4)


def _fft_b_kernel(gf_ref, a_ref, *rest, raw, n_fft):
    nb, _, kb = a_ref.shape[:3]
    if raw:
        n_ref, o_ref = rest
        s = 1.0 / (n_fft * (n_ref[0] + n_ref[1]))
    else:
        gi_ref, k_ref, o_ref = rest
    for kk in range(kb):
        gf = gf_ref[kk]
        if not raw:
            gi = gi_ref[kk]
            kr = k_ref[0, 0, kk]
            ki = k_ref[0, 1, kk]
        for b in range(nb):
            a = a_ref[b, :, kk].reshape(2 * FFT_RADIX, D_HYENA)
            x = _dot(gf, a)
            if raw:
                o_ref[b, :, kk] = (x * s).reshape(2, FFT_RADIX, D_HYENA)
                continue
            xr = x[:FFT_RADIX]
            xi = x[FFT_RADIX:]
            y = jnp.concatenate([xr * kr - xi * ki, xr * ki + xi * kr], axis=0).astype(BF16)
            o_ref[b, :, kk] = _dot(gi, y).reshape(2, FFT_RADIX, D_HYENA).astype(BF16)


def _hy_fft_b(a5, gf, gi=None, kspec=None, norms=None):
    Bx, _, n_hi, _, C = a5.shape
    raw = kspec is None
    kb = 4 if raw else 2
    blk = pl.BlockSpec((Bx, 2, kb, FFT_RADIX, C), lambda k: (0, 0, k, 0, 0))
    tab = pl.BlockSpec((kb, 2 * FFT_RADIX, 2 * FFT_RADIX), lambda k: (k, 0, 0))
    if raw:
        in_specs = [tab, blk, pl.BlockSpec((2, 1, C), lambda k: (0, 0, 0))]
        args = (gf, a5, norms)
    else:
        kblk = pl.BlockSpec((1, 2, kb, FFT_RADIX, C), lambda k: (0, 0, k, 0, 0))
        in_specs = [tab, blk, tab, kblk]
        args = (gf, a5, gi, kspec)
    return pl.pallas_call(
        functools.partial(_fft_b_kernel, raw=raw, n_fft=n_hi * FFT_RADIX),
        out_shape=jax.ShapeDtypeStruct(a5.shape, F32 if raw else BF16),
        grid=(n_hi // kb,),
        in_specs=in_specs,
        out_specs=blk,
        compiler_params=_cparams(("arbitrary",)),
        name="hyena_fft_spec" if raw else "hyena_fft_b",
    )(*args)


def _fft_c_kernel(t_ref, v_ref, x0_ref, u_ref, b_ref, o_ref):
    _, n_hi, rows, c = v_ref.shape[1:]
    for j in range(rows // FFT_KRON):
        grp = slice(j * FFT_KRON, (j + 1) * FFT_KRON)
        y = _dot(t_ref[...], v_ref[0, :, :, grp, :].reshape(2 * n_hi * FFT_KRON, c))
        y = y.reshape(y.shape[0] // FFT_KRON, FFT_KRON, c)
        o_ref[0, :, grp, :] = (x0_ref[0, :, grp, :] * (y + u_ref[0, :, grp, :].astype(F32) * b_ref[...])).astype(BF16)


def _hy_fft_c(v5, t3, x0, u, bias):
    B, _, n_hi, _, C = v5.shape
    R = t3.shape[0] // FFT_KRON
    tile = pl.BlockSpec((1, R, FFT_ROWS, C), lambda b, i: (b, 0, i, 0))
    return pl.pallas_call(
        _fft_c_kernel,
        out_shape=jax.ShapeDtypeStruct((B, R, FFT_RADIX, C), BF16),
        grid=(B, FFT_RADIX // FFT_ROWS),
        in_specs=[pl.BlockSpec(t3.shape, lambda b, i: (0, 0)),
                  pl.BlockSpec((1, 2, n_hi, FFT_ROWS, C), lambda b, i: (b, 0, 0, i, 0)),
                  tile, tile, pl.BlockSpec((1, C), lambda b, i: (0, 0))],
        out_specs=tile,
        compiler_params=_cparams(("parallel", "arbitrary")),
        name="hyena_fft_c",
    )(t3, v5, x0, u, bias.reshape(1, C))


def _ffn_kernel(xc_ref, xp_ref, xn_ref, g1_ref, wo_ref, g_ref, sh_ref, sc_ref, gate_ref, wu_ref, cw_ref, cb_ref,
                wd_ref, *rest, tm, n_y):
    y_refs = rest[:3 * n_y]
    o_ref, yext_ref, hn_ref, act_ref = rest[3 * n_y:]
    i = pl.program_id(1)
    nt = pl.num_programs(1)
    off = 0
    for k in range(n_y):
        yc_ref, yp_ref, yn_ref = y_refs[3 * k:3 * k + 3]
        c = yc_ref.shape[-1]
        yext_ref[0:HALO, off:off + c] = yp_ref[0].astype(BF16)
        yext_ref[HALO:HALO + tm, off:off + c] = yc_ref[0].astype(BF16)
        yext_ref[HALO + tm:, off:off + c] = yn_ref[0].astype(BF16)
        off += c
    mix = g1_ref[0] * _dot(yext_ref[...], wo_ref[...])
    x1 = xc_ref[0] + mix[HALO:HALO + tm]
    _fill_hn_rows(hn_ref, x1, xp_ref[0] + mix[:HALO], xn_ref[0] + mix[HALO + tm:], g_ref[...], sh_ref[0], sc_ref[0],
                  tm, i, nt)
    for c in range(0, D_FF, FFN_CHUNK):
        cols = slice(c, c + FFN_CHUNK)
        gx = _dot(hn_ref[...], wu_ref[:, cols])
        vx = _dot(hn_ref[HALO:HALO + tm, :], wu_ref[:, D_FF + c:D_FF + c + FFN_CHUNK])
        cv = _dwconv3_rows(gx, cw_ref[:, cols], tm) + cb_ref[:, cols]
        act_ref[:, cols] = (_silu(cv) * vx).astype(BF16)
    o_ref[0] = x1 + gate_ref[0] * _dot(act_ref[...], wd_ref[...])


def _mix_ffn(x, ys, w_o, gate1, g, shift, scale, gate2, w_up, conv_w, conv_b, w_down):
    B, L, D = x.shape
    tm = _row_tile(L, 1024)
    cur, prev, nxt = _halo_specs(tm, D, L)
    const = lambda shape: pl.BlockSpec(shape, lambda b, i: (0,) * len(shape), pipeline_mode=pl.Buffered(1))
    y_specs, y_args = [], []
    for y in ys:
        y_specs += list(_halo_specs(tm, y.shape[-1], L))
        y_args += [y, y, y]
    return pl.pallas_call(
        functools.partial(_ffn_kernel, tm=tm, n_y=len(ys)),
        out_shape=jax.ShapeDtypeStruct((B, L, D), F32),
        grid=(B, L // tm),
        in_specs=[cur, prev, nxt, _mod_spec(gate1), const(w_o.shape), const((1, D)), _mod_spec(shift),
                  _mod_spec(scale), _mod_spec(gate2), const((D, 2 * D_FF)), const((3, D_FF)), const((1, D_FF)),
                  const((D_FF, D))] + y_specs,
        out_specs=pl.BlockSpec((1, tm, D), lambda b, i: (b, i, 0)),
        scratch_shapes=[pltpu.VMEM((tm + 2 * HALO, w_o.shape[0]), BF16), pltpu.VMEM((tm + 2 * HALO, D), BF16),
                        pltpu.VMEM((tm, D_FF), BF16)],
        compiler_params=_cparams(("parallel", "arbitrary")),
        name="mix_ffn",
    )(x, x, x, gate1, w_o, g.reshape(1, D), shift, scale, gate2, w_up, conv_w, conv_b.reshape(1, D_FF), w_down,
      *y_args)


def _rms(v, g, n):
    return v * lax.rsqrt(jnp.sum(v * v, axis=-1, keepdims=True) * (1.0 / n) + EPS) * g


def _rope_gain_tables(gain, cos, sa, sb):
    rot = slice(NOPE, QK_DIM)
    gr = gain[rot]
    return gain[:NOPE], gr * cos[rot], pltpu.roll(gr, 8, 0) * sa[rot], pltpu.roll(gr, ROPE - 8, 0) * sb[rot]


def _rope_t(yr, gc, gsa, gsb):
    return yr * gc + pltpu.roll(yr, 8, 0) * gsa + pltpu.roll(yr, ROPE - 8, 0) * gsb


def _sumsq(x):
    return jnp.sum(x * x, axis=0, keepdims=True)


def _mla_proj_kernel(x_ref, g_ref, sh_ref, sc_ref, win_ref, qan_ref, wuq_ref, qn_ref, kvn_ref, wuk_ref,
                     wuv_ref, kn_ref, cos_ref, sa_ref, sb_ref, *outs, with_q):
    hn = _norm_mod(x_ref[0], g_ref[...], sh_ref[0], sc_ref[0]).astype(BF16)
    a = _dot(hn, win_ref[...])
    tm = a.shape[0]
    cos = cos_ref[...]
    sa = sa_ref[...]
    sb = sb_ref[...]
    pad = jnp.zeros((HEAD_PAD - QK_DIM, tm), ATTN_DTYPE)
    if with_q:
        qt_ref, k_ref, vt_ref = outs
        qat = _rms(a[:, :Q_LORA], qan_ref[...], Q_LORA).T.astype(BF16)
        qft = _dot(wuq_ref[...], qat)
        gn, gc, gsa, gsb = _rope_gain_tables(qn_ref[...], cos, sa, sb)
        for h in range(MLA_HEADS):
            base = h * HEAD_PAD
            xn = qft[base:base + NOPE]
            xr = qft[base + NOPE:base + QK_DIM]
            r = lax.rsqrt((_sumsq(xn) + _sumsq(xr)) * (1.0 / QK_DIM) + EPS)
            qt_ref[0, base:base + NOPE, :] = (xn * r * gn).astype(ATTN_DTYPE)
            qt_ref[0, base + NOPE:base + QK_DIM, :] = _rope_t(xr * r, gc, gsa, gsb).astype(ATTN_DTYPE)
            qt_ref[0, base + QK_DIM:base + HEAD_PAD, :] = pad
    else:
        k_ref, vt_ref = outs
    ckvt = _rms(a[:, Q_LORA:Q_LORA + KV_LORA], kvn_ref[...], KV_LORA).T.astype(BF16)
    kft = _dot(wuk_ref[...], ckvt)
    vt_ref[0] = _dot(wuv_ref[...], ckvt).astype(ATTN_DTYPE)
    kpe = a[:, Q_LORA + KV_LORA:].T[:ROPE]
    gn, gc, gsa, gsb = _rope_gain_tables(kn_ref[...], cos, sa, sb)
    kpe_rot = _rope_t(kpe, gc, gsa, gsb)
    kpe_ss = _sumsq(kpe)
    for h in range(MLA_HEADS):
        base = h * HEAD_PAD
        xn = kft[base:base + NOPE]
        r = lax.rsqrt((_sumsq(xn) + kpe_ss) * (1.0 / QK_DIM) + EPS)
        kt = jnp.concatenate([xn * r * gn, kpe_rot * r, jnp.zeros((HEAD_PAD - QK_DIM, tm), F32)], axis=0)
        k_ref[0, :, base:base + HEAD_PAD] = kt.T.astype(ATTN_DTYPE)


def _mla_proj(x, g, shift, scale, wts, rope_tabs, with_q):
    B, L, D = x.shape
    tm = _row_tile(L, 1024)
    win, qan, wuq, qn, kvn, wuk, wuv, kn = wts
    cos, sa, sb = rope_tabs
    HP = MLA_HEADS * HEAD_PAD
    HV = MLA_HEADS * V_DIM
    full = lambda a: pl.BlockSpec(a.shape, lambda b, i: (0,) * a.ndim)
    tab = pl.BlockSpec((HEAD_PAD, tm), lambda b, i: (0, i))
    out_shape = [jax.ShapeDtypeStruct((B, L, HP), ATTN_DTYPE), jax.ShapeDtypeStruct((B, HV, L), ATTN_DTYPE)]
    out_specs = [pl.BlockSpec((1, tm, HP), lambda b, i: (b, i, 0)),
                 pl.BlockSpec((1, HV, tm), lambda b, i: (b, 0, i))]
    if with_q:
        out_shape = [jax.ShapeDtypeStruct((B, HP, L), ATTN_DTYPE)] + out_shape
        out_specs = [pl.BlockSpec((1, HP, tm), lambda b, i: (b, 0, i))] + out_specs
    g2 = g.reshape(1, D)
    gain_tab = lambda v: jnp.broadcast_to(v.reshape(HEAD_PAD, 1), (HEAD_PAD, tm))
    qn = gain_tab(qn)
    kn = gain_tab(kn)
    return pl.pallas_call(
        functools.partial(_mla_proj_kernel, with_q=with_q),
        out_shape=tuple(out_shape),
        grid=(B, L // tm),
        in_specs=[pl.BlockSpec((1, tm, D), lambda b, i: (b, i, 0)), full(g2), _mod_spec(shift), _mod_spec(scale),
                  full(win), full(qan), full(wuq), full(qn), full(kvn), full(wuk), full(wuv), full(kn),
                  tab, tab, tab],
        out_specs=tuple(out_specs),
        compiler_params=_cparams(("parallel", "arbitrary")),
        name="mla_proj_q" if with_q else "mla_proj_kv",
    )(x, g2, shift, scale, win, qan, wuq, qn, kvn, wuk, wuv, kn, cos, sa, sb)


def _attn_kernel(qt_ref, kx_ref, kc_ref, vtx_ref, vtc_ref, o_ref, ax_ref, ac_ref, am_ref, bx_ref, bc_ref, bm_ref,
                 px_ref, pc_ref):
    t = pl.program_id(0)
    io = (qt_ref, kx_ref, kc_ref, vtx_ref, vtc_ref, o_ref, px_ref, pc_ref)

    @pl.when(t == 0)
    def _():
        bx_ref[...] = jnp.zeros(bx_ref.shape, BF16)
        bc_ref[...] = jnp.zeros(bc_ref.shape, BF16)
        bm_ref[...] = jnp.zeros(bm_ref.shape, BF16)

    @pl.when(t % 2 == 0)
    def _():
        _attn_step(*io, (ax_ref, ac_ref, am_ref), (bx_ref, bc_ref, bm_ref))

    @pl.when(t % 2 == 1)
    def _():
        _attn_step(*io, (bx_ref, bc_ref, bm_ref), (ax_ref, ac_ref, am_ref))


def _attn_step(qt_ref, kx_ref, kc_ref, vtx_ref, vtc_ref, o_ref, px_ref, pc_ref, parked, ready):
    wx_ref, wc_ref, wm_ref = parked
    rx_ref, rc_ref, rm_ref = ready
    tq = qt_ref.shape[2]
    off = rm_ref[...][None]

    def probs(s_ref):
        n = s_ref.shape[0]
        s = s_ref[...].reshape(n // 16, 16, 2 * tq)
        return jnp.exp2(s - off).reshape(n, 2 * tq).astype(ATTN_DTYPE)

    px_ref[...] = probs(rx_ref)
    pc_ref[...] = probs(rc_ref)

    q0 = qt_ref[0, :HEAD_PAD, :]
    q1 = qt_ref[0, HEAD_PAD:, :]
    zq = jnp.zeros_like(q0)
    qbd = jnp.concatenate([jnp.concatenate([q0, zq], axis=1), jnp.concatenate([zq, q1], axis=1)], axis=0)
    nx = _dot(kx_ref[0], qbd).astype(BF16)
    nc = _dot(kc_ref[0], qbd).astype(BF16)
    wx_ref[...] = nx
    wc_ref[...] = nc
    m = jnp.maximum(jnp.max(nx, axis=0, keepdims=True), jnp.max(nc, axis=0, keepdims=True)).astype(F32)
    wm_ref[...] = jnp.broadcast_to((m - ATTN_P_SHIFT).astype(BF16), wm_ref.shape)

    vtx = vtx_ref[0]
    vtc = vtc_ref[0]
    halves = []
    for j in range(2):
        own = slice(j * V_DIM, (j + 1) * V_DIM)
        oth = slice((1 - j) * V_DIM, (2 - j) * V_DIM)
        cols = slice(j * tq, (j + 1) * tq)
        parts_x = [vtx[own], jnp.ones_like(vtx[oth])]
        parts_c = [vtc[own], jnp.ones_like(vtc[oth])]
        if j == 1:
            parts_x.reverse()
            parts_c.reverse()
        r = (_dot(jnp.concatenate(parts_x, axis=0), px_ref[:, cols])
             + _dot(jnp.concatenate(parts_c, axis=0), pc_ref[:, cols]))
        halves.append(r[own] * (1.0 / r[oth][0:1]))
    o_ref[0] = jnp.concatenate(halves, axis=0).T.astype(BF16)


def _attention(qt, kx, kc, vtx, vtc):
    B, HP, L = qt.shape
    Lc = kc.shape[1]
    tq = _row_tile(L, 1024)
    nhp = MLA_HEADS // 2
    nq = L // tq
    n_tiles = B * nhp * nq

    def tile(t):
        t = jnp.clip(t, 0, n_tiles - 1)
        return t // (nhp * nq), (t // nq) % nhp, t % nq

    def logits_side(f):
        return lambda t: f(*tile(t))

    def value_side(f):
        return lambda t: f(*tile(t - 1))

    return pl.pallas_call(
        _attn_kernel,
        out_shape=jax.ShapeDtypeStruct((B, L, MLA_HEADS * V_DIM), BF16),
        grid=(n_tiles + 1,),
        in_specs=[pl.BlockSpec((1, 2 * HEAD_PAD, tq), logits_side(lambda b, h, i: (b, h, i))),
                  pl.BlockSpec((1, L, 2 * HEAD_PAD), logits_side(lambda b, h, i: (b, 0, h))),
                  pl.BlockSpec((1, Lc, 2 * HEAD_PAD), logits_side(lambda b, h, i: (b, 0, h))),
                  pl.BlockSpec((1, 2 * V_DIM, L), value_side(lambda b, h, i: (b, h, 0))),
                  pl.BlockSpec((1, 2 * V_DIM, Lc), value_side(lambda b, h, i: (b, h, 0)))],
        out_specs=pl.BlockSpec((1, tq, 2 * V_DIM), value_side(lambda b, h, i: (b, i, h))),
        scratch_shapes=[pltpu.VMEM((L, 2 * tq), BF16), pltpu.VMEM((Lc, 2 * tq), BF16), pltpu.VMEM((16, 2 * tq), BF16)] * 2
                       + [pltpu.VMEM((L, 2 * tq), ATTN_DTYPE), pltpu.VMEM((Lc, 2 * tq), ATTN_DTYPE)],
        compiler_params=_cparams(("arbitrary",)),
        name="mla_attention",
    )(qt, kx, kc, vtx, vtc)


def _const(a, dtype=F32):
    return jnp.asarray(np.asarray(a, np.float32)).astype(dtype)


def _angle(num, period):
    return (num % period).astype(np.float64) * (2.0 * math.pi / period)


def _trig_matrix(n, period):
    r = np.arange(n, dtype=np.int64)
    a = _angle(r[:, None] * r[None, :], period)
    return np.cos(a), np.sin(a)


def _dft_tables(L):
    cl, sl = _trig_matrix(L, L)
    cb, sb = _trig_matrix(L, 2 * L)
    idx = np.arange(L)
    alt = (1 - 2 * (idx & 1)).astype(np.float64)
    sbf = np.where(idx[:, None] == 0, -alt[None, :], sb)
    sbi = np.where(idx[None, :] == 0, -alt[:, None], sb)
    return tuple(_const(m, BF16) for m in (cl, sl, cb, sbf, sbi))


def _fourier_fft_tables(L):
    n_hi = L // FFT_RADIX
    i = np.arange(FFT_RADIX, dtype=np.int64)
    a = _angle(i[:, None] * i[None, :], FFT_RADIX)
    cs = np.concatenate([np.cos(a), np.sin(a)], axis=0)
    ng = FFT_RADIX // 8
    g = np.arange(ng, dtype=np.int64)[:, None, None, None]
    k1 = np.arange(n_hi, dtype=np.int64)[None, :, None, None]
    j = np.arange(8, dtype=np.int64)[None, None, :, None]
    n1 = np.arange(n_hi, dtype=np.int64)[None, None, None, :]
    ang = _angle(n1 * (FFT_RADIX * k1 + 8 * g + j), L)
    eye = np.eye(8)
    blocks = [t[:, :, :, :, None] * eye[None, None, :, None, :] for t in (np.cos(ang), np.sin(ang))]
    tmat = np.stack(blocks, axis=3)
    return _const(cs, BF16), _const(tmat.reshape(ng, n_hi * 8, 2 * n_hi * 8), BF16)


def _hyena_fft_tables(L):
    n_fft = 2 * L
    n_hi = n_fft // FFT_RADIX
    k2 = np.arange(n_hi, dtype=np.int64)
    a1 = _angle(k2[:, None] * k2[None, :], n_hi)
    c1, s1 = np.cos(a1), np.sin(a1)
    eye = np.eye(FFT_KRON)
    kron = lambda m: (m[:, None, :, None] * eye[None, :, None, :]).reshape(m.shape[0] * FFT_KRON, m.shape[1] * FFT_KRON)
    t1 = _const(np.concatenate([kron(c1), kron(-s1)], axis=0), BF16)
    t1_half = t1[:, :n_hi // 2 * FFT_KRON]
    t3 = t1_half.T
    r = np.arange(FFT_RADIX, dtype=np.int64)
    k = k2[:, None, None] + n_hi * r[None, :, None]
    a2 = _angle(k * r[None, None, :], n_fft)
    c2, s2 = _const(np.cos(a2), BF16), _const(np.sin(a2), BF16)
    gf = jnp.concatenate([jnp.concatenate([c2, s2], axis=2), jnp.concatenate([-s2, c2], axis=2)], axis=1)
    c2t, s2t = jnp.swapaxes(c2, 1, 2), jnp.swapaxes(s2, 1, 2)
    gi = jnp.concatenate([jnp.concatenate([c2t, -s2t], axis=2), jnp.concatenate([s2t, c2t], axis=2)], axis=1)
    return t1, t1_half, t3, gf, gi


def _group_dft_matrix():
    g = FOURIER_GROUP_DIM
    ng = D_FOURIER // g
    c, s = _trig_matrix(g, g)
    eye = np.eye(ng)
    return _const(np.concatenate([np.kron(eye, c), -np.kron(eye, s)], axis=1), BF16)


def _rope_tables(L):
    pos = np.arange(L)
    rows = (pos // GRID_W).astype(np.float64)
    cols = (pos % GRID_W).astype(np.float64)
    nf = ROPE // 4
    inv_freq = ROPE_THETA ** (-np.arange(nf, dtype=np.float64) / nf)
    ar = rows[:, None] * inv_freq[None, :]
    ac = cols[:, None] * inv_freq[None, :]
    one = np.ones((L, NOPE))
    z_n = np.zeros((L, NOPE))
    z_f = np.zeros((L, nf))
    tail = np.zeros((L, HEAD_PAD - QK_DIM))
    cos = np.concatenate([one, np.cos(ar), np.cos(ar), np.cos(ac), np.cos(ac), tail], axis=1)
    sa = np.concatenate([z_n, z_f, np.sin(ar), z_f, np.sin(ac), tail], axis=1)
    sb = np.concatenate([z_n, -np.sin(ar), z_f, -np.sin(ac), z_f, tail], axis=1)
    return _const(cos.T), _const(sa.T), _const(sb.T)


def _identity_rope_tables(L):
    return jnp.ones((HEAD_PAD, L), F32), jnp.zeros((HEAD_PAD, L), F32), jnp.zeros((HEAD_PAD, L), F32)


def _pad_heads(w, per_head, keep):
    K = w.shape[0]
    w = w.reshape(K, MLA_HEADS, per_head)[:, :, :keep]
    w = jnp.pad(w, ((0, 0), (0, 0), (0, HEAD_PAD - keep)))
    return w.reshape(K, MLA_HEADS * HEAD_PAD)


def _pad_lanes(v, n):
    return jnp.pad(v, (0, n - v.shape[0])).reshape(1, n)


def _fourier_hyena_mixer(x, g, shift, scale, w_in, conv_w, conv_b, gmat, taps, norms, hy_bias):
    B, L, _ = x.shape
    if (L // FFT_RADIX) % 16 == 0:
        z, u, x0 = _fh_in(x, g, shift, scale, w_in, conv_w, conv_b, gmat, n_hi=L // FFT_RADIX)
        cs, tmat = _fourier_fft_tables(L)
        y_f = _fourier_seq_fft(z, cs, tmat)
        t1, t1_half, t3, gf, gi = _hyena_fft_tables(L)
        n_hi = 2 * L // FFT_RADIX
        rows = lambda a, r: a.reshape(a.shape[0], r, FFT_RADIX, D_HYENA)
        kspec = _hy_fft_b(_hy_fft_a(rows(taps.astype(BF16).reshape(1, 2 * L, D_HYENA), n_hi), t1), gf, norms=norms)
        v = _hy_fft_b(_hy_fft_a(rows(u, n_hi // 2), t1_half), gf, gi, kspec)
        y_h = _hy_fft_c(v, t3, rows(x0, n_hi // 2), rows(u, n_hi // 2), hy_bias)
        return y_f, y_h.reshape(B, L, D_HYENA)
    z, u, x0 = _fh_in(x, g, shift, scale, w_in, conv_w, conv_b, gmat)
    cl, sl, cb, sbf, sbi = _dft_tables(L)
    k_re, k_im = _hy_fwd(taps.astype(BF16), cb, sbf)
    kre, kim = _spec_combine(k_re, k_im, norms)
    y_f = _fourier_seq(z, cl, sl)
    yre, yim = _hy_fwd(u, cb, sbf, kre, kim)
    y_h = _hy_inv(yre, yim, x0, u, hy_bias, cb, sbi)
    return y_f, y_h


def kernel(x, c, ctx, c_ctx, norm1, norm2, w_mod, b_mod, ffn_w_up, ffn_conv_w, ffn_conv_b, ffn_w_down,
           fh_w_in, fh_w_out, hy_conv_w, hy_conv_b, hy_filt_w1, hy_filt_b1, hy_filt_w2, hy_filt_b2,
           hy_filt_w3, hy_filt_b3, hy_filt_w4, hy_freq, hy_bias, mla_w_in, mla_q_a_norm, mla_w_uq,
           mla_kv_a_norm, mla_w_ukv, mla_q_norm, mla_k_norm, mla_w_o):
    B, L, D = x.shape
    Lc = ctx.shape[1]
    bf = lambda a: a.astype(BF16)

    n_cond = -(-(B + 1) // 8) * 8
    cond = jnp.concatenate([c, c_ctx[None, :], jnp.zeros((n_cond - B - 1, D), F32)], axis=0)

    def mods(i):
        m = _adaln(cond, w_mod, b_mod, i)
        mx = [m[:B, j * D:(j + 1) * D].reshape(B, 1, D) for j in range(6)]
        mc = [m[B:B + 1, j * D:(j + 1) * D].reshape(1, 1, D) for j in range(6)]
        return mx, mc

    (sh1, sc1, g1, sh2, sc2, g2), (csh1, csc1, cg1, csh2, csc2, cg2) = mods(0)
    gmat = _group_dft_matrix()
    w_in0 = bf(fh_w_in[0])
    w_out0 = bf(fh_w_out[0])
    ffn0 = (bf(ffn_w_up[0]), ffn_conv_w[0], ffn_conv_b[0], bf(ffn_w_down[0]))
    filt = (hy_filt_w1[0], hy_filt_b1[0], hy_filt_w2[0], hy_filt_b2[0], hy_filt_w3[0], hy_filt_b3[0],
            hy_filt_w4[0], hy_freq[0])

    def mixer_layer(h, n_seq, m1, m2):
        s1, c1, gt1 = m1
        s2, c2, gt2 = m2
        taps, norms = _hyena_filter(n_seq, *filt)
        y_f, y_h = _fourier_hyena_mixer(h, norm1[0], s1, c1, w_in0, hy_conv_w[0], hy_conv_b[0], gmat, taps,
                                        norms, hy_bias[0])
        return _mix_ffn(h, [y_f, y_h], w_out0, gt1, norm2[0], s2, c2, gt2, *ffn0)

    x = mixer_layer(x, L, (sh1, sc1, g1), (sh2, sc2, g2))
    ctx = mixer_layer(ctx, Lc, (csh1, csc1, cg1), (csh2, csc2, cg2))

    (sh1, sc1, g1, sh2, sc2, g2), (csh1, csc1, _, _, _, _) = mods(1)
    w_in1 = bf(jnp.pad(mla_w_in[0], ((0, 0), (0, MLA_IN_PAD - mla_w_in.shape[2]))))
    wuq = bf(_pad_heads(mla_w_uq[0], QK_DIM, QK_DIM).T)
    wuk = bf(_pad_heads(mla_w_ukv[0], NOPE + V_DIM, NOPE).T)
    wuv = bf(mla_w_ukv[0].reshape(KV_LORA, MLA_HEADS, NOPE + V_DIM)[:, :, NOPE:].reshape(KV_LORA, MLA_HEADS * V_DIM).T)
    half_scale = math.sqrt(QK_DIM ** -0.5 * math.log2(math.e))
    wts = (w_in1, mla_q_a_norm[0].reshape(1, Q_LORA), wuq, _pad_lanes(mla_q_norm[0] * half_scale, HEAD_PAD),
           mla_kv_a_norm[0].reshape(1, KV_LORA), wuk, wuv, _pad_lanes(mla_k_norm[0] * half_scale, HEAD_PAD))
    qt, kx, vtx = _mla_proj(x, norm1[1], sh1, sc1, wts, _rope_tables(L), True)
    kc, vtc = _mla_proj(ctx, norm1[1], csh1, csc1, wts, _identity_rope_tables(Lc), False)
    o = _attention(qt, kx, kc, vtx, vtc)
    ffn1 = (bf(ffn_w_up[1]), ffn_conv_w[1], ffn_conv_b[1], bf(ffn_w_down[1]))
    return _mix_ffn(x, [o], bf(mla_w_o[0]), g1, norm2[1], sh2, sc2, g2, *ffn1)
```
